```python
import jax, jax.numpy as jnp
from jax import lax
import numpy as np

D_MODEL = 2048
BATCH = 1
SEQ = 8192
DEPTH = 1
DEC_BATCH = 32
DEC_SEQ = 8
PAST_LEN = 16384
PAGE_SIZE = 128

HEAD_DIM = 128
HEADS_PER_GROUP = 4
WINDOWS = (128, 512, 2048)
DILATIONS = (1, 4, 16)
N_ATT_GROUPS = len(WINDOWS)
N_HEADS = N_ATT_GROUPS * HEADS_PER_GROUP
ATT_WIDTH = N_HEADS * HEAD_DIM
ATT_OUT_WIDTH = HEADS_PER_GROUP * HEAD_DIM
SCALE = HEAD_DIM ** -0.5
D_CONV = D_MODEL // 2
CONV_WIDTH = 3
PLE_DIM = 256
N_EXPERT_GROUPS = 4
EXPERTS_PER_GROUP = 8
N_EXPERTS = N_EXPERT_GROUPS * EXPERTS_PER_GROUP
TOP_K_FINE = 2
D_EXPERT = D_MODEL // 4
MOE_BLOCK = 128
EPS = 1e-6

COL_C = D_CONV
COL_H = 2 * D_CONV
COL_Q = 3 * D_CONV
COL_K = COL_Q + ATT_WIDTH
COL_V = COL_K + ATT_WIDTH
COL_GC = COL_V + ATT_WIDTH
COL_GA = COL_GC + D_MODEL
IN_COLS = COL_GA + D_MODEL
SPLIT_POINTS = (COL_C, COL_H, COL_Q, COL_K, COL_V, COL_GC, COL_GA)

kernel_name = "hybrid_shortconv_dilated_swa_hmoe_step"


def rmsnorm(x, g):
    xf = x.astype(jnp.float32)
    y = xf * lax.rsqrt(jnp.mean(xf * xf, axis=-1, keepdims=True) + EPS)
    return y.astype(x.dtype) * g


def dilated_attention_prompt(q, k, v, window, dil):
    B, S, H, E = q.shape
    nk = window // dil
    blk = nk
    unit = dil * blk
    s_pad = -(-S // unit) * unit
    m_len = s_pad // dil
    nb = s_pad // unit

    def to_blocks(a):
        a = jnp.pad(a, ((0, 0), (0, s_pad - S), (0, 0), (0, 0)))
        a = a.reshape(B, m_len, dil, H, E).transpose(0, 2, 1, 3, 4)
        return a.reshape(B, dil, nb, blk, H, E)

    def with_prev(a):
        prev = jnp.pad(a, ((0, 0), (0, 0), (1, 0), (0, 0), (0, 0), (0, 0)))[:, :, :-1]
        return jnp.concatenate([prev, a], axis=3)

    qb = to_blocks(q)
    kb = with_prev(to_blocks(k))
    vb = with_prev(to_blocks(v))
    qi = jnp.arange(blk)[:, None] + blk
    ki = jnp.arange(2 * blk)[None, :]
    dist = qi - ki
    band = (dist >= 0) & (dist <= nk)
    exists = (jnp.arange(nb)[:, None, None] > 0) | (ki[None] >= blk)
    mask = band[None] & exists
    s = jnp.einsum('brnqhe,brnkhe->brnhqk', qb, kb).astype(jnp.float32) * SCALE
    s = jnp.where(mask[None, None, :, None], s, -jnp.inf)
    lse = jax.nn.logsumexp(s, axis=-1, keepdims=True)
    p = jnp.exp(s - lse).astype(v.dtype)
    o = jnp.einsum('brnhqk,brnkhe->brnqhe', p, vb)
    o = o.reshape(B, dil, m_len, H, E).transpose(0, 2, 1, 3, 4).reshape(B, s_pad, H, E)[:, :S]
    lse = lse[..., 0].transpose(0, 1, 2, 4, 3).reshape(B, dil, m_len, H)
    lse = lse.transpose(0, 2, 1, 3).reshape(B, s_pad, H)[:, :S]
    return o, lse


def dilated_attention_sample(q, k_all, v_all, past, window, dil):
    T = q.shape[1]
    nk = window // dil
    idx = past + jnp.arange(T)[:, None] - dil * jnp.arange(nk + 1)[None, :]
    valid = idx >= 0
    idx = jnp.maximum(idx, 0)
    kg = k_all[:, idx]
    vg = v_all[:, idx]
    s = jnp.einsum('bthe,btjhe->bthj', q, kg).astype(jnp.float32) * SCALE
    s = jnp.where(valid[None, :, None, :], s, -jnp.inf)
    lse = jax.nn.logsumexp(s, axis=-1, keepdims=True)
    p = jnp.exp(s - lse).astype(v_all.dtype)
    o = jnp.einsum('bthj,btjhe->bthe', p, vg)
    return o, lse[..., 0]


def routed_experts(h, expert_idx, gate, w_gate_e, w_up_e, w_down_e):
    n_tok = h.shape[0]
    n_asg = n_tok * TOP_K_FINE
    flat_e = expert_idx.reshape(-1)
    flat_tok = jnp.arange(n_asg, dtype=jnp.int32) // TOP_K_FINE
    flat_w = gate.reshape(-1)
    order = jnp.argsort(flat_e)
    e_sorted = flat_e[order]
    tok_sorted = flat_tok[order]
    w_sorted = flat_w[order]
    counts = jnp.bincount(flat_e, length=N_EXPERTS)
    padded = (counts + MOE_BLOCK - 1) // MOE_BLOCK * MOE_BLOCK
    pad_end = jnp.cumsum(padded)
    pad_start = pad_end - padded
    start = jnp.cumsum(counts) - counts
    slot = pad_start[e_sorted] + jnp.arange(n_asg) - start[e_sorted]
    n_blocks = -(-n_asg // MOE_BLOCK) + N_EXPERTS
    slot_tok = jnp.full((n_blocks * MOE_BLOCK,), n_tok, jnp.int32).at[slot].set(tok_sorted)
    block_expert = jnp.minimum(
        jnp.searchsorted(pad_end, jnp.arange(n_blocks) * MOE_BLOCK, side='right'), N_EXPERTS - 1)
    h_ext = jnp.concatenate([h, jnp.zeros((1, h.shape[1]), h.dtype)], axis=0)
    xb = h_ext[slot_tok].reshape(n_blocks, MOE_BLOCK, h.shape[1])

    def expert_block(args):
        xe, e = args
        a = xe @ w_gate_e[e]
        u = xe @ w_up_e[e]
        return (jax.nn.silu(a) * u) @ w_down_e[e]

    yb = lax.map(expert_block, (xb, block_expert)).reshape(-1, h.shape[1])
    y_sorted = yb[slot] * w_sorted[:, None].astype(h.dtype)
    return jnp.zeros_like(h).at[tok_sorted].add(y_sorted)


def channel_mixer(x, norm2_g, w_coarse, b_coarse, w_fine, b_fine, w_gate_e, w_up_e, w_down_e):
    lead = x.shape[:-1]
    h = rmsnorm(x, norm2_g).reshape(-1, D_MODEL)
    coarse = (h @ w_coarse).astype(jnp.float32) + b_coarse
    coarse_p = jax.nn.softmax(coarse, axis=-1)
    grp = jnp.argmax(coarse, axis=-1)
    p_grp = jnp.take_along_axis(coarse_p, grp[:, None], axis=-1)
    fine = jnp.einsum('nd,gde->nge', h, w_fine).astype(jnp.float32) + b_fine
    fine_sel = jnp.take_along_axis(fine, grp[:, None, None], axis=1)[:, 0]
    top_v, top_i = lax.top_k(fine_sel, TOP_K_FINE)
    gate = jax.nn.softmax(top_v, axis=-1) * p_grp
    expert_idx = grp[:, None] * EXPERTS_PER_GROUP + top_i
    y = routed_experts(h, expert_idx, gate, w_gate_e, w_up_e, w_down_e)
    return x + y.reshape(lead + (D_MODEL,))


def decoder_layer(x, p, conv_prefix, kv_cache, lw):
    (norm1_g, w_in, q_gain, k_gain, conv_w, w_conv_out, w_attn_out, w_o, norm2_g, w_coarse,
     b_coarse, w_fine, b_fine, w_gate_e, w_up_e, w_down_e, norm3_g, w_ple_gate, w_ple) = lw
    T = x.shape[1]
    h = rmsnorm(x, norm1_g)
    b_g, c_g, h_c, q, k, v, g_c, g_a = jnp.split(h @ w_in, SPLIT_POINTS, axis=-1)
    heads = x.shape[:-1] + (N_HEADS, HEAD_DIM)
    q = rmsnorm(q.reshape(heads), q_gain)
    k = rmsnorm(k.reshape(heads), k_gain)
    v = v.reshape(heads)

    u_ext = jnp.concatenate([conv_prefix.astype(x.dtype), c_g * h_c], axis=1)
    conv_y = sum(conv_w[j] * u_ext[:, j:j + T] for j in range(CONV_WIDTH))
    y_conv = (b_g * conv_y) @ w_conv_out

    new_state, outs, lses = [], [], []
    for g in range(N_ATT_GROUPS):
        win, dil = WINDOWS[g], DILATIONS[g]
        hs = slice(g * HEADS_PER_GROUP, (g + 1) * HEADS_PER_GROUP)
        qg, kg, vg = q[:, :, hs], k[:, :, hs], v[:, :, hs]
        if kv_cache is None:
            o, l = dilated_attention_prompt(qg, kg, vg, win, dil)
            keep = min(win, T)
            new_state += [kg[:, T - keep:], vg[:, T - keep:]]
        else:
            ck, cv = kv_cache[g]
            past = ck.shape[1]
            k_all = jnp.concatenate([ck.astype(kg.dtype), kg], axis=1)
            v_all = jnp.concatenate([cv.astype(vg.dtype), vg], axis=1)
            o, l = dilated_attention_sample(qg, k_all, v_all, past, win, dil)
            new_state += [k_all[:, -past:], v_all[:, -past:]]
        outs.append(o)
        lses.append(l)
    new_state.append(u_ext[:, -(CONV_WIDTH - 1):])
    wts = jax.nn.softmax(jnp.stack(lses), axis=0)
    o = jnp.einsum('gbth,gbthe->bthe', wts.astype(x.dtype), jnp.stack(outs))
    y_att = o.reshape(x.shape[:-1] + (ATT_OUT_WIDTH,)) @ w_attn_out

    x = x + (jax.nn.sigmoid(g_c) * y_conv + jax.nn.sigmoid(g_a) * y_att) @ w_o
    x = channel_mixer(x, norm2_g, w_coarse, b_coarse, w_fine, b_fine, w_gate_e, w_up_e, w_down_e)
    x = x + jax.nn.sigmoid(rmsnorm(x, norm3_g) @ w_ple_gate) * (p @ w_ple)
    return x, new_state


def setup_inputs(seed: int = 0) -> dict:
    key = jax.random.key(seed)
    ks = list(jax.random.split(key, 40))

    def nrm(i, shape, scale):
        return jax.random.normal(ks[i], shape, jnp.float32) * scale

    lens = [min(w, PAST_LEN) for w in WINDOWS]
    kvs = (DEPTH, DEC_BATCH)
    return {
        "x_prompt": nrm(0, (BATCH, SEQ, D_MODEL), 1.0),
        "x_sample": nrm(1, (DEC_BATCH, DEC_SEQ, D_MODEL), 1.0),
        "p_prompt": nrm(2, (DEPTH, BATCH, SEQ, PLE_DIM), 1.0),
        "p_sample": nrm(3, (DEPTH, DEC_BATCH, DEC_SEQ, PLE_DIM), 1.0),
        "cache_k_g0": nrm(4, kvs + (lens[0], HEADS_PER_GROUP, HEAD_DIM), 1.0),
        "cache_v_g0": nrm(5, kvs + (lens[0], HEADS_PER_GROUP, HEAD_DIM), 1.0),
        "cache_k_g1": nrm(6, kvs + (lens[1], HEADS_PER_GROUP, HEAD_DIM), 1.0),
        "cache_v_g1": nrm(7, kvs + (lens[1], HEADS_PER_GROUP, HEAD_DIM), 1.0),
        "cache_k_g2": nrm(8, kvs + (lens[2], HEADS_PER_GROUP, HEAD_DIM), 1.0),
        "cache_v_g2": nrm(9, kvs + (lens[2], HEADS_PER_GROUP, HEAD_DIM), 1.0),
        "state_conv": nrm(10, kvs + (CONV_WIDTH - 1, D_CONV), 1.0),
        "norm1_g": 1.0 + nrm(11, (DEPTH, D_MODEL), 0.05),
        "w_in": nrm(12, (DEPTH, D_MODEL, IN_COLS), D_MODEL ** -0.5),
        "q_gain": 1.0 + nrm(13, (DEPTH, N_HEADS, HEAD_DIM), 0.05),
        "k_gain": 1.0 + nrm(14, (DEPTH, N_HEADS, HEAD_DIM), 0.05),
        "conv_w": nrm(15, (DEPTH, CONV_WIDTH, D_CONV), CONV_WIDTH ** -0.5),
        "w_conv_out": nrm(16, (DEPTH, D_CONV, D_MODEL), D_CONV ** -0.5),
        "w_attn_out": nrm(17, (DEPTH, ATT_OUT_WIDTH, D_MODEL), ATT_OUT_WIDTH ** -0.5),
        "w_o": nrm(18, (DEPTH, D_MODEL, D_MODEL), D_MODEL ** -0.5),
        "norm2_g": 1.0 + nrm(19, (DEPTH, D_MODEL), 0.05),
        "w_coarse": nrm(20, (DEPTH, D_MODEL, N_EXPERT_GROUPS), D_MODEL ** -0.5),
        "b_coarse": nrm(21, (DEPTH, N_EXPERT_GROUPS), 0.01),
        "w_fine": nrm(22, (DEPTH, N_EXPERT_GROUPS, D_MODEL, EXPERTS_PER_GROUP), D_MODEL ** -0.5),
        "b_fine": nrm(23, (DEPTH, N_EXPERT_GROUPS, EXPERTS_PER_GROUP), 0.01),
        "w_gate_e": nrm(24, (DEPTH, N_EXPERTS, D_MODEL, D_EXPERT), D_MODEL ** -0.5),
        "w_up_e": nrm(25, (DEPTH, N_EXPERTS, D_MODEL, D_EXPERT), D_MODEL ** -0.5),
        "w_down_e": nrm(26, (DEPTH, N_EXPERTS, D_EXPERT, D_MODEL), D_EXPERT ** -0.5),
        "norm3_g": 1.0 + nrm(27, (DEPTH, D_MODEL), 0.05),
        "w_ple_gate": nrm(28, (DEPTH, D_MODEL, D_MODEL), D_MODEL ** -0.5),
        "w_ple": nrm(29, (DEPTH, PLE_DIM, D_MODEL), PLE_DIM ** -0.5),
    }


def reference(x_prompt, x_sample, p_prompt, p_sample, cache_k_g0, cache_v_g0, cache_k_g1,
              cache_v_g1, cache_k_g2, cache_v_g2, state_conv, norm1_g, w_in, q_gain, k_gain,
              conv_w, w_conv_out, w_attn_out, w_o, norm2_g, w_coarse, b_coarse, w_fine, b_fine,
              w_gate_e, w_up_e, w_down_e, norm3_g, w_ple_gate, w_ple):
    caches_k = (cache_k_g0, cache_k_g1, cache_k_g2)
    caches_v = (cache_v_g0, cache_v_g1, cache_v_g2)
    xp, xs = x_prompt, x_sample
    prompt_states, sample_states = [], []
    for i in range(DEPTH):
        lw = (norm1_g[i], w_in[i], q_gain[i], k_gain[i], conv_w[i], w_conv_out[i], w_attn_out[i],
              w_o[i], norm2_g[i], w_coarse[i], b_coarse[i], w_fine[i], b_fine[i], w_gate_e[i],
              w_up_e[i], w_down_e[i], norm3_g[i], w_ple_gate[i], w_ple[i])
        conv0 = jnp.zeros((xp.shape[0], CONV_WIDTH - 1, D_CONV), xp.dtype)
        xp, st_p = decoder_layer(xp, p_prompt[i], conv0, None, lw)
        kv_i = [(caches_k[g][i], caches_v[g][i]) for g in range(N_ATT_GROUPS)]
        xs, st_s = decoder_layer(xs, p_sample[i], state_conv[i], kv_i, lw)
        prompt_states.append(st_p)
        sample_states.append(st_s)
    pk0, pv0, pk1, pv1, pk2, pv2, pconv = [jnp.stack(s) for s in zip(*prompt_states)]
    sk0, sv0, sk1, sv1, sk2, sv2, sconv = [jnp.stack(s) for s in zip(*sample_states)]
    return (xp, xs, pk0, pv0, pk1, pv1, pk2, pv2, pconv, sk0, sv0, sk1, sv1, sk2, sv2, sconv)
```

```python
import functools

import jax
import jax.numpy as jnp
from jax import lax
from jax.experimental import pallas as pl
from jax.experimental.pallas import tpu as pltpu

D_MODEL = 2048
HEAD_DIM = 128
HEADS_PER_GROUP = 4
WINDOWS = (128, 512, 2048)
DILATIONS = (1, 4, 16)
N_GROUPS = 3
N_HEADS = N_GROUPS * HEADS_PER_GROUP
ATT_WIDTH = N_HEADS * HEAD_DIM
ATT_OUT_WIDTH = HEADS_PER_GROUP * HEAD_DIM
SCALE = HEAD_DIM ** -0.5
D_CONV = D_MODEL // 2
PLE_DIM = 256
N_EXPERT_GROUPS = 4
EXPERTS_PER_GROUP = 8
N_EXPERTS = N_EXPERT_GROUPS * EXPERTS_PER_GROUP
TOP_K = 2
D_EXPERT = D_MODEL // 4
EPS = 1e-6

COL_B = 0
COL_C = D_CONV
COL_H = 2 * D_CONV
COL_Q = 3 * D_CONV
COL_K = COL_Q + ATT_WIDTH
COL_V = COL_K + ATT_WIDTH
COL_GC = COL_V + ATT_WIDTH
COL_GA = COL_GC + D_MODEL
IN_COLS = COL_GA + D_MODEL

LANE = 128
SUBLANE = 8
NEG = -1e30
ATT_UNIT = max(WINDOWS)
ATT_BLK = 128
MOE_TM = 256
VMEM_LIMIT = 56 * 1024 * 1024

BF16 = jnp.bfloat16
F32 = jnp.float32


def _params(sem):
    return pltpu.CompilerParams(dimension_semantics=sem, vmem_limit_bytes=VMEM_LIMIT)


def _rms(x, gain):
    return x * lax.rsqrt(jnp.mean(x * x, axis=-1, keepdims=True) + EPS) * gain


def _dot(a, b):
    return jnp.dot(a, b, preferred_element_type=F32)


def _dot_nt(a, b):
    return lax.dot_general(a, b, (((1,), (1,)), ((), ())), preferred_element_type=F32)


def _inproj_kernel(x_ref, g_ref, w_ref, z_ref, h_scr):
    @pl.when(pl.program_id(1) == 0)
    def _():
        h_scr[...] = _rms(x_ref[...], g_ref[...]).astype(BF16)

    z_ref[...] = _dot(h_scr[...], w_ref[...].astype(BF16))


def _inproj(x, norm_g, w_in, tm, tn):
    m = x.shape[0]
    return pl.pallas_call(
        _inproj_kernel,
        grid=(m // tm, IN_COLS // tn),
        in_specs=[
            pl.BlockSpec((tm, D_MODEL), lambda i, j: (i, 0)),
            pl.BlockSpec((1, D_MODEL), lambda i, j: (0, 0)),
            pl.BlockSpec((D_MODEL, tn), lambda i, j: (0, j)),
        ],
        out_specs=pl.BlockSpec((tm, tn), lambda i, j: (i, j)),
        out_shape=jax.ShapeDtypeStruct((m, IN_COLS), F32),
        scratch_shapes=[pltpu.VMEM((tm, D_MODEL), BF16)],
        compiler_params=_params(("parallel", "arbitrary")),
    )(x, norm_g.reshape(1, D_MODEL), w_in)


def _prompt_attn_kernel(*refs):
    (q0, q1, q2, kc0, kc1, kc2, vc0, vc1, vc2, kp0, kp1, kp2, vp0, vp1, vp2, qg_ref, kg_ref,
     o_ref, pk0, pk1, pk2, pv0, pv1, pv2, kext, vext, acc_s, m_s, l_s) = refs
    q_refs, kc_refs, vc_refs = (q0, q1, q2), (kc0, kc1, kc2), (vc0, vc1, vc2)
    kp_refs, vp_refs = (kp0, kp1, kp2), (vp0, vp1, vp2)
    pk_refs, pv_refs = (pk0, pk1, pk2), (pv0, pv1, pv2)
    n = pl.program_id(1)
    last = pl.num_programs(1) - 1

    qi = lax.broadcasted_iota(jnp.int32, (ATT_BLK, 2 * ATT_BLK), 0) + ATT_BLK
    ki = lax.broadcasted_iota(jnp.int32, (ATT_BLK, 2 * ATT_BLK), 1)
    dist = qi - ki
    band = (dist >= 0) & (dist <= ATT_BLK)

    for g in range(N_GROUPS):
        win, dil = WINDOWS[g], DILATIONS[g]
        q_ref = q_refs[g]
        qgain = qg_ref[g, 0]
        kgain = kg_ref[g, 0]
        kext[0:win, :] = _rms(kp_refs[g][...], kgain)
        kext[win:win + ATT_UNIT, :] = _rms(kc_refs[g][...], kgain)
        vext[0:win, :] = vp_refs[g][...]
        vext[win:win + ATT_UNIT, :] = vc_refs[g][...]

        @pl.when(n == last)
        def _():
            pk_refs[g][...] = kext[ATT_UNIT:ATT_UNIT + win, :]
            pv_refs[g][...] = vext[ATT_UNIT:ATT_UNIT + win, :]

        def tile(t, carry, g=g, win=win, dil=dil, q_ref=q_ref, qgain=qgain):
            u = t // dil
            r = t % dil
            off = u * win + r
            if dil == 1:
                off = pl.multiple_of(off, ATT_BLK)
                rows_q = pl.ds(off, ATT_BLK)
                rows_k = pl.ds(off, 2 * ATT_BLK)
            else:
                rows_q = pl.ds(off, ATT_BLK, stride=dil)
                rows_k = pl.ds(off, 2 * ATT_BLK, stride=dil)
            q = _rms(q_ref[rows_q, :], qgain).astype(BF16)
            k = kext[rows_k, :].astype(BF16)
            v = vext[rows_k, :].astype(BF16)
            s = _dot_nt(q, k) * SCALE
            k_min = jnp.where((n > 0) | (u > 0), 0, ATT_BLK)
            s = jnp.where(band & (ki >= k_min), s, NEG)
            m_t = jnp.max(s, axis=-1, keepdims=True)
            if g == 0:
                m_new = jnp.broadcast_to(m_t, (ATT_BLK, LANE))
                p = jnp.exp(s - m_t)
                l_new = jnp.broadcast_to(jnp.sum(p, axis=-1, keepdims=True), (ATT_BLK, LANE))
                acc_new = _dot(p.astype(BF16), v)
            else:
                m_old = m_s[rows_q, :]
                m_new = jnp.maximum(m_old, m_t)
                alpha = jnp.exp(m_old - m_new)
                p = jnp.exp(s - jnp.concatenate([m_new, m_new], axis=1))
                l_new = alpha * l_s[rows_q, :] + jnp.sum(p, axis=-1, keepdims=True)
                acc_new = alpha * acc_s[rows_q, :] + _dot(p.astype(BF16), v)
            m_s[rows_q, :] = m_new
            l_s[rows_q, :] = l_new
            acc_s[rows_q, :] = acc_new
            return carry

        lax.fori_loop(0, ATT_UNIT // ATT_BLK, tile, 0)

    def finish(c, carry):
        rows = pl.ds(pl.multiple_of(c * ATT_BLK, ATT_BLK), ATT_BLK)
        o_ref[rows, :] = acc_s[rows, :] / l_s[rows, :]
        return carry

    lax.fori_loop(0, ATT_UNIT // ATT_BLK, finish, 0)


def _prompt_attn(z, q_gain, k_gain):
    s_len = z.shape[0]
    n_units = s_len // ATT_UNIT
    in_specs, args = [], []

    def col(base, g):
        return lambda hh, n: (n, base // LANE + g * HEADS_PER_GROUP + hh)

    for g in range(N_GROUPS):
        in_specs.append(pl.BlockSpec((ATT_UNIT, LANE), col(COL_Q, g)))
    for base in (COL_K, COL_V):
        for g in range(N_GROUPS):
            in_specs.append(pl.BlockSpec((ATT_UNIT, LANE), col(base, g)))
    for base in (COL_K, COL_V):
        for g in range(N_GROUPS):
            per = ATT_UNIT // WINDOWS[g]
            in_specs.append(pl.BlockSpec(
                (WINDOWS[g], LANE),
                lambda hh, n, base=base, g=g, per=per: (
                    jnp.maximum(n * per - 1, 0), base // LANE + g * HEADS_PER_GROUP + hh)))
    args = [z] * 15
    gain_spec = pl.BlockSpec((N_GROUPS, 1, 1, LANE), lambda hh, n: (0, hh, 0, 0))
    in_specs += [gain_spec, gain_spec]
    args += [q_gain.reshape(N_GROUPS, HEADS_PER_GROUP, 1, LANE),
             k_gain.reshape(N_GROUPS, HEADS_PER_GROUP, 1, LANE)]

    out_specs = [pl.BlockSpec((ATT_UNIT, LANE), lambda hh, n: (n, hh))]
    out_shape = [jax.ShapeDtypeStruct((s_len, ATT_OUT_WIDTH), F32)]
    for _ in range(2):
        for g in range(N_GROUPS):
            out_specs.append(pl.BlockSpec((WINDOWS[g], LANE), lambda hh, n: (0, hh)))
            out_shape.append(jax.ShapeDtypeStruct((WINDOWS[g], ATT_OUT_WIDTH), F32))

    return pl.pallas_call(
        _prompt_attn_kernel,
        grid=(HEADS_PER_GROUP, n_units),
        in_specs=in_specs,
        out_specs=out_specs,
        out_shape=out_shape,
        scratch_shapes=[
            pltpu.VMEM((2 * ATT_UNIT, LANE), F32),
            pltpu.VMEM((2 * ATT_UNIT, LANE), F32),
            pltpu.VMEM((ATT_UNIT, LANE), F32),
            pltpu.VMEM((ATT_UNIT, LANE), F32),
            pltpu.VMEM((ATT_UNIT, LANE), F32),
        ],
        compiler_params=_params(("parallel", "arbitrary")),
    )(*args)


def _sample_attn_kernel(*refs):
    (q0, q1, q2, kn0, kn1, kn2, vn0, vn1, vn2, ck0, ck1, ck2, cv0, cv1, cv2, qg_ref, kg_ref,
     o_ref, sk0, sk1, sk2, sv0, sv1, sv2) = refs
    q_refs, kn_refs, vn_refs = (q0, q1, q2), (kn0, kn1, kn2), (vn0, vn1, vn2)
    ck_refs, cv_refs = (ck0, ck1, ck2), (cv0, cv1, cv2)
    sk_refs, sv_refs = (sk0, sk1, sk2), (sv0, sv1, sv2)
    t_new = q0.shape[0]

    parts = []
    for g in range(N_GROUPS):
        win, dil = WINDOWS[g], DILATIONS[g]
        q = _rms(q_refs[g][...], qg_ref[g, 0]).astype(BF16)
        k_new = _rms(kn_refs[g][...], kg_ref[g, 0])
        v_new = vn_refs[g][...]
        ck = ck_refs[g][0]
        cv = cv_refs[g][0]
        sk_refs[g][0, 0:win - t_new, :] = ck[t_new:win, :]
        sk_refs[g][0, win - t_new:win, :] = k_new
        sv_refs[g][0, 0:win - t_new, :] = cv[t_new:win, :]
        sv_refs[g][0, win - t_new:win, :] = v_new

        s_c = _dot_nt(q, ck.astype(BF16)) * SCALE
        s_n = _dot_nt(q, k_new.astype(BF16)) * SCALE
        ic = lax.broadcasted_iota(jnp.int32, (t_new, win), 0)
        cc = lax.broadcasted_iota(jnp.int32, (t_new, win), 1)
        mask_c = (cc >= ic) & (((cc - ic) & (dil - 1)) == 0)
        i_n = lax.broadcasted_iota(jnp.int32, (t_new, t_new), 0)
        c_n = lax.broadcasted_iota(jnp.int32, (t_new, t_new), 1)
        mask_n = (c_n <= i_n) & (((i_n - c_n) & (dil - 1)) == 0)
        s_c = jnp.where(mask_c, s_c, NEG)
        s_n = jnp.where(mask_n, s_n, NEG)
        m = jnp.maximum(jnp.max(s_c, axis=-1, keepdims=True), jnp.max(s_n, axis=-1, keepdims=True))
        p_c = jnp.exp(s_c - m)
        p_n = jnp.exp(s_n - m)
        l = jnp.sum(p_c, axis=-1, keepdims=True) + jnp.sum(p_n, axis=-1, keepdims=True)
        acc = _dot(p_c.astype(BF16), cv.astype(BF16)) + _dot(p_n.astype(BF16), v_new.astype(BF16))
        parts.append((m, l, acc))

    m_all = jnp.maximum(jnp.maximum(parts[0][0], parts[1][0]), parts[2][0])
    num = jnp.zeros((t_new, LANE), F32)
    den = jnp.zeros((t_new, 1), F32)
    for m, l, acc in parts:
        w = jnp.exp(m - m_all)
        num = num + w * acc
        den = den + w * l
    o_ref[...] = num / den


def _sample_attn(z, caches_k, caches_v, q_gain, k_gain, n_seq, t_new):
    def col(base, g):
        return lambda b, hh: (b, base // LANE + g * HEADS_PER_GROUP + hh)

    in_specs, args = [], []
    for base in (COL_Q, COL_K, COL_V):
        for g in range(N_GROUPS):
            in_specs.append(pl.BlockSpec((t_new, LANE), col(base, g)))
            args.append(z)
    for caches in (caches_k, caches_v):
        for g in range(N_GROUPS):
            in_specs.append(pl.BlockSpec((1, WINDOWS[g], LANE), lambda b, hh: (b, 0, hh)))
            args.append(caches[g])
    gain_spec = pl.BlockSpec((N_GROUPS, 1, 1, LANE), lambda b, hh: (0, hh, 0, 0))
    in_specs += [gain_spec, gain_spec]
    args += [q_gain.reshape(N_GROUPS, HEADS_PER_GROUP, 1, LANE),
             k_gain.reshape(N_GROUPS, HEADS_PER_GROUP, 1, LANE)]

    out_specs = [pl.BlockSpec((t_new, LANE), lambda b, hh: (b, hh))]
    out_shape = [jax.ShapeDtypeStruct((n_seq * t_new, ATT_OUT_WIDTH), F32)]
    for _ in range(2):
        for g in range(N_GROUPS):
            out_specs.append(pl.BlockSpec((1, WINDOWS[g], LANE), lambda b, hh: (b, 0, hh)))
            out_shape.append(jax.ShapeDtypeStruct((n_seq, WINDOWS[g], ATT_OUT_WIDTH), F32))

    return pl.pallas_call(
        _sample_attn_kernel,
        grid=(n_seq, HEADS_PER_GROUP),
        in_specs=in_specs,
        out_specs=out_specs,
        out_shape=out_shape,
        compiler_params=_params(("parallel", "arbitrary")),
    )(*args)


def _conv_taps(u, r1, r2, cw_ref):
    return cw_ref[2:3, :] * u + cw_ref[1:2, :] * r1 + cw_ref[0:1, :] * r2


def _merge_kernel(b_ref, c_ref, h_ref, pc_ref, ph_ref, cw_ref, o_ref, gc_ref, ga_ref, wc_ref,
                  wa_ref, out_ref, u_ref, yb_scr, *, per_seq):
    i = pl.program_id(0)

    @pl.when(pl.program_id(1) == 0)
    def _():
        tm = b_ref.shape[0]
        u = c_ref[...] * h_ref[...]
        r1 = pltpu.roll(u, 1, axis=0)
        r2 = pltpu.roll(u, 2, axis=0)
        if per_seq is None:
            yb_scr[...] = b_ref[...] * _conv_taps(u, r1, r2, cw_ref)
            up = jnp.where(i > 0, pc_ref[...] * ph_ref[...], 0.0)
            row = lax.broadcasted_iota(jnp.int32, (SUBLANE, D_CONV), 0)
            u8 = u[0:SUBLANE, :]
            r1_8 = jnp.where(row < 1, pltpu.roll(up, 1, axis=0), pltpu.roll(u8, 1, axis=0))
            r2_8 = jnp.where(row < 2, pltpu.roll(up, 2, axis=0), pltpu.roll(u8, 2, axis=0))
            yb_scr[0:SUBLANE, :] = b_ref[0:SUBLANE, :] * _conv_taps(u8, r1_8, r2_8, cw_ref)
            u_ref[...] = u[tm - SUBLANE:tm, :]
        else:
            pre = pc_ref[...]
            t = lax.broadcasted_iota(jnp.int32, (tm, D_CONV), 0) & (per_seq - 1)
            r1 = jnp.where(t == 0, pltpu.roll(pre, tm - 1, axis=0), r1)
            r2 = jnp.where(t < 2, pre, r2)
            yb_scr[...] = b_ref[...] * _conv_taps(u, r1, r2, cw_ref)
            u_ref[...] = u

    y_conv = _dot(yb_scr[...].astype(BF16), wc_ref[...])
    y_att = _dot(o_ref[...].astype(BF16), wa_ref[...])
    out_ref[...] = (jax.nn.sigmoid(gc_ref[...]) * y_conv
                    + jax.nn.sigmoid(ga_ref[...]) * y_att).astype(BF16)


def _merge(z, o_att, conv_w, wc, wa, tm, tn, per_seq=None, prefix=None):
    m = z.shape[0]
    if per_seq is None:
        prev = lambda i, j, c: (jnp.maximum(i * (tm // SUBLANE) - 1, 0), c)
        pc_spec = pl.BlockSpec((SUBLANE, D_CONV), lambda i, j: prev(i, j, COL_C // D_CONV))
        ph_spec = pl.BlockSpec((SUBLANE, D_CONV), lambda i, j: prev(i, j, COL_H // D_CONV))
        pc_arg, ph_arg = z, z
        u_rows = SUBLANE
    else:
        pc_spec = pl.BlockSpec((tm, D_CONV), lambda i, j: (i, 0))
        ph_spec = pl.BlockSpec((SUBLANE, D_CONV), lambda i, j: (0, 0))
        pc_arg, ph_arg = prefix, prefix
        u_rows = tm
    return pl.pallas_call(
        functools.partial(_merge_kernel, per_seq=per_seq),
        grid=(m // tm, D_MODEL // tn),
        in_specs=[
            pl.BlockSpec((tm, D_CONV), lambda i, j: (i, COL_B // D_CONV)),
            pl.BlockSpec((tm, D_CONV), lambda i, j: (i, COL_C // D_CONV)),
            pl.BlockSpec((tm, D_CONV), lambda i, j: (i, COL_H // D_CONV)),
            pc_spec,
            ph_spec,
            pl.BlockSpec((3, D_CONV), lambda i, j: (0, 0)),
            pl.BlockSpec((tm, ATT_OUT_WIDTH), lambda i, j: (i, 0)),
            pl.BlockSpec((tm, tn), lambda i, j: (i, COL_GC // tn + j)),
            pl.BlockSpec((tm, tn), lambda i, j: (i, COL_GA // tn + j)),
            pl.BlockSpec((D_CONV, tn), lambda i, j: (0, j)),
            pl.BlockSpec((ATT_OUT_WIDTH, tn), lambda i, j: (0, j)),
        ],
        out_specs=[
            pl.BlockSpec((tm, tn), lambda i, j: (i, j)),
            pl.BlockSpec((u_rows, D_CONV), lambda i, j: (i, 0)),
        ],
        out_shape=[
            jax.ShapeDtypeStruct((m, D_MODEL), BF16),
            jax.ShapeDtypeStruct((m // tm * u_rows, D_CONV), F32),
        ],
        scratch_shapes=[pltpu.VMEM((tm, D_CONV), F32)],
        compiler_params=_params(("parallel", "arbitrary")),
    )(z, z, z, pc_arg, ph_arg, conv_w, o_att, z, z, wc, wa)


def _outproj_router_kernel(xp_ref, mgp_ref, xs_ref, mgs_ref, wo_ref, g2_ref, wr_hi_ref, wr_lo_ref,
                           br_ref, x1_ref, route_ref):
    is_sample = pl.program_id(0) == pl.num_programs(0) - 1

    @pl.when(jnp.logical_not(is_sample))
    def _():
        _outproj_router_tile(xp_ref, mgp_ref, wo_ref, g2_ref, wr_hi_ref, wr_lo_ref, br_ref,
                             x1_ref, route_ref)

    @pl.when(is_sample)
    def _():
        _outproj_router_tile(xs_ref, mgs_ref, wo_ref, g2_ref, wr_hi_ref, wr_lo_ref, br_ref,
                             x1_ref, route_ref)


def _outproj_router_tile(x_ref, mg_ref, wo_ref, g2_ref, wr_hi_ref, wr_lo_ref, br_ref, x1_ref,
                         route_ref):
    x1 = x_ref[...] + _dot(mg_ref[...], wo_ref[...])
    x1_ref[...] = x1
    h = _rms(x1, g2_ref[...])
    h_hi = h.astype(BF16)
    h_lo = (h - h_hi.astype(F32)).astype(BF16)
    logits = (_dot(h_hi, wr_hi_ref[...]) + _dot(h_hi, wr_lo_ref[...]) + _dot(h_lo, wr_hi_ref[...])
              + br_ref[...])
    lane = lax.broadcasted_iota(jnp.int32, logits.shape, 1)
    is_coarse = lane < N_EXPERT_GROUPS
    coarse = jnp.where(is_coarse, logits, NEG)
    cmax = jnp.max(coarse, axis=-1, keepdims=True)
    grp = jnp.min(jnp.where(coarse == cmax, lane, LANE), axis=-1, keepdims=True)
    p_grp = 1.0 / jnp.sum(jnp.where(is_coarse, jnp.exp(coarse - cmax), 0.0), axis=-1, keepdims=True)
    eid = lane - N_EXPERT_GROUPS
    in_grp = (eid >= 0) & (eid < N_EXPERTS) & ((eid >> 3) == grp)
    fine = jnp.where(in_grp, logits, NEG)
    v1 = jnp.max(fine, axis=-1, keepdims=True)
    i1 = jnp.min(jnp.where(fine == v1, lane, LANE), axis=-1, keepdims=True)
    fine2 = jnp.where(lane == i1, NEG, fine)
    v2 = jnp.max(fine2, axis=-1, keepdims=True)
    i2 = jnp.min(jnp.where(fine2 == v2, lane, LANE), axis=-1, keepdims=True)
    e = jnp.exp(v2 - v1)
    gate1 = p_grp / (1.0 + e)
    gate2 = p_grp * e / (1.0 + e)
    route = jnp.where(lane == 0, (i1 - N_EXPERT_GROUPS).astype(F32),
                      jnp.where(lane == 1, (i2 - N_EXPERT_GROUPS).astype(F32),
                                jnp.where(lane == 2, gate1, jnp.where(lane == 3, gate2, 0.0))))
    route_ref[...] = route


def _outproj_router(xp, mg_p, xs, mg_s, wo, norm2_g, wr_hi, wr_lo, br):
    m_p, tm = xp.shape[0], xs.shape[0]
    assert m_p % tm == 0
    n_p = m_p // tm
    prompt_blk = lambda i: (jnp.minimum(i, n_p - 1), 0)
    fixed = lambda i: (0, 0)
    return pl.pallas_call(
        _outproj_router_kernel,
        grid=(n_p + 1,),
        in_specs=[
            pl.BlockSpec((tm, D_MODEL), prompt_blk),
            pl.BlockSpec((tm, D_MODEL), prompt_blk),
            pl.BlockSpec((tm, D_MODEL), fixed),
            pl.BlockSpec((tm, D_MODEL), fixed),
            pl.BlockSpec((D_MODEL, D_MODEL), fixed),
            pl.BlockSpec((1, D_MODEL), fixed),
            pl.BlockSpec((D_MODEL, LANE), fixed),
            pl.BlockSpec((D_MODEL, LANE), fixed),
            pl.BlockSpec((1, LANE), fixed),
        ],
        out_specs=[
            pl.BlockSpec((tm, D_MODEL), lambda i: (i, 0)),
            pl.BlockSpec((tm, LANE), lambda i: (i, 0)),
        ],
        out_shape=[
            jax.ShapeDtypeStruct((m_p + tm, D_MODEL), F32),
            jax.ShapeDtypeStruct((m_p + tm, LANE), F32),
        ],
        compiler_params=_params(("arbitrary",)),
    )(xp, mg_p, xs, mg_s, wo, norm2_g.reshape(1, D_MODEL), wr_hi, wr_lo, br)


def _moe_kernel(blk_e_ref, nused_ref, tok_ref, x1_hbm, g2_ref, wg_ref, wu_ref, wd_ref, y_ref,
                xbuf, sem, wg_b, wu_b, wd_b):
    i = pl.program_id(0)
    n_used = nused_ref[0]
    slot = i & 1

    def start_gather(blk, to_slot):
        def body(r, carry):
            tok = tok_ref[blk * MOE_TM + r]
            pltpu.make_async_copy(x1_hbm.at[pl.ds(tok, 1), :], xbuf.at[to_slot, pl.ds(r, 1), :],
                                  sem.at[to_slot]).start()
            return carry
        lax.fori_loop(0, MOE_TM, body, 0)

    @pl.when(i == 0)
    def _():
        start_gather(0, 0)

    @pl.when(i + 1 < n_used)
    def _():
        start_gather(i + 1, 1 - slot)

    @pl.when(i < n_used)
    def _():
        new_expert = (i == 0) | (blk_e_ref[i] != blk_e_ref[jnp.maximum(i - 1, 0)])

        @pl.when(new_expert)
        def _():
            wg_b[...] = wg_ref[0].astype(BF16)
            wu_b[...] = wu_ref[0].astype(BF16)
            wd_b[...] = wd_ref[0].astype(BF16)

        pltpu.make_async_copy(x1_hbm.at[pl.ds(0, MOE_TM), :], xbuf.at[slot], sem.at[slot]).wait()
        h = _rms(xbuf[slot], g2_ref[...]).astype(BF16)
        a = _dot(h, wg_b[...])
        u = _dot(h, wu_b[...])
        hm = (a * jax.nn.sigmoid(a) * u).astype(BF16)
        y_ref[...] = _dot(hm, wd_b[...])

    @pl.when(i >= n_used)
    def _():
        y_ref[...] = jnp.zeros_like(y_ref)


def _moe(x1_all, norm2_g, w_gate_e, w_up_e, w_down_e, blk_expert, n_used, slot_tok, n_blocks):
    def w_idx(i, be, nu, tok):
        return (be[i], 0, 0)

    grid_spec = pltpu.PrefetchScalarGridSpec(
        num_scalar_prefetch=3,
        grid=(n_blocks,),
        in_specs=[
            pl.BlockSpec(memory_space=pl.ANY),
            pl.BlockSpec((1, D_MODEL), lambda i, be, nu, tok: (0, 0)),
            pl.BlockSpec((1, D_MODEL, D_EXPERT), w_idx),
            pl.BlockSpec((1, D_MODEL, D_EXPERT), w_idx),
            pl.BlockSpec((1, D_EXPERT, D_MODEL), w_idx),
        ],
        out_specs=pl.BlockSpec((MOE_TM, D_MODEL), lambda i, be, nu, tok: (i, 0)),
        scratch_shapes=[
            pltpu.VMEM((2, MOE_TM, D_MODEL), F32),
            pltpu.SemaphoreType.DMA((2,)),
            pltpu.VMEM((D_MODEL, D_EXPERT), BF16),
            pltpu.VMEM((D_MODEL, D_EXPERT), BF16),
            pltpu.VMEM((D_EXPERT, D_MODEL), BF16),
        ],
    )
    return pl.pallas_call(
        _moe_kernel,
        grid_spec=grid_spec,
        out_shape=jax.ShapeDtypeStruct((n_blocks * MOE_TM, D_MODEL), F32),
        compiler_params=_params(("arbitrary",)),
    )(blk_expert, n_used, slot_tok, x1_all, norm2_g.reshape(1, D_MODEL), w_gate_e, w_up_e, w_down_e)


def _combine_ple_kernel(slots_ref, x1_ref, route_ref, y_hbm, p_ref, g3_ref, wpg_ref, wple_ref,
                        out_ref, ybuf, sem, *, tm, row0):
    i = pl.program_id(0)
    n_steps = pl.num_programs(0)
    slot = i & 1

    def start_gather(step, to_slot):
        base = (row0 + step * tm) * TOP_K

        def body(r, carry):
            for k in range(TOP_K):
                src = slots_ref[base + r * TOP_K + k]
                pltpu.make_async_copy(y_hbm.at[pl.ds(src, 1), :],
                                      ybuf.at[to_slot, k, pl.ds(r, 1), :], sem.at[to_slot]).start()
            return carry
        lax.fori_loop(0, tm, body, 0)

    @pl.when(i == 0)
    def _():
        start_gather(0, 0)

    @pl.when(i + 1 < n_steps)
    def _():
        start_gather(i + 1, 1 - slot)

    for k in range(TOP_K):
        pltpu.make_async_copy(y_hbm.at[pl.ds(0, tm), :], ybuf.at[slot, k], sem.at[slot]).wait()
    route = route_ref[...]
    x2 = x1_ref[...] + (route[:, 2:3] * ybuf[slot, 0] + route[:, 3:4] * ybuf[slot, 1])
    h = _rms(x2, g3_ref[...]).astype(BF16)
    gate = jax.nn.sigmoid(_dot(h, wpg_ref[...]))
    out_ref[...] = x2 + gate * _dot(p_ref[...].astype(BF16), wple_ref[...])


def _combine_ple(x1_all, route_all, y_slots, tok_slots, p, norm3_g, wpg, wple, m, tm, row0):
    blk0 = row0 // tm
    grid_spec = pltpu.PrefetchScalarGridSpec(
        num_scalar_prefetch=1,
        grid=(m // tm,),
        in_specs=[
            pl.BlockSpec((tm, D_MODEL), lambda i, s: (blk0 + i, 0)),
            pl.BlockSpec((tm, LANE), lambda i, s: (blk0 + i, 0)),
            pl.BlockSpec(memory_space=pl.ANY),
            pl.BlockSpec((tm, PLE_DIM), lambda i, s: (i, 0)),
            pl.BlockSpec((1, D_MODEL), lambda i, s: (0, 0)),
            pl.BlockSpec((D_MODEL, D_MODEL), lambda i, s: (0, 0)),
            pl.BlockSpec((PLE_DIM, D_MODEL), lambda i, s: (0, 0)),
        ],
        out_specs=pl.BlockSpec((tm, D_MODEL), lambda i, s: (i, 0)),
        scratch_shapes=[
            pltpu.VMEM((2, TOP_K, tm, D_MODEL), F32),
            pltpu.SemaphoreType.DMA((2,)),
        ],
    )
    return pl.pallas_call(
        functools.partial(_combine_ple_kernel, tm=tm, row0=row0),
        grid_spec=grid_spec,
        out_shape=jax.ShapeDtypeStruct((m, D_MODEL), F32),
        compiler_params=_params(("arbitrary",)),
    )(tok_slots, x1_all, route_all, y_slots, p, norm3_g.reshape(1, D_MODEL), wpg, wple)


def _routing_tables(route_all, n_blocks):
    n_tok = route_all.shape[0]
    flat_e = route_all[:, 0:TOP_K].astype(jnp.int32).reshape(-1)
    onehot = (flat_e[:, None] == jnp.arange(N_EXPERTS, dtype=jnp.int32)[None, :]).astype(jnp.int32)
    csum = jnp.cumsum(onehot, axis=0)
    rank = jnp.take_along_axis(csum, flat_e[:, None], axis=1)[:, 0] - 1
    counts = csum[-1]
    nblk = (counts + MOE_TM - 1) // MOE_TM
    blk_end = jnp.cumsum(nblk)
    blk_start = blk_end - nblk
    slot = blk_start[flat_e] * MOE_TM + rank
    flat_tok = jnp.arange(n_tok * TOP_K, dtype=jnp.int32) // TOP_K
    slot_tok = jnp.zeros((n_blocks * MOE_TM,), jnp.int32).at[slot].set(flat_tok)
    n_used = blk_end[-1]
    blk_ids = jnp.minimum(jnp.arange(n_blocks, dtype=jnp.int32), n_used - 1)
    blk_expert = jnp.searchsorted(blk_end, blk_ids, side='right').astype(jnp.int32)
    return blk_expert, n_used.reshape(1).astype(jnp.int32), slot_tok, slot.astype(jnp.int32)


def kernel(x_prompt, x_sample, p_prompt, p_sample, cache_k_g0, cache_v_g0, cache_k_g1, cache_v_g1,
           cache_k_g2, cache_v_g2, state_conv, norm1_g, w_in, q_gain, k_gain, conv_w, w_conv_out,
           w_attn_out, w_o, norm2_g, w_coarse, b_coarse, w_fine, b_fine, w_gate_e, w_up_e, w_down_e,
           norm3_g, w_ple_gate, w_ple):
    seq = x_prompt.shape[1]
    assert x_prompt.shape == (1, seq, D_MODEL) and norm1_g.shape[0] == 1, "one prompt, one layer"
    n_seq, t_new, _ = x_sample.shape
    assert seq % ATT_UNIT == 0 and t_new == SUBLANE
    caches_k = (cache_k_g0, cache_k_g1, cache_k_g2)
    caches_v = (cache_v_g0, cache_v_g1, cache_v_g2)
    for g in range(N_GROUPS):
        assert caches_k[g].shape == (1, n_seq, WINDOWS[g], HEADS_PER_GROUP, HEAD_DIM)
    n_s = n_seq * t_new
    n_all = seq + n_s

    xp = x_prompt[0]
    xs = x_sample.reshape(n_s, D_MODEL)
    wc = w_conv_out[0].astype(BF16)
    wa = w_attn_out[0].astype(BF16)
    wo = w_o[0].astype(BF16)
    wpg = w_ple_gate[0].astype(BF16)
    wple = w_ple[0].astype(BF16)
    wr = jnp.concatenate(
        [w_coarse[0], jnp.transpose(w_fine[0], (1, 0, 2)).reshape(D_MODEL, N_EXPERTS)], axis=1)
    wr = jnp.pad(wr, ((0, 0), (0, LANE - wr.shape[1])))
    wr_hi = wr.astype(BF16)
    wr_lo = (wr - wr_hi.astype(F32)).astype(BF16)
    br = jnp.pad(jnp.concatenate([b_coarse[0], b_fine[0].reshape(-1)]), (0, LANE - 36)).reshape(1, LANE)

    z_p = _inproj(xp, norm1_g[0], w_in[0], tm=1024, tn=512)
    z_s = _inproj(xs, norm1_g[0], w_in[0], tm=n_s, tn=512)

    att_p = _prompt_attn(z_p, q_gain[0], k_gain[0])
    o_p, pks, pvs = att_p[0], att_p[1:4], att_p[4:7]
    ck = [c[0].reshape(n_seq, WINDOWS[g], ATT_OUT_WIDTH) for g, c in enumerate(caches_k)]
    cv = [c[0].reshape(n_seq, WINDOWS[g], ATT_OUT_WIDTH) for g, c in enumerate(caches_v)]
    att_s = _sample_attn(z_s, ck, cv, q_gain[0], k_gain[0], n_seq, t_new)
    o_s, sks, svs = att_s[0], att_s[1:4], att_s[4:7]

    mg_p, u_tail = _merge(z_p, o_p, conv_w[0], wc, wa, tm=512, tn=512)
    prefix = jnp.pad(state_conv[0], ((0, 0), (0, t_new - 2), (0, 0))).reshape(n_s, D_CONV)
    mg_s, u_s = _merge(z_s, o_s, conv_w[0], wc, wa, tm=n_s, tn=512, per_seq=t_new, prefix=prefix)
    x1_all, route_all = _outproj_router(xp, mg_p, xs, mg_s, wo, norm2_g[0], wr_hi, wr_lo, br)

    n_blocks = n_all * TOP_K // MOE_TM + N_EXPERTS
    blk_expert, n_used, slot_tok, tok_slots = _routing_tables(route_all, n_blocks)
    y_slots = _moe(x1_all, norm2_g[0], w_gate_e[0], w_up_e[0], w_down_e[0], blk_expert, n_used,
                   slot_tok, n_blocks)

    y_p = _combine_ple(x1_all, route_all, y_slots, tok_slots, p_prompt[0, 0], norm3_g[0], wpg, wple,
                       m=seq, tm=256, row0=0)
    y_s = _combine_ple(x1_all, route_all, y_slots, tok_slots, p_sample[0].reshape(n_s, PLE_DIM),
                       norm3_g[0], wpg, wple, m=n_s, tm=n_s, row0=seq)

    def state(a, lead):
        return a.reshape((1,) + lead + (a.shape[-2], HEADS_PER_GROUP, HEAD_DIM))

    pk = [state(a, (1,)) for a in pks]
    pv = [state(a, (1,)) for a in pvs]
    sk = [state(a, (n_seq,)) for a in sks]
    sv = [state(a, (n_seq,)) for a in svs]
    pconv = u_tail[-2:].reshape(1, 1, 2, D_CONV)
    sconv = u_s.reshape(n_seq, t_new, D_CONV)[:, t_new - 2:].reshape(1, n_seq, 2, D_CONV)
    return (y_p.reshape(1, seq, D_MODEL), y_s.reshape(n_seq, t_new, D_MODEL),
            pk[0], pv[0], pk[1], pv[1], pk[2], pv[2], pconv,
            sk[0], sv[0], sk[1], sv[1], sk[2], sv[2], sconv)
```

```python
import functools

import jax
import jax.numpy as jnp
from jax import lax
from jax.experimental import pallas as pl
from jax.experimental.pallas import tpu as pltpu

D_MODEL = 2048
HEAD_DIM = 128
HEADS_PER_GROUP = 4
WINDOWS = (128, 512, 2048)
DILATIONS = (1, 4, 16)
N_GROUPS = 3
N_HEADS = N_GROUPS * HEADS_PER_GROUP
ATT_WIDTH = N_HEADS * HEAD_DIM
ATT_OUT_WIDTH = HEADS_PER_GROUP * HEAD_DIM
SCALE = HEAD_DIM ** -0.5
D_CONV = D_MODEL // 2
PLE_DIM = 256
N_EXPERT_GROUPS = 4
EXPERTS_PER_GROUP = 8
N_EXPERTS = N_EXPERT_GROUPS * EXPERTS_PER_GROUP
TOP_K = 2
D_EXPERT = D_MODEL // 4
EPS = 1e-6

COL_B = 0
COL_C = D_CONV
COL_H = 2 * D_CONV
COL_Q = 3 * D_CONV
COL_K = COL_Q + ATT_WIDTH
COL_V = COL_K + ATT_WIDTH
COL_GC = COL_V + ATT_WIDTH
COL_GA = COL_GC + D_MODEL
IN_COLS = COL_GA + D_MODEL

LANE = 128
SUBLANE = 8
NEG = -1e30
ATT_UNIT = max(WINDOWS)
ATT_BLK = 128
MOE_TM = 256
VMEM_LIMIT = 56 * 1024 * 1024

BF16 = jnp.bfloat16
F32 = jnp.float32


def _params(sem):
    return pltpu.CompilerParams(dimension_semantics=sem, vmem_limit_bytes=VMEM_LIMIT)


def _rms(x, gain):
    return x * lax.rsqrt(jnp.mean(x * x, axis=-1, keepdims=True) + EPS) * gain


def _dot(a, b):
    return jnp.dot(a, b, preferred_element_type=F32)


def _dot_nt(a, b):
    return lax.dot_general(a, b, (((1,), (1,)), ((), ())), preferred_element_type=F32)


def _inproj_kernel(x_ref, g_ref, w_ref, z_ref, h_scr):
    @pl.when(pl.program_id(1) == 0)
    def _():
        h_scr[...] = _rms(x_ref[...], g_ref[...]).astype(BF16)

    z_ref[...] = _dot(h_scr[...], w_ref[...].astype(BF16))


def _inproj(x, norm_g, w_in, tm, tn):
    m = x.shape[0]
    return pl.pallas_call(
        _inproj_kernel,
        grid=(m // tm, IN_COLS // tn),
        in_specs=[
            pl.BlockSpec((tm, D_MODEL), lambda i, j: (i, 0)),
            pl.BlockSpec((1, D_MODEL), lambda i, j: (0, 0)),
            pl.BlockSpec((D_MODEL, tn), lambda i, j: (0, j)),
        ],
        out_specs=pl.BlockSpec((tm, tn), lambda i, j: (i, j)),
        out_shape=jax.ShapeDtypeStruct((m, IN_COLS), F32),
        scratch_shapes=[pltpu.VMEM((tm, D_MODEL), BF16)],
        compiler_params=_params(("parallel", "arbitrary")),
    )(x, norm_g.reshape(1, D_MODEL), w_in)


def _prompt_attn_kernel(*refs):
    (q0, q1, q2, kc0, kc1, kc2, vc0, vc1, vc2, kp0, kp1, kp2, vp0, vp1, vp2, qg_ref, kg_ref,
     o_ref, pk0, pk1, pk2, pv0, pv1, pv2, kext, vext, acc_s, m_s, l_s) = refs
    q_refs, kc_refs, vc_refs = (q0, q1, q2), (kc0, kc1, kc2), (vc0, vc1, vc2)
    kp_refs, vp_refs = (kp0, kp1, kp2), (vp0, vp1, vp2)
    pk_refs, pv_refs = (pk0, pk1, pk2), (pv0, pv1, pv2)
    n = pl.program_id(1)
    last = pl.num_programs(1) - 1

    qi = lax.broadcasted_iota(jnp.int32, (ATT_BLK, 2 * ATT_BLK), 0) + ATT_BLK
    ki = lax.broadcasted_iota(jnp.int32, (ATT_BLK, 2 * ATT_BLK), 1)
    dist = qi - ki
    band = (dist >= 0) & (dist <= ATT_BLK)

    for g in range(N_GROUPS):
        win, dil = WINDOWS[g], DILATIONS[g]
        q_ref = q_refs[g]
        qgain = qg_ref[g, 0]
        kgain = kg_ref[g, 0]
        kext[0:win, :] = _rms(kp_refs[g][...], kgain)
        kext[win:win + ATT_UNIT, :] = _rms(kc_refs[g][...], kgain)
        vext[0:win, :] = vp_refs[g][...]
        vext[win:win + ATT_UNIT, :] = vc_refs[g][...]

        @pl.when(n == last)
        def _():
            pk_refs[g][...] = kext[ATT_UNIT:ATT_UNIT + win, :]
            pv_refs[g][...] = vext[ATT_UNIT:ATT_UNIT + win, :]

        def tile(t, carry, g=g, win=win, dil=dil, q_ref=q_ref, qgain=qgain):
            u = t // dil
            r = t % dil
            off = u * win + r
            if dil == 1:
                off = pl.multiple_of(off, ATT_BLK)
                rows_q = pl.ds(off, ATT_BLK)
                rows_k = pl.ds(off, 2 * ATT_BLK)
            else:
                rows_q = pl.ds(off, ATT_BLK, stride=dil)
                rows_k = pl.ds(off, 2 * ATT_BLK, stride=dil)
            q = _rms(q_ref[rows_q, :], qgain).astype(BF16)
            k = kext[rows_k, :].astype(BF16)
            v = vext[rows_k, :].astype(BF16)
            s = _dot_nt(q, k) * SCALE
            k_min = jnp.where((n > 0) | (u > 0), 0, ATT_BLK)
            s = jnp.where(band & (ki >= k_min), s, NEG)
            m_t = jnp.max(s, axis=-1, keepdims=True)
            if g == 0:
                m_new = jnp.broadcast_to(m_t, (ATT_BLK, LANE))
                p = jnp.exp(s - m_t)
                l_new = jnp.broadcast_to(jnp.sum(p, axis=-1, keepdims=True), (ATT_BLK, LANE))
                acc_new = _dot(p.astype(BF16), v)
            else:
                m_old = m_s[rows_q, :]
                m_new = jnp.maximum(m_old, m_t)
                alpha = jnp.exp(m_old - m_new)
                p = jnp.exp(s - jnp.concatenate([m_new, m_new], axis=1))
                l_new = alpha * l_s[rows_q, :] + jnp.sum(p, axis=-1, keepdims=True)
                acc_new = alpha * acc_s[rows_q, :] + _dot(p.astype(BF16), v)
            m_s[rows_q, :] = m_new
            l_s[rows_q, :] = l_new
            acc_s[rows_q, :] = acc_new
            return carry

        lax.fori_loop(0, ATT_UNIT // ATT_BLK, tile, 0)

    def finish(c, carry):
        rows = pl.ds(pl.multiple_of(c * ATT_BLK, ATT_BLK), ATT_BLK)
        o_ref[rows, :] = acc_s[rows, :] / l_s[rows, :]
        return carry

    lax.fori_loop(0, ATT_UNIT // ATT_BLK, finish, 0)


def _prompt_attn(z, q_gain, k_gain):
    s_len = z.shape[0]
    n_units = s_len // ATT_UNIT
    in_specs, args = [], []

    def col(base, g):
        return lambda hh, n: (n, base // LANE + g * HEADS_PER_GROUP + hh)

    for g in range(N_GROUPS):
        in_specs.append(pl.BlockSpec((ATT_UNIT, LANE), col(COL_Q, g)))
    for base in (COL_K, COL_V):
        for g in range(N_GROUPS):
            in_specs.append(pl.BlockSpec((ATT_UNIT, LANE), col(base, g)))
    for base in (COL_K, COL_V):
        for g in range(N_GROUPS):
            per = ATT_UNIT // WINDOWS[g]
            in_specs.append(pl.BlockSpec(
                (WINDOWS[g], LANE),
                lambda hh, n, base=base, g=g, per=per: (
                    jnp.maximum(n * per - 1, 0), base // LANE + g * HEADS_PER_GROUP + hh)))
    args = [z] * 15
    gain_spec = pl.BlockSpec((N_GROUPS, 1, 1, LANE), lambda hh, n: (0, hh, 0, 0))
    in_specs += [gain_spec, gain_spec]
    args += [q_gain.reshape(N_GROUPS, HEADS_PER_GROUP, 1, LANE),
             k_gain.reshape(N_GROUPS, HEADS_PER_GROUP, 1, LANE)]

    out_specs = [pl.BlockSpec((ATT_UNIT, LANE), lambda hh, n: (n, hh))]
    out_shape = [jax.ShapeDtypeStruct((s_len, ATT_OUT_WIDTH), F32)]
    for _ in range(2):
        for g in range(N_GROUPS):
            out_specs.append(pl.BlockSpec((WINDOWS[g], LANE), lambda hh, n: (0, hh)))
            out_shape.append(jax.ShapeDtypeStruct((WINDOWS[g], ATT_OUT_WIDTH), F32))

    return pl.pallas_call(
        _prompt_attn_kernel,
        grid=(HEADS_PER_GROUP, n_units),
        in_specs=in_specs,
        out_specs=out_specs,
        out_shape=out_shape,
        scratch_shapes=[
            pltpu.VMEM((2 * ATT_UNIT, LANE), F32),
            pltpu.VMEM((2 * ATT_UNIT, LANE), F32),
            pltpu.VMEM((ATT_UNIT, LANE), F32),
            pltpu.VMEM((ATT_UNIT, LANE), F32),
            pltpu.VMEM((ATT_UNIT, LANE), F32),
        ],
        compiler_params=_params(("parallel", "arbitrary")),
    )(*args)


def _sample_attn_kernel(*refs):
    (q0, q1, q2, kn0, kn1, kn2, vn0, vn1, vn2, ck0, ck1, ck2, cv0, cv1, cv2, qg_ref, kg_ref,
     o_ref, sk0, sk1, sk2, sv0, sv1, sv2) = refs
    q_refs, kn_refs, vn_refs = (q0, q1, q2), (kn0, kn1, kn2), (vn0, vn1, vn2)
    ck_refs, cv_refs = (ck0, ck1, ck2), (cv0, cv1, cv2)
    sk_refs, sv_refs = (sk0, sk1, sk2), (sv0, sv1, sv2)
    t_new = q0.shape[0]
    nh = HEADS_PER_GROUP
    n_q = nh * t_new
    t_bits = t_new.bit_length() - 1

    def heads(ref):
        return [ref[:, h * LANE:(h + 1) * LANE] for h in range(nh)]

    parts = []
    for g in range(N_GROUPS):
        win, dil = WINDOWS[g], DILATIONS[g]
        q_all = jnp.concatenate(
            [_rms(x, qg_ref[g, h]) for h, x in enumerate(heads(q_refs[g]))], axis=0).astype(BF16)
        k_new = [_rms(x, kg_ref[g, h]) for h, x in enumerate(heads(kn_refs[g]))]
        v_new = heads(vn_refs[g])
        k_new_all = jnp.concatenate(k_new, axis=0).astype(BF16)
        v_new_all = jnp.concatenate(v_new, axis=0).astype(BF16)
        ck = ck_refs[g][0]
        cv = cv_refs[g][0]
        old = (win - t_new) * nh
        sk_refs[g][0, 0:old, :] = ck[t_new * nh:, :]
        sv_refs[g][0, 0:old, :] = cv[t_new * nh:, :]
        for h in range(nh):
            sk_refs[g][0, pl.ds(old + h, t_new, stride=nh), :] = k_new[h]
            sv_refs[g][0, pl.ds(old + h, t_new, stride=nh), :] = v_new[h]

        s_c = _dot_nt(q_all, ck.astype(BF16)) * SCALE
        s_n = _dot_nt(q_all, k_new_all) * SCALE
        row = lax.broadcasted_iota(jnp.int32, (n_q, win * nh), 0)
        col = lax.broadcasted_iota(jnp.int32, (n_q, win * nh), 1)
        i_q, pos = row & (t_new - 1), col >> 2
        mask_c = (((col & (nh - 1)) == (row >> t_bits)) & (pos >= i_q)
                  & (((pos - i_q) & (dil - 1)) == 0))
        row_n = lax.broadcasted_iota(jnp.int32, (n_q, n_q), 0)
        col_n = lax.broadcasted_iota(jnp.int32, (n_q, n_q), 1)
        d_n = (row_n & (t_new - 1)) - (col_n & (t_new - 1))
        mask_n = ((row_n >> t_bits) == (col_n >> t_bits)) & (d_n >= 0) & ((d_n & (dil - 1)) == 0)
        s_c = jnp.where(mask_c, s_c, NEG)
        s_n = jnp.where(mask_n, s_n, NEG)
        m = jnp.maximum(jnp.max(s_c, axis=-1, keepdims=True), jnp.max(s_n, axis=-1, keepdims=True))
        p_c = jnp.exp(s_c - m)
        p_n = jnp.exp(s_n - m)
        l = jnp.sum(p_c, axis=-1, keepdims=True) + jnp.sum(p_n, axis=-1, keepdims=True)
        acc = _dot(p_c.astype(BF16), cv.astype(BF16)) + _dot(p_n.astype(BF16), v_new_all)
        parts.append((m, l, acc))

    m_all = jnp.maximum(jnp.maximum(parts[0][0], parts[1][0]), parts[2][0])
    num = jnp.zeros((n_q, LANE), F32)
    den = jnp.zeros((n_q, 1), F32)
    for m, l, acc in parts:
        w = jnp.exp(m - m_all)
        num = num + w * acc
        den = den + w * l
    o_all = num / den
    for h in range(nh):
        o_ref[:, h * LANE:(h + 1) * LANE] = o_all[h * t_new:(h + 1) * t_new, :]


def _sample_attn(z, caches_k, caches_v, q_gain, k_gain, n_seq, t_new):
    assert HEADS_PER_GROUP == 4 and t_new & (t_new - 1) == 0
    in_specs, args = [], []
    for base in (COL_Q, COL_K, COL_V):
        for g in range(N_GROUPS):
            in_specs.append(pl.BlockSpec(
                (t_new, ATT_OUT_WIDTH), lambda b, base=base, g=g: (b, base // ATT_OUT_WIDTH + g)))
            args.append(z)
    for caches in (caches_k, caches_v):
        for g in range(N_GROUPS):
            in_specs.append(pl.BlockSpec((1, WINDOWS[g] * HEADS_PER_GROUP, LANE), lambda b: (b, 0, 0)))
            args.append(caches[g])
    gain_spec = pl.BlockSpec((N_GROUPS, HEADS_PER_GROUP, 1, LANE), lambda b: (0, 0, 0, 0))
    in_specs += [gain_spec, gain_spec]
    args += [q_gain.reshape(N_GROUPS, HEADS_PER_GROUP, 1, LANE),
             k_gain.reshape(N_GROUPS, HEADS_PER_GROUP, 1, LANE)]

    out_specs = [pl.BlockSpec((t_new, ATT_OUT_WIDTH), lambda b: (b, 0))]
    out_shape = [jax.ShapeDtypeStruct((n_seq * t_new, ATT_OUT_WIDTH), F32)]
    for _ in range(2):
        for g in range(N_GROUPS):
            rows = WINDOWS[g] * HEADS_PER_GROUP
            out_specs.append(pl.BlockSpec((1, rows, LANE), lambda b: (b, 0, 0)))
            out_shape.append(jax.ShapeDtypeStruct((n_seq, rows, LANE), F32))

    return pl.pallas_call(
        _sample_attn_kernel,
        grid=(n_seq,),
        in_specs=in_specs,
        out_specs=out_specs,
        out_shape=out_shape,
        compiler_params=_params(("parallel",)),
    )(*args)


def _conv_taps(u, r1, r2, cw_ref):
    return cw_ref[2:3, :] * u + cw_ref[1:2, :] * r1 + cw_ref[0:1, :] * r2


def _merge_kernel(b_ref, c_ref, h_ref, pc_ref, ph_ref, cw_ref, o_ref, gc_ref, ga_ref, wc_ref,
                  wa_ref, out_ref, u_ref, yb_scr, *, per_seq):
    i = pl.program_id(0)

    @pl.when(pl.program_id(1) == 0)
    def _():
        tm = b_ref.shape[0]
        u = c_ref[...] * h_ref[...]
        r1 = pltpu.roll(u, 1, axis=0)
        r2 = pltpu.roll(u, 2, axis=0)
        if per_seq is None:
            yb_scr[...] = b_ref[...] * _conv_taps(u, r1, r2, cw_ref)
            up = jnp.where(i > 0, pc_ref[...] * ph_ref[...], 0.0)
            row = lax.broadcasted_iota(jnp.int32, (SUBLANE, D_CONV), 0)
            u8 = u[0:SUBLANE, :]
            r1_8 = jnp.where(row < 1, pltpu.roll(up, 1, axis=0), pltpu.roll(u8, 1, axis=0))
            r2_8 = jnp.where(row < 2, pltpu.roll(up, 2, axis=0), pltpu.roll(u8, 2, axis=0))
            yb_scr[0:SUBLANE, :] = b_ref[0:SUBLANE, :] * _conv_taps(u8, r1_8, r2_8, cw_ref)
            u_ref[...] = u[tm - SUBLANE:tm, :]
        else:
            pre = pc_ref[...]
            t = lax.broadcasted_iota(jnp.int32, (tm, D_CONV), 0) & (per_seq - 1)
            r1 = jnp.where(t == 0, pltpu.roll(pre, tm - 1, axis=0), r1)
            r2 = jnp.where(t < 2, pre, r2)
            yb_scr[...] = b_ref[...] * _conv_taps(u, r1, r2, cw_ref)
            u_ref[...] = u

    y_conv = _dot(yb_scr[...].astype(BF16), wc_ref[...])
    y_att = _dot(o_ref[...].astype(BF16), wa_ref[...])
    out_ref[...] = (jax.nn.sigmoid(gc_ref[...]) * y_conv
                    + jax.nn.sigmoid(ga_ref[...]) * y_att).astype(BF16)


def _merge(z, o_att, conv_w, wc, wa, tm, tn, per_seq=None, prefix=None):
    m = z.shape[0]
    if per_seq is None:
        prev = lambda i, j, c: (jnp.maximum(i * (tm // SUBLANE) - 1, 0), c)
        pc_spec = pl.BlockSpec((SUBLANE, D_CONV), lambda i, j: prev(i, j, COL_C // D_CONV))
        ph_spec = pl.BlockSpec((SUBLANE, D_CONV), lambda i, j: prev(i, j, COL_H // D_CONV))
        pc_arg, ph_arg = z, z
        u_rows = SUBLANE
    else:
        pc_spec = pl.BlockSpec((tm, D_CONV), lambda i, j: (i, 0))
        ph_spec = pl.BlockSpec((SUBLANE, D_CONV), lambda i, j: (0, 0))
        pc_arg, ph_arg = prefix, prefix
        u_rows = tm
    return pl.pallas_call(
        functools.partial(_merge_kernel, per_seq=per_seq),
        grid=(m // tm, D_MODEL // tn),
        in_specs=[
            pl.BlockSpec((tm, D_CONV), lambda i, j: (i, COL_B // D_CONV)),
            pl.BlockSpec((tm, D_CONV), lambda i, j: (i, COL_C // D_CONV)),
            pl.BlockSpec((tm, D_CONV), lambda i, j: (i, COL_H // D_CONV)),
            pc_spec,
            ph_spec,
            pl.BlockSpec((3, D_CONV), lambda i, j: (0, 0)),
            pl.BlockSpec((tm, ATT_OUT_WIDTH), lambda i, j: (i, 0)),
            pl.BlockSpec((tm, tn), lambda i, j: (i, COL_GC // tn + j)),
            pl.BlockSpec((tm, tn), lambda i, j: (i, COL_GA // tn + j)),
            pl.BlockSpec((D_CONV, tn), lambda i, j: (0, j)),
            pl.BlockSpec((ATT_OUT_WIDTH, tn), lambda i, j: (0, j)),
        ],
        out_specs=[
            pl.BlockSpec((tm, tn), lambda i, j: (i, j)),
            pl.BlockSpec((u_rows, D_CONV), lambda i, j: (i, 0)),
        ],
        out_shape=[
            jax.ShapeDtypeStruct((m, D_MODEL), BF16),
            jax.ShapeDtypeStruct((m // tm * u_rows, D_CONV), F32),
        ],
        scratch_shapes=[pltpu.VMEM((tm, D_CONV), F32)],
        compiler_params=_params(("parallel", "arbitrary")),
    )(z, z, z, pc_arg, ph_arg, conv_w, o_att, z, z, wc, wa)


def _outproj_router_kernel(xp_ref, mgp_ref, xs_ref, mgs_ref, wo_ref, g2_ref, wr_hi_ref, wr_lo_ref,
                           br_ref, x1_ref, route_ref):
    is_sample = pl.program_id(0) == pl.num_programs(0) - 1

    @pl.when(jnp.logical_not(is_sample))
    def _():
        _outproj_router_tile(xp_ref, mgp_ref, wo_ref, g2_ref, wr_hi_ref, wr_lo_ref, br_ref,
                             x1_ref, route_ref)

    @pl.when(is_sample)
    def _():
        _outproj_router_tile(xs_ref, mgs_ref, wo_ref, g2_ref, wr_hi_ref, wr_lo_ref, br_ref,
                             x1_ref, route_ref)


def _outproj_router_tile(x_ref, mg_ref, wo_ref, g2_ref, wr_hi_ref, wr_lo_ref, br_ref, x1_ref,
                         route_ref):
    x1 = x_ref[...] + _dot(mg_ref[...], wo_ref[...])
    x1_ref[...] = x1
    h = _rms(x1, g2_ref[...])
    h_hi = h.astype(BF16)
    h_lo = (h - h_hi.astype(F32)).astype(BF16)
    logits = (_dot(h_hi, wr_hi_ref[...]) + _dot(h_hi, wr_lo_ref[...]) + _dot(h_lo, wr_hi_ref[...])
              + br_ref[...])
    lane = lax.broadcasted_iota(jnp.int32, logits.shape, 1)
    is_coarse = lane < N_EXPERT_GROUPS
    coarse = jnp.where(is_coarse, logits, NEG)
    cmax = jnp.max(coarse, axis=-1, keepdims=True)
    grp = jnp.min(jnp.where(coarse == cmax, lane, LANE), axis=-1, keepdims=True)
    p_grp = 1.0 / jnp.sum(jnp.where(is_coarse, jnp.exp(coarse - cmax), 0.0), axis=-1, keepdims=True)
    eid = lane - N_EXPERT_GROUPS
    in_grp = (eid >= 0) & (eid < N_EXPERTS) & ((eid >> 3) == grp)
    fine = jnp.where(in_grp, logits, NEG)
    v1 = jnp.max(fine, axis=-1, keepdims=True)
    i1 = jnp.min(jnp.where(fine == v1, lane, LANE), axis=-1, keepdims=True)
    fine2 = jnp.where(lane == i1, NEG, fine)
    v2 = jnp.max(fine2, axis=-1, keepdims=True)
    i2 = jnp.min(jnp.where(fine2 == v2, lane, LANE), axis=-1, keepdims=True)
    e = jnp.exp(v2 - v1)
    gate1 = p_grp / (1.0 + e)
    gate2 = p_grp * e / (1.0 + e)
    route = jnp.where(lane == 0, (i1 - N_EXPERT_GROUPS).astype(F32),
                      jnp.where(lane == 1, (i2 - N_EXPERT_GROUPS).astype(F32),
                                jnp.where(lane == 2, gate1, jnp.where(lane == 3, gate2, 0.0))))
    route_ref[...] = route


def _outproj_router(xp, mg_p, xs, mg_s, wo, norm2_g, wr_hi, wr_lo, br):
    m_p, tm = xp.shape[0], xs.shape[0]
    assert m_p % tm == 0
    n_p = m_p // tm
    prompt_blk = lambda i: (jnp.minimum(i, n_p - 1), 0)
    fixed = lambda i: (0, 0)
    return pl.pallas_call(
        _outproj_router_kernel,
        grid=(n_p + 1,),
        in_specs=[
            pl.BlockSpec((tm, D_MODEL), prompt_blk),
            pl.BlockSpec((tm, D_MODEL), prompt_blk),
            pl.BlockSpec((tm, D_MODEL), fixed),
            pl.BlockSpec((tm, D_MODEL), fixed),
            pl.BlockSpec((D_MODEL, D_MODEL), fixed),
            pl.BlockSpec((1, D_MODEL), fixed),
            pl.BlockSpec((D_MODEL, LANE), fixed),
            pl.BlockSpec((D_MODEL, LANE), fixed),
            pl.BlockSpec((1, LANE), fixed),
        ],
        out_specs=[
            pl.BlockSpec((tm, D_MODEL), lambda i: (i, 0)),
            pl.BlockSpec((tm, LANE), lambda i: (i, 0)),
        ],
        out_shape=[
            jax.ShapeDtypeStruct((m_p + tm, D_MODEL), F32),
            jax.ShapeDtypeStruct((m_p + tm, LANE), F32),
        ],
        compiler_params=_params(("arbitrary",)),
    )(xp, mg_p, xs, mg_s, wo, norm2_g.reshape(1, D_MODEL), wr_hi, wr_lo, br)


def _moe_kernel(blk_e_ref, nused_ref, tok_ref, x1_hbm, g2_ref, wg_ref, wu_ref, wd_ref, y_ref,
                xbuf, sem, wg_b, wu_b, wd_b):
    i = pl.program_id(0)
    n_used = nused_ref[0]
    slot = i & 1

    def start_gather(blk, to_slot):
        def body(r, carry):
            tok = tok_ref[blk * MOE_TM + r]
            pltpu.make_async_copy(x1_hbm.at[pl.ds(tok, 1), :], xbuf.at[to_slot, pl.ds(r, 1), :],
                                  sem.at[to_slot]).start()
            return carry
        lax.fori_loop(0, MOE_TM, body, 0)

    @pl.when(i == 0)
    def _():
        start_gather(0, 0)

    @pl.when(i + 1 < n_used)
    def _():
        start_gather(i + 1, 1 - slot)

    @pl.when(i < n_used)
    def _():
        new_expert = (i == 0) | (blk_e_ref[i] != blk_e_ref[jnp.maximum(i - 1, 0)])

        @pl.when(new_expert)
        def _():
            wg_b[...] = wg_ref[0].astype(BF16)
            wu_b[...] = wu_ref[0].astype(BF16)
            wd_b[...] = wd_ref[0].astype(BF16)

        pltpu.make_async_copy(x1_hbm.at[pl.ds(0, MOE_TM), :], xbuf.at[slot], sem.at[slot]).wait()
        h = _rms(xbuf[slot], g2_ref[...]).astype(BF16)
        a = _dot(h, wg_b[...])
        u = _dot(h, wu_b[...])
        hm = (a * jax.nn.sigmoid(a) * u).astype(BF16)
        y_ref[...] = _dot(hm, wd_b[...])

    @pl.when(i >= n_used)
    def _():
        y_ref[...] = jnp.zeros_like(y_ref)


def _moe(x1_all, norm2_g, w_gate_e, w_up_e, w_down_e, blk_expert, n_used, slot_tok, n_blocks):
    def w_idx(i, be, nu, tok):
        return (be[i], 0, 0)

    grid_spec = pltpu.PrefetchScalarGridSpec(
        num_scalar_prefetch=3,
        grid=(n_blocks,),
        in_specs=[
            pl.BlockSpec(memory_space=pl.ANY),
            pl.BlockSpec((1, D_MODEL), lambda i, be, nu, tok: (0, 0)),
            pl.BlockSpec((1, D_MODEL, D_EXPERT), w_idx),
            pl.BlockSpec((1, D_MODEL, D_EXPERT), w_idx),
            pl.BlockSpec((1, D_EXPERT, D_MODEL), w_idx),
        ],
        out_specs=pl.BlockSpec((MOE_TM, D_MODEL), lambda i, be, nu, tok: (i, 0)),
        scratch_shapes=[
            pltpu.VMEM((2, MOE_TM, D_MODEL), F32),
            pltpu.SemaphoreType.DMA((2,)),
            pltpu.VMEM((D_MODEL, D_EXPERT), BF16),
            pltpu.VMEM((D_MODEL, D_EXPERT), BF16),
            pltpu.VMEM((D_EXPERT, D_MODEL), BF16),
        ],
    )
    return pl.pallas_call(
        _moe_kernel,
        grid_spec=grid_spec,
        out_shape=jax.ShapeDtypeStruct((n_blocks * MOE_TM, D_MODEL), F32),
        compiler_params=_params(("arbitrary",)),
    )(blk_expert, n_used, slot_tok, x1_all, norm2_g.reshape(1, D_MODEL), w_gate_e, w_up_e, w_down_e)


def _combine_ple_kernel(slots_ref, x1_ref, route_ref, y_hbm, p_ref, g3_ref, wpg_ref, wple_ref,
                        out_ref, ybuf, sem, *, tm, row0):
    i = pl.program_id(0)
    n_steps = pl.num_programs(0)
    slot = i & 1

    def start_gather(step, to_slot):
        base = (row0 + step * tm) * TOP_K

        def body(r, carry):
            for k in range(TOP_K):
                src = slots_ref[base + r * TOP_K + k]
                pltpu.make_async_copy(y_hbm.at[pl.ds(src, 1), :],
                                      ybuf.at[to_slot, k, pl.ds(r, 1), :], sem.at[to_slot]).start()
            return carry
        lax.fori_loop(0, tm, body, 0)

    @pl.when(i == 0)
    def _():
        start_gather(0, 0)

    @pl.when(i + 1 < n_steps)
    def _():
        start_gather(i + 1, 1 - slot)

    for k in range(TOP_K):
        pltpu.make_async_copy(y_hbm.at[pl.ds(0, tm), :], ybuf.at[slot, k], sem.at[slot]).wait()
    route = route_ref[...]
    x2 = x1_ref[...] + (route[:, 2:3] * ybuf[slot, 0] + route[:, 3:4] * ybuf[slot, 1])
    h = _rms(x2, g3_ref[...]).astype(BF16)
    gate = jax.nn.sigmoid(_dot(h, wpg_ref[...]))
    out_ref[...] = x2 + gate * _dot(p_ref[...].astype(BF16), wple_ref[...])


def _combine_ple(x1_all, route_all, y_slots, tok_slots, p, norm3_g, wpg, wple, m, tm, row0):
    blk0 = row0 // tm
    grid_spec = pltpu.PrefetchScalarGridSpec(
        num_scalar_prefetch=1,
        grid=(m // tm,),
        in_specs=[
            pl.BlockSpec((tm, D_MODEL), lambda i, s: (blk0 + i, 0)),
            pl.BlockSpec((tm, LANE), lambda i, s: (blk0 + i, 0)),
            pl.BlockSpec(memory_space=pl.ANY),
            pl.BlockSpec((tm, PLE_DIM), lambda i, s: (i, 0)),
            pl.BlockSpec((1, D_MODEL), lambda i, s: (0, 0)),
            pl.BlockSpec((D_MODEL, D_MODEL), lambda i, s: (0, 0)),
            pl.BlockSpec((PLE_DIM, D_MODEL), lambda i, s: (0, 0)),
        ],
        out_specs=pl.BlockSpec((tm, D_MODEL), lambda i, s: (i, 0)),
        scratch_shapes=[
            pltpu.VMEM((2, TOP_K, tm, D_MODEL), F32),
            pltpu.SemaphoreType.DMA((2,)),
        ],
    )
    return pl.pallas_call(
        functools.partial(_combine_ple_kernel, tm=tm, row0=row0),
        grid_spec=grid_spec,
        out_shape=jax.ShapeDtypeStruct((m, D_MODEL), F32),
        compiler_params=_params(("arbitrary",)),
    )(tok_slots, x1_all, route_all, y_slots, p, norm3_g.reshape(1, D_MODEL), wpg, wple)


def _routing_tables(route_all, n_blocks):
    n_tok = route_all.shape[0]
    flat_e = route_all[:, 0:TOP_K].astype(jnp.int32).reshape(-1)
    onehot = (flat_e[:, None] == jnp.arange(N_EXPERTS, dtype=jnp.int32)[None, :]).astype(jnp.int32)
    csum = jnp.cumsum(onehot, axis=0)
    rank = jnp.take_along_axis(csum, flat_e[:, None], axis=1)[:, 0] - 1
    counts = csum[-1]
    nblk = (counts + MOE_TM - 1) // MOE_TM
    blk_end = jnp.cumsum(nblk)
    blk_start = blk_end - nblk
    slot = blk_start[flat_e] * MOE_TM + rank
    flat_tok = jnp.arange(n_tok * TOP_K, dtype=jnp.int32) // TOP_K
    slot_tok = jnp.zeros((n_blocks * MOE_TM,), jnp.int32).at[slot].set(flat_tok)
    n_used = blk_end[-1]
    blk_ids = jnp.minimum(jnp.arange(n_blocks, dtype=jnp.int32), n_used - 1)
    blk_expert = jnp.sum((blk_end[None, :] <= blk_ids[:, None]).astype(jnp.int32), axis=1)
    return blk_expert, n_used.reshape(1).astype(jnp.int32), slot_tok, slot.astype(jnp.int32)


def kernel(x_prompt, x_sample, p_prompt, p_sample, cache_k_g0, cache_v_g0, cache_k_g1, cache_v_g1,
           cache_k_g2, cache_v_g2, state_conv, norm1_g, w_in, q_gain, k_gain, conv_w, w_conv_out,
           w_attn_out, w_o, norm2_g, w_coarse, b_coarse, w_fine, b_fine, w_gate_e, w_up_e, w_down_e,
           norm3_g, w_ple_gate, w_ple):
    seq = x_prompt.shape[1]
    assert x_prompt.shape == (1, seq, D_MODEL) and norm1_g.shape[0] == 1, "one prompt, one layer"
    n_seq, t_new, _ = x_sample.shape
    assert seq % ATT_UNIT == 0 and t_new == SUBLANE
    caches_k = (cache_k_g0, cache_k_g1, cache_k_g2)
    caches_v = (cache_v_g0, cache_v_g1, cache_v_g2)
    for g in range(N_GROUPS):
        assert caches_k[g].shape == (1, n_seq, WINDOWS[g], HEADS_PER_GROUP, HEAD_DIM)
    n_s = n_seq * t_new
    n_all = seq + n_s

    xp = x_prompt[0]
    xs = x_sample.reshape(n_s, D_MODEL)
    wc = w_conv_out[0].astype(BF16)
    wa = w_attn_out[0].astype(BF16)
    wo = w_o[0].astype(BF16)
    wpg = w_ple_gate[0].astype(BF16)
    wple = w_ple[0].astype(BF16)
    wr = jnp.concatenate(
        [w_coarse[0], jnp.transpose(w_fine[0], (1, 0, 2)).reshape(D_MODEL, N_EXPERTS)], axis=1)
    wr = jnp.pad(wr, ((0, 0), (0, LANE - wr.shape[1])))
    wr_hi = wr.astype(BF16)
    wr_lo = (wr - wr_hi.astype(F32)).astype(BF16)
    br = jnp.pad(jnp.concatenate([b_coarse[0], b_fine[0].reshape(-1)]), (0, LANE - 36)).reshape(1, LANE)

    z_p = _inproj(xp, norm1_g[0], w_in[0], tm=1024, tn=512)
    z_s = _inproj(xs, norm1_g[0], w_in[0], tm=n_s, tn=512)

    att_p = _prompt_attn(z_p, q_gain[0], k_gain[0])
    o_p, pks, pvs = att_p[0], att_p[1:4], att_p[4:7]
    ck = [c.reshape(n_seq, WINDOWS[g] * HEADS_PER_GROUP, HEAD_DIM) for g, c in enumerate(caches_k)]
    cv = [c.reshape(n_seq, WINDOWS[g] * HEADS_PER_GROUP, HEAD_DIM) for g, c in enumerate(caches_v)]
    att_s = _sample_attn(z_s, ck, cv, q_gain[0], k_gain[0], n_seq, t_new)
    o_s, sks, svs = att_s[0], att_s[1:4], att_s[4:7]

    mg_p, u_tail = _merge(z_p, o_p, conv_w[0], wc, wa, tm=512, tn=512)
    prefix = jnp.pad(state_conv[0], ((0, 0), (0, t_new - 2), (0, 0))).reshape(n_s, D_CONV)
    mg_s, u_s = _merge(z_s, o_s, conv_w[0], wc, wa, tm=n_s, tn=512, per_seq=t_new, prefix=prefix)
    x1_all, route_all = _outproj_router(xp, mg_p, xs, mg_s, wo, norm2_g[0], wr_hi, wr_lo, br)

    n_blocks = n_all * TOP_K // MOE_TM + N_EXPERTS
    blk_expert, n_used, slot_tok, tok_slots = _routing_tables(route_all, n_blocks)
    y_slots = _moe(x1_all, norm2_g[0], w_gate_e[0], w_up_e[0], w_down_e[0], blk_expert, n_used,
                   slot_tok, n_blocks)

    y_p = _combine_ple(x1_all, route_all, y_slots, tok_slots, p_prompt[0, 0], norm3_g[0], wpg, wple,
                       m=seq, tm=256, row0=0)
    y_s = _combine_ple(x1_all, route_all, y_slots, tok_slots, p_sample[0].reshape(n_s, PLE_DIM),
                       norm3_g[0], wpg, wple, m=n_s, tm=n_s, row0=seq)

    def state(a, n, g):
        return a.reshape(1, n, WINDOWS[g], HEADS_PER_GROUP, HEAD_DIM)

    pk = [state(a, 1, g) for g, a in enumerate(pks)]
    pv = [state(a, 1, g) for g, a in enumerate(pvs)]
    sk = [state(a, n_seq, g) for g, a in enumerate(sks)]
    sv = [state(a, n_seq, g) for g, a in enumerate(svs)]
    pconv = u_tail[-2:].reshape(1, 1, 2, D_CONV)
    sconv = u_s.reshape(n_seq, t_new, D_CONV)[:, t_new - 2:].reshape(1, n_seq, 2, D_CONV)
    return (y_p.reshape(1, seq, D_MODEL), y_s.reshape(n_seq, t_new, D_MODEL),
            pk[0], pv[0], pk[1], pv[1], pk[2], pv[2], pconv,
            sk[0], sv[0], sk[1], sv[1], sk[2], sv[2], sconv)
```

```python
import functools

import jax
import jax.numpy as jnp
from jax import lax
from jax.experimental import pallas as pl
from jax.experimental.pallas import tpu as pltpu

D_MODEL = 2048
HEAD_DIM = 128
HEADS_PER_GROUP = 4
WINDOWS = (128, 512, 2048)
DILATIONS = (1, 4, 16)
N_GROUPS = 3
N_HEADS = N_GROUPS * HEADS_PER_GROUP
ATT_WIDTH = N_HEADS * HEAD_DIM
ATT_OUT_WIDTH = HEADS_PER_GROUP * HEAD_DIM
SCALE = HEAD_DIM ** -0.5
D_CONV = D_MODEL // 2
PLE_DIM = 256
N_EXPERT_GROUPS = 4
EXPERTS_PER_GROUP = 8
N_EXPERTS = N_EXPERT_GROUPS * EXPERTS_PER_GROUP
TOP_K = 2
D_EXPERT = D_MODEL // 4
EPS = 1e-6

COL_B = 0
COL_C = D_CONV
COL_H = 2 * D_CONV
COL_Q = 3 * D_CONV
COL_K = COL_Q + ATT_WIDTH
COL_V = COL_K + ATT_WIDTH
COL_GC = COL_V + ATT_WIDTH
COL_GA = COL_GC + D_MODEL
IN_COLS = COL_GA + D_MODEL

LANE = 128
SUBLANE = 8
NEG = -1e30
ATT_UNIT = max(WINDOWS)
ATT_BLK = 128
ATT_TILE_UNROLL = 4
MXU_N = 256
TOK_ROWS = D_MODEL // LANE
MOE_TM = 256
VMEM_LIMIT = 56 * 1024 * 1024

BF16 = jnp.bfloat16
F32 = jnp.float32


def _params(sem):
    return pltpu.CompilerParams(dimension_semantics=sem, vmem_limit_bytes=VMEM_LIMIT)


def _rms(x, gain):
    return x * lax.rsqrt(jnp.mean(x * x, axis=-1, keepdims=True) + EPS) * gain


def _dot(a, b):
    return jnp.dot(a, b, preferred_element_type=F32)


def _dot_nt(a, b):
    return lax.dot_general(a, b, (((1,), (1,)), ((), ())), preferred_element_type=F32)


def _inproj_kernel(x_ref, g_ref, w_ref, z_ref, h_scr):
    @pl.when(pl.program_id(1) == 0)
    def _():
        h_scr[...] = _rms(x_ref[...], g_ref[...]).astype(BF16)

    for c in range(w_ref.shape[1] // MXU_N):
        cols = slice(c * MXU_N, (c + 1) * MXU_N)
        z_ref[:, cols] = _dot(h_scr[...], w_ref[:, cols].astype(BF16))


def _inproj(x, norm_g, w_in, tm, tn):
    m = x.shape[0]
    return pl.pallas_call(
        _inproj_kernel,
        grid=(m // tm, IN_COLS // tn),
        in_specs=[
            pl.BlockSpec((tm, D_MODEL), lambda i, j: (i, 0), pipeline_mode=pl.Buffered(1)),
            pl.BlockSpec((1, D_MODEL), lambda i, j: (0, 0)),
            pl.BlockSpec((D_MODEL, tn), lambda i, j: (0, j)),
        ],
        out_specs=pl.BlockSpec((tm, tn), lambda i, j: (i, j)),
        out_shape=jax.ShapeDtypeStruct((m, IN_COLS), F32),
        scratch_shapes=[pltpu.VMEM((tm, D_MODEL), BF16)],
        compiler_params=_params(("parallel", "arbitrary")),
    )(x, norm_g.reshape(1, D_MODEL), w_in)


def _prompt_attn_kernel(*refs):
    (q0, q1, q2, kc0, kc1, kc2, vc0, vc1, vc2, kp0, kp1, kp2, vp0, vp1, vp2, qg_ref, kg_ref,
     o_ref, pk0, pk1, pk2, pv0, pv1, pv2, kext, vext, acc_s, m_s, l_s) = refs
    q_refs, kc_refs, vc_refs = (q0, q1, q2), (kc0, kc1, kc2), (vc0, vc1, vc2)
    kp_refs, vp_refs = (kp0, kp1, kp2), (vp0, vp1, vp2)
    pk_refs, pv_refs = (pk0, pk1, pk2), (pv0, pv1, pv2)
    n = pl.program_id(1)
    last = pl.num_programs(1) - 1

    qi = lax.broadcasted_iota(jnp.int32, (ATT_BLK, 2 * ATT_BLK), 0) + ATT_BLK
    ki = lax.broadcasted_iota(jnp.int32, (ATT_BLK, 2 * ATT_BLK), 1)
    dist = qi - ki
    band = (dist >= 0) & (dist <= ATT_BLK)

    for g in range(N_GROUPS):
        win, dil = WINDOWS[g], DILATIONS[g]
        q_ref = q_refs[g]
        qgain = qg_ref[g, 0]
        kgain = kg_ref[g, 0]
        kext[0:win, :] = _rms(kp_refs[g][...], kgain)
        kext[win:win + ATT_UNIT, :] = _rms(kc_refs[g][...], kgain)
        vext[0:win, :] = vp_refs[g][...]
        vext[win:win + ATT_UNIT, :] = vc_refs[g][...]

        @pl.when(n == last)
        def _():
            pk_refs[g][...] = kext[ATT_UNIT:ATT_UNIT + win, :]
            pv_refs[g][...] = vext[ATT_UNIT:ATT_UNIT + win, :]

        def tile(t, carry, g=g, win=win, dil=dil, q_ref=q_ref, qgain=qgain):
            u = t // dil
            r = t % dil
            off = u * win + r
            if dil == 1:
                off = pl.multiple_of(off, ATT_BLK)
                rows_q = pl.ds(off, ATT_BLK)
                rows_k = pl.ds(off, 2 * ATT_BLK)
            else:
                rows_q = pl.ds(off, ATT_BLK, stride=dil)
                rows_k = pl.ds(off, 2 * ATT_BLK, stride=dil)
            q = _rms(q_ref[rows_q, :], qgain).astype(BF16)
            k = kext[rows_k, :].astype(BF16)
            v = vext[rows_k, :].astype(BF16)
            s = _dot_nt(q, k) * SCALE
            k_min = jnp.where((n > 0) | (u > 0), 0, ATT_BLK)
            s = jnp.where(band & (ki >= k_min), s, NEG)
            m_t = jnp.max(s, axis=-1, keepdims=True)
            p = jnp.exp(s - m_t)
            l_t = jnp.sum(p, axis=-1, keepdims=True)
            m_s[g, rows_q, :] = jnp.broadcast_to(m_t, (ATT_BLK, LANE))
            l_s[g, rows_q, :] = jnp.broadcast_to(l_t, (ATT_BLK, LANE))
            acc_s[g, rows_q, :] = _dot(p.astype(BF16), v)
            return carry

        lax.fori_loop(0, ATT_UNIT // ATT_BLK, tile, 0, unroll=ATT_TILE_UNROLL)

    def finish(c, carry):
        rows = pl.ds(pl.multiple_of(c * ATT_BLK, ATT_BLK), ATT_BLK)
        m = [m_s[g, rows, :] for g in range(N_GROUPS)]
        m_all = jnp.maximum(jnp.maximum(m[0], m[1]), m[2])
        num = jnp.zeros((ATT_BLK, LANE), F32)
        den = jnp.zeros((ATT_BLK, LANE), F32)
        for g in range(N_GROUPS):
            w = jnp.exp(m[g] - m_all)
            num = num + w * acc_s[g, rows, :]
            den = den + w * l_s[g, rows, :]
        o_ref[rows, :] = num / den
        return carry

    lax.fori_loop(0, ATT_UNIT // ATT_BLK, finish, 0, unroll=2)


def _prompt_attn(z, q_gain, k_gain):
    s_len = z.shape[0]
    n_units = s_len // ATT_UNIT
    in_specs, args = [], []

    def col(base, g):
        return lambda hh, n: (n, base // LANE + g * HEADS_PER_GROUP + hh)

    for g in range(N_GROUPS):
        in_specs.append(pl.BlockSpec((ATT_UNIT, LANE), col(COL_Q, g)))
    for base in (COL_K, COL_V):
        for g in range(N_GROUPS):
            in_specs.append(pl.BlockSpec((ATT_UNIT, LANE), col(base, g)))
    for base in (COL_K, COL_V):
        for g in range(N_GROUPS):
            per = ATT_UNIT // WINDOWS[g]
            in_specs.append(pl.BlockSpec(
                (WINDOWS[g], LANE),
                lambda hh, n, base=base, g=g, per=per: (
                    jnp.maximum(n * per - 1, 0), base // LANE + g * HEADS_PER_GROUP + hh)))
    args = [z] * 15
    gain_spec = pl.BlockSpec((N_GROUPS, 1, 1, LANE), lambda hh, n: (0, hh, 0, 0))
    in_specs += [gain_spec, gain_spec]
    args += [q_gain.reshape(N_GROUPS, HEADS_PER_GROUP, 1, LANE),
             k_gain.reshape(N_GROUPS, HEADS_PER_GROUP, 1, LANE)]

    out_specs = [pl.BlockSpec((ATT_UNIT, LANE), lambda hh, n: (n, hh))]
    out_shape = [jax.ShapeDtypeStruct((s_len, ATT_OUT_WIDTH), F32)]
    for _ in range(2):
        for g in range(N_GROUPS):
            out_specs.append(pl.BlockSpec((WINDOWS[g], LANE), lambda hh, n: (0, hh)))
            out_shape.append(jax.ShapeDtypeStruct((WINDOWS[g], ATT_OUT_WIDTH), F32))

    return pl.pallas_call(
        _prompt_attn_kernel,
        grid=(HEADS_PER_GROUP, n_units),
        in_specs=in_specs,
        out_specs=out_specs,
        out_shape=out_shape,
        scratch_shapes=[
            pltpu.VMEM((2 * ATT_UNIT, LANE), F32),
            pltpu.VMEM((2 * ATT_UNIT, LANE), F32),
            pltpu.VMEM((N_GROUPS, ATT_UNIT, LANE), F32),
            pltpu.VMEM((N_GROUPS, ATT_UNIT, LANE), F32),
            pltpu.VMEM((N_GROUPS, ATT_UNIT, LANE), F32),
        ],
        compiler_params=_params(("parallel", "arbitrary")),
    )(*args)


def _sample_attn_kernel(*refs):
    (q0, q1, q2, kn0, kn1, kn2, vn0, vn1, vn2, ck0, ck1, ck2, cv0, cv1, cv2, qg_ref, kg_ref,
     o_ref, sk0, sk1, sk2, sv0, sv1, sv2) = refs
    q_refs, kn_refs, vn_refs = (q0, q1, q2), (kn0, kn1, kn2), (vn0, vn1, vn2)
    ck_refs, cv_refs = (ck0, ck1, ck2), (cv0, cv1, cv2)
    sk_refs, sv_refs = (sk0, sk1, sk2), (sv0, sv1, sv2)
    t_new = q0.shape[0]
    nh = HEADS_PER_GROUP
    n_q = nh * t_new
    t_bits = t_new.bit_length() - 1

    def heads(ref):
        return [ref[:, h * LANE:(h + 1) * LANE] for h in range(nh)]

    parts = []
    for g in range(N_GROUPS):
        win, dil = WINDOWS[g], DILATIONS[g]
        q_all = jnp.concatenate(
            [_rms(x, qg_ref[g, h]) for h, x in enumerate(heads(q_refs[g]))], axis=0).astype(BF16)
        k_new = [_rms(x, kg_ref[g, h]) for h, x in enumerate(heads(kn_refs[g]))]
        v_new = heads(vn_refs[g])
        k_new_all = jnp.concatenate(k_new, axis=0).astype(BF16)
        v_new_all = jnp.concatenate(v_new, axis=0).astype(BF16)
        ck = ck_refs[g][0]
        cv = cv_refs[g][0]
        old = (win - t_new) * nh
        sk_refs[g][0, 0:old, :] = ck[t_new * nh:, :]
        sv_refs[g][0, 0:old, :] = cv[t_new * nh:, :]
        for h in range(nh):
            sk_refs[g][0, pl.ds(old + h, t_new, stride=nh), :] = k_new[h]
            sv_refs[g][0, pl.ds(old + h, t_new, stride=nh), :] = v_new[h]

        s_c = _dot_nt(q_all, ck.astype(BF16)) * SCALE
        s_n = _dot_nt(q_all, k_new_all) * SCALE
        row = lax.broadcasted_iota(jnp.int32, (n_q, win * nh), 0)
        col = lax.broadcasted_iota(jnp.int32, (n_q, win * nh), 1)
        i_q, pos = row & (t_new - 1), col >> 2
        mask_c = (((col & (nh - 1)) == (row >> t_bits)) & (pos >= i_q)
                  & (((pos - i_q) & (dil - 1)) == 0))
        row_n = lax.broadcasted_iota(jnp.int32, (n_q, n_q), 0)
        col_n = lax.broadcasted_iota(jnp.int32, (n_q, n_q), 1)
        d_n = (row_n & (t_new - 1)) - (col_n & (t_new - 1))
        mask_n = ((row_n >> t_bits) == (col_n >> t_bits)) & (d_n >= 0) & ((d_n & (dil - 1)) == 0)
        s_c = jnp.where(mask_c, s_c, NEG)
        s_n = jnp.where(mask_n, s_n, NEG)
        m = jnp.maximum(jnp.max(s_c, axis=-1, keepdims=True), jnp.max(s_n, axis=-1, keepdims=True))
        p_c = jnp.exp(s_c - m)
        p_n = jnp.exp(s_n - m)
        l = jnp.sum(p_c, axis=-1, keepdims=True) + jnp.sum(p_n, axis=-1, keepdims=True)
        acc = _dot(p_c.astype(BF16), cv.astype(BF16)) + _dot(p_n.astype(BF16), v_new_all)
        parts.append((m, l, acc))

    m_all = jnp.maximum(jnp.maximum(parts[0][0], parts[1][0]), parts[2][0])
    num = jnp.zeros((n_q, LANE), F32)
    den = jnp.zeros((n_q, 1), F32)
    for m, l, acc in parts:
        w = jnp.exp(m - m_all)
        num = num + w * acc
        den = den + w * l
    o_all = num / den
    for h in range(nh):
        o_ref[:, h * LANE:(h + 1) * LANE] = o_all[h * t_new:(h + 1) * t_new, :]


def _sample_attn(z, caches_k, caches_v, q_gain, k_gain, n_seq, t_new):
    assert HEADS_PER_GROUP == 4 and t_new & (t_new - 1) == 0
    in_specs, args = [], []
    for base in (COL_Q, COL_K, COL_V):
        for g in range(N_GROUPS):
            in_specs.append(pl.BlockSpec(
                (t_new, ATT_OUT_WIDTH), lambda b, base=base, g=g: (b, base // ATT_OUT_WIDTH + g)))
            args.append(z)
    for caches in (caches_k, caches_v):
        for g in range(N_GROUPS):
            in_specs.append(pl.BlockSpec((1, WINDOWS[g] * HEADS_PER_GROUP, LANE), lambda b: (b, 0, 0)))
            args.append(caches[g])
    gain_spec = pl.BlockSpec((N_GROUPS, HEADS_PER_GROUP, 1, LANE), lambda b: (0, 0, 0, 0))
    in_specs += [gain_spec, gain_spec]
    args += [q_gain.reshape(N_GROUPS, HEADS_PER_GROUP, 1, LANE),
             k_gain.reshape(N_GROUPS, HEADS_PER_GROUP, 1, LANE)]

    out_specs = [pl.BlockSpec((t_new, ATT_OUT_WIDTH), lambda b: (b, 0))]
    out_shape = [jax.ShapeDtypeStruct((n_seq * t_new, ATT_OUT_WIDTH), F32)]
    for _ in range(2):
        for g in range(N_GROUPS):
            rows = WINDOWS[g] * HEADS_PER_GROUP
            out_specs.append(pl.BlockSpec((1, rows, LANE), lambda b: (b, 0, 0)))
            out_shape.append(jax.ShapeDtypeStruct((n_seq, rows, LANE), F32))

    return pl.pallas_call(
        _sample_attn_kernel,
        grid=(n_seq,),
        in_specs=in_specs,
        out_specs=out_specs,
        out_shape=out_shape,
        compiler_params=_params(("parallel",)),
    )(*args)


def _conv_taps(u, r1, r2, cw_ref):
    return cw_ref[2:3, :] * u + cw_ref[1:2, :] * r1 + cw_ref[0:1, :] * r2


def _merge_kernel(b_ref, c_ref, h_ref, pc_ref, ph_ref, cw_ref, o_ref, gc_ref, ga_ref, wc_ref,
                  wa_ref, out_ref, u_ref, yb_scr, *, per_seq):
    i = pl.program_id(0)

    @pl.when(pl.program_id(1) == 0)
    def _():
        tm = b_ref.shape[0]
        u = c_ref[...] * h_ref[...]
        r1 = pltpu.roll(u, 1, axis=0)
        r2 = pltpu.roll(u, 2, axis=0)
        if per_seq is None:
            yb_scr[...] = b_ref[...] * _conv_taps(u, r1, r2, cw_ref)
            up = jnp.where(i > 0, pc_ref[...] * ph_ref[...], 0.0)
            row = lax.broadcasted_iota(jnp.int32, (SUBLANE, D_CONV), 0)
            u8 = u[0:SUBLANE, :]
            r1_8 = jnp.where(row < 1, pltpu.roll(up, 1, axis=0), pltpu.roll(u8, 1, axis=0))
            r2_8 = jnp.where(row < 2, pltpu.roll(up, 2, axis=0), pltpu.roll(u8, 2, axis=0))
            yb_scr[0:SUBLANE, :] = b_ref[0:SUBLANE, :] * _conv_taps(u8, r1_8, r2_8, cw_ref)
            u_ref[...] = u[tm - SUBLANE:tm, :]
        else:
            pre = pc_ref[...]
            t = lax.broadcasted_iota(jnp.int32, (tm, D_CONV), 0) & (per_seq - 1)
            r1 = jnp.where(t == 0, pltpu.roll(pre, tm - 1, axis=0), r1)
            r2 = jnp.where(t < 2, pre, r2)
            yb_scr[...] = b_ref[...] * _conv_taps(u, r1, r2, cw_ref)
            u_ref[...] = u

    y_conv = _dot(yb_scr[...].astype(BF16), wc_ref[...])
    y_att = _dot(o_ref[...].astype(BF16), wa_ref[...])
    out_ref[...] = (jax.nn.sigmoid(gc_ref[...]) * y_conv
                    + jax.nn.sigmoid(ga_ref[...]) * y_att).astype(BF16)


def _merge(z, o_att, conv_w, wc, wa, tm, tn, per_seq=None, prefix=None):
    m = z.shape[0]
    if per_seq is None:
        prev = lambda i, j, c: (jnp.maximum(i * (tm // SUBLANE) - 1, 0), c)
        pc_spec = pl.BlockSpec((SUBLANE, D_CONV), lambda i, j: prev(i, j, COL_C // D_CONV))
        ph_spec = pl.BlockSpec((SUBLANE, D_CONV), lambda i, j: prev(i, j, COL_H // D_CONV))
        pc_arg, ph_arg = z, z
        u_rows = SUBLANE
    else:
        pc_spec = pl.BlockSpec((tm, D_CONV), lambda i, j: (i, 0))
        ph_spec = pl.BlockSpec((SUBLANE, D_CONV), lambda i, j: (0, 0))
        pc_arg, ph_arg = prefix, prefix
        u_rows = tm
    return pl.pallas_call(
        functools.partial(_merge_kernel, per_seq=per_seq),
        grid=(m // tm, D_MODEL // tn),
        in_specs=[
            pl.BlockSpec((tm, D_CONV), lambda i, j: (i, COL_B // D_CONV)),
            pl.BlockSpec((tm, D_CONV), lambda i, j: (i, COL_C // D_CONV)),
            pl.BlockSpec((tm, D_CONV), lambda i, j: (i, COL_H // D_CONV)),
            pc_spec,
            ph_spec,
            pl.BlockSpec((3, D_CONV), lambda i, j: (0, 0)),
            pl.BlockSpec((tm, ATT_OUT_WIDTH), lambda i, j: (i, 0)),
            pl.BlockSpec((tm, tn), lambda i, j: (i, COL_GC // tn + j)),
            pl.BlockSpec((tm, tn), lambda i, j: (i, COL_GA // tn + j)),
            pl.BlockSpec((D_CONV, tn), lambda i, j: (0, j)),
            pl.BlockSpec((ATT_OUT_WIDTH, tn), lambda i, j: (0, j)),
        ],
        out_specs=[
            pl.BlockSpec((tm, tn), lambda i, j: (i, j)),
            pl.BlockSpec((u_rows, D_CONV), lambda i, j: (i, 0)),
        ],
        out_shape=[
            jax.ShapeDtypeStruct((m, D_MODEL), BF16),
            jax.ShapeDtypeStruct((m // tm * u_rows, D_CONV), F32),
        ],
        scratch_shapes=[pltpu.VMEM((tm, D_CONV), F32)],
        compiler_params=_params(("parallel", "arbitrary")),
    )(z, z, z, pc_arg, ph_arg, conv_w, o_att, z, z, wc, wa)


def _outproj_router_kernel(xp_ref, mgp_ref, xs_ref, mgs_ref, wo_ref, g2_ref, wr_hi_ref, wr_lo_ref,
                           br_ref, x1_ref, x1t_ref, route_ref):
    is_sample = pl.program_id(0) == pl.num_programs(0) - 1

    @pl.when(jnp.logical_not(is_sample))
    def _():
        _outproj_router_tile(xp_ref, mgp_ref, wo_ref, g2_ref, wr_hi_ref, wr_lo_ref, br_ref,
                             x1_ref, x1t_ref, route_ref)

    @pl.when(is_sample)
    def _():
        _outproj_router_tile(xs_ref, mgs_ref, wo_ref, g2_ref, wr_hi_ref, wr_lo_ref, br_ref,
                             x1_ref, x1t_ref, route_ref)


def _outproj_router_tile(x_ref, mg_ref, wo_ref, g2_ref, wr_hi_ref, wr_lo_ref, br_ref, x1_ref,
                         x1t_ref, route_ref):
    x1 = x_ref[...] + _dot(mg_ref[...], wo_ref[...])
    x1_ref[...] = x1
    _to_slabs(x1t_ref, x1)
    h = _rms(x1, g2_ref[...])
    h_hi = h.astype(BF16)
    h_lo = (h - h_hi.astype(F32)).astype(BF16)
    logits = (_dot(h_hi, wr_hi_ref[...]) + _dot(h_hi, wr_lo_ref[...]) + _dot(h_lo, wr_hi_ref[...])
              + br_ref[...])
    lane = lax.broadcasted_iota(jnp.int32, logits.shape, 1)
    is_coarse = lane < N_EXPERT_GROUPS
    coarse = jnp.where(is_coarse, logits, NEG)
    cmax = jnp.max(coarse, axis=-1, keepdims=True)
    grp = jnp.min(jnp.where(coarse == cmax, lane, LANE), axis=-1, keepdims=True)
    p_grp = 1.0 / jnp.sum(jnp.where(is_coarse, jnp.exp(coarse - cmax), 0.0), axis=-1, keepdims=True)
    eid = lane - N_EXPERT_GROUPS
    in_grp = (eid >= 0) & (eid < N_EXPERTS) & ((eid >> 3) == grp)
    fine = jnp.where(in_grp, logits, NEG)
    v1 = jnp.max(fine, axis=-1, keepdims=True)
    i1 = jnp.min(jnp.where(fine == v1, lane, LANE), axis=-1, keepdims=True)
    fine2 = jnp.where(lane == i1, NEG, fine)
    v2 = jnp.max(fine2, axis=-1, keepdims=True)
    i2 = jnp.min(jnp.where(fine2 == v2, lane, LANE), axis=-1, keepdims=True)
    e = jnp.exp(v2 - v1)
    gate1 = p_grp / (1.0 + e)
    gate2 = p_grp * e / (1.0 + e)
    route = jnp.where(lane == 0, (i1 - N_EXPERT_GROUPS).astype(F32),
                      jnp.where(lane == 1, (i2 - N_EXPERT_GROUPS).astype(F32),
                                jnp.where(lane == 2, gate1, jnp.where(lane == 3, gate2, 0.0))))
    route_ref[...] = route


def _outproj_router(xp, mg_p, xs, mg_s, wo, norm2_g, wr_hi, wr_lo, br):
    m_p, tm = xp.shape[0], xs.shape[0]
    assert m_p % tm == 0
    n_p = m_p // tm
    prompt_blk = lambda i: (jnp.minimum(i, n_p - 1), 0)
    fixed = lambda i: (0, 0)
    return pl.pallas_call(
        _outproj_router_kernel,
        grid=(n_p + 1,),
        in_specs=[
            pl.BlockSpec((tm, D_MODEL), prompt_blk),
            pl.BlockSpec((tm, D_MODEL), prompt_blk),
            pl.BlockSpec((tm, D_MODEL), fixed),
            pl.BlockSpec((tm, D_MODEL), fixed),
            pl.BlockSpec((D_MODEL, D_MODEL), fixed),
            pl.BlockSpec((1, D_MODEL), fixed),
            pl.BlockSpec((D_MODEL, LANE), fixed),
            pl.BlockSpec((D_MODEL, LANE), fixed),
            pl.BlockSpec((1, LANE), fixed),
        ],
        out_specs=[
            pl.BlockSpec((tm, D_MODEL), lambda i: (i, 0)),
            pl.BlockSpec((tm * TOK_ROWS, LANE), lambda i: (i, 0)),
            pl.BlockSpec((tm, LANE), lambda i: (i, 0)),
        ],
        out_shape=[
            jax.ShapeDtypeStruct((m_p + tm, D_MODEL), F32),
            jax.ShapeDtypeStruct(((m_p + tm) * TOK_ROWS, LANE), F32),
            jax.ShapeDtypeStruct((m_p + tm, LANE), F32),
        ],
        compiler_params=_params(("arbitrary",)),
    )(xp, mg_p, xs, mg_s, wo, norm2_g.reshape(1, D_MODEL), wr_hi, wr_lo, br)


def _to_slabs(dst_ref, x):
    rows = x.shape[0]
    for s in range(TOK_ROWS):
        dst_ref[pl.ds(s, rows, stride=TOK_ROWS), :] = x[:, s * LANE:(s + 1) * LANE]


def _from_slabs(src_ref, rows):
    return jnp.concatenate(
        [src_ref[pl.ds(s, rows, stride=TOK_ROWS), :] for s in range(TOK_ROWS)], axis=1)


def _moe_kernel(blk0_ref, nblk_ref, nused_ref, tok_ref, x1t_hbm, g2_ref, wg_ref, wu_ref, wd_ref,
                yt_hbm, xbuf, ybuf, gsem, osem, wg_b, wu_b, wd_b, *, n_blocks):
    e = pl.program_id(0)
    n_used = nused_ref[0]
    slab = MOE_TM * TOK_ROWS

    def gather_copy(tok, r, to_slot):
        return pltpu.make_async_copy(
            x1t_hbm.at[pl.ds(pl.multiple_of(tok * TOK_ROWS, TOK_ROWS), TOK_ROWS), :],
            xbuf.at[to_slot, pl.ds(pl.multiple_of(r * TOK_ROWS, TOK_ROWS), TOK_ROWS), :],
            gsem.at[to_slot])

    def start_gather(blk, to_slot):
        def body(r, carry):
            gather_copy(tok_ref[blk * MOE_TM + r], r, to_slot).start()
            return carry
        lax.fori_loop(0, MOE_TM, body, 0, unroll=8)

    def out_copy(blk, from_slot):
        return pltpu.make_async_copy(
            ybuf.at[from_slot], yt_hbm.at[pl.ds(pl.multiple_of(blk * slab, slab), slab), :],
            osem.at[from_slot])

    wg_b[...] = wg_ref[0].astype(BF16)
    wu_b[...] = wu_ref[0].astype(BF16)
    wd_b[...] = wd_ref[0].astype(BF16)

    @pl.when(e == 0)
    def _():
        start_gather(0, 0)

    def block(b, carry):
        blk = blk0_ref[e] + b
        slot = blk & 1

        @pl.when(blk + 1 < n_used)
        def _():
            start_gather(blk + 1, 1 - slot)

        pltpu.make_async_copy(x1t_hbm.at[pl.ds(0, slab), :], xbuf.at[slot], gsem.at[slot]).wait()
        h = _rms(_from_slabs(xbuf.at[slot], MOE_TM), g2_ref[...]).astype(BF16)
        a = _dot(h, wg_b[...])
        u = _dot(h, wu_b[...])
        hm = (a * jax.nn.sigmoid(a) * u).astype(BF16)
        y = _dot(hm, wd_b[...])

        @pl.when(blk >= 2)
        def _():
            out_copy(blk - 2, slot).wait()

        _to_slabs(ybuf.at[slot], y)
        out_copy(blk, slot).start()
        return carry

    lax.fori_loop(0, nblk_ref[e], block, 0)

    @pl.when(e == pl.num_programs(0) - 1)
    def _():
        @pl.when(n_used >= 2)
        def _():
            out_copy(n_used - 2, (n_used - 2) & 1).wait()
        out_copy(n_used - 1, (n_used - 1) & 1).wait()
        ybuf[0] = jnp.zeros((slab, LANE), F32)

        def fill(blk, carry):
            out_copy(blk, 0).start()
            out_copy(blk, 0).wait()
            return carry
        lax.fori_loop(n_used, n_blocks, fill, 0)


def _moe(x1t, norm2_g, w_gate_e, w_up_e, w_down_e, blk_start, nblk, n_used, slot_tok, n_blocks):
    w_idx = lambda e, *_: (e, 0, 0)
    slab = MOE_TM * TOK_ROWS
    grid_spec = pltpu.PrefetchScalarGridSpec(
        num_scalar_prefetch=4,
        grid=(N_EXPERTS,),
        in_specs=[
            pl.BlockSpec(memory_space=pl.ANY),
            pl.BlockSpec((1, D_MODEL), lambda e, *_: (0, 0)),
            pl.BlockSpec((1, D_MODEL, D_EXPERT), w_idx),
            pl.BlockSpec((1, D_MODEL, D_EXPERT), w_idx),
            pl.BlockSpec((1, D_EXPERT, D_MODEL), w_idx),
        ],
        out_specs=pl.BlockSpec(memory_space=pl.ANY),
        scratch_shapes=[
            pltpu.VMEM((2, slab, LANE), F32),
            pltpu.VMEM((2, slab, LANE), F32),
            pltpu.SemaphoreType.DMA((2,)),
            pltpu.SemaphoreType.DMA((2,)),
            pltpu.VMEM((D_MODEL, D_EXPERT), BF16),
            pltpu.VMEM((D_MODEL, D_EXPERT), BF16),
            pltpu.VMEM((D_EXPERT, D_MODEL), BF16),
        ],
    )
    return pl.pallas_call(
        functools.partial(_moe_kernel, n_blocks=n_blocks),
        grid_spec=grid_spec,
        out_shape=jax.ShapeDtypeStruct((n_blocks * slab, LANE), F32),
        compiler_params=_params(("arbitrary",)),
    )(blk_start, nblk, n_used, slot_tok, x1t, norm2_g.reshape(1, D_MODEL), w_gate_e, w_up_e, w_down_e)


def _combine_ple_kernel(slots_ref, x1_ref, route_ref, yt_hbm, p_ref, g3_ref, wpg_ref, wple_ref,
                        out_ref, ybuf, sem, *, tm, row0):
    i = pl.program_id(0)
    n_steps = pl.num_programs(0)
    slot = i & 1
    slab = tm * TOK_ROWS

    def start_gather(step, to_slot):
        base = (row0 + step * tm) * TOP_K

        def body(r, carry):
            for k in range(TOP_K):
                src = slots_ref[base + r * TOP_K + k]
                pltpu.make_async_copy(
                    yt_hbm.at[pl.ds(pl.multiple_of(src * TOK_ROWS, TOK_ROWS), TOK_ROWS), :],
                    ybuf.at[to_slot, k, pl.ds(pl.multiple_of(r * TOK_ROWS, TOK_ROWS), TOK_ROWS), :],
                    sem.at[to_slot]).start()
            return carry
        lax.fori_loop(0, tm, body, 0, unroll=4)

    @pl.when(i == 0)
    def _():
        start_gather(0, 0)

    @pl.when(i + 1 < n_steps)
    def _():
        start_gather(i + 1, 1 - slot)

    for k in range(TOP_K):
        pltpu.make_async_copy(yt_hbm.at[pl.ds(0, slab), :], ybuf.at[slot, k], sem.at[slot]).wait()
    route = route_ref[...]
    x2 = x1_ref[...] + (route[:, 2:3] * _from_slabs(ybuf.at[slot, 0], tm)
                        + route[:, 3:4] * _from_slabs(ybuf.at[slot, 1], tm))
    h = _rms(x2, g3_ref[...]).astype(BF16)
    gate = jax.nn.sigmoid(_dot(h, wpg_ref[...]))
    out_ref[...] = x2 + gate * _dot(p_ref[...].astype(BF16), wple_ref[...])


def _combine_ple(x1_all, route_all, y_slots, tok_slots, p, norm3_g, wpg, wple, m, tm, row0):
    blk0 = row0 // tm
    grid_spec = pltpu.PrefetchScalarGridSpec(
        num_scalar_prefetch=1,
        grid=(m // tm,),
        in_specs=[
            pl.BlockSpec((tm, D_MODEL), lambda i, s: (blk0 + i, 0)),
            pl.BlockSpec((tm, LANE), lambda i, s: (blk0 + i, 0)),
            pl.BlockSpec(memory_space=pl.ANY),
            pl.BlockSpec((tm, PLE_DIM), lambda i, s: (i, 0)),
            pl.BlockSpec((1, D_MODEL), lambda i, s: (0, 0)),
            pl.BlockSpec((D_MODEL, D_MODEL), lambda i, s: (0, 0)),
            pl.BlockSpec((PLE_DIM, D_MODEL), lambda i, s: (0, 0)),
        ],
        out_specs=pl.BlockSpec((tm, D_MODEL), lambda i, s: (i, 0)),
        scratch_shapes=[
            pltpu.VMEM((2, TOP_K, tm * TOK_ROWS, LANE), F32),
            pltpu.SemaphoreType.DMA((2,)),
        ],
    )
    return pl.pallas_call(
        functools.partial(_combine_ple_kernel, tm=tm, row0=row0),
        grid_spec=grid_spec,
        out_shape=jax.ShapeDtypeStruct((m, D_MODEL), F32),
        compiler_params=_params(("arbitrary",)),
    )(tok_slots, x1_all, route_all, y_slots, p, norm3_g.reshape(1, D_MODEL), wpg, wple)


def _routing_tables(route_all, n_blocks):
    n_tok = route_all.shape[0]
    flat_e = route_all[:, 0:TOP_K].astype(jnp.int32).reshape(-1)
    onehot = (flat_e[:, None] == jnp.arange(N_EXPERTS, dtype=jnp.int32)[None, :]).astype(jnp.int32)
    csum = jnp.cumsum(onehot, axis=0)
    rank = jnp.take_along_axis(csum, flat_e[:, None], axis=1)[:, 0] - 1
    counts = csum[-1]
    nblk = (counts + MOE_TM - 1) // MOE_TM
    blk_end = jnp.cumsum(nblk)
    blk_start = blk_end - nblk
    slot = blk_start[flat_e] * MOE_TM + rank
    flat_tok = jnp.arange(n_tok * TOP_K, dtype=jnp.int32) // TOP_K
    slot_tok = jnp.zeros((n_blocks * MOE_TM,), jnp.int32).at[slot].set(flat_tok)
    n_used = blk_end[-1].reshape(1)
    i32 = lambda a: a.astype(jnp.int32)
    return i32(blk_start), i32(nblk), i32(n_used), slot_tok, i32(slot)


def kernel(x_prompt, x_sample, p_prompt, p_sample, cache_k_g0, cache_v_g0, cache_k_g1, cache_v_g1,
           cache_k_g2, cache_v_g2, state_conv, norm1_g, w_in, q_gain, k_gain, conv_w, w_conv_out,
           w_attn_out, w_o, norm2_g, w_coarse, b_coarse, w_fine, b_fine, w_gate_e, w_up_e, w_down_e,
           norm3_g, w_ple_gate, w_ple):
    seq = x_prompt.shape[1]
    assert x_prompt.shape == (1, seq, D_MODEL) and norm1_g.shape[0] == 1, "one prompt, one layer"
    n_seq, t_new, _ = x_sample.shape
    assert seq % ATT_UNIT == 0 and t_new == SUBLANE
    caches_k = (cache_k_g0, cache_k_g1, cache_k_g2)
    caches_v = (cache_v_g0, cache_v_g1, cache_v_g2)
    for g in range(N_GROUPS):
        assert caches_k[g].shape == (1, n_seq, WINDOWS[g], HEADS_PER_GROUP, HEAD_DIM)
    n_s = n_seq * t_new
    n_all = seq + n_s

    xp = x_prompt[0]
    xs = x_sample.reshape(n_s, D_MODEL)
    wc = w_conv_out[0].astype(BF16)
    wa = w_attn_out[0].astype(BF16)
    wo = w_o[0].astype(BF16)
    wpg = w_ple_gate[0].astype(BF16)
    wple = w_ple[0].astype(BF16)
    wr = jnp.concatenate(
        [w_coarse[0], jnp.transpose(w_fine[0], (1, 0, 2)).reshape(D_MODEL, N_EXPERTS)], axis=1)
    wr = jnp.pad(wr, ((0, 0), (0, LANE - wr.shape[1])))
    wr_hi = wr.astype(BF16)
    wr_lo = (wr - wr_hi.astype(F32)).astype(BF16)
    br = jnp.pad(jnp.concatenate([b_coarse[0], b_fine[0].reshape(-1)]), (0, LANE - 36)).reshape(1, LANE)

    z_p = _inproj(xp, norm1_g[0], w_in[0], tm=2048, tn=512)
    z_s = _inproj(xs, norm1_g[0], w_in[0], tm=n_s, tn=512)

    att_p = _prompt_attn(z_p, q_gain[0], k_gain[0])
    o_p, pks, pvs = att_p[0], att_p[1:4], att_p[4:7]
    ck = [c.reshape(n_seq, WINDOWS[g] * HEADS_PER_GROUP, HEAD_DIM) for g, c in enumerate(caches_k)]
    cv = [c.reshape(n_seq, WINDOWS[g] * HEADS_PER_GROUP, HEAD_DIM) for g, c in enumerate(caches_v)]
    att_s = _sample_attn(z_s, ck, cv, q_gain[0], k_gain[0], n_seq, t_new)
    o_s, sks, svs = att_s[0], att_s[1:4], att_s[4:7]

    mg_p, u_tail = _merge(z_p, o_p, conv_w[0], wc, wa, tm=512, tn=512)
    prefix = jnp.pad(state_conv[0], ((0, 0), (0, t_new - 2), (0, 0))).reshape(n_s, D_CONV)
    mg_s, u_s = _merge(z_s, o_s, conv_w[0], wc, wa, tm=n_s, tn=512, per_seq=t_new, prefix=prefix)
    x1_all, x1t_all, route_all = _outproj_router(xp, mg_p, xs, mg_s, wo, norm2_g[0], wr_hi, wr_lo, br)

    n_blocks = n_all * TOP_K // MOE_TM + N_EXPERTS
    blk_start, nblk, n_used, slot_tok, tok_slots = _routing_tables(route_all, n_blocks)
    y_slots = _moe(x1t_all, norm2_g[0], w_gate_e[0], w_up_e[0], w_down_e[0], blk_start, nblk, n_used,
                   slot_tok, n_blocks)

    y_p = _combine_ple(x1_all, route_all, y_slots, tok_slots, p_prompt[0, 0], norm3_g[0], wpg, wple,
                       m=seq, tm=256, row0=0)
    y_s = _combine_ple(x1_all, route_all, y_slots, tok_slots, p_sample[0].reshape(n_s, PLE_DIM),
                       norm3_g[0], wpg, wple, m=n_s, tm=n_s, row0=seq)

    def state(a, n, g):
        return a.reshape(1, n, WINDOWS[g], HEADS_PER_GROUP, HEAD_DIM)

    pk = [state(a, 1, g) for g, a in enumerate(pks)]
    pv = [state(a, 1, g) for g, a in enumerate(pvs)]
    sk = [state(a, n_seq, g) for g, a in enumerate(sks)]
    sv = [state(a, n_seq, g) for g, a in enumerate(svs)]
    pconv = u_tail[-2:].reshape(1, 1, 2, D_CONV)
    sconv = u_s.reshape(n_seq, t_new, D_CONV)[:, t_new - 2:].reshape(1, n_seq, 2, D_CONV)
    return (y_p.reshape(1, seq, D_MODEL), y_s.reshape(n_seq, t_new, D_MODEL),
            pk[0], pv[0], pk[1], pv[1], pk[2], pv[2], pconv,
            sk[0], sv[0], sk[1], sv[1], sk[2], sv[2], sconv)
```

```python
import functools

import jax
import jax.numpy as jnp
from jax import lax
from jax.experimental import pallas as pl
from jax.experimental.pallas import tpu as pltpu

D_MODEL = 2048
HEAD_DIM = 128
HEADS_PER_GROUP = 4
WINDOWS = (128, 512, 2048)
DILATIONS = (1, 4, 16)
N_GROUPS = 3
N_HEADS = N_GROUPS * HEADS_PER_GROUP
ATT_WIDTH = N_HEADS * HEAD_DIM
ATT_OUT_WIDTH = HEADS_PER_GROUP * HEAD_DIM
SCALE = HEAD_DIM ** -0.5
D_CONV = D_MODEL // 2
PLE_DIM = 256
N_EXPERT_GROUPS = 4
EXPERTS_PER_GROUP = 8
N_EXPERTS = N_EXPERT_GROUPS * EXPERTS_PER_GROUP
TOP_K = 2
D_EXPERT = D_MODEL // 4
EPS = 1e-6

COL_B = 0
COL_C = D_CONV
COL_H = 2 * D_CONV
COL_Q = 3 * D_CONV
COL_K = COL_Q + ATT_WIDTH
COL_V = COL_K + ATT_WIDTH
COL_GC = COL_V + ATT_WIDTH
COL_GA = COL_GC + D_MODEL
IN_COLS = COL_GA + D_MODEL

LANE = 128
SUBLANE = 8
NEG = -1e30
ATT_UNIT = max(WINDOWS)
ATT_BLK = 128
ATT_TILE_UNROLL = 8
MXU_N = 256
TOK_ROWS = D_MODEL // LANE
MOE_TM = 256
VMEM_LIMIT = 56 * 1024 * 1024

BF16 = jnp.bfloat16
F32 = jnp.float32


def _params(sem):
    return pltpu.CompilerParams(dimension_semantics=sem, vmem_limit_bytes=VMEM_LIMIT)


def _rms(x, gain):
    return x * lax.rsqrt(jnp.mean(x * x, axis=-1, keepdims=True) + EPS) * gain


def _dot(a, b):
    return jnp.dot(a, b, preferred_element_type=F32)


def _dot_nt(a, b):
    return lax.dot_general(a, b, (((1,), (1,)), ((), ())), preferred_element_type=F32)


def _inproj_kernel(x_ref, g_ref, w_ref, z_ref, h_scr):
    @pl.when(pl.program_id(1) == 0)
    def _():
        h_scr[...] = _rms(x_ref[...], g_ref[...]).astype(BF16)

    for c in range(w_ref.shape[1] // MXU_N):
        cols = slice(c * MXU_N, (c + 1) * MXU_N)
        z_ref[:, cols] = _dot(h_scr[...], w_ref[:, cols].astype(BF16))


def _inproj(x, norm_g, w_in, tm, tn):
    m = x.shape[0]
    return pl.pallas_call(
        _inproj_kernel,
        grid=(m // tm, IN_COLS // tn),
        in_specs=[
            pl.BlockSpec((tm, D_MODEL), lambda i, j: (i, 0), pipeline_mode=pl.Buffered(1)),
            pl.BlockSpec((1, D_MODEL), lambda i, j: (0, 0)),
            pl.BlockSpec((D_MODEL, tn), lambda i, j: (0, j)),
        ],
        out_specs=pl.BlockSpec((tm, tn), lambda i, j: (i, j)),
        out_shape=jax.ShapeDtypeStruct((m, IN_COLS), F32),
        scratch_shapes=[pltpu.VMEM((tm, D_MODEL), BF16)],
        compiler_params=_params(("parallel", "arbitrary")),
    )(x, norm_g.reshape(1, D_MODEL), w_in)


def _prompt_attn_kernel(*refs):
    (q0, q1, q2, kc0, kc1, kc2, vc0, vc1, vc2, kp0, kp1, kp2, vp0, vp1, vp2, qg_ref, kg_ref,
     o_ref, pk0, pk1, pk2, pv0, pv1, pv2, kext, vext, acc_s, m_s, l_s) = refs
    q_refs, kc_refs, vc_refs = (q0, q1, q2), (kc0, kc1, kc2), (vc0, vc1, vc2)
    kp_refs, vp_refs = (kp0, kp1, kp2), (vp0, vp1, vp2)
    pk_refs, pv_refs = (pk0, pk1, pk2), (pv0, pv1, pv2)
    n = pl.program_id(1)
    last = pl.num_programs(1) - 1

    qi = lax.broadcasted_iota(jnp.int32, (ATT_BLK, 2 * ATT_BLK), 0) + ATT_BLK
    ki = lax.broadcasted_iota(jnp.int32, (ATT_BLK, 2 * ATT_BLK), 1)
    dist = qi - ki
    band = (dist >= 0) & (dist <= ATT_BLK)

    for g in range(N_GROUPS):
        win, dil = WINDOWS[g], DILATIONS[g]
        q_ref = q_refs[g]
        qgain = qg_ref[g, 0]
        kgain = kg_ref[g, 0]
        kext[0:win, :] = _rms(kp_refs[g][...], kgain)
        kext[win:win + ATT_UNIT, :] = _rms(kc_refs[g][...], kgain)
        vext[0:win, :] = vp_refs[g][...]
        vext[win:win + ATT_UNIT, :] = vc_refs[g][...]

        @pl.when(n == last)
        def _():
            pk_refs[g][...] = kext[ATT_UNIT:ATT_UNIT + win, :]
            pv_refs[g][...] = vext[ATT_UNIT:ATT_UNIT + win, :]

        def tile(t, carry, g=g, win=win, dil=dil, q_ref=q_ref, qgain=qgain):
            u = t // dil
            r = t % dil
            off = u * win + r
            if dil == 1:
                off = pl.multiple_of(off, ATT_BLK)
                rows_q = pl.ds(off, ATT_BLK)
                rows_k = pl.ds(off, 2 * ATT_BLK)
            else:
                rows_q = pl.ds(off, ATT_BLK, stride=dil)
                rows_k = pl.ds(off, 2 * ATT_BLK, stride=dil)
            q = _rms(q_ref[rows_q, :], qgain).astype(BF16)
            k = kext[rows_k, :].astype(BF16)
            v = vext[rows_k, :].astype(BF16)
            s = _dot_nt(q, k) * SCALE
            k_min = jnp.where((n > 0) | (u > 0), 0, ATT_BLK)
            s = jnp.where(band & (ki >= k_min), s, NEG)
            m_t = jnp.max(s, axis=-1, keepdims=True)
            p = jnp.exp(s - m_t)
            l_t = jnp.sum(p, axis=-1, keepdims=True)
            m_s[g, rows_q, :] = jnp.broadcast_to(m_t, (ATT_BLK, LANE))
            l_s[g, rows_q, :] = jnp.broadcast_to(l_t, (ATT_BLK, LANE))
            acc_s[g, rows_q, :] = _dot(p.astype(BF16), v)
            return carry

        lax.fori_loop(0, ATT_UNIT // ATT_BLK, tile, 0, unroll=ATT_TILE_UNROLL)

    def finish(c, carry):
        rows = pl.ds(pl.multiple_of(c * ATT_BLK, ATT_BLK), ATT_BLK)
        m = [m_s[g, rows, :] for g in range(N_GROUPS)]
        m_all = jnp.maximum(jnp.maximum(m[0], m[1]), m[2])
        num = jnp.zeros((ATT_BLK, LANE), F32)
        den = jnp.zeros((ATT_BLK, LANE), F32)
        for g in range(N_GROUPS):
            w = jnp.exp(m[g] - m_all)
            num = num + w * acc_s[g, rows, :]
            den = den + w * l_s[g, rows, :]
        o_ref[rows, :] = num / den
        return carry

    lax.fori_loop(0, ATT_UNIT // ATT_BLK, finish, 0, unroll=2)


def _prompt_attn(z, q_gain, k_gain):
    s_len = z.shape[0]
    n_units = s_len // ATT_UNIT
    in_specs, args = [], []

    def col(base, g):
        return lambda hh, n: (n, base // LANE + g * HEADS_PER_GROUP + hh)

    for g in range(N_GROUPS):
        in_specs.append(pl.BlockSpec((ATT_UNIT, LANE), col(COL_Q, g)))
    for base in (COL_K, COL_V):
        for g in range(N_GROUPS):
            in_specs.append(pl.BlockSpec((ATT_UNIT, LANE), col(base, g)))
    for base in (COL_K, COL_V):
        for g in range(N_GROUPS):
            per = ATT_UNIT // WINDOWS[g]
            in_specs.append(pl.BlockSpec(
                (WINDOWS[g], LANE),
                lambda hh, n, base=base, g=g, per=per: (
                    jnp.maximum(n * per - 1, 0), base // LANE + g * HEADS_PER_GROUP + hh)))
    args = [z] * 15
    gain_spec = pl.BlockSpec((N_GROUPS, 1, 1, LANE), lambda hh, n: (0, hh, 0, 0))
    in_specs += [gain_spec, gain_spec]
    args += [q_gain.reshape(N_GROUPS, HEADS_PER_GROUP, 1, LANE),
             k_gain.reshape(N_GROUPS, HEADS_PER_GROUP, 1, LANE)]

    out_specs = [pl.BlockSpec((ATT_UNIT, LANE), lambda hh, n: (n, hh))]
    out_shape = [jax.ShapeDtypeStruct((s_len, ATT_OUT_WIDTH), F32)]
    for _ in range(2):
        for g in range(N_GROUPS):
            out_specs.append(pl.BlockSpec((WINDOWS[g], LANE), lambda hh, n: (0, hh)))
            out_shape.append(jax.ShapeDtypeStruct((WINDOWS[g], ATT_OUT_WIDTH), F32))

    return pl.pallas_call(
        _prompt_attn_kernel,
        grid=(HEADS_PER_GROUP, n_units),
        in_specs=in_specs,
        out_specs=out_specs,
        out_shape=out_shape,
        scratch_shapes=[
            pltpu.VMEM((2 * ATT_UNIT, LANE), F32),
            pltpu.VMEM((2 * ATT_UNIT, LANE), F32),
            pltpu.VMEM((N_GROUPS, ATT_UNIT, LANE), F32),
            pltpu.VMEM((N_GROUPS, ATT_UNIT, LANE), F32),
            pltpu.VMEM((N_GROUPS, ATT_UNIT, LANE), F32),
        ],
        compiler_params=_params(("parallel", "arbitrary")),
    )(*args)


def _sample_attn_kernel(*refs):
    (q0, q1, q2, kn0, kn1, kn2, vn0, vn1, vn2, ck0, ck1, ck2, cv0, cv1, cv2, qg_ref, kg_ref,
     o_ref, sk0, sk1, sk2, sv0, sv1, sv2) = refs
    q_refs, kn_refs, vn_refs = (q0, q1, q2), (kn0, kn1, kn2), (vn0, vn1, vn2)
    ck_refs, cv_refs = (ck0, ck1, ck2), (cv0, cv1, cv2)
    sk_refs, sv_refs = (sk0, sk1, sk2), (sv0, sv1, sv2)
    t_new = q0.shape[0]
    nh = HEADS_PER_GROUP
    n_q = nh * t_new
    t_bits = t_new.bit_length() - 1

    def heads(ref):
        return [ref[:, h * LANE:(h + 1) * LANE] for h in range(nh)]

    parts = []
    for g in range(N_GROUPS):
        win, dil = WINDOWS[g], DILATIONS[g]
        q_all = jnp.concatenate(
            [_rms(x, qg_ref[g, h]) for h, x in enumerate(heads(q_refs[g]))], axis=0).astype(BF16)
        k_new = [_rms(x, kg_ref[g, h]) for h, x in enumerate(heads(kn_refs[g]))]
        v_new = heads(vn_refs[g])
        k_new_all = jnp.concatenate(k_new, axis=0).astype(BF16)
        v_new_all = jnp.concatenate(v_new, axis=0).astype(BF16)
        ck = ck_refs[g][0]
        cv = cv_refs[g][0]
        old = (win - t_new) * nh
        sk_refs[g][0, 0:old, :] = ck[t_new * nh:, :]
        sv_refs[g][0, 0:old, :] = cv[t_new * nh:, :]
        for h in range(nh):
            sk_refs[g][0, pl.ds(old + h, t_new, stride=nh), :] = k_new[h]
            sv_refs[g][0, pl.ds(old + h, t_new, stride=nh), :] = v_new[h]

        s_c = _dot_nt(q_all, ck.astype(BF16)) * SCALE
        s_n = _dot_nt(q_all, k_new_all) * SCALE
        row = lax.broadcasted_iota(jnp.int32, (n_q, win * nh), 0)
        col = lax.broadcasted_iota(jnp.int32, (n_q, win * nh), 1)
        i_q, pos = row & (t_new - 1), col >> 2
        mask_c = (((col & (nh - 1)) == (row >> t_bits)) & (pos >= i_q)
                  & (((pos - i_q) & (dil - 1)) == 0))
        row_n = lax.broadcasted_iota(jnp.int32, (n_q, n_q), 0)
        col_n = lax.broadcasted_iota(jnp.int32, (n_q, n_q), 1)
        d_n = (row_n & (t_new - 1)) - (col_n & (t_new - 1))
        mask_n = ((row_n >> t_bits) == (col_n >> t_bits)) & (d_n >= 0) & ((d_n & (dil - 1)) == 0)
        s_c = jnp.where(mask_c, s_c, NEG)
        s_n = jnp.where(mask_n, s_n, NEG)
        m = jnp.maximum(jnp.max(s_c, axis=-1, keepdims=True), jnp.max(s_n, axis=-1, keepdims=True))
        p_c = jnp.exp(s_c - m)
        p_n = jnp.exp(s_n - m)
        l = jnp.sum(p_c, axis=-1, keepdims=True) + jnp.sum(p_n, axis=-1, keepdims=True)
        acc = _dot(p_c.astype(BF16), cv.astype(BF16)) + _dot(p_n.astype(BF16), v_new_all)
        parts.append((m, l, acc))

    m_all = jnp.maximum(jnp.maximum(parts[0][0], parts[1][0]), parts[2][0])
    num = jnp.zeros((n_q, LANE), F32)
    den = jnp.zeros((n_q, 1), F32)
    for m, l, acc in parts:
        w = jnp.exp(m - m_all)
        num = num + w * acc
        den = den + w * l
    o_all = num / den
    for h in range(nh):
        o_ref[:, h * LANE:(h + 1) * LANE] = o_all[h * t_new:(h + 1) * t_new, :]


def _sample_attn(z, caches_k, caches_v, q_gain, k_gain, n_seq, t_new):
    assert HEADS_PER_GROUP == 4 and t_new & (t_new - 1) == 0
    in_specs, args = [], []
    for base in (COL_Q, COL_K, COL_V):
        for g in range(N_GROUPS):
            in_specs.append(pl.BlockSpec(
                (t_new, ATT_OUT_WIDTH), lambda b, base=base, g=g: (b, base // ATT_OUT_WIDTH + g)))
            args.append(z)
    for caches in (caches_k, caches_v):
        for g in range(N_GROUPS):
            in_specs.append(pl.BlockSpec((1, WINDOWS[g] * HEADS_PER_GROUP, LANE), lambda b: (b, 0, 0)))
            args.append(caches[g])
    gain_spec = pl.BlockSpec((N_GROUPS, HEADS_PER_GROUP, 1, LANE), lambda b: (0, 0, 0, 0))
    in_specs += [gain_spec, gain_spec]
    args += [q_gain.reshape(N_GROUPS, HEADS_PER_GROUP, 1, LANE),
             k_gain.reshape(N_GROUPS, HEADS_PER_GROUP, 1, LANE)]

    out_specs = [pl.BlockSpec((t_new, ATT_OUT_WIDTH), lambda b: (b, 0))]
    out_shape = [jax.ShapeDtypeStruct((n_seq * t_new, ATT_OUT_WIDTH), F32)]
    for _ in range(2):
        for g in range(N_GROUPS):
            rows = WINDOWS[g] * HEADS_PER_GROUP
            out_specs.append(pl.BlockSpec((1, rows, LANE), lambda b: (b, 0, 0)))
            out_shape.append(jax.ShapeDtypeStruct((n_seq, rows, LANE), F32))

    return pl.pallas_call(
        _sample_attn_kernel,
        grid=(n_seq,),
        in_specs=in_specs,
        out_specs=out_specs,
        out_shape=out_shape,
        compiler_params=_params(("parallel",)),
    )(*args)


def _conv_taps(u, r1, r2, cw_ref):
    return cw_ref[2:3, :] * u + cw_ref[1:2, :] * r1 + cw_ref[0:1, :] * r2


def _merge_kernel(*refs, per_seq, n_chunks):
    b_ref, c_ref, h_ref, pc_ref, ph_ref, cw_ref, o_ref = refs[:7]
    gc_refs = refs[7:7 + n_chunks]
    ga_refs = refs[7 + n_chunks:7 + 2 * n_chunks]
    wc_ref, wa_ref, out_ref, u_ref, yb_scr = refs[7 + 2 * n_chunks:]
    i = pl.program_id(0)
    tm = b_ref.shape[0]
    u = c_ref[...] * h_ref[...]
    r1 = pltpu.roll(u, 1, axis=0)
    r2 = pltpu.roll(u, 2, axis=0)
    if per_seq is None:
        yb_scr[...] = b_ref[...] * _conv_taps(u, r1, r2, cw_ref)
        up = jnp.where(i > 0, pc_ref[...] * ph_ref[...], 0.0)
        row = lax.broadcasted_iota(jnp.int32, (SUBLANE, D_CONV), 0)
        u8 = u[0:SUBLANE, :]
        r1_8 = jnp.where(row < 1, pltpu.roll(up, 1, axis=0), pltpu.roll(u8, 1, axis=0))
        r2_8 = jnp.where(row < 2, pltpu.roll(up, 2, axis=0), pltpu.roll(u8, 2, axis=0))
        yb_scr[0:SUBLANE, :] = b_ref[0:SUBLANE, :] * _conv_taps(u8, r1_8, r2_8, cw_ref)
        u_ref[...] = u[tm - SUBLANE:tm, :]
    else:
        pre = pc_ref[...]
        t = lax.broadcasted_iota(jnp.int32, (tm, D_CONV), 0) & (per_seq - 1)
        r1 = jnp.where(t == 0, pltpu.roll(pre, tm - 1, axis=0), r1)
        r2 = jnp.where(t < 2, pre, r2)
        yb_scr[...] = b_ref[...] * _conv_taps(u, r1, r2, cw_ref)
        u_ref[...] = u

    yb = yb_scr[...].astype(BF16)
    ob = o_ref[...].astype(BF16)
    tn = D_MODEL // n_chunks
    for c in range(n_chunks):
        cols = slice(c * tn, (c + 1) * tn)
        y_conv = _dot(yb, wc_ref[:, cols])
        y_att = _dot(ob, wa_ref[:, cols])
        out_ref[:, cols] = (jax.nn.sigmoid(gc_refs[c][...]) * y_conv
                            + jax.nn.sigmoid(ga_refs[c][...]) * y_att).astype(BF16)


def _merge(z, o_att, conv_w, wc, wa, tm, per_seq=None, prefix=None):
    m = z.shape[0]
    tn = ATT_OUT_WIDTH
    n_chunks = D_MODEL // tn
    if per_seq is None:
        prev = lambda i, c: (jnp.maximum(i * (tm // SUBLANE) - 1, 0), c)
        pc_spec = pl.BlockSpec((SUBLANE, D_CONV), lambda i: prev(i, COL_C // D_CONV))
        ph_spec = pl.BlockSpec((SUBLANE, D_CONV), lambda i: prev(i, COL_H // D_CONV))
        pc_arg, ph_arg = z, z
        u_rows = SUBLANE
    else:
        pc_spec = pl.BlockSpec((tm, D_CONV), lambda i: (i, 0))
        ph_spec = pl.BlockSpec((SUBLANE, D_CONV), lambda i: (0, 0))
        pc_arg, ph_arg = prefix, prefix
        u_rows = tm
    gate_specs = [pl.BlockSpec((tm, tn), lambda i, base=base, c=c: (i, base // tn + c))
                  for base in (COL_GC, COL_GA) for c in range(n_chunks)]
    return pl.pallas_call(
        functools.partial(_merge_kernel, per_seq=per_seq, n_chunks=n_chunks),
        grid=(m // tm,),
        in_specs=[
            pl.BlockSpec((tm, D_CONV), lambda i: (i, COL_B // D_CONV)),
            pl.BlockSpec((tm, D_CONV), lambda i: (i, COL_C // D_CONV)),
            pl.BlockSpec((tm, D_CONV), lambda i: (i, COL_H // D_CONV)),
            pc_spec,
            ph_spec,
            pl.BlockSpec((3, D_CONV), lambda i: (0, 0)),
            pl.BlockSpec((tm, ATT_OUT_WIDTH), lambda i: (i, 0)),
            *gate_specs,
            pl.BlockSpec((D_CONV, D_MODEL), lambda i: (0, 0)),
            pl.BlockSpec((ATT_OUT_WIDTH, D_MODEL), lambda i: (0, 0)),
        ],
        out_specs=[
            pl.BlockSpec((tm, D_MODEL), lambda i: (i, 0)),
            pl.BlockSpec((u_rows, D_CONV), lambda i: (i, 0)),
        ],
        out_shape=[
            jax.ShapeDtypeStruct((m, D_MODEL), BF16),
            jax.ShapeDtypeStruct((m // tm * u_rows, D_CONV), F32),
        ],
        scratch_shapes=[pltpu.VMEM((tm, D_CONV), F32)],
        compiler_params=_params(("parallel",)),
    )(z, z, z, pc_arg, ph_arg, conv_w, o_att, *([z] * (2 * n_chunks)), wc, wa)


def _outproj_router_kernel(xp_ref, mgp_ref, xs_ref, mgs_ref, wo_ref, g2_ref, wr_hi_ref, wr_lo_ref,
                           br_ref, x1_ref, x1t_ref, route_ref):
    is_sample = pl.program_id(0) == pl.num_programs(0) - 1

    @pl.when(jnp.logical_not(is_sample))
    def _():
        _outproj_router_tile(xp_ref, mgp_ref, wo_ref, g2_ref, wr_hi_ref, wr_lo_ref, br_ref,
                             x1_ref, x1t_ref, route_ref)

    @pl.when(is_sample)
    def _():
        _outproj_router_tile(xs_ref, mgs_ref, wo_ref, g2_ref, wr_hi_ref, wr_lo_ref, br_ref,
                             x1_ref, x1t_ref, route_ref)


def _outproj_router_tile(x_ref, mg_ref, wo_ref, g2_ref, wr_hi_ref, wr_lo_ref, br_ref, x1_ref,
                         x1t_ref, route_ref):
    x1 = x_ref[...] + _dot(mg_ref[...], wo_ref[...])
    x1_ref[...] = x1
    _to_slabs(x1t_ref, x1)
    h = _rms(x1, g2_ref[...])
    h_hi = h.astype(BF16)
    h_lo = (h - h_hi.astype(F32)).astype(BF16)
    logits = (_dot(h_hi, wr_hi_ref[...]) + _dot(h_hi, wr_lo_ref[...]) + _dot(h_lo, wr_hi_ref[...])
              + br_ref[...])
    lane = lax.broadcasted_iota(jnp.int32, logits.shape, 1)
    is_coarse = lane < N_EXPERT_GROUPS
    coarse = jnp.where(is_coarse, logits, NEG)
    cmax = jnp.max(coarse, axis=-1, keepdims=True)
    grp = jnp.min(jnp.where(coarse == cmax, lane, LANE), axis=-1, keepdims=True)
    p_grp = 1.0 / jnp.sum(jnp.where(is_coarse, jnp.exp(coarse - cmax), 0.0), axis=-1, keepdims=True)
    eid = lane - N_EXPERT_GROUPS
    in_grp = (eid >= 0) & (eid < N_EXPERTS) & ((eid >> 3) == grp)
    fine = jnp.where(in_grp, logits, NEG)
    v1 = jnp.max(fine, axis=-1, keepdims=True)
    i1 = jnp.min(jnp.where(fine == v1, lane, LANE), axis=-1, keepdims=True)
    fine2 = jnp.where(lane == i1, NEG, fine)
    v2 = jnp.max(fine2, axis=-1, keepdims=True)
    i2 = jnp.min(jnp.where(fine2 == v2, lane, LANE), axis=-1, keepdims=True)
    e = jnp.exp(v2 - v1)
    gate1 = p_grp / (1.0 + e)
    gate2 = p_grp * e / (1.0 + e)
    route = jnp.where(lane == 0, (i1 - N_EXPERT_GROUPS).astype(F32),
                      jnp.where(lane == 1, (i2 - N_EXPERT_GROUPS).astype(F32),
                                jnp.where(lane == 2, gate1, jnp.where(lane == 3, gate2, 0.0))))
    route_ref[...] = route


def _outproj_router(xp, mg_p, xs, mg_s, wo, norm2_g, wr_hi, wr_lo, br):
    m_p, tm = xp.shape[0], xs.shape[0]
    assert m_p % tm == 0
    n_p = m_p // tm
    prompt_blk = lambda i: (jnp.minimum(i, n_p - 1), 0)
    fixed = lambda i: (0, 0)
    return pl.pallas_call(
        _outproj_router_kernel,
        grid=(n_p + 1,),
        in_specs=[
            pl.BlockSpec((tm, D_MODEL), prompt_blk),
            pl.BlockSpec((tm, D_MODEL), prompt_blk),
            pl.BlockSpec((tm, D_MODEL), fixed),
            pl.BlockSpec((tm, D_MODEL), fixed),
            pl.BlockSpec((D_MODEL, D_MODEL), fixed),
            pl.BlockSpec((1, D_MODEL), fixed),
            pl.BlockSpec((D_MODEL, LANE), fixed),
            pl.BlockSpec((D_MODEL, LANE), fixed),
            pl.BlockSpec((1, LANE), fixed),
        ],
        out_specs=[
            pl.BlockSpec((tm, D_MODEL), lambda i: (i, 0)),
            pl.BlockSpec((tm * TOK_ROWS, LANE), lambda i: (i, 0)),
            pl.BlockSpec((tm, LANE), lambda i: (i, 0)),
        ],
        out_shape=[
            jax.ShapeDtypeStruct((m_p + tm, D_MODEL), F32),
            jax.ShapeDtypeStruct(((m_p + tm) * TOK_ROWS, LANE), F32),
            jax.ShapeDtypeStruct((m_p + tm, LANE), F32),
        ],
        compiler_params=_params(("arbitrary",)),
    )(xp, mg_p, xs, mg_s, wo, norm2_g.reshape(1, D_MODEL), wr_hi, wr_lo, br)


def _to_slabs(dst_ref, x):
    rows = x.shape[0]
    for s in range(TOK_ROWS):
        dst_ref[pl.ds(s, rows, stride=TOK_ROWS), :] = x[:, s * LANE:(s + 1) * LANE]


def _from_slabs(src_ref, rows):
    return jnp.concatenate(
        [src_ref[pl.ds(s, rows, stride=TOK_ROWS), :] for s in range(TOK_ROWS)], axis=1)


def _moe_kernel(blk0_ref, nblk_ref, nused_ref, tok_ref, x1t_hbm, g2_ref, wg_ref, wu_ref, wd_ref,
                yt_hbm, xbuf, ybuf, gsem, osem, wg_b, wu_b, wd_b, *, n_blocks):
    e = pl.program_id(0)
    n_used = nused_ref[0]
    slab = MOE_TM * TOK_ROWS

    def gather_copy(tok, r, to_slot):
        return pltpu.make_async_copy(
            x1t_hbm.at[pl.ds(pl.multiple_of(tok * TOK_ROWS, TOK_ROWS), TOK_ROWS), :],
            xbuf.at[to_slot, pl.ds(pl.multiple_of(r * TOK_ROWS, TOK_ROWS), TOK_ROWS), :],
            gsem.at[to_slot])

    def start_gather(blk, to_slot):
        def body(r, carry):
            gather_copy(tok_ref[blk * MOE_TM + r], r, to_slot).start(priority=1)
            return carry
        lax.fori_loop(0, MOE_TM, body, 0, unroll=8)

    def out_copy(blk, from_slot):
        return pltpu.make_async_copy(
            ybuf.at[from_slot], yt_hbm.at[pl.ds(pl.multiple_of(blk * slab, slab), slab), :],
            osem.at[from_slot])

    wg_b[...] = wg_ref[0].astype(BF16)
    wu_b[...] = wu_ref[0].astype(BF16)
    wd_b[...] = wd_ref[0].astype(BF16)

    @pl.when(e == 0)
    def _():
        start_gather(0, 0)

    def block(b, carry):
        blk = blk0_ref[e] + b
        slot = blk & 1

        @pl.when(blk + 1 < n_used)
        def _():
            start_gather(blk + 1, 1 - slot)

        pltpu.make_async_copy(x1t_hbm.at[pl.ds(0, slab), :], xbuf.at[slot], gsem.at[slot]).wait()
        h = _rms(_from_slabs(xbuf.at[slot], MOE_TM), g2_ref[...]).astype(BF16)
        a = _dot(h, wg_b[...])
        u = _dot(h, wu_b[...])
        hm = (a * jax.nn.sigmoid(a) * u).astype(BF16)
        y = _dot(hm, wd_b[...])

        @pl.when(blk >= 2)
        def _():
            out_copy(blk - 2, slot).wait()

        _to_slabs(ybuf.at[slot], y)
        out_copy(blk, slot).start()
        return carry

    lax.fori_loop(0, nblk_ref[e], block, 0)

    @pl.when(e == pl.num_programs(0) - 1)
    def _():
        @pl.when(n_used >= 2)
        def _():
            out_copy(n_used - 2, (n_used - 2) & 1).wait()
        out_copy(n_used - 1, (n_used - 1) & 1).wait()
        ybuf[0] = jnp.zeros((slab, LANE), F32)

        def fill(blk, carry):
            out_copy(blk, 0).start()
            out_copy(blk, 0).wait()
            return carry
        lax.fori_loop(n_used, n_blocks, fill, 0)


def _moe(x1t, norm2_g, w_gate_e, w_up_e, w_down_e, blk_start, nblk, n_used, slot_tok, n_blocks):
    w_idx = lambda e, *_: (e, 0, 0)
    slab = MOE_TM * TOK_ROWS
    grid_spec = pltpu.PrefetchScalarGridSpec(
        num_scalar_prefetch=4,
        grid=(N_EXPERTS,),
        in_specs=[
            pl.BlockSpec(memory_space=pl.ANY),
            pl.BlockSpec((1, D_MODEL), lambda e, *_: (0, 0)),
            pl.BlockSpec((1, D_MODEL, D_EXPERT), w_idx),
            pl.BlockSpec((1, D_MODEL, D_EXPERT), w_idx),
            pl.BlockSpec((1, D_EXPERT, D_MODEL), w_idx),
        ],
        out_specs=pl.BlockSpec(memory_space=pl.ANY),
        scratch_shapes=[
            pltpu.VMEM((2, slab, LANE), F32),
            pltpu.VMEM((2, slab, LANE), F32),
            pltpu.SemaphoreType.DMA((2,)),
            pltpu.SemaphoreType.DMA((2,)),
            pltpu.VMEM((D_MODEL, D_EXPERT), BF16),
            pltpu.VMEM((D_MODEL, D_EXPERT), BF16),
            pltpu.VMEM((D_EXPERT, D_MODEL), BF16),
        ],
    )
    return pl.pallas_call(
        functools.partial(_moe_kernel, n_blocks=n_blocks),
        grid_spec=grid_spec,
        out_shape=jax.ShapeDtypeStruct((n_blocks * slab, LANE), F32),
        compiler_params=_params(("arbitrary",)),
    )(blk_start, nblk, n_used, slot_tok, x1t, norm2_g.reshape(1, D_MODEL), w_gate_e, w_up_e, w_down_e)


def _combine_ple_kernel(slots_ref, x1_ref, route_ref, yt_hbm, p_ref, g3_ref, wpg_ref, wple_ref,
                        out_ref, ybuf, sem, *, tm, row0):
    i = pl.program_id(0)
    n_steps = pl.num_programs(0)
    slot = i & 1
    slab = tm * TOK_ROWS

    def start_gather(step, to_slot):
        base = (row0 + step * tm) * TOP_K

        def body(r, carry):
            for k in range(TOP_K):
                src = slots_ref[base + r * TOP_K + k]
                pltpu.make_async_copy(
                    yt_hbm.at[pl.ds(pl.multiple_of(src * TOK_ROWS, TOK_ROWS), TOK_ROWS), :],
                    ybuf.at[to_slot, k, pl.ds(pl.multiple_of(r * TOK_ROWS, TOK_ROWS), TOK_ROWS), :],
                    sem.at[to_slot]).start(priority=k)
            return carry
        lax.fori_loop(0, tm, body, 0, unroll=4)

    @pl.when(i == 0)
    def _():
        start_gather(0, 0)

    @pl.when(i + 1 < n_steps)
    def _():
        start_gather(i + 1, 1 - slot)

    for k in range(TOP_K):
        pltpu.make_async_copy(yt_hbm.at[pl.ds(0, slab), :], ybuf.at[slot, k], sem.at[slot]).wait()
    route = route_ref[...]
    x2 = x1_ref[...] + (route[:, 2:3] * _from_slabs(ybuf.at[slot, 0], tm)
                        + route[:, 3:4] * _from_slabs(ybuf.at[slot, 1], tm))
    h = _rms(x2, g3_ref[...]).astype(BF16)
    gate = jax.nn.sigmoid(_dot(h, wpg_ref[...]))
    out_ref[...] = x2 + gate * _dot(p_ref[...].astype(BF16), wple_ref[...])


def _combine_ple(x1_all, route_all, y_slots, tok_slots, p, norm3_g, wpg, wple, m, tm, row0):
    blk0 = row0 // tm
    grid_spec = pltpu.PrefetchScalarGridSpec(
        num_scalar_prefetch=1,
        grid=(m // tm,),
        in_specs=[
            pl.BlockSpec((tm, D_MODEL), lambda i, s: (blk0 + i, 0)),
            pl.BlockSpec((tm, LANE), lambda i, s: (blk0 + i, 0)),
            pl.BlockSpec(memory_space=pl.ANY),
            pl.BlockSpec((tm, PLE_DIM), lambda i, s: (i, 0)),
            pl.BlockSpec((1, D_MODEL), lambda i, s: (0, 0)),
            pl.BlockSpec((D_MODEL, D_MODEL), lambda i, s: (0, 0)),
            pl.BlockSpec((PLE_DIM, D_MODEL), lambda i, s: (0, 0)),
        ],
        out_specs=pl.BlockSpec((tm, D_MODEL), lambda i, s: (i, 0)),
        scratch_shapes=[
            pltpu.VMEM((2, TOP_K, tm * TOK_ROWS, LANE), F32),
            pltpu.SemaphoreType.DMA((2,)),
        ],
    )
    return pl.pallas_call(
        functools.partial(_combine_ple_kernel, tm=tm, row0=row0),
        grid_spec=grid_spec,
        out_shape=jax.ShapeDtypeStruct((m, D_MODEL), F32),
        compiler_params=_params(("arbitrary",)),
    )(tok_slots, x1_all, route_all, y_slots, p, norm3_g.reshape(1, D_MODEL), wpg, wple)


def _routing_tables(route_all, n_blocks):
    n_tok = route_all.shape[0]
    flat_e = route_all[:, 0:TOP_K].astype(jnp.int32).reshape(-1)
    onehot = (flat_e[:, None] == jnp.arange(N_EXPERTS, dtype=jnp.int32)[None, :]).astype(jnp.int32)
    csum = jnp.cumsum(onehot, axis=0)
    rank = jnp.take_along_axis(csum, flat_e[:, None], axis=1)[:, 0] - 1
    counts = csum[-1]
    nblk = (counts + MOE_TM - 1) // MOE_TM
    blk_end = jnp.cumsum(nblk)
    blk_start = blk_end - nblk
    slot = blk_start[flat_e] * MOE_TM + rank
    flat_tok = jnp.arange(n_tok * TOP_K, dtype=jnp.int32) // TOP_K
    slot_tok = jnp.zeros((n_blocks * MOE_TM,), jnp.int32).at[slot].set(flat_tok)
    n_used = blk_end[-1].reshape(1)
    i32 = lambda a: a.astype(jnp.int32)
    return i32(blk_start), i32(nblk), i32(n_used), slot_tok, i32(slot)


def kernel(x_prompt, x_sample, p_prompt, p_sample, cache_k_g0, cache_v_g0, cache_k_g1, cache_v_g1,
           cache_k_g2, cache_v_g2, state_conv, norm1_g, w_in, q_gain, k_gain, conv_w, w_conv_out,
           w_attn_out, w_o, norm2_g, w_coarse, b_coarse, w_fine, b_fine, w_gate_e, w_up_e, w_down_e,
           norm3_g, w_ple_gate, w_ple):
    seq = x_prompt.shape[1]
    assert x_prompt.shape == (1, seq, D_MODEL) and norm1_g.shape[0] == 1, "one prompt, one layer"
    n_seq, t_new, _ = x_sample.shape
    assert seq % ATT_UNIT == 0 and t_new == SUBLANE
    caches_k = (cache_k_g0, cache_k_g1, cache_k_g2)
    caches_v = (cache_v_g0, cache_v_g1, cache_v_g2)
    for g in range(N_GROUPS):
        assert caches_k[g].shape == (1, n_seq, WINDOWS[g], HEADS_PER_GROUP, HEAD_DIM)
    n_s = n_seq * t_new
    n_all = seq + n_s

    xp = x_prompt[0]
    xs = x_sample.reshape(n_s, D_MODEL)
    wc = w_conv_out[0].astype(BF16)
    wa = w_attn_out[0].astype(BF16)
    wo = w_o[0].astype(BF16)
    wpg = w_ple_gate[0].astype(BF16)
    wple = w_ple[0].astype(BF16)
    wr = jnp.concatenate(
        [w_coarse[0], jnp.transpose(w_fine[0], (1, 0, 2)).reshape(D_MODEL, N_EXPERTS)], axis=1)
    wr = jnp.pad(wr, ((0, 0), (0, LANE - wr.shape[1])))
    wr_hi = wr.astype(BF16)
    wr_lo = (wr - wr_hi.astype(F32)).astype(BF16)
    br = jnp.pad(jnp.concatenate([b_coarse[0], b_fine[0].reshape(-1)]), (0, LANE - 36)).reshape(1, LANE)

    z_p = _inproj(xp, norm1_g[0], w_in[0], tm=2048, tn=512)
    z_s = _inproj(xs, norm1_g[0], w_in[0], tm=n_s, tn=512)

    att_p = _prompt_attn(z_p, q_gain[0], k_gain[0])
    o_p, pks, pvs = att_p[0], att_p[1:4], att_p[4:7]
    ck = [c.reshape(n_seq, WINDOWS[g] * HEADS_PER_GROUP, HEAD_DIM) for g, c in enumerate(caches_k)]
    cv = [c.reshape(n_seq, WINDOWS[g] * HEADS_PER_GROUP, HEAD_DIM) for g, c in enumerate(caches_v)]
    att_s = _sample_attn(z_s, ck, cv, q_gain[0], k_gain[0], n_seq, t_new)
    o_s, sks, svs = att_s[0], att_s[1:4], att_s[4:7]

    mg_p, u_tail = _merge(z_p, o_p, conv_w[0], wc, wa, tm=512)
    prefix = jnp.pad(state_conv[0], ((0, 0), (0, t_new - 2), (0, 0))).reshape(n_s, D_CONV)
    mg_s, u_s = _merge(z_s, o_s, conv_w[0], wc, wa, tm=n_s, per_seq=t_new, prefix=prefix)
    x1_all, x1t_all, route_all = _outproj_router(xp, mg_p, xs, mg_s, wo, norm2_g[0], wr_hi, wr_lo, br)

    n_blocks = n_all * TOP_K // MOE_TM + N_EXPERTS
    blk_start, nblk, n_used, slot_tok, tok_slots = _routing_tables(route_all, n_blocks)
    y_slots = _moe(x1t_all, norm2_g[0], w_gate_e[0], w_up_e[0], w_down_e[0], blk_start, nblk, n_used,
                   slot_tok, n_blocks)

    y_p = _combine_ple(x1_all, route_all, y_slots, tok_slots, p_prompt[0, 0], norm3_g[0], wpg, wple,
                       m=seq, tm=256, row0=0)
    y_s = _combine_ple(x1_all, route_all, y_slots, tok_slots, p_sample[0].reshape(n_s, PLE_DIM),
                       norm3_g[0], wpg, wple, m=n_s, tm=n_s, row0=seq)

    def state(a, n, g):
        return a.reshape(1, n, WINDOWS[g], HEADS_PER_GROUP, HEAD_DIM)

    pk = [state(a, 1, g) for g, a in enumerate(pks)]
    pv = [state(a, 1, g) for g, a in enumerate(pvs)]
    sk = [state(a, n_seq, g) for g, a in enumerate(sks)]
    sv = [state(a, n_seq, g) for g, a in enumerate(svs)]
    pconv = u_tail[-2:].reshape(1, 1, 2, D_CONV)
    sconv = u_s.reshape(n_seq, t_new, D_CONV)[:, t_new - 2:].reshape(1, n_seq, 2, D_CONV)
    return (y_p.reshape(1, seq, D_MODEL), y_s.reshape(n_seq, t_new, D_MODEL),
            pk[0], pv[0], pk[1], pv[1], pk[2], pv[2], pconv,
            sk[0], sv[0], sk[1], sv[1], sk[2], sv[2], sconv)
```

```python
import functools
import math

import jax
import jax.numpy as jnp
from jax import lax
from jax.experimental import pallas as pl
from jax.experimental.pallas import tpu as pltpu

D_MODEL = 2048
HEAD_DIM = 128
HEADS_PER_GROUP = 4
WINDOWS = (128, 512, 2048)
DILATIONS = (1, 4, 16)
N_GROUPS = 3
N_HEADS = N_GROUPS * HEADS_PER_GROUP
ATT_WIDTH = N_HEADS * HEAD_DIM
ATT_OUT_WIDTH = HEADS_PER_GROUP * HEAD_DIM
SCALE = HEAD_DIM ** -0.5
D_CONV = D_MODEL // 2
PLE_DIM = 256
N_EXPERT_GROUPS = 4
EXPERTS_PER_GROUP = 8
N_EXPERTS = N_EXPERT_GROUPS * EXPERTS_PER_GROUP
TOP_K = 2
D_EXPERT = D_MODEL // 4
EPS = 1e-6

COL_B = 0
COL_C = D_CONV
COL_H = 2 * D_CONV
COL_Q = 3 * D_CONV
COL_K = COL_Q + ATT_WIDTH
COL_V = COL_K + ATT_WIDTH
COL_GC = COL_V + ATT_WIDTH
COL_GA = COL_GC + D_MODEL
IN_COLS = COL_GA + D_MODEL

LANE = 128
SUBLANE = 8
NEG = -1e30
ATT_UNIT = max(WINDOWS)
ATT_BLK = 128
ATT_TILE_UNROLL = 8
MXU_N = 256
TOK_ROWS = D_MODEL // LANE
MOE_TM = 288
VMEM_LIMIT = 56 * 1024 * 1024

BF16 = jnp.bfloat16
F32 = jnp.float32


def _params(sem):
    return pltpu.CompilerParams(dimension_semantics=sem, vmem_limit_bytes=VMEM_LIMIT)


def _rms(x, gain):
    return x * lax.rsqrt(jnp.mean(x * x, axis=-1, keepdims=True) + EPS) * gain


def _dot(a, b):
    return jnp.dot(a, b, preferred_element_type=F32)


def _dot_nt(a, b):
    return lax.dot_general(a, b, (((1,), (1,)), ((), ())), preferred_element_type=F32)


def _inproj_kernel(x_ref, g_ref, w_ref, z_ref, h_scr):
    @pl.when(pl.program_id(1) == 0)
    def _():
        h_scr[...] = _rms(x_ref[...], g_ref[...]).astype(BF16)

    for c in range(w_ref.shape[1] // MXU_N):
        cols = slice(c * MXU_N, (c + 1) * MXU_N)
        z_ref[:, cols] = _dot(h_scr[...], w_ref[:, cols].astype(BF16))


def _inproj(x, norm_g, w_in, tm, tn):
    m = x.shape[0]
    return pl.pallas_call(
        _inproj_kernel,
        grid=(m // tm, IN_COLS // tn),
        in_specs=[
            pl.BlockSpec((tm, D_MODEL), lambda i, j: (i, 0), pipeline_mode=pl.Buffered(1)),
            pl.BlockSpec((1, D_MODEL), lambda i, j: (0, 0)),
            pl.BlockSpec((D_MODEL, tn), lambda i, j: (0, j)),
        ],
        out_specs=pl.BlockSpec((tm, tn), lambda i, j: (i, j)),
        out_shape=jax.ShapeDtypeStruct((m, IN_COLS), F32),
        scratch_shapes=[pltpu.VMEM((tm, D_MODEL), BF16)],
        compiler_params=_params(("parallel", "arbitrary")),
    )(x, norm_g.reshape(1, D_MODEL), w_in)


def _prompt_attn_kernel(*refs):
    (q0, q1, q2, kc0, kc1, kc2, vc0, vc1, vc2, kp0, kp1, kp2, vp0, vp1, vp2, qg_ref, kg_ref,
     o_ref, pk0, pk1, pk2, pv0, pv1, pv2, kext, vext, acc_s, m_s, l_s) = refs
    q_refs, kc_refs, vc_refs = (q0, q1, q2), (kc0, kc1, kc2), (vc0, vc1, vc2)
    kp_refs, vp_refs = (kp0, kp1, kp2), (vp0, vp1, vp2)
    pk_refs, pv_refs = (pk0, pk1, pk2), (pv0, pv1, pv2)
    n = pl.program_id(1)
    last = pl.num_programs(1) - 1

    qi = lax.broadcasted_iota(jnp.int32, (ATT_BLK, 2 * ATT_BLK), 0) + ATT_BLK
    ki = lax.broadcasted_iota(jnp.int32, (ATT_BLK, 2 * ATT_BLK), 1)
    dist = qi - ki
    band = (dist >= 0) & (dist <= ATT_BLK)

    for g in range(N_GROUPS):
        win, dil = WINDOWS[g], DILATIONS[g]
        q_ref = q_refs[g]
        qgain = qg_ref[g, 0]
        kgain = kg_ref[g, 0]
        kext[0:win, :] = _rms(kp_refs[g][...], kgain)
        kext[win:win + ATT_UNIT, :] = _rms(kc_refs[g][...], kgain)
        vext[0:win, :] = vp_refs[g][...]
        vext[win:win + ATT_UNIT, :] = vc_refs[g][...]

        @pl.when(n == last)
        def _():
            pk_refs[g][...] = kext[ATT_UNIT:ATT_UNIT + win, :]
            pv_refs[g][...] = vext[ATT_UNIT:ATT_UNIT + win, :]

        def tile(t, carry, g=g, win=win, dil=dil, q_ref=q_ref, qgain=qgain):
            u = t // dil
            r = t % dil
            off = u * win + r
            if dil == 1:
                off = pl.multiple_of(off, ATT_BLK)
                rows_q = pl.ds(off, ATT_BLK)
                rows_k = pl.ds(off, 2 * ATT_BLK)
            else:
                rows_q = pl.ds(off, ATT_BLK, stride=dil)
                rows_k = pl.ds(off, 2 * ATT_BLK, stride=dil)
            q = _rms(q_ref[rows_q, :], qgain).astype(BF16)
            k = kext[rows_k, :].astype(BF16)
            v = vext[rows_k, :].astype(BF16)
            s = _dot_nt(q, k) * SCALE
            k_min = jnp.where((n > 0) | (u > 0), 0, ATT_BLK)
            s = jnp.where(band & (ki >= k_min), s, NEG)
            m_t = jnp.max(s, axis=-1, keepdims=True)
            p = jnp.exp(s - m_t)
            l_t = jnp.sum(p, axis=-1, keepdims=True)
            m_s[g, rows_q, :] = jnp.broadcast_to(m_t, (ATT_BLK, LANE))
            l_s[g, rows_q, :] = jnp.broadcast_to(l_t, (ATT_BLK, LANE))
            acc_s[g, rows_q, :] = _dot(p.astype(BF16), v)
            return carry

        lax.fori_loop(0, ATT_UNIT // ATT_BLK, tile, 0, unroll=ATT_TILE_UNROLL)

    def finish(c, carry):
        rows = pl.ds(pl.multiple_of(c * ATT_BLK, ATT_BLK), ATT_BLK)
        m = [m_s[g, rows, :] for g in range(N_GROUPS)]
        m_all = jnp.maximum(jnp.maximum(m[0], m[1]), m[2])
        num = jnp.zeros((ATT_BLK, LANE), F32)
        den = jnp.zeros((ATT_BLK, LANE), F32)
        for g in range(N_GROUPS):
            w = jnp.exp(m[g] - m_all)
            num = num + w * acc_s[g, rows, :]
            den = den + w * l_s[g, rows, :]
        o_ref[rows, :] = num / den
        return carry

    lax.fori_loop(0, ATT_UNIT // ATT_BLK, finish, 0, unroll=2)


def _prompt_attn(z, q_gain, k_gain):
    s_len = z.shape[0]
    n_units = s_len // ATT_UNIT
    in_specs, args = [], []

    def col(base, g):
        return lambda hh, n: (n, base // LANE + g * HEADS_PER_GROUP + hh)

    for g in range(N_GROUPS):
        in_specs.append(pl.BlockSpec((ATT_UNIT, LANE), col(COL_Q, g)))
    for base in (COL_K, COL_V):
        for g in range(N_GROUPS):
            in_specs.append(pl.BlockSpec((ATT_UNIT, LANE), col(base, g)))
    for base in (COL_K, COL_V):
        for g in range(N_GROUPS):
            per = ATT_UNIT // WINDOWS[g]
            in_specs.append(pl.BlockSpec(
                (WINDOWS[g], LANE),
                lambda hh, n, base=base, g=g, per=per: (
                    jnp.maximum(n * per - 1, 0), base // LANE + g * HEADS_PER_GROUP + hh)))
    args = [z] * 15
    gain_spec = pl.BlockSpec((N_GROUPS, 1, 1, LANE), lambda hh, n: (0, hh, 0, 0))
    in_specs += [gain_spec, gain_spec]
    args += [q_gain.reshape(N_GROUPS, HEADS_PER_GROUP, 1, LANE),
             k_gain.reshape(N_GROUPS, HEADS_PER_GROUP, 1, LANE)]

    out_specs = [pl.BlockSpec((ATT_UNIT, LANE), lambda hh, n: (n, hh))]
    out_shape = [jax.ShapeDtypeStruct((s_len, ATT_OUT_WIDTH), F32)]
    for _ in range(2):
        for g in range(N_GROUPS):
            out_specs.append(pl.BlockSpec((WINDOWS[g], LANE), lambda hh, n: (0, hh)))
            out_shape.append(jax.ShapeDtypeStruct((WINDOWS[g], ATT_OUT_WIDTH), F32))

    return pl.pallas_call(
        _prompt_attn_kernel,
        grid=(HEADS_PER_GROUP, n_units),
        in_specs=in_specs,
        out_specs=out_specs,
        out_shape=out_shape,
        scratch_shapes=[
            pltpu.VMEM((2 * ATT_UNIT, LANE), F32),
            pltpu.VMEM((2 * ATT_UNIT, LANE), F32),
            pltpu.VMEM((N_GROUPS, ATT_UNIT, LANE), F32),
            pltpu.VMEM((N_GROUPS, ATT_UNIT, LANE), F32),
            pltpu.VMEM((N_GROUPS, ATT_UNIT, LANE), F32),
        ],
        compiler_params=_params(("parallel", "arbitrary")),
    )(*args)


def _sample_attn_kernel(*refs):
    (q0, q1, q2, kn0, kn1, kn2, vn0, vn1, vn2, ck0, ck1, ck2, cv0, cv1, cv2, qg_ref, kg_ref,
     o_ref, sk0, sk1, sk2, sv0, sv1, sv2) = refs
    q_refs, kn_refs, vn_refs = (q0, q1, q2), (kn0, kn1, kn2), (vn0, vn1, vn2)
    ck_refs, cv_refs = (ck0, ck1, ck2), (cv0, cv1, cv2)
    sk_refs, sv_refs = (sk0, sk1, sk2), (sv0, sv1, sv2)
    t_new = q0.shape[0]
    nh = HEADS_PER_GROUP
    n_q = nh * t_new
    t_bits = t_new.bit_length() - 1

    def heads(ref):
        return [ref[:, h * LANE:(h + 1) * LANE] for h in range(nh)]

    parts = []
    for g in range(N_GROUPS):
        win, dil = WINDOWS[g], DILATIONS[g]
        q_all = jnp.concatenate(
            [_rms(x, qg_ref[g, h]) for h, x in enumerate(heads(q_refs[g]))], axis=0).astype(BF16)
        k_new = [_rms(x, kg_ref[g, h]) for h, x in enumerate(heads(kn_refs[g]))]
        v_new = heads(vn_refs[g])
        k_new_all = jnp.concatenate(k_new, axis=0).astype(BF16)
        v_new_all = jnp.concatenate(v_new, axis=0).astype(BF16)
        ck = ck_refs[g][0]
        cv = cv_refs[g][0]
        old = (win - t_new) * nh
        sk_refs[g][0, 0:old, :] = ck[t_new * nh:, :]
        sv_refs[g][0, 0:old, :] = cv[t_new * nh:, :]
        for h in range(nh):
            sk_refs[g][0, pl.ds(old + h, t_new, stride=nh), :] = k_new[h]
            sv_refs[g][0, pl.ds(old + h, t_new, stride=nh), :] = v_new[h]

        s_c = _dot_nt(q_all, ck.astype(BF16)) * SCALE
        s_n = _dot_nt(q_all, k_new_all) * SCALE
        row = lax.broadcasted_iota(jnp.int32, (n_q, win * nh), 0)
        col = lax.broadcasted_iota(jnp.int32, (n_q, win * nh), 1)
        i_q, pos = row & (t_new - 1), col >> 2
        mask_c = (((col & (nh - 1)) == (row >> t_bits)) & (pos >= i_q)
                  & (((pos - i_q) & (dil - 1)) == 0))
        row_n = lax.broadcasted_iota(jnp.int32, (n_q, n_q), 0)
        col_n = lax.broadcasted_iota(jnp.int32, (n_q, n_q), 1)
        d_n = (row_n & (t_new - 1)) - (col_n & (t_new - 1))
        mask_n = ((row_n >> t_bits) == (col_n >> t_bits)) & (d_n >= 0) & ((d_n & (dil - 1)) == 0)
        s_c = jnp.where(mask_c, s_c, NEG)
        s_n = jnp.where(mask_n, s_n, NEG)
        m = jnp.maximum(jnp.max(s_c, axis=-1, keepdims=True), jnp.max(s_n, axis=-1, keepdims=True))
        p_c = jnp.exp(s_c - m)
        p_n = jnp.exp(s_n - m)
        l = jnp.sum(p_c, axis=-1, keepdims=True) + jnp.sum(p_n, axis=-1, keepdims=True)
        acc = _dot(p_c.astype(BF16), cv.astype(BF16)) + _dot(p_n.astype(BF16), v_new_all)
        parts.append((m, l, acc))

    m_all = jnp.maximum(jnp.maximum(parts[0][0], parts[1][0]), parts[2][0])
    num = jnp.zeros((n_q, LANE), F32)
    den = jnp.zeros((n_q, 1), F32)
    for m, l, acc in parts:
        w = jnp.exp(m - m_all)
        num = num + w * acc
        den = den + w * l
    o_all = num / den
    for h in range(nh):
        o_ref[:, h * LANE:(h + 1) * LANE] = o_all[h * t_new:(h + 1) * t_new, :]


def _sample_attn(z, caches_k, caches_v, q_gain, k_gain, n_seq, t_new):
    assert HEADS_PER_GROUP == 4 and t_new & (t_new - 1) == 0
    in_specs, args = [], []
    for base in (COL_Q, COL_K, COL_V):
        for g in range(N_GROUPS):
            in_specs.append(pl.BlockSpec(
                (t_new, ATT_OUT_WIDTH), lambda b, base=base, g=g: (b, base // ATT_OUT_WIDTH + g)))
            args.append(z)
    for caches in (caches_k, caches_v):
        for g in range(N_GROUPS):
            in_specs.append(pl.BlockSpec((1, WINDOWS[g] * HEADS_PER_GROUP, LANE), lambda b: (b, 0, 0)))
            args.append(caches[g])
    gain_spec = pl.BlockSpec((N_GROUPS, HEADS_PER_GROUP, 1, LANE), lambda b: (0, 0, 0, 0))
    in_specs += [gain_spec, gain_spec]
    args += [q_gain.reshape(N_GROUPS, HEADS_PER_GROUP, 1, LANE),
             k_gain.reshape(N_GROUPS, HEADS_PER_GROUP, 1, LANE)]

    out_specs = [pl.BlockSpec((t_new, ATT_OUT_WIDTH), lambda b: (b, 0))]
    out_shape = [jax.ShapeDtypeStruct((n_seq * t_new, ATT_OUT_WIDTH), F32)]
    for _ in range(2):
        for g in range(N_GROUPS):
            rows = WINDOWS[g] * HEADS_PER_GROUP
            out_specs.append(pl.BlockSpec((1, rows, LANE), lambda b: (b, 0, 0)))
            out_shape.append(jax.ShapeDtypeStruct((n_seq, rows, LANE), F32))

    return pl.pallas_call(
        _sample_attn_kernel,
        grid=(n_seq,),
        in_specs=in_specs,
        out_specs=out_specs,
        out_shape=out_shape,
        compiler_params=_params(("parallel",)),
    )(*args)


def _conv_taps(u, r1, r2, cw_ref):
    return cw_ref[2:3, :] * u + cw_ref[1:2, :] * r1 + cw_ref[0:1, :] * r2


def _merge_kernel(*refs, per_seq, n_chunks):
    b_ref, c_ref, h_ref, pc_ref, ph_ref, cw_ref, o_ref = refs[:7]
    gc_refs = refs[7:7 + n_chunks]
    ga_refs = refs[7 + n_chunks:7 + 2 * n_chunks]
    wc_ref, wa_ref, out_ref, u_ref, yb_scr = refs[7 + 2 * n_chunks:]
    i = pl.program_id(0)
    tm = b_ref.shape[0]
    u = c_ref[...] * h_ref[...]
    r1 = pltpu.roll(u, 1, axis=0)
    r2 = pltpu.roll(u, 2, axis=0)
    if per_seq is None:
        yb_scr[...] = b_ref[...] * _conv_taps(u, r1, r2, cw_ref)
        up = jnp.where(i > 0, pc_ref[...] * ph_ref[...], 0.0)
        row = lax.broadcasted_iota(jnp.int32, (SUBLANE, D_CONV), 0)
        u8 = u[0:SUBLANE, :]
        r1_8 = jnp.where(row < 1, pltpu.roll(up, 1, axis=0), pltpu.roll(u8, 1, axis=0))
        r2_8 = jnp.where(row < 2, pltpu.roll(up, 2, axis=0), pltpu.roll(u8, 2, axis=0))
        yb_scr[0:SUBLANE, :] = b_ref[0:SUBLANE, :] * _conv_taps(u8, r1_8, r2_8, cw_ref)
        u_ref[...] = u[tm - SUBLANE:tm, :]
    else:
        pre = pc_ref[...]
        t = lax.broadcasted_iota(jnp.int32, (tm, D_CONV), 0) & (per_seq - 1)
        r1 = jnp.where(t == 0, pltpu.roll(pre, tm - 1, axis=0), r1)
        r2 = jnp.where(t < 2, pre, r2)
        yb_scr[...] = b_ref[...] * _conv_taps(u, r1, r2, cw_ref)
        u_ref[...] = u

    yb = yb_scr[...].astype(BF16)
    ob = o_ref[...].astype(BF16)
    tn = D_MODEL // n_chunks
    for c in range(n_chunks):
        cols = slice(c * tn, (c + 1) * tn)
        y_conv = _dot(yb, wc_ref[:, cols])
        y_att = _dot(ob, wa_ref[:, cols])
        out_ref[:, cols] = (jax.nn.sigmoid(gc_refs[c][...]) * y_conv
                            + jax.nn.sigmoid(ga_refs[c][...]) * y_att).astype(BF16)


def _merge(z, o_att, conv_w, wc, wa, tm, per_seq=None, prefix=None):
    m = z.shape[0]
    tn = ATT_OUT_WIDTH
    n_chunks = D_MODEL // tn
    if per_seq is None:
        prev = lambda i, c: (jnp.maximum(i * (tm // SUBLANE) - 1, 0), c)
        pc_spec = pl.BlockSpec((SUBLANE, D_CONV), lambda i: prev(i, COL_C // D_CONV))
        ph_spec = pl.BlockSpec((SUBLANE, D_CONV), lambda i: prev(i, COL_H // D_CONV))
        pc_arg, ph_arg = z, z
        u_rows = SUBLANE
    else:
        pc_spec = pl.BlockSpec((tm, D_CONV), lambda i: (i, 0))
        ph_spec = pl.BlockSpec((SUBLANE, D_CONV), lambda i: (0, 0))
        pc_arg, ph_arg = prefix, prefix
        u_rows = tm
    gate_specs = [pl.BlockSpec((tm, tn), lambda i, base=base, c=c: (i, base // tn + c))
                  for base in (COL_GC, COL_GA) for c in range(n_chunks)]
    return pl.pallas_call(
        functools.partial(_merge_kernel, per_seq=per_seq, n_chunks=n_chunks),
        grid=(m // tm,),
        in_specs=[
            pl.BlockSpec((tm, D_CONV), lambda i: (i, COL_B // D_CONV)),
            pl.BlockSpec((tm, D_CONV), lambda i: (i, COL_C // D_CONV)),
            pl.BlockSpec((tm, D_CONV), lambda i: (i, COL_H // D_CONV)),
            pc_spec,
            ph_spec,
            pl.BlockSpec((3, D_CONV), lambda i: (0, 0)),
            pl.BlockSpec((tm, ATT_OUT_WIDTH), lambda i: (i, 0)),
            *gate_specs,
            pl.BlockSpec((D_CONV, D_MODEL), lambda i: (0, 0)),
            pl.BlockSpec((ATT_OUT_WIDTH, D_MODEL), lambda i: (0, 0)),
        ],
        out_specs=[
            pl.BlockSpec((tm, D_MODEL), lambda i: (i, 0)),
            pl.BlockSpec((u_rows, D_CONV), lambda i: (i, 0)),
        ],
        out_shape=[
            jax.ShapeDtypeStruct((m, D_MODEL), BF16),
            jax.ShapeDtypeStruct((m // tm * u_rows, D_CONV), F32),
        ],
        scratch_shapes=[pltpu.VMEM((tm, D_CONV), F32)],
        compiler_params=_params(("parallel",)),
    )(z, z, z, pc_arg, ph_arg, conv_w, o_att, *([z] * (2 * n_chunks)), wc, wa)


def _outproj_router_kernel(xp_ref, mgp_ref, xs_ref, mgs_ref, wo_ref, g2_ref, wr_hi_ref, wr_lo_ref,
                           br_ref, x1_ref, x1t_ref, route_ref):
    is_sample = pl.program_id(0) == pl.num_programs(0) - 1

    @pl.when(jnp.logical_not(is_sample))
    def _():
        _outproj_router_tile(xp_ref, mgp_ref, wo_ref, g2_ref, wr_hi_ref, wr_lo_ref, br_ref,
                             x1_ref, x1t_ref, route_ref)

    @pl.when(is_sample)
    def _():
        _outproj_router_tile(xs_ref, mgs_ref, wo_ref, g2_ref, wr_hi_ref, wr_lo_ref, br_ref,
                             x1_ref, x1t_ref, route_ref)


def _outproj_router_tile(x_ref, mg_ref, wo_ref, g2_ref, wr_hi_ref, wr_lo_ref, br_ref, x1_ref,
                         x1t_ref, route_ref):
    x1 = x_ref[...] + _dot(mg_ref[...], wo_ref[...])
    x1_ref[...] = x1
    _to_slabs(x1t_ref, x1)
    h = _rms(x1, g2_ref[...])
    h_hi = h.astype(BF16)
    h_lo = (h - h_hi.astype(F32)).astype(BF16)
    logits = (_dot(h_hi, wr_hi_ref[...]) + _dot(h_hi, wr_lo_ref[...]) + _dot(h_lo, wr_hi_ref[...])
              + br_ref[...])
    lane = lax.broadcasted_iota(jnp.int32, logits.shape, 1)
    is_coarse = lane < N_EXPERT_GROUPS
    coarse = jnp.where(is_coarse, logits, NEG)
    cmax = jnp.max(coarse, axis=-1, keepdims=True)
    grp = jnp.min(jnp.where(coarse == cmax, lane, LANE), axis=-1, keepdims=True)
    p_grp = 1.0 / jnp.sum(jnp.where(is_coarse, jnp.exp(coarse - cmax), 0.0), axis=-1, keepdims=True)
    eid = lane - N_EXPERT_GROUPS
    in_grp = (eid >= 0) & (eid < N_EXPERTS) & ((eid >> 3) == grp)
    fine = jnp.where(in_grp, logits, NEG)
    v1 = jnp.max(fine, axis=-1, keepdims=True)
    i1 = jnp.min(jnp.where(fine == v1, lane, LANE), axis=-1, keepdims=True)
    fine2 = jnp.where(lane == i1, NEG, fine)
    v2 = jnp.max(fine2, axis=-1, keepdims=True)
    i2 = jnp.min(jnp.where(fine2 == v2, lane, LANE), axis=-1, keepdims=True)
    e = jnp.exp(v2 - v1)
    gate1 = p_grp / (1.0 + e)
    gate2 = p_grp * e / (1.0 + e)
    route = jnp.where(lane == 0, (i1 - N_EXPERT_GROUPS).astype(F32),
                      jnp.where(lane == 1, (i2 - N_EXPERT_GROUPS).astype(F32),
                                jnp.where(lane == 2, gate1, jnp.where(lane == 3, gate2, 0.0))))
    route_ref[...] = route


def _outproj_router(xp, mg_p, xs, mg_s, wo, norm2_g, wr_hi, wr_lo, br):
    m_p, tm = xp.shape[0], xs.shape[0]
    assert m_p % tm == 0
    n_p = m_p // tm
    prompt_blk = lambda i: (jnp.minimum(i, n_p - 1), 0)
    fixed = lambda i: (0, 0)
    return pl.pallas_call(
        _outproj_router_kernel,
        grid=(n_p + 1,),
        in_specs=[
            pl.BlockSpec((tm, D_MODEL), prompt_blk),
            pl.BlockSpec((tm, D_MODEL), prompt_blk),
            pl.BlockSpec((tm, D_MODEL), fixed),
            pl.BlockSpec((tm, D_MODEL), fixed),
            pl.BlockSpec((D_MODEL, D_MODEL), fixed),
            pl.BlockSpec((1, D_MODEL), fixed),
            pl.BlockSpec((D_MODEL, LANE), fixed),
            pl.BlockSpec((D_MODEL, LANE), fixed),
            pl.BlockSpec((1, LANE), fixed),
        ],
        out_specs=[
            pl.BlockSpec((tm, D_MODEL), lambda i: (i, 0)),
            pl.BlockSpec((tm * TOK_ROWS, LANE), lambda i: (i, 0)),
            pl.BlockSpec((tm, LANE), lambda i: (i, 0)),
        ],
        out_shape=[
            jax.ShapeDtypeStruct((m_p + tm, D_MODEL), F32),
            jax.ShapeDtypeStruct(((m_p + tm) * TOK_ROWS, LANE), F32),
            jax.ShapeDtypeStruct((m_p + tm, LANE), F32),
        ],
        compiler_params=_params(("arbitrary",)),
    )(xp, mg_p, xs, mg_s, wo, norm2_g.reshape(1, D_MODEL), wr_hi, wr_lo, br)


def _to_slabs(dst_ref, x):
    rows = x.shape[0]
    for s in range(TOK_ROWS):
        dst_ref[pl.ds(s, rows, stride=TOK_ROWS), :] = x[:, s * LANE:(s + 1) * LANE]


def _from_slabs(src_ref, rows):
    return jnp.concatenate(
        [src_ref[pl.ds(s, rows, stride=TOK_ROWS), :] for s in range(TOK_ROWS)], axis=1)


def _dispatch_kernel(slots_ref, pad0_ref, padn_ref, nused_ref, x1t_hbm, xst_hbm, buf, zero, lsem,
                     ssem, zsem, *, tm, n_blocks):
    i = pl.program_id(0)
    n_steps = pl.num_programs(0)
    slab = tm * TOK_ROWS

    def load(step):
        return pltpu.make_async_copy(
            x1t_hbm.at[pl.ds(pl.multiple_of(step * slab, slab), slab), :], buf.at[step % 3],
            lsem.at[step % 3])

    def wait_scatter(step):
        for _ in range(TOP_K):
            pltpu.make_async_copy(buf.at[step % 3], xst_hbm.at[pl.ds(0, slab), :],
                                  ssem.at[step % 3]).wait()

    def pad_rows(start):
        def expert(e, carry):
            def row(j, c):
                dst = pl.multiple_of((pad0_ref[e] + j) * TOK_ROWS, TOK_ROWS)
                cp = pltpu.make_async_copy(zero, xst_hbm.at[pl.ds(dst, TOK_ROWS), :], zsem.at[0])
                if start:
                    cp.start()
                else:
                    cp.wait()
                return c
            return lax.fori_loop(0, padn_ref[e], row, carry)
        lax.fori_loop(0, N_EXPERTS, expert, 0)

    @pl.when(i == 0)
    def _():
        load(0).start()
        zero[...] = jnp.zeros(zero.shape, F32)
        pad_rows(start=True)

    @pl.when(i >= 2)
    def _():
        wait_scatter(i - 2)

    @pl.when(i + 1 < n_steps)
    def _():
        load(i + 1).start()

    load(i).wait()
    cur = i % 3

    def body(r, carry):
        src = buf.at[cur, pl.ds(pl.multiple_of(r * TOK_ROWS, TOK_ROWS), TOK_ROWS), :]
        for k in range(TOP_K):
            dst = pl.multiple_of(slots_ref[(i * tm + r) * TOP_K + k] * TOK_ROWS, TOK_ROWS)
            pltpu.make_async_copy(src, xst_hbm.at[pl.ds(dst, TOK_ROWS), :],
                                  ssem.at[cur]).start(priority=k)
        return carry
    lax.fori_loop(0, tm, body, 0, unroll=4)

    @pl.when(i == n_steps - 1)
    def _():
        @pl.when(i >= 1)
        def _():
            wait_scatter(i - 1)
        wait_scatter(i)
        pad_rows(start=False)
        chunk = math.gcd(tm, MOE_TM)
        rows = chunk * TOK_ROWS
        buf[0, 0:rows, :] = jnp.zeros((rows, LANE), F32)
        first = nused_ref[0] * MOE_TM
        n_tail = (n_blocks - nused_ref[0]) * (MOE_TM // chunk)

        def tail(j, carry, start):
            dst = pl.multiple_of((first + j * chunk) * TOK_ROWS, TOK_ROWS)
            cp = pltpu.make_async_copy(buf.at[0, pl.ds(0, rows), :],
                                       xst_hbm.at[pl.ds(dst, rows), :], zsem.at[0])
            if start:
                cp.start()
            else:
                cp.wait()
            return carry
        lax.fori_loop(0, n_tail, functools.partial(tail, start=True), 0)
        lax.fori_loop(0, n_tail, functools.partial(tail, start=False), 0)


def _dispatch(x1t, tok_slots, pad_start, pad_count, n_used, n_blocks, tm):
    n_tok = x1t.shape[0] // TOK_ROWS
    assert n_tok % tm == 0
    grid_spec = pltpu.PrefetchScalarGridSpec(
        num_scalar_prefetch=4,
        grid=(n_tok // tm,),
        in_specs=[pl.BlockSpec(memory_space=pl.ANY)],
        out_specs=pl.BlockSpec(memory_space=pl.ANY),
        scratch_shapes=[
            pltpu.VMEM((3, tm * TOK_ROWS, LANE), F32),
            pltpu.VMEM((TOK_ROWS, LANE), F32),
            pltpu.SemaphoreType.DMA((3,)),
            pltpu.SemaphoreType.DMA((3,)),
            pltpu.SemaphoreType.DMA((1,)),
        ],
    )
    return pl.pallas_call(
        functools.partial(_dispatch_kernel, tm=tm, n_blocks=n_blocks),
        grid_spec=grid_spec,
        out_shape=jax.ShapeDtypeStruct((n_blocks * MOE_TM * TOK_ROWS, LANE), F32),
        compiler_params=_params(("arbitrary",)),
    )(tok_slots, pad_start, pad_count, n_used, x1t)


def _moe_kernel(blk0_ref, nblk_ref, nused_ref, xst_hbm, g2_ref, wg_ref, wu_ref, wd_ref,
                yt_hbm, xbuf, ybuf, gsem, osem, wg_b, wu_b, wd_b, *, n_blocks):
    e = pl.program_id(0)
    n_used = nused_ref[0]
    slab = MOE_TM * TOK_ROWS

    def in_copy(blk, to_slot):
        return pltpu.make_async_copy(
            xst_hbm.at[pl.ds(pl.multiple_of(blk * slab, slab), slab), :], xbuf.at[to_slot],
            gsem.at[to_slot])

    def out_copy(blk, from_slot):
        return pltpu.make_async_copy(
            ybuf.at[from_slot], yt_hbm.at[pl.ds(pl.multiple_of(blk * slab, slab), slab), :],
            osem.at[from_slot])

    wg_b[...] = wg_ref[0].astype(BF16)
    wu_b[...] = wu_ref[0].astype(BF16)
    wd_b[...] = wd_ref[0].astype(BF16)

    @pl.when(e == 0)
    def _():
        in_copy(0, 0).start()

    def block(b, carry):
        blk = blk0_ref[e] + b
        slot = blk & 1

        @pl.when(blk + 1 < n_used)
        def _():
            in_copy(blk + 1, 1 - slot).start()

        in_copy(blk, slot).wait()
        h = _rms(_from_slabs(xbuf.at[slot], MOE_TM), g2_ref[...]).astype(BF16)
        a = _dot(h, wg_b[...])
        u = _dot(h, wu_b[...])
        hm = (a * jax.nn.sigmoid(a) * u).astype(BF16)
        y = _dot(hm, wd_b[...])

        @pl.when(blk >= 2)
        def _():
            out_copy(blk - 2, slot).wait()

        _to_slabs(ybuf.at[slot], y)
        out_copy(blk, slot).start()
        return carry

    lax.fori_loop(0, nblk_ref[e], block, 0)

    @pl.when(e == pl.num_programs(0) - 1)
    def _():
        @pl.when(n_used >= 2)
        def _():
            out_copy(n_used - 2, (n_used - 2) & 1).wait()
        out_copy(n_used - 1, (n_used - 1) & 1).wait()
        ybuf[0] = jnp.zeros((slab, LANE), F32)

        def fill(blk, carry):
            out_copy(blk, 0).start()
            out_copy(blk, 0).wait()
            return carry
        lax.fori_loop(n_used, n_blocks, fill, 0)


def _moe(xst, norm2_g, w_gate_e, w_up_e, w_down_e, blk_start, nblk, n_used, n_blocks):
    w_idx = lambda e, *_: (e, 0, 0)
    slab = MOE_TM * TOK_ROWS
    grid_spec = pltpu.PrefetchScalarGridSpec(
        num_scalar_prefetch=3,
        grid=(N_EXPERTS,),
        in_specs=[
            pl.BlockSpec(memory_space=pl.ANY),
            pl.BlockSpec((1, D_MODEL), lambda e, *_: (0, 0)),
            pl.BlockSpec((1, D_MODEL, D_EXPERT), w_idx),
            pl.BlockSpec((1, D_MODEL, D_EXPERT), w_idx),
            pl.BlockSpec((1, D_EXPERT, D_MODEL), w_idx),
        ],
        out_specs=pl.BlockSpec(memory_space=pl.ANY),
        scratch_shapes=[
            pltpu.VMEM((2, slab, LANE), F32),
            pltpu.VMEM((2, slab, LANE), F32),
            pltpu.SemaphoreType.DMA((2,)),
            pltpu.SemaphoreType.DMA((2,)),
            pltpu.VMEM((D_MODEL, D_EXPERT), BF16),
            pltpu.VMEM((D_MODEL, D_EXPERT), BF16),
            pltpu.VMEM((D_EXPERT, D_MODEL), BF16),
        ],
    )
    return pl.pallas_call(
        functools.partial(_moe_kernel, n_blocks=n_blocks),
        grid_spec=grid_spec,
        out_shape=jax.ShapeDtypeStruct((n_blocks * slab, LANE), F32),
        compiler_params=_params(("arbitrary",)),
    )(blk_start, nblk, n_used, xst, norm2_g.reshape(1, D_MODEL), w_gate_e, w_up_e, w_down_e)


def _combine_ple_kernel(slots_ref, x1_ref, route_ref, yt_hbm, p_ref, g3_ref, wpg_ref, wple_ref,
                        out_ref, ybuf, sem, *, tm, row0):
    i = pl.program_id(0)
    n_steps = pl.num_programs(0)
    slot = i & 1
    slab = tm * TOK_ROWS

    def start_gather(step, to_slot):
        base = (row0 + step * tm) * TOP_K

        def body(r, carry):
            for k in range(TOP_K):
                src = slots_ref[base + r * TOP_K + k]
                pltpu.make_async_copy(
                    yt_hbm.at[pl.ds(pl.multiple_of(src * TOK_ROWS, TOK_ROWS), TOK_ROWS), :],
                    ybuf.at[to_slot, k, pl.ds(pl.multiple_of(r * TOK_ROWS, TOK_ROWS), TOK_ROWS), :],
                    sem.at[to_slot]).start(priority=k)
            return carry
        lax.fori_loop(0, tm, body, 0, unroll=4)

    @pl.when(i == 0)
    def _():
        start_gather(0, 0)

    @pl.when(i + 1 < n_steps)
    def _():
        start_gather(i + 1, 1 - slot)

    for k in range(TOP_K):
        pltpu.make_async_copy(yt_hbm.at[pl.ds(0, slab), :], ybuf.at[slot, k], sem.at[slot]).wait()
    route = route_ref[...]
    x2 = x1_ref[...] + (route[:, 2:3] * _from_slabs(ybuf.at[slot, 0], tm)
                        + route[:, 3:4] * _from_slabs(ybuf.at[slot, 1], tm))
    h = _rms(x2, g3_ref[...]).astype(BF16)
    gate = jax.nn.sigmoid(_dot(h, wpg_ref[...]))
    out_ref[...] = x2 + gate * _dot(p_ref[...].astype(BF16), wple_ref[...])


def _combine_ple(x1_all, route_all, y_slots, tok_slots, p, norm3_g, wpg, wple, m, tm, row0):
    blk0 = row0 // tm
    grid_spec = pltpu.PrefetchScalarGridSpec(
        num_scalar_prefetch=1,
        grid=(m // tm,),
        in_specs=[
            pl.BlockSpec((tm, D_MODEL), lambda i, s: (blk0 + i, 0)),
            pl.BlockSpec((tm, LANE), lambda i, s: (blk0 + i, 0)),
            pl.BlockSpec(memory_space=pl.ANY),
            pl.BlockSpec((tm, PLE_DIM), lambda i, s: (i, 0)),
            pl.BlockSpec((1, D_MODEL), lambda i, s: (0, 0)),
            pl.BlockSpec((D_MODEL, D_MODEL), lambda i, s: (0, 0)),
            pl.BlockSpec((PLE_DIM, D_MODEL), lambda i, s: (0, 0)),
        ],
        out_specs=pl.BlockSpec((tm, D_MODEL), lambda i, s: (i, 0)),
        scratch_shapes=[
            pltpu.VMEM((2, TOP_K, tm * TOK_ROWS, LANE), F32),
            pltpu.SemaphoreType.DMA((2,)),
        ],
    )
    return pl.pallas_call(
        functools.partial(_combine_ple_kernel, tm=tm, row0=row0),
        grid_spec=grid_spec,
        out_shape=jax.ShapeDtypeStruct((m, D_MODEL), F32),
        compiler_params=_params(("arbitrary",)),
    )(tok_slots, x1_all, route_all, y_slots, p, norm3_g.reshape(1, D_MODEL), wpg, wple)


def _routing_tables(route_all):
    flat_e = route_all[:, 0:TOP_K].astype(jnp.int32).reshape(-1)
    onehot = (flat_e[:, None] == jnp.arange(N_EXPERTS, dtype=jnp.int32)[None, :]).astype(jnp.int32)
    csum = jnp.cumsum(onehot, axis=0)
    rank = jnp.take_along_axis(csum, flat_e[:, None], axis=1)[:, 0] - 1
    counts = csum[-1]
    nblk = (counts + MOE_TM - 1) // MOE_TM
    blk_end = jnp.cumsum(nblk)
    blk_start = blk_end - nblk
    slot = blk_start[flat_e] * MOE_TM + rank
    n_used = blk_end[-1].reshape(1)
    pad_start = blk_start * MOE_TM + counts
    pad_count = nblk * MOE_TM - counts
    i32 = lambda a: a.astype(jnp.int32)
    return i32(blk_start), i32(nblk), i32(n_used), i32(slot), i32(pad_start), i32(pad_count)


def kernel(x_prompt, x_sample, p_prompt, p_sample, cache_k_g0, cache_v_g0, cache_k_g1, cache_v_g1,
           cache_k_g2, cache_v_g2, state_conv, norm1_g, w_in, q_gain, k_gain, conv_w, w_conv_out,
           w_attn_out, w_o, norm2_g, w_coarse, b_coarse, w_fine, b_fine, w_gate_e, w_up_e, w_down_e,
           norm3_g, w_ple_gate, w_ple):
    seq = x_prompt.shape[1]
    assert x_prompt.shape == (1, seq, D_MODEL) and norm1_g.shape[0] == 1, "one prompt, one layer"
    n_seq, t_new, _ = x_sample.shape
    assert seq % ATT_UNIT == 0 and t_new == SUBLANE
    caches_k = (cache_k_g0, cache_k_g1, cache_k_g2)
    caches_v = (cache_v_g0, cache_v_g1, cache_v_g2)
    for g in range(N_GROUPS):
        assert caches_k[g].shape == (1, n_seq, WINDOWS[g], HEADS_PER_GROUP, HEAD_DIM)
    n_s = n_seq * t_new
    n_all = seq + n_s

    xp = x_prompt[0]
    xs = x_sample.reshape(n_s, D_MODEL)
    wc = w_conv_out[0].astype(BF16)
    wa = w_attn_out[0].astype(BF16)
    wo = w_o[0].astype(BF16)
    wpg = w_ple_gate[0].astype(BF16)
    wple = w_ple[0].astype(BF16)
    wr = jnp.concatenate(
        [w_coarse[0], jnp.transpose(w_fine[0], (1, 0, 2)).reshape(D_MODEL, N_EXPERTS)], axis=1)
    wr = jnp.pad(wr, ((0, 0), (0, LANE - wr.shape[1])))
    wr_hi = wr.astype(BF16)
    wr_lo = (wr - wr_hi.astype(F32)).astype(BF16)
    br = jnp.pad(jnp.concatenate([b_coarse[0], b_fine[0].reshape(-1)]), (0, LANE - 36)).reshape(1, LANE)

    z_p = _inproj(xp, norm1_g[0], w_in[0], tm=2048, tn=512)
    z_s = _inproj(xs, norm1_g[0], w_in[0], tm=n_s, tn=512)

    att_p = _prompt_attn(z_p, q_gain[0], k_gain[0])
    o_p, pks, pvs = att_p[0], att_p[1:4], att_p[4:7]
    ck = [c.reshape(n_seq, WINDOWS[g] * HEADS_PER_GROUP, HEAD_DIM) for g, c in enumerate(caches_k)]
    cv = [c.reshape(n_seq, WINDOWS[g] * HEADS_PER_GROUP, HEAD_DIM) for g, c in enumerate(caches_v)]
    att_s = _sample_attn(z_s, ck, cv, q_gain[0], k_gain[0], n_seq, t_new)
    o_s, sks, svs = att_s[0], att_s[1:4], att_s[4:7]

    mg_p, u_tail = _merge(z_p, o_p, conv_w[0], wc, wa, tm=512)
    prefix = jnp.pad(state_conv[0], ((0, 0), (0, t_new - 2), (0, 0))).reshape(n_s, D_CONV)
    mg_s, u_s = _merge(z_s, o_s, conv_w[0], wc, wa, tm=n_s, per_seq=t_new, prefix=prefix)
    x1_all, x1t_all, route_all = _outproj_router(xp, mg_p, xs, mg_s, wo, norm2_g[0], wr_hi, wr_lo, br)

    n_blocks = n_all * TOP_K // MOE_TM + N_EXPERTS
    blk_start, nblk, n_used, tok_slots, pad_start, pad_count = _routing_tables(route_all)
    xst = _dispatch(x1t_all, tok_slots, pad_start, pad_count, n_used, n_blocks, tm=n_s)
    y_slots = _moe(xst, norm2_g[0], w_gate_e[0], w_up_e[0], w_down_e[0], blk_start, nblk, n_used,
                   n_blocks)

    y_p = _combine_ple(x1_all, route_all, y_slots, tok_slots, p_prompt[0, 0], norm3_g[0], wpg, wple,
                       m=seq, tm=256, row0=0)
    y_s = _combine_ple(x1_all, route_all, y_slots, tok_slots, p_sample[0].reshape(n_s, PLE_DIM),
                       norm3_g[0], wpg, wple, m=n_s, tm=n_s, row0=seq)

    def state(a, n, g):
        return a.reshape(1, n, WINDOWS[g], HEADS_PER_GROUP, HEAD_DIM)

    pk = [state(a, 1, g) for g, a in enumerate(pks)]
    pv = [state(a, 1, g) for g, a in enumerate(pvs)]
    sk = [state(a, n_seq, g) for g, a in enumerate(sks)]
    sv = [state(a, n_seq, g) for g, a in enumerate(svs)]
    pconv = u_tail[-2:].reshape(1, 1, 2, D_CONV)
    sconv = u_s.reshape(n_seq, t_new, D_CONV)[:, t_new - 2:].reshape(1, n_seq, 2, D_CONV)
    return (y_p.reshape(1, seq, D_MODEL), y_s.reshape(n_seq, t_new, D_MODEL),
            pk[0], pv[0], pk[1], pv[1], pk[2], pv[2], pconv,
            sk[0], sv[0], sk[1], sv[1], sk[2], sv[2], sconv)
```

```python
import functools
import math

import jax
import jax.numpy as jnp
from jax import lax
from jax.experimental import pallas as pl
from jax.experimental.pallas import tpu as pltpu

D_MODEL = 2048
HEAD_DIM = 128
HEADS_PER_GROUP = 4
WINDOWS = (128, 512, 2048)
DILATIONS = (1, 4, 16)
N_GROUPS = 3
N_HEADS = N_GROUPS * HEADS_PER_GROUP
ATT_WIDTH = N_HEADS * HEAD_DIM
ATT_OUT_WIDTH = HEADS_PER_GROUP * HEAD_DIM
SCALE = HEAD_DIM ** -0.5
D_CONV = D_MODEL // 2
PLE_DIM = 256
N_EXPERT_GROUPS = 4
EXPERTS_PER_GROUP = 8
N_EXPERTS = N_EXPERT_GROUPS * EXPERTS_PER_GROUP
TOP_K = 2
D_EXPERT = D_MODEL // 4
EPS = 1e-6

COL_B = 0
COL_C = D_CONV
COL_H = 2 * D_CONV
COL_Q = 3 * D_CONV
COL_K = COL_Q + ATT_WIDTH
COL_V = COL_K + ATT_WIDTH
COL_GC = COL_V + ATT_WIDTH
COL_GA = COL_GC + D_MODEL
IN_COLS = COL_GA + D_MODEL

LANE = 128
SUBLANE = 8
NEG = -1e30
ATT_UNIT = max(WINDOWS)
ATT_BLK = 128
ATT_TILE_UNROLL = 8
MXU_N = 256
TOK_ROWS = D_MODEL // LANE
MOE_TM = 288
VMEM_LIMIT = 56 * 1024 * 1024

BF16 = jnp.bfloat16
F32 = jnp.float32


def _params(sem):
    return pltpu.CompilerParams(dimension_semantics=sem, vmem_limit_bytes=VMEM_LIMIT)


def _rms(x, gain):
    return x * lax.rsqrt(jnp.mean(x * x, axis=-1, keepdims=True) + EPS) * gain


def _dot(a, b):
    return jnp.dot(a, b, preferred_element_type=F32)


def _dot_nt(a, b):
    return lax.dot_general(a, b, (((1,), (1,)), ((), ())), preferred_element_type=F32)


def _shift_copies(cache_refs, new_refs, out_refs, sems):
    descs = []
    for a, (src, new, dst) in enumerate(zip(cache_refs, new_refs, out_refs)):
        rows, t = src.shape[1], new.shape[1]
        descs.append(pltpu.make_async_copy(src.at[:, pl.ds(t, rows - t), :],
                                           dst.at[:, pl.ds(0, rows - t), :], sems.at[2 * a]))
        descs.append(pltpu.make_async_copy(new, dst.at[:, pl.ds(rows - t, t), :], sems.at[2 * a + 1]))
    return descs


def _hosted_shift(shift_refs, n_shift):
    if n_shift == 0:
        return
    descs = _shift_copies(shift_refs[:n_shift], shift_refs[n_shift:2 * n_shift],
                          shift_refs[2 * n_shift:3 * n_shift], shift_refs[3 * n_shift])
    ids = [pl.program_id(d) for d in range(2)]
    first = (ids[0] == 0) & (ids[1] == 0)
    last = (ids[0] == pl.num_programs(0) - 1) & (ids[1] == pl.num_programs(1) - 1)

    @pl.when(first)
    def _():
        for d in descs:
            d.start(priority=1)

    @pl.when(last)
    def _():
        for d in descs:
            d.wait()


def _shift_specs(shift):
    if shift is None:
        return [], [], [], [], []
    caches, news = shift
    any_spec = pl.BlockSpec(memory_space=pl.ANY)
    n = len(caches)
    return ([any_spec] * (2 * n), list(caches) + list(news), [any_spec] * n,
            [jax.ShapeDtypeStruct(c.shape, c.dtype) for c in caches],
            [pltpu.SemaphoreType.DMA((2 * n,))])


def _inproj_kernel(x_ref, g_ref, w_ref, *rest, n_shift):
    z_ref = rest[2 * n_shift]
    h_scr = rest[3 * n_shift + 1]
    _hosted_shift(rest[:2 * n_shift] + rest[2 * n_shift + 1:3 * n_shift + 1] + rest[3 * n_shift + 2:],
                  n_shift)

    @pl.when(pl.program_id(1) == 0)
    def _():
        h_scr[...] = _rms(x_ref[...], g_ref[...]).astype(BF16)

    for c in range(w_ref.shape[1] // MXU_N):
        cols = slice(c * MXU_N, (c + 1) * MXU_N)
        z_ref[:, cols] = _dot(h_scr[...], w_ref[:, cols].astype(BF16))


def _inproj(x, norm_g, w_in, tm, tn, shift=None):
    m = x.shape[0]
    s_in, s_args, s_out, s_shape, s_scr = _shift_specs(shift)
    res = pl.pallas_call(
        functools.partial(_inproj_kernel, n_shift=len(s_out)),
        grid=(m // tm, IN_COLS // tn),
        in_specs=[
            pl.BlockSpec((tm, D_MODEL), lambda i, j: (i, 0), pipeline_mode=pl.Buffered(1)),
            pl.BlockSpec((1, D_MODEL), lambda i, j: (0, 0)),
            pl.BlockSpec((D_MODEL, tn), lambda i, j: (0, j)),
        ] + s_in,
        out_specs=[pl.BlockSpec((tm, tn), lambda i, j: (i, j))] + s_out,
        out_shape=[jax.ShapeDtypeStruct((m, IN_COLS), F32)] + s_shape,
        scratch_shapes=[pltpu.VMEM((tm, D_MODEL), BF16)] + s_scr,
        compiler_params=_params(("arbitrary", "arbitrary")),
    )(x, norm_g.reshape(1, D_MODEL), w_in, *s_args)
    return res[0], list(res[1:])


def _prompt_attn_kernel(*refs, n_shift):
    n_in, n_out, n_scr = 17, 7, 5
    shift_refs = (refs[n_in:n_in + 2 * n_shift]
                  + refs[n_in + 2 * n_shift + n_out:n_in + 3 * n_shift + n_out]
                  + refs[n_in + 3 * n_shift + n_out + n_scr:])
    refs = (refs[:n_in] + refs[n_in + 2 * n_shift:n_in + 2 * n_shift + n_out]
            + refs[n_in + 3 * n_shift + n_out:n_in + 3 * n_shift + n_out + n_scr])
    (q0, q1, q2, kc0, kc1, kc2, vc0, vc1, vc2, kp0, kp1, kp2, vp0, vp1, vp2, qg_ref, kg_ref,
     o_ref, pk0, pk1, pk2, pv0, pv1, pv2, kext, vext, acc_s, m_s, l_s) = refs
    _hosted_shift(shift_refs, n_shift)
    q_refs, kc_refs, vc_refs = (q0, q1, q2), (kc0, kc1, kc2), (vc0, vc1, vc2)
    kp_refs, vp_refs = (kp0, kp1, kp2), (vp0, vp1, vp2)
    pk_refs, pv_refs = (pk0, pk1, pk2), (pv0, pv1, pv2)
    n = pl.program_id(1)
    last = pl.num_programs(1) - 1

    qi = lax.broadcasted_iota(jnp.int32, (ATT_BLK, 2 * ATT_BLK), 0) + ATT_BLK
    ki = lax.broadcasted_iota(jnp.int32, (ATT_BLK, 2 * ATT_BLK), 1)
    dist = qi - ki
    band = (dist >= 0) & (dist <= ATT_BLK)

    for g in range(N_GROUPS):
        win, dil = WINDOWS[g], DILATIONS[g]
        q_ref = q_refs[g]
        qgain = qg_ref[g, 0]
        kgain = kg_ref[g, 0]
        kext[0:win, :] = _rms(kp_refs[g][...], kgain)
        kext[win:win + ATT_UNIT, :] = _rms(kc_refs[g][...], kgain)
        vext[0:win, :] = vp_refs[g][...]
        vext[win:win + ATT_UNIT, :] = vc_refs[g][...]

        @pl.when(n == last)
        def _():
            pk_refs[g][...] = kext[ATT_UNIT:ATT_UNIT + win, :]
            pv_refs[g][...] = vext[ATT_UNIT:ATT_UNIT + win, :]

        def tile(t, carry, g=g, win=win, dil=dil, q_ref=q_ref, qgain=qgain):
            u = t // dil
            r = t % dil
            off = u * win + r
            if dil == 1:
                off = pl.multiple_of(off, ATT_BLK)
                rows_q = pl.ds(off, ATT_BLK)
                rows_k = pl.ds(off, 2 * ATT_BLK)
            else:
                rows_q = pl.ds(off, ATT_BLK, stride=dil)
                rows_k = pl.ds(off, 2 * ATT_BLK, stride=dil)
            q = _rms(q_ref[rows_q, :], qgain).astype(BF16)
            k = kext[rows_k, :].astype(BF16)
            v = vext[rows_k, :].astype(BF16)
            s = _dot_nt(q, k) * SCALE
            k_min = jnp.where((n > 0) | (u > 0), 0, ATT_BLK)
            s = jnp.where(band & (ki >= k_min), s, NEG)
            m_t = jnp.max(s, axis=-1, keepdims=True)
            p = jnp.exp(s - m_t)
            l_t = jnp.sum(p, axis=-1, keepdims=True)
            m_s[g, rows_q, :] = jnp.broadcast_to(m_t, (ATT_BLK, LANE))
            l_s[g, rows_q, :] = jnp.broadcast_to(l_t, (ATT_BLK, LANE))
            acc_s[g, rows_q, :] = _dot(p.astype(BF16), v)
            return carry

        lax.fori_loop(0, ATT_UNIT // ATT_BLK, tile, 0, unroll=ATT_TILE_UNROLL)

    def finish(c, carry):
        rows = pl.ds(pl.multiple_of(c * ATT_BLK, ATT_BLK), ATT_BLK)
        m = [m_s[g, rows, :] for g in range(N_GROUPS)]
        m_all = jnp.maximum(jnp.maximum(m[0], m[1]), m[2])
        num = jnp.zeros((ATT_BLK, LANE), F32)
        den = jnp.zeros((ATT_BLK, LANE), F32)
        for g in range(N_GROUPS):
            w = jnp.exp(m[g] - m_all)
            num = num + w * acc_s[g, rows, :]
            den = den + w * l_s[g, rows, :]
        o_ref[rows, :] = num / den
        return carry

    lax.fori_loop(0, ATT_UNIT // ATT_BLK, finish, 0, unroll=2)


def _prompt_attn(z, q_gain, k_gain, shift=None):
    s_len = z.shape[0]
    s_in, s_args, s_out, s_shape, s_scr = _shift_specs(shift)
    n_units = s_len // ATT_UNIT
    in_specs, args = [], []

    def col(base, g):
        return lambda hh, n: (n, base // LANE + g * HEADS_PER_GROUP + hh)

    for g in range(N_GROUPS):
        in_specs.append(pl.BlockSpec((ATT_UNIT, LANE), col(COL_Q, g)))
    for base in (COL_K, COL_V):
        for g in range(N_GROUPS):
            in_specs.append(pl.BlockSpec((ATT_UNIT, LANE), col(base, g)))
    for base in (COL_K, COL_V):
        for g in range(N_GROUPS):
            per = ATT_UNIT // WINDOWS[g]
            in_specs.append(pl.BlockSpec(
                (WINDOWS[g], LANE),
                lambda hh, n, base=base, g=g, per=per: (
                    jnp.maximum(n * per - 1, 0), base // LANE + g * HEADS_PER_GROUP + hh)))
    args = [z] * 15
    gain_spec = pl.BlockSpec((N_GROUPS, 1, 1, LANE), lambda hh, n: (0, hh, 0, 0))
    in_specs += [gain_spec, gain_spec]
    args += [q_gain.reshape(N_GROUPS, HEADS_PER_GROUP, 1, LANE),
             k_gain.reshape(N_GROUPS, HEADS_PER_GROUP, 1, LANE)]

    out_specs = [pl.BlockSpec((ATT_UNIT, LANE), lambda hh, n: (n, hh))]
    out_shape = [jax.ShapeDtypeStruct((s_len, ATT_OUT_WIDTH), F32)]
    for _ in range(2):
        for g in range(N_GROUPS):
            out_specs.append(pl.BlockSpec((WINDOWS[g], LANE), lambda hh, n: (0, hh)))
            out_shape.append(jax.ShapeDtypeStruct((WINDOWS[g], ATT_OUT_WIDTH), F32))

    res = pl.pallas_call(
        functools.partial(_prompt_attn_kernel, n_shift=len(s_out)),
        grid=(HEADS_PER_GROUP, n_units),
        in_specs=in_specs + s_in,
        out_specs=out_specs + s_out,
        out_shape=out_shape + s_shape,
        scratch_shapes=[
            pltpu.VMEM((2 * ATT_UNIT, LANE), F32),
            pltpu.VMEM((2 * ATT_UNIT, LANE), F32),
            pltpu.VMEM((N_GROUPS, ATT_UNIT, LANE), F32),
            pltpu.VMEM((N_GROUPS, ATT_UNIT, LANE), F32),
            pltpu.VMEM((N_GROUPS, ATT_UNIT, LANE), F32),
        ] + s_scr,
        compiler_params=_params(("arbitrary", "arbitrary")),
    )(*args, *s_args)
    return res[:7], list(res[7:])


def _sample_attn_kernel(*refs):
    (q0, q1, q2, kn0, kn1, kn2, vn0, vn1, vn2, ck0, ck1, ck2, cv0, cv1, cv2, qg_ref, kg_ref,
     o_ref, sk0, sk1, sk2, sv0, sv1, sv2) = refs
    q_refs, kn_refs, vn_refs = (q0, q1, q2), (kn0, kn1, kn2), (vn0, vn1, vn2)
    ck_refs, cv_refs = (ck0, ck1, ck2), (cv0, cv1, cv2)
    sk_refs, sv_refs = (sk0, sk1, sk2), (sv0, sv1, sv2)
    t_new = q0.shape[0]
    nh = HEADS_PER_GROUP
    n_q = nh * t_new
    t_bits = t_new.bit_length() - 1

    def heads(ref):
        return [ref[:, h * LANE:(h + 1) * LANE] for h in range(nh)]

    parts = []
    for g in range(N_GROUPS):
        win, dil = WINDOWS[g], DILATIONS[g]
        q_all = jnp.concatenate(
            [_rms(x, qg_ref[g, h]) for h, x in enumerate(heads(q_refs[g]))], axis=0).astype(BF16)
        k_new = [_rms(x, kg_ref[g, h]) for h, x in enumerate(heads(kn_refs[g]))]
        v_new = heads(vn_refs[g])
        k_new_all = jnp.concatenate(k_new, axis=0).astype(BF16)
        v_new_all = jnp.concatenate(v_new, axis=0).astype(BF16)
        ck = ck_refs[g][0]
        cv = cv_refs[g][0]
        for h in range(nh):
            sk_refs[g][0, pl.ds(h, t_new, stride=nh), :] = k_new[h]
            sv_refs[g][0, pl.ds(h, t_new, stride=nh), :] = v_new[h]

        s_c = _dot_nt(q_all, ck.astype(BF16)) * SCALE
        s_n = _dot_nt(q_all, k_new_all) * SCALE
        row = lax.broadcasted_iota(jnp.int32, (n_q, win * nh), 0)
        col = lax.broadcasted_iota(jnp.int32, (n_q, win * nh), 1)
        i_q, pos = row & (t_new - 1), col >> 2
        mask_c = (((col & (nh - 1)) == (row >> t_bits)) & (pos >= i_q)
                  & (((pos - i_q) & (dil - 1)) == 0))
        row_n = lax.broadcasted_iota(jnp.int32, (n_q, n_q), 0)
        col_n = lax.broadcasted_iota(jnp.int32, (n_q, n_q), 1)
        d_n = (row_n & (t_new - 1)) - (col_n & (t_new - 1))
        mask_n = ((row_n >> t_bits) == (col_n >> t_bits)) & (d_n >= 0) & ((d_n & (dil - 1)) == 0)
        s_c = jnp.where(mask_c, s_c, NEG)
        s_n = jnp.where(mask_n, s_n, NEG)
        m = jnp.maximum(jnp.max(s_c, axis=-1, keepdims=True), jnp.max(s_n, axis=-1, keepdims=True))
        p_c = jnp.exp(s_c - m)
        p_n = jnp.exp(s_n - m)
        l = jnp.sum(p_c, axis=-1, keepdims=True) + jnp.sum(p_n, axis=-1, keepdims=True)
        acc = _dot(p_c.astype(BF16), cv.astype(BF16)) + _dot(p_n.astype(BF16), v_new_all)
        parts.append((m, l, acc))

    m_all = jnp.maximum(jnp.maximum(parts[0][0], parts[1][0]), parts[2][0])
    num = jnp.zeros((n_q, LANE), F32)
    den = jnp.zeros((n_q, 1), F32)
    for m, l, acc in parts:
        w = jnp.exp(m - m_all)
        num = num + w * acc
        den = den + w * l
    o_all = num / den
    for h in range(nh):
        o_ref[:, h * LANE:(h + 1) * LANE] = o_all[h * t_new:(h + 1) * t_new, :]


def _sample_attn(z, caches_k, caches_v, q_gain, k_gain, n_seq, t_new):
    assert HEADS_PER_GROUP == 4 and t_new & (t_new - 1) == 0
    in_specs, args = [], []
    for base in (COL_Q, COL_K, COL_V):
        for g in range(N_GROUPS):
            in_specs.append(pl.BlockSpec(
                (t_new, ATT_OUT_WIDTH), lambda b, base=base, g=g: (b, base // ATT_OUT_WIDTH + g)))
            args.append(z)
    for caches in (caches_k, caches_v):
        for g in range(N_GROUPS):
            in_specs.append(pl.BlockSpec((1, WINDOWS[g] * HEADS_PER_GROUP, LANE), lambda b: (b, 0, 0)))
            args.append(caches[g])
    gain_spec = pl.BlockSpec((N_GROUPS, HEADS_PER_GROUP, 1, LANE), lambda b: (0, 0, 0, 0))
    in_specs += [gain_spec, gain_spec]
    args += [q_gain.reshape(N_GROUPS, HEADS_PER_GROUP, 1, LANE),
             k_gain.reshape(N_GROUPS, HEADS_PER_GROUP, 1, LANE)]

    out_specs = [pl.BlockSpec((t_new, ATT_OUT_WIDTH), lambda b: (b, 0))]
    out_shape = [jax.ShapeDtypeStruct((n_seq * t_new, ATT_OUT_WIDTH), F32)]
    for _ in range(2 * N_GROUPS):
        out_specs.append(pl.BlockSpec((1, t_new * HEADS_PER_GROUP, LANE), lambda b: (b, 0, 0)))
        out_shape.append(jax.ShapeDtypeStruct((n_seq, t_new * HEADS_PER_GROUP, LANE), F32))

    return pl.pallas_call(
        _sample_attn_kernel,
        grid=(n_seq,),
        in_specs=in_specs,
        out_specs=out_specs,
        out_shape=out_shape,
        compiler_params=_params(("parallel",)),
    )(*args)


def _conv_taps(u, r1, r2, cw_ref):
    return cw_ref[2:3, :] * u + cw_ref[1:2, :] * r1 + cw_ref[0:1, :] * r2


def _merge_kernel(*refs, per_seq, n_chunks):
    b_ref, c_ref, h_ref, pc_ref, ph_ref, cw_ref, o_ref = refs[:7]
    gc_refs = refs[7:7 + n_chunks]
    ga_refs = refs[7 + n_chunks:7 + 2 * n_chunks]
    wc_ref, wa_ref, out_ref, u_ref, yb_scr = refs[7 + 2 * n_chunks:]
    i = pl.program_id(0)
    tm = b_ref.shape[0]
    u = c_ref[...] * h_ref[...]
    r1 = pltpu.roll(u, 1, axis=0)
    r2 = pltpu.roll(u, 2, axis=0)
    if per_seq is None:
        yb_scr[...] = b_ref[...] * _conv_taps(u, r1, r2, cw_ref)
        up = jnp.where(i > 0, pc_ref[...] * ph_ref[...], 0.0)
        row = lax.broadcasted_iota(jnp.int32, (SUBLANE, D_CONV), 0)
        u8 = u[0:SUBLANE, :]
        r1_8 = jnp.where(row < 1, pltpu.roll(up, 1, axis=0), pltpu.roll(u8, 1, axis=0))
        r2_8 = jnp.where(row < 2, pltpu.roll(up, 2, axis=0), pltpu.roll(u8, 2, axis=0))
        yb_scr[0:SUBLANE, :] = b_ref[0:SUBLANE, :] * _conv_taps(u8, r1_8, r2_8, cw_ref)
        u_ref[...] = u[tm - SUBLANE:tm, :]
    else:
        pre = pc_ref[...]
        t = lax.broadcasted_iota(jnp.int32, (tm, D_CONV), 0) & (per_seq - 1)
        r1 = jnp.where(t == 0, pltpu.roll(pre, tm - 1, axis=0), r1)
        r2 = jnp.where(t < 2, pre, r2)
        yb_scr[...] = b_ref[...] * _conv_taps(u, r1, r2, cw_ref)
        u_ref[...] = u

    yb = yb_scr[...].astype(BF16)
    ob = o_ref[...].astype(BF16)
    tn = D_MODEL // n_chunks
    for c in range(n_chunks):
        cols = slice(c * tn, (c + 1) * tn)
        y_conv = _dot(yb, wc_ref[:, cols])
        y_att = _dot(ob, wa_ref[:, cols])
        out_ref[:, cols] = (jax.nn.sigmoid(gc_refs[c][...]) * y_conv
                            + jax.nn.sigmoid(ga_refs[c][...]) * y_att).astype(BF16)


def _merge(z, o_att, conv_w, wc, wa, tm, per_seq=None, prefix=None):
    m = z.shape[0]
    tn = ATT_OUT_WIDTH
    n_chunks = D_MODEL // tn
    if per_seq is None:
        prev = lambda i, c: (jnp.maximum(i * (tm // SUBLANE) - 1, 0), c)
        pc_spec = pl.BlockSpec((SUBLANE, D_CONV), lambda i: prev(i, COL_C // D_CONV))
        ph_spec = pl.BlockSpec((SUBLANE, D_CONV), lambda i: prev(i, COL_H // D_CONV))
        pc_arg, ph_arg = z, z
        u_rows = SUBLANE
    else:
        pc_spec = pl.BlockSpec((tm, D_CONV), lambda i: (i, 0))
        ph_spec = pl.BlockSpec((SUBLANE, D_CONV), lambda i: (0, 0))
        pc_arg, ph_arg = prefix, prefix
        u_rows = tm
    gate_specs = [pl.BlockSpec((tm, tn), lambda i, base=base, c=c: (i, base // tn + c))
                  for base in (COL_GC, COL_GA) for c in range(n_chunks)]
    return pl.pallas_call(
        functools.partial(_merge_kernel, per_seq=per_seq, n_chunks=n_chunks),
        grid=(m // tm,),
        in_specs=[
            pl.BlockSpec((tm, D_CONV), lambda i: (i, COL_B // D_CONV)),
            pl.BlockSpec((tm, D_CONV), lambda i: (i, COL_C // D_CONV)),
            pl.BlockSpec((tm, D_CONV), lambda i: (i, COL_H // D_CONV)),
            pc_spec,
            ph_spec,
            pl.BlockSpec((3, D_CONV), lambda i: (0, 0)),
            pl.BlockSpec((tm, ATT_OUT_WIDTH), lambda i: (i, 0)),
            *gate_specs,
            pl.BlockSpec((D_CONV, D_MODEL), lambda i: (0, 0)),
            pl.BlockSpec((ATT_OUT_WIDTH, D_MODEL), lambda i: (0, 0)),
        ],
        out_specs=[
            pl.BlockSpec((tm, D_MODEL), lambda i: (i, 0)),
            pl.BlockSpec((u_rows, D_CONV), lambda i: (i, 0)),
        ],
        out_shape=[
            jax.ShapeDtypeStruct((m, D_MODEL), BF16),
            jax.ShapeDtypeStruct((m // tm * u_rows, D_CONV), F32),
        ],
        scratch_shapes=[pltpu.VMEM((tm, D_CONV), F32)],
        compiler_params=_params(("parallel",)),
    )(z, z, z, pc_arg, ph_arg, conv_w, o_att, *([z] * (2 * n_chunks)), wc, wa)


def _outproj_router_kernel(xp_ref, mgp_ref, xs_ref, mgs_ref, wo_ref, g2_ref, wr_hi_ref, wr_lo_ref,
                           br_ref, x1_ref, x1t_ref, route_ref):
    is_sample = pl.program_id(0) == pl.num_programs(0) - 1

    @pl.when(jnp.logical_not(is_sample))
    def _():
        _outproj_router_tile(xp_ref, mgp_ref, wo_ref, g2_ref, wr_hi_ref, wr_lo_ref, br_ref,
                             x1_ref, x1t_ref, route_ref)

    @pl.when(is_sample)
    def _():
        _outproj_router_tile(xs_ref, mgs_ref, wo_ref, g2_ref, wr_hi_ref, wr_lo_ref, br_ref,
                             x1_ref, x1t_ref, route_ref)


def _outproj_router_tile(x_ref, mg_ref, wo_ref, g2_ref, wr_hi_ref, wr_lo_ref, br_ref, x1_ref,
                         x1t_ref, route_ref):
    x1 = x_ref[...] + _dot(mg_ref[...], wo_ref[...])
    x1_ref[...] = x1
    _to_slabs(x1t_ref, x1)
    h = _rms(x1, g2_ref[...])
    h_hi = h.astype(BF16)
    h_lo = (h - h_hi.astype(F32)).astype(BF16)
    logits = (_dot(h_hi, wr_hi_ref[...]) + _dot(h_hi, wr_lo_ref[...]) + _dot(h_lo, wr_hi_ref[...])
              + br_ref[...])
    lane = lax.broadcasted_iota(jnp.int32, logits.shape, 1)
    is_coarse = lane < N_EXPERT_GROUPS
    coarse = jnp.where(is_coarse, logits, NEG)
    cmax = jnp.max(coarse, axis=-1, keepdims=True)
    grp = jnp.min(jnp.where(coarse == cmax, lane, LANE), axis=-1, keepdims=True)
    p_grp = 1.0 / jnp.sum(jnp.where(is_coarse, jnp.exp(coarse - cmax), 0.0), axis=-1, keepdims=True)
    eid = lane - N_EXPERT_GROUPS
    in_grp = (eid >= 0) & (eid < N_EXPERTS) & ((eid >> 3) == grp)
    fine = jnp.where(in_grp, logits, NEG)
    v1 = jnp.max(fine, axis=-1, keepdims=True)
    i1 = jnp.min(jnp.where(fine == v1, lane, LANE), axis=-1, keepdims=True)
    fine2 = jnp.where(lane == i1, NEG, fine)
    v2 = jnp.max(fine2, axis=-1, keepdims=True)
    i2 = jnp.min(jnp.where(fine2 == v2, lane, LANE), axis=-1, keepdims=True)
    e = jnp.exp(v2 - v1)
    gate1 = p_grp / (1.0 + e)
    gate2 = p_grp * e / (1.0 + e)
    route = jnp.where(lane == 0, (i1 - N_EXPERT_GROUPS).astype(F32),
                      jnp.where(lane == 1, (i2 - N_EXPERT_GROUPS).astype(F32),
                                jnp.where(lane == 2, gate1, jnp.where(lane == 3, gate2, 0.0))))
    route_ref[...] = route


def _outproj_router(xp, mg_p, xs, mg_s, wo, norm2_g, wr_hi, wr_lo, br):
    m_p, tm = xp.shape[0], xs.shape[0]
    assert m_p % tm == 0
    n_p = m_p // tm
    prompt_blk = lambda i: (jnp.minimum(i, n_p - 1), 0)
    fixed = lambda i: (0, 0)
    return pl.pallas_call(
        _outproj_router_kernel,
        grid=(n_p + 1,),
        in_specs=[
            pl.BlockSpec((tm, D_MODEL), prompt_blk),
            pl.BlockSpec((tm, D_MODEL), prompt_blk),
            pl.BlockSpec((tm, D_MODEL), fixed),
            pl.BlockSpec((tm, D_MODEL), fixed),
            pl.BlockSpec((D_MODEL, D_MODEL), fixed),
            pl.BlockSpec((1, D_MODEL), fixed),
            pl.BlockSpec((D_MODEL, LANE), fixed),
            pl.BlockSpec((D_MODEL, LANE), fixed),
            pl.BlockSpec((1, LANE), fixed),
        ],
        out_specs=[
            pl.BlockSpec((tm, D_MODEL), lambda i: (i, 0)),
            pl.BlockSpec((tm * TOK_ROWS, LANE), lambda i: (i, 0)),
            pl.BlockSpec((tm, LANE), lambda i: (i, 0)),
        ],
        out_shape=[
            jax.ShapeDtypeStruct((m_p + tm, D_MODEL), F32),
            jax.ShapeDtypeStruct(((m_p + tm) * TOK_ROWS, LANE), F32),
            jax.ShapeDtypeStruct((m_p + tm, LANE), F32),
        ],
        compiler_params=_params(("arbitrary",)),
    )(xp, mg_p, xs, mg_s, wo, norm2_g.reshape(1, D_MODEL), wr_hi, wr_lo, br)


def _to_slabs(dst_ref, x):
    rows = x.shape[0]
    for s in range(TOK_ROWS):
        dst_ref[pl.ds(s, rows, stride=TOK_ROWS), :] = x[:, s * LANE:(s + 1) * LANE]


def _from_slabs(src_ref, rows):
    return jnp.concatenate(
        [src_ref[pl.ds(s, rows, stride=TOK_ROWS), :] for s in range(TOK_ROWS)], axis=1)


def _dispatch_kernel(slots_ref, pad0_ref, padn_ref, nused_ref, x1t_hbm, xst_hbm, buf, zero, lsem,
                     ssem, zsem, *, tm, n_blocks):
    i = pl.program_id(0)
    n_steps = pl.num_programs(0)
    slab = tm * TOK_ROWS

    def load(step):
        return pltpu.make_async_copy(
            x1t_hbm.at[pl.ds(pl.multiple_of(step * slab, slab), slab), :], buf.at[step % 3],
            lsem.at[step % 3])

    def wait_scatter(step):
        for _ in range(TOP_K):
            pltpu.make_async_copy(buf.at[step % 3], xst_hbm.at[pl.ds(0, slab), :],
                                  ssem.at[step % 3]).wait()

    def pad_rows(start):
        def expert(e, carry):
            def row(j, c):
                dst = pl.multiple_of((pad0_ref[e] + j) * TOK_ROWS, TOK_ROWS)
                cp = pltpu.make_async_copy(zero, xst_hbm.at[pl.ds(dst, TOK_ROWS), :], zsem.at[0])
                if start:
                    cp.start()
                else:
                    cp.wait()
                return c
            return lax.fori_loop(0, padn_ref[e], row, carry)
        lax.fori_loop(0, N_EXPERTS, expert, 0)

    @pl.when(i == 0)
    def _():
        load(0).start()
        zero[...] = jnp.zeros(zero.shape, F32)
        pad_rows(start=True)

    @pl.when(i >= 2)
    def _():
        wait_scatter(i - 2)

    @pl.when(i + 1 < n_steps)
    def _():
        load(i + 1).start()

    load(i).wait()
    cur = i % 3

    def body(r, carry):
        src = buf.at[cur, pl.ds(pl.multiple_of(r * TOK_ROWS, TOK_ROWS), TOK_ROWS), :]
        for k in range(TOP_K):
            dst = pl.multiple_of(slots_ref[(i * tm + r) * TOP_K + k] * TOK_ROWS, TOK_ROWS)
            pltpu.make_async_copy(src, xst_hbm.at[pl.ds(dst, TOK_ROWS), :],
                                  ssem.at[cur]).start(priority=k)
        return carry
    lax.fori_loop(0, tm, body, 0, unroll=4)

    @pl.when(i == n_steps - 1)
    def _():
        @pl.when(i >= 1)
        def _():
            wait_scatter(i - 1)
        wait_scatter(i)
        pad_rows(start=False)
        chunk = math.gcd(tm, MOE_TM)
        rows = chunk * TOK_ROWS
        buf[0, 0:rows, :] = jnp.zeros((rows, LANE), F32)
        first = nused_ref[0] * MOE_TM
        n_tail = (n_blocks - nused_ref[0]) * (MOE_TM // chunk)

        def tail(j, carry, start):
            dst = pl.multiple_of((first + j * chunk) * TOK_ROWS, TOK_ROWS)
            cp = pltpu.make_async_copy(buf.at[0, pl.ds(0, rows), :],
                                       xst_hbm.at[pl.ds(dst, rows), :], zsem.at[0])
            if start:
                cp.start()
            else:
                cp.wait()
            return carry
        lax.fori_loop(0, n_tail, functools.partial(tail, start=True), 0)
        lax.fori_loop(0, n_tail, functools.partial(tail, start=False), 0)


def _dispatch(x1t, tok_slots, pad_start, pad_count, n_used, n_blocks, tm):
    n_tok = x1t.shape[0] // TOK_ROWS
    assert n_tok % tm == 0
    grid_spec = pltpu.PrefetchScalarGridSpec(
        num_scalar_prefetch=4,
        grid=(n_tok // tm,),
        in_specs=[pl.BlockSpec(memory_space=pl.ANY)],
        out_specs=pl.BlockSpec(memory_space=pl.ANY),
        scratch_shapes=[
            pltpu.VMEM((3, tm * TOK_ROWS, LANE), F32),
            pltpu.VMEM((TOK_ROWS, LANE), F32),
            pltpu.SemaphoreType.DMA((3,)),
            pltpu.SemaphoreType.DMA((3,)),
            pltpu.SemaphoreType.DMA((1,)),
        ],
    )
    return pl.pallas_call(
        functools.partial(_dispatch_kernel, tm=tm, n_blocks=n_blocks),
        grid_spec=grid_spec,
        out_shape=jax.ShapeDtypeStruct((n_blocks * MOE_TM * TOK_ROWS, LANE), F32),
        compiler_params=_params(("arbitrary",)),
    )(tok_slots, pad_start, pad_count, n_used, x1t)


def _moe_kernel(blk0_ref, nblk_ref, nused_ref, xst_hbm, g2_ref, wg_ref, wu_ref, wd_ref,
                yt_hbm, xbuf, ybuf, gsem, osem, wg_b, wu_b, wd_b):
    e = pl.program_id(0)
    n_used = nused_ref[0]
    slab = MOE_TM * TOK_ROWS

    def in_copy(blk, to_slot):
        return pltpu.make_async_copy(
            xst_hbm.at[pl.ds(pl.multiple_of(blk * slab, slab), slab), :], xbuf.at[to_slot],
            gsem.at[to_slot])

    def out_copy(blk, from_slot):
        return pltpu.make_async_copy(
            ybuf.at[from_slot], yt_hbm.at[pl.ds(pl.multiple_of(blk * slab, slab), slab), :],
            osem.at[from_slot])

    wg_b[...] = wg_ref[0].astype(BF16)
    wu_b[...] = wu_ref[0].astype(BF16)
    wd_b[...] = wd_ref[0].astype(BF16)

    @pl.when(e == 0)
    def _():
        in_copy(0, 0).start()

    def block(b, carry):
        blk = blk0_ref[e] + b
        slot = blk & 1

        @pl.when(blk + 1 < n_used)
        def _():
            in_copy(blk + 1, 1 - slot).start()

        in_copy(blk, slot).wait()
        h = _rms(_from_slabs(xbuf.at[slot], MOE_TM), g2_ref[...]).astype(BF16)
        a = _dot(h, wg_b[...])
        u = _dot(h, wu_b[...])
        hm = (a * jax.nn.sigmoid(a) * u).astype(BF16)
        y = _dot(hm, wd_b[...])

        @pl.when(blk >= 2)
        def _():
            out_copy(blk - 2, slot).wait()

        _to_slabs(ybuf.at[slot], y)
        out_copy(blk, slot).start()
        return carry

    lax.fori_loop(0, nblk_ref[e], block, 0)

    @pl.when(e == pl.num_programs(0) - 1)
    def _():
        @pl.when(n_used >= 2)
        def _():
            out_copy(n_used - 2, (n_used - 2) & 1).wait()
        out_copy(n_used - 1, (n_used - 1) & 1).wait()


def _moe(xst, norm2_g, w_gate_e, w_up_e, w_down_e, blk_start, nblk, n_used):
    w_idx = lambda e, *_: (e, 0, 0)
    slab = MOE_TM * TOK_ROWS
    grid_spec = pltpu.PrefetchScalarGridSpec(
        num_scalar_prefetch=3,
        grid=(N_EXPERTS,),
        in_specs=[
            pl.BlockSpec(memory_space=pl.ANY),
            pl.BlockSpec((1, D_MODEL), lambda e, *_: (0, 0)),
            pl.BlockSpec((1, D_MODEL, D_EXPERT), w_idx),
            pl.BlockSpec((1, D_MODEL, D_EXPERT), w_idx),
            pl.BlockSpec((1, D_EXPERT, D_MODEL), w_idx),
        ],
        out_specs=pl.BlockSpec(memory_space=pl.ANY),
        scratch_shapes=[
            pltpu.VMEM((2, slab, LANE), F32),
            pltpu.VMEM((2, slab, LANE), F32),
            pltpu.SemaphoreType.DMA((2,)),
            pltpu.SemaphoreType.DMA((2,)),
            pltpu.VMEM((D_MODEL, D_EXPERT), BF16),
            pltpu.VMEM((D_MODEL, D_EXPERT), BF16),
            pltpu.VMEM((D_EXPERT, D_MODEL), BF16),
        ],
    )
    return pl.pallas_call(
        _moe_kernel,
        grid_spec=grid_spec,
        out_shape=jax.ShapeDtypeStruct(xst.shape, F32),
        input_output_aliases={3: 0},
        compiler_params=_params(("arbitrary",)),
    )(blk_start, nblk, n_used, xst, norm2_g.reshape(1, D_MODEL), w_gate_e, w_up_e, w_down_e)


def _combine_ple_kernel(slots_ref, x1_ref, route_ref, yt_hbm, p_ref, g3_ref, wpg_ref, wple_ref,
                        out_ref, ybuf, sem, *, tm, row0):
    i = pl.program_id(0)
    n_steps = pl.num_programs(0)
    slot = i & 1
    slab = tm * TOK_ROWS

    def start_gather(step, to_slot):
        base = (row0 + step * tm) * TOP_K

        def body(r, carry):
            for k in range(TOP_K):
                src = slots_ref[base + r * TOP_K + k]
                pltpu.make_async_copy(
                    yt_hbm.at[pl.ds(pl.multiple_of(src * TOK_ROWS, TOK_ROWS), TOK_ROWS), :],
                    ybuf.at[to_slot, k, pl.ds(pl.multiple_of(r * TOK_ROWS, TOK_ROWS), TOK_ROWS), :],
                    sem.at[to_slot]).start(priority=k)
            return carry
        lax.fori_loop(0, tm, body, 0, unroll=4)

    @pl.when(i == 0)
    def _():
        start_gather(0, 0)

    @pl.when(i + 1 < n_steps)
    def _():
        start_gather(i + 1, 1 - slot)

    for k in range(TOP_K):
        pltpu.make_async_copy(yt_hbm.at[pl.ds(0, slab), :], ybuf.at[slot, k], sem.at[slot]).wait()
    route = route_ref[...]
    x2 = x1_ref[...] + (route[:, 2:3] * _from_slabs(ybuf.at[slot, 0], tm)
                        + route[:, 3:4] * _from_slabs(ybuf.at[slot, 1], tm))
    h = _rms(x2, g3_ref[...]).astype(BF16)
    gate = jax.nn.sigmoid(_dot(h, wpg_ref[...]))
    out_ref[...] = x2 + gate * _dot(p_ref[...].astype(BF16), wple_ref[...])


def _combine_ple(x1_all, route_all, y_slots, tok_slots, p, norm3_g, wpg, wple, m, tm, row0):
    blk0 = row0 // tm
    grid_spec = pltpu.PrefetchScalarGridSpec(
        num_scalar_prefetch=1,
        grid=(m // tm,),
        in_specs=[
            pl.BlockSpec((tm, D_MODEL), lambda i, s: (blk0 + i, 0)),
            pl.BlockSpec((tm, LANE), lambda i, s: (blk0 + i, 0)),
            pl.BlockSpec(memory_space=pl.ANY),
            pl.BlockSpec((tm, PLE_DIM), lambda i, s: (i, 0)),
            pl.BlockSpec((1, D_MODEL), lambda i, s: (0, 0)),
            pl.BlockSpec((D_MODEL, D_MODEL), lambda i, s: (0, 0)),
            pl.BlockSpec((PLE_DIM, D_MODEL), lambda i, s: (0, 0)),
        ],
        out_specs=pl.BlockSpec((tm, D_MODEL), lambda i, s: (i, 0)),
        scratch_shapes=[
            pltpu.VMEM((2, TOP_K, tm * TOK_ROWS, LANE), F32),
            pltpu.SemaphoreType.DMA((2,)),
        ],
    )
    return pl.pallas_call(
        functools.partial(_combine_ple_kernel, tm=tm, row0=row0),
        grid_spec=grid_spec,
        out_shape=jax.ShapeDtypeStruct((m, D_MODEL), F32),
        compiler_params=_params(("arbitrary",)),
    )(tok_slots, x1_all, route_all, y_slots, p, norm3_g.reshape(1, D_MODEL), wpg, wple)


def _routing_tables(route_all):
    flat_e = route_all[:, 0:TOP_K].astype(jnp.int32).reshape(-1)
    onehot = (flat_e[:, None] == jnp.arange(N_EXPERTS, dtype=jnp.int32)[None, :]).astype(jnp.int32)
    csum = jnp.cumsum(onehot, axis=0)
    rank = jnp.take_along_axis(csum, flat_e[:, None], axis=1)[:, 0] - 1
    counts = csum[-1]
    nblk = (counts + MOE_TM - 1) // MOE_TM
    blk_end = jnp.cumsum(nblk)
    blk_start = blk_end - nblk
    slot = blk_start[flat_e] * MOE_TM + rank
    n_used = blk_end[-1].reshape(1)
    pad_start = blk_start * MOE_TM + counts
    pad_count = nblk * MOE_TM - counts
    i32 = lambda a: a.astype(jnp.int32)
    return i32(blk_start), i32(nblk), i32(n_used), i32(slot), i32(pad_start), i32(pad_count)


def kernel(x_prompt, x_sample, p_prompt, p_sample, cache_k_g0, cache_v_g0, cache_k_g1, cache_v_g1,
           cache_k_g2, cache_v_g2, state_conv, norm1_g, w_in, q_gain, k_gain, conv_w, w_conv_out,
           w_attn_out, w_o, norm2_g, w_coarse, b_coarse, w_fine, b_fine, w_gate_e, w_up_e, w_down_e,
           norm3_g, w_ple_gate, w_ple):
    seq = x_prompt.shape[1]
    assert x_prompt.shape == (1, seq, D_MODEL) and norm1_g.shape[0] == 1, "one prompt, one layer"
    n_seq, t_new, _ = x_sample.shape
    assert seq % ATT_UNIT == 0 and t_new == SUBLANE
    caches_k = (cache_k_g0, cache_k_g1, cache_k_g2)
    caches_v = (cache_v_g0, cache_v_g1, cache_v_g2)
    for g in range(N_GROUPS):
        assert caches_k[g].shape == (1, n_seq, WINDOWS[g], HEADS_PER_GROUP, HEAD_DIM)
    n_s = n_seq * t_new
    n_all = seq + n_s

    xp = x_prompt[0]
    xs = x_sample.reshape(n_s, D_MODEL)
    wc = w_conv_out[0].astype(BF16)
    wa = w_attn_out[0].astype(BF16)
    wo = w_o[0].astype(BF16)
    wpg = w_ple_gate[0].astype(BF16)
    wple = w_ple[0].astype(BF16)
    wr = jnp.concatenate(
        [w_coarse[0], jnp.transpose(w_fine[0], (1, 0, 2)).reshape(D_MODEL, N_EXPERTS)], axis=1)
    wr = jnp.pad(wr, ((0, 0), (0, LANE - wr.shape[1])))
    wr_hi = wr.astype(BF16)
    wr_lo = (wr - wr_hi.astype(F32)).astype(BF16)
    br = jnp.pad(jnp.concatenate([b_coarse[0], b_fine[0].reshape(-1)]), (0, LANE - 36)).reshape(1, LANE)

    z_s, _ = _inproj(xs, norm1_g[0], w_in[0], tm=n_s, tn=512)
    ck = [c.reshape(n_seq, WINDOWS[g] * HEADS_PER_GROUP, HEAD_DIM) for g, c in enumerate(caches_k)]
    cv = [c.reshape(n_seq, WINDOWS[g] * HEADS_PER_GROUP, HEAD_DIM) for g, c in enumerate(caches_v)]
    att_s = _sample_attn(z_s, ck, cv, q_gain[0], k_gain[0], n_seq, t_new)
    o_s, k_rows, v_rows = att_s[0], att_s[1:4], att_s[4:7]
    z_p, (sk2, sk1, sv1) = _inproj(xp, norm1_g[0], w_in[0], tm=2048, tn=512,
                                   shift=([ck[2], ck[1], cv[1]], [k_rows[2], k_rows[1], v_rows[1]]))
    att_p, (sv2, sk0, sv0) = _prompt_attn(z_p, q_gain[0], k_gain[0],
                                          shift=([cv[2], ck[0], cv[0]], [v_rows[2], k_rows[0], v_rows[0]]))
    o_p, pks, pvs = att_p[0], att_p[1:4], att_p[4:7]
    sks, svs = (sk0, sk1, sk2), (sv0, sv1, sv2)

    mg_p, u_tail = _merge(z_p, o_p, conv_w[0], wc, wa, tm=512)
    prefix = jnp.pad(state_conv[0], ((0, 0), (0, t_new - 2), (0, 0))).reshape(n_s, D_CONV)
    mg_s, u_s = _merge(z_s, o_s, conv_w[0], wc, wa, tm=n_s, per_seq=t_new, prefix=prefix)
    x1_all, x1t_all, route_all = _outproj_router(xp, mg_p, xs, mg_s, wo, norm2_g[0], wr_hi, wr_lo, br)

    n_blocks = n_all * TOP_K // MOE_TM + N_EXPERTS
    blk_start, nblk, n_used, tok_slots, pad_start, pad_count = _routing_tables(route_all)
    xst = _dispatch(x1t_all, tok_slots, pad_start, pad_count, n_used, n_blocks, tm=n_s)
    y_slots = _moe(xst, norm2_g[0], w_gate_e[0], w_up_e[0], w_down_e[0], blk_start, nblk, n_used)

    y_p = _combine_ple(x1_all, route_all, y_slots, tok_slots, p_prompt[0, 0], norm3_g[0], wpg, wple,
                       m=seq, tm=256, row0=0)
    y_s = _combine_ple(x1_all, route_all, y_slots, tok_slots, p_sample[0].reshape(n_s, PLE_DIM),
                       norm3_g[0], wpg, wple, m=n_s, tm=n_s, row0=seq)

    def state(a, n, g):
        return a.reshape(1, n, WINDOWS[g], HEADS_PER_GROUP, HEAD_DIM)

    pk = [state(a, 1, g) for g, a in enumerate(pks)]
    pv = [state(a, 1, g) for g, a in enumerate(pvs)]
    sk = [state(a, n_seq, g) for g, a in enumerate(sks)]
    sv = [state(a, n_seq, g) for g, a in enumerate(svs)]
    pconv = u_tail[-2:].reshape(1, 1, 2, D_CONV)
    sconv = u_s.reshape(n_seq, t_new, D_CONV)[:, t_new - 2:].reshape(1, n_seq, 2, D_CONV)
    return (y_p.reshape(1, seq, D_MODEL), y_s.reshape(n_seq, t_new, D_MODEL),
            pk[0], pv[0], pk[1], pv[1], pk[2], pv[2], pconv,
            sk[0], sv[0], sk[1], sv[1], sk[2], sv[2], sconv)
```

```python
import functools
import math

import jax
import jax.numpy as jnp
from jax import lax
from jax.experimental import pallas as pl
from jax.experimental.pallas import tpu as pltpu

D_MODEL = 2048
HEAD_DIM = 128
HEADS_PER_GROUP = 4
WINDOWS = (128, 512, 2048)
DILATIONS = (1, 4, 16)
N_GROUPS = 3
N_HEADS = N_GROUPS * HEADS_PER_GROUP
ATT_WIDTH = N_HEADS * HEAD_DIM
ATT_OUT_WIDTH = HEADS_PER_GROUP * HEAD_DIM
SCALE = HEAD_DIM ** -0.5
D_CONV = D_MODEL // 2
PLE_DIM = 256
N_EXPERT_GROUPS = 4
EXPERTS_PER_GROUP = 8
N_EXPERTS = N_EXPERT_GROUPS * EXPERTS_PER_GROUP
TOP_K = 2
D_EXPERT = D_MODEL // 4
EPS = 1e-6

COL_B = 0
COL_C = D_CONV
COL_H = 2 * D_CONV
COL_Q = 3 * D_CONV
COL_K = COL_Q + ATT_WIDTH
COL_V = COL_K + ATT_WIDTH
COL_GC = COL_V + ATT_WIDTH
COL_GA = COL_GC + D_MODEL
IN_COLS = COL_GA + D_MODEL

LANE = 128
SUBLANE = 8
NEG = -1e30
ATT_UNIT = max(WINDOWS)
ATT_BLK = 128
ATT_TILE_UNROLL = 8
MXU_N = 256
TOK_ROWS = D_MODEL // LANE
MOE_TM = 288
VMEM_LIMIT = 56 * 1024 * 1024

BF16 = jnp.bfloat16
F32 = jnp.float32


def _params(sem):
    return pltpu.CompilerParams(dimension_semantics=sem, vmem_limit_bytes=VMEM_LIMIT)


def _rms(x, gain):
    return x * lax.rsqrt(jnp.mean(x * x, axis=-1, keepdims=True) + EPS) * gain


def _dot(a, b):
    return jnp.dot(a, b, preferred_element_type=F32)


def _dot_nt(a, b):
    return lax.dot_general(a, b, (((1,), (1,)), ((), ())), preferred_element_type=F32)


def _inproj_kernel(x_ref, g_ref, w_ref, z_ref, h_scr):
    @pl.when(pl.program_id(1) == 0)
    def _():
        h_scr[...] = _rms(x_ref[...], g_ref[...]).astype(BF16)

    for c in range(w_ref.shape[1] // MXU_N):
        cols = slice(c * MXU_N, (c + 1) * MXU_N)
        z_ref[:, cols] = _dot(h_scr[...], w_ref[:, cols].astype(BF16))


def _inproj(x, norm_g, w_in, tm, tn):
    m = x.shape[0]
    return pl.pallas_call(
        _inproj_kernel,
        grid=(m // tm, IN_COLS // tn),
        in_specs=[
            pl.BlockSpec((tm, D_MODEL), lambda i, j: (i, 0), pipeline_mode=pl.Buffered(1)),
            pl.BlockSpec((1, D_MODEL), lambda i, j: (0, 0)),
            pl.BlockSpec((D_MODEL, tn), lambda i, j: (0, j)),
        ],
        out_specs=pl.BlockSpec((tm, tn), lambda i, j: (i, j)),
        out_shape=jax.ShapeDtypeStruct((m, IN_COLS), F32),
        scratch_shapes=[pltpu.VMEM((tm, D_MODEL), BF16)],
        compiler_params=_params(("parallel", "arbitrary")),
    )(x, norm_g.reshape(1, D_MODEL), w_in)


def _prompt_attn_kernel(*refs):
    (q0, q1, q2, kc0, kc1, kc2, vc0, vc1, vc2, kp0, kp1, kp2, vp0, vp1, vp2, qg_ref, kg_ref,
     o_ref, pk0, pk1, pk2, pv0, pv1, pv2, kext, vext, acc_s, m_s, l_s) = refs
    q_refs, kc_refs, vc_refs = (q0, q1, q2), (kc0, kc1, kc2), (vc0, vc1, vc2)
    kp_refs, vp_refs = (kp0, kp1, kp2), (vp0, vp1, vp2)
    pk_refs, pv_refs = (pk0, pk1, pk2), (pv0, pv1, pv2)
    n = pl.program_id(1)
    last = pl.num_programs(1) - 1

    qi = lax.broadcasted_iota(jnp.int32, (ATT_BLK, 2 * ATT_BLK), 0) + ATT_BLK
    ki = lax.broadcasted_iota(jnp.int32, (ATT_BLK, 2 * ATT_BLK), 1)
    dist = qi - ki
    band = (dist >= 0) & (dist <= ATT_BLK)

    for g in range(N_GROUPS):
        win, dil = WINDOWS[g], DILATIONS[g]
        q_ref = q_refs[g]
        qgain = qg_ref[g, 0]
        kgain = kg_ref[g, 0]
        kext[0:win, :] = _rms(kp_refs[g][...], kgain)
        kext[win:win + ATT_UNIT, :] = _rms(kc_refs[g][...], kgain)
        vext[0:win, :] = vp_refs[g][...]
        vext[win:win + ATT_UNIT, :] = vc_refs[g][...]

        @pl.when(n == last)
        def _():
            pk_refs[g][...] = kext[ATT_UNIT:ATT_UNIT + win, :]
            pv_refs[g][...] = vext[ATT_UNIT:ATT_UNIT + win, :]

        def tile(t, carry, g=g, win=win, dil=dil, q_ref=q_ref, qgain=qgain):
            u = t // dil
            r = t % dil
            off = u * win + r
            if dil == 1:
                off = pl.multiple_of(off, ATT_BLK)
                rows_q = pl.ds(off, ATT_BLK)
                rows_k = pl.ds(off, 2 * ATT_BLK)
            else:
                rows_q = pl.ds(off, ATT_BLK, stride=dil)
                rows_k = pl.ds(off, 2 * ATT_BLK, stride=dil)
            q = _rms(q_ref[rows_q, :], qgain).astype(BF16)
            k = kext[rows_k, :].astype(BF16)
            v = vext[rows_k, :].astype(BF16)
            s = _dot_nt(q, k) * SCALE
            k_min = jnp.where((n > 0) | (u > 0), 0, ATT_BLK)
            s = jnp.where(band & (ki >= k_min), s, NEG)
            m_t = jnp.max(s, axis=-1, keepdims=True)
            p = jnp.exp(s - m_t)
            l_t = jnp.sum(p, axis=-1, keepdims=True)
            m_s[g, rows_q, :] = jnp.broadcast_to(m_t, (ATT_BLK, LANE))
            l_s[g, rows_q, :] = jnp.broadcast_to(l_t, (ATT_BLK, LANE))
            acc_s[g, rows_q, :] = _dot(p.astype(BF16), v)
            return carry

        lax.fori_loop(0, ATT_UNIT // ATT_BLK, tile, 0, unroll=ATT_TILE_UNROLL)

    def finish(c, carry):
        rows = pl.ds(pl.multiple_of(c * ATT_BLK, ATT_BLK), ATT_BLK)
        m = [m_s[g, rows, :] for g in range(N_GROUPS)]
        m_all = jnp.maximum(jnp.maximum(m[0], m[1]), m[2])
        num = jnp.zeros((ATT_BLK, LANE), F32)
        den = jnp.zeros((ATT_BLK, LANE), F32)
        for g in range(N_GROUPS):
            w = jnp.exp(m[g] - m_all)
            num = num + w * acc_s[g, rows, :]
            den = den + w * l_s[g, rows, :]
        o_ref[rows, :] = num / den
        return carry

    lax.fori_loop(0, ATT_UNIT // ATT_BLK, finish, 0, unroll=2)


def _prompt_attn(z, q_gain, k_gain):
    s_len = z.shape[0]
    n_units = s_len // ATT_UNIT
    in_specs, args = [], []

    def col(base, g):
        return lambda hh, n: (n, base // LANE + g * HEADS_PER_GROUP + hh)

    for g in range(N_GROUPS):
        in_specs.append(pl.BlockSpec((ATT_UNIT, LANE), col(COL_Q, g)))
    for base in (COL_K, COL_V):
        for g in range(N_GROUPS):
            in_specs.append(pl.BlockSpec((ATT_UNIT, LANE), col(base, g)))
    for base in (COL_K, COL_V):
        for g in range(N_GROUPS):
            per = ATT_UNIT // WINDOWS[g]
            in_specs.append(pl.BlockSpec(
                (WINDOWS[g], LANE),
                lambda hh, n, base=base, g=g, per=per: (
                    jnp.maximum(n * per - 1, 0), base // LANE + g * HEADS_PER_GROUP + hh)))
    args = [z] * 15
    gain_spec = pl.BlockSpec((N_GROUPS, 1, 1, LANE), lambda hh, n: (0, hh, 0, 0))
    in_specs += [gain_spec, gain_spec]
    args += [q_gain.reshape(N_GROUPS, HEADS_PER_GROUP, 1, LANE),
             k_gain.reshape(N_GROUPS, HEADS_PER_GROUP, 1, LANE)]

    out_specs = [pl.BlockSpec((ATT_UNIT, LANE), lambda hh, n: (n, hh))]
    out_shape = [jax.ShapeDtypeStruct((s_len, ATT_OUT_WIDTH), F32)]
    for _ in range(2):
        for g in range(N_GROUPS):
            out_specs.append(pl.BlockSpec((WINDOWS[g], LANE), lambda hh, n: (0, hh)))
            out_shape.append(jax.ShapeDtypeStruct((WINDOWS[g], ATT_OUT_WIDTH), F32))

    return pl.pallas_call(
        _prompt_attn_kernel,
        grid=(HEADS_PER_GROUP, n_units),
        in_specs=in_specs,
        out_specs=out_specs,
        out_shape=out_shape,
        scratch_shapes=[
            pltpu.VMEM((2 * ATT_UNIT, LANE), F32),
            pltpu.VMEM((2 * ATT_UNIT, LANE), F32),
            pltpu.VMEM((N_GROUPS, ATT_UNIT, LANE), F32),
            pltpu.VMEM((N_GROUPS, ATT_UNIT, LANE), F32),
            pltpu.VMEM((N_GROUPS, ATT_UNIT, LANE), F32),
        ],
        compiler_params=_params(("parallel", "arbitrary")),
    )(*args)


def _sample_attn_kernel(*refs):
    (q0, q1, q2, kn0, kn1, kn2, vn0, vn1, vn2, ck0, ck1, ck2, cv0, cv1, cv2, qg_ref, kg_ref,
     o_ref, sk0, sk1, sk2, sv0, sv1, sv2) = refs
    q_refs, kn_refs, vn_refs = (q0, q1, q2), (kn0, kn1, kn2), (vn0, vn1, vn2)
    ck_refs, cv_refs = (ck0, ck1, ck2), (cv0, cv1, cv2)
    sk_refs, sv_refs = (sk0, sk1, sk2), (sv0, sv1, sv2)
    t_new = q0.shape[0]
    nh = HEADS_PER_GROUP
    n_q = nh * t_new
    t_bits = t_new.bit_length() - 1

    def heads(ref):
        return [ref[:, h * LANE:(h + 1) * LANE] for h in range(nh)]

    parts = []
    for g in range(N_GROUPS):
        win, dil = WINDOWS[g], DILATIONS[g]
        q_all = jnp.concatenate(
            [_rms(x, qg_ref[g, h]) for h, x in enumerate(heads(q_refs[g]))], axis=0).astype(BF16)
        k_new = [_rms(x, kg_ref[g, h]) for h, x in enumerate(heads(kn_refs[g]))]
        v_new = heads(vn_refs[g])
        k_new_all = jnp.concatenate(k_new, axis=0).astype(BF16)
        v_new_all = jnp.concatenate(v_new, axis=0).astype(BF16)
        ck = ck_refs[g][0]
        cv = cv_refs[g][0]
        old = (win - t_new) * nh
        sk_refs[g][0, 0:old, :] = ck[t_new * nh:, :]
        sv_refs[g][0, 0:old, :] = cv[t_new * nh:, :]
        for h in range(nh):
            sk_refs[g][0, pl.ds(old + h, t_new, stride=nh), :] = k_new[h]
            sv_refs[g][0, pl.ds(old + h, t_new, stride=nh), :] = v_new[h]

        s_c = _dot_nt(q_all, ck.astype(BF16)) * SCALE
        s_n = _dot_nt(q_all, k_new_all) * SCALE
        row = lax.broadcasted_iota(jnp.int32, (n_q, win * nh), 0)
        col = lax.broadcasted_iota(jnp.int32, (n_q, win * nh), 1)
        i_q, pos = row & (t_new - 1), col >> 2
        mask_c = (((col & (nh - 1)) == (row >> t_bits)) & (pos >= i_q)
                  & (((pos - i_q) & (dil - 1)) == 0))
        row_n = lax.broadcasted_iota(jnp.int32, (n_q, n_q), 0)
        col_n = lax.broadcasted_iota(jnp.int32, (n_q, n_q), 1)
        d_n = (row_n & (t_new - 1)) - (col_n & (t_new - 1))
        mask_n = ((row_n >> t_bits) == (col_n >> t_bits)) & (d_n >= 0) & ((d_n & (dil - 1)) == 0)
        s_c = jnp.where(mask_c, s_c, NEG)
        s_n = jnp.where(mask_n, s_n, NEG)
        m = jnp.maximum(jnp.max(s_c, axis=-1, keepdims=True), jnp.max(s_n, axis=-1, keepdims=True))
        p_c = jnp.exp(s_c - m)
        p_n = jnp.exp(s_n - m)
        l = jnp.sum(p_c, axis=-1, keepdims=True) + jnp.sum(p_n, axis=-1, keepdims=True)
        acc = _dot(p_c.astype(BF16), cv.astype(BF16)) + _dot(p_n.astype(BF16), v_new_all)
        parts.append((m, l, acc))

    m_all = jnp.maximum(jnp.maximum(parts[0][0], parts[1][0]), parts[2][0])
    num = jnp.zeros((n_q, LANE), F32)
    den = jnp.zeros((n_q, 1), F32)
    for m, l, acc in parts:
        w = jnp.exp(m - m_all)
        num = num + w * acc
        den = den + w * l
    o_all = num / den
    for h in range(nh):
        o_ref[:, h * LANE:(h + 1) * LANE] = o_all[h * t_new:(h + 1) * t_new, :]


def _sample_attn(z, caches_k, caches_v, q_gain, k_gain, n_seq, t_new):
    assert HEADS_PER_GROUP == 4 and t_new & (t_new - 1) == 0
    in_specs, args = [], []
    for base in (COL_Q, COL_K, COL_V):
        for g in range(N_GROUPS):
            in_specs.append(pl.BlockSpec(
                (t_new, ATT_OUT_WIDTH), lambda b, base=base, g=g: (b, base // ATT_OUT_WIDTH + g)))
            args.append(z)
    for caches in (caches_k, caches_v):
        for g in range(N_GROUPS):
            in_specs.append(pl.BlockSpec((1, WINDOWS[g] * HEADS_PER_GROUP, LANE), lambda b: (b, 0, 0)))
            args.append(caches[g])
    gain_spec = pl.BlockSpec((N_GROUPS, HEADS_PER_GROUP, 1, LANE), lambda b: (0, 0, 0, 0))
    in_specs += [gain_spec, gain_spec]
    args += [q_gain.reshape(N_GROUPS, HEADS_PER_GROUP, 1, LANE),
             k_gain.reshape(N_GROUPS, HEADS_PER_GROUP, 1, LANE)]

    out_specs = [pl.BlockSpec((t_new, ATT_OUT_WIDTH), lambda b: (b, 0))]
    out_shape = [jax.ShapeDtypeStruct((n_seq * t_new, ATT_OUT_WIDTH), F32)]
    for _ in range(2):
        for g in range(N_GROUPS):
            rows = WINDOWS[g] * HEADS_PER_GROUP
            out_specs.append(pl.BlockSpec((1, rows, LANE), lambda b: (b, 0, 0)))
            out_shape.append(jax.ShapeDtypeStruct((n_seq, rows, LANE), F32))

    return pl.pallas_call(
        _sample_attn_kernel,
        grid=(n_seq,),
        in_specs=in_specs,
        out_specs=out_specs,
        out_shape=out_shape,
        compiler_params=_params(("parallel",)),
    )(*args)


def _conv_taps(u, r1, r2, cw_ref):
    return cw_ref[2:3, :] * u + cw_ref[1:2, :] * r1 + cw_ref[0:1, :] * r2


def _merge_kernel(*refs, per_seq, n_chunks):
    b_ref, c_ref, h_ref, pc_ref, ph_ref, cw_ref, o_ref = refs[:7]
    gc_refs = refs[7:7 + n_chunks]
    ga_refs = refs[7 + n_chunks:7 + 2 * n_chunks]
    wc_ref, wa_ref, out_ref, u_ref, yb_scr = refs[7 + 2 * n_chunks:]
    i = pl.program_id(0)
    tm = b_ref.shape[0]
    u = c_ref[...] * h_ref[...]
    r1 = pltpu.roll(u, 1, axis=0)
    r2 = pltpu.roll(u, 2, axis=0)
    if per_seq is None:
        yb_scr[...] = b_ref[...] * _conv_taps(u, r1, r2, cw_ref)
        up = jnp.where(i > 0, pc_ref[...] * ph_ref[...], 0.0)
        row = lax.broadcasted_iota(jnp.int32, (SUBLANE, D_CONV), 0)
        u8 = u[0:SUBLANE, :]
        r1_8 = jnp.where(row < 1, pltpu.roll(up, 1, axis=0), pltpu.roll(u8, 1, axis=0))
        r2_8 = jnp.where(row < 2, pltpu.roll(up, 2, axis=0), pltpu.roll(u8, 2, axis=0))
        yb_scr[0:SUBLANE, :] = b_ref[0:SUBLANE, :] * _conv_taps(u8, r1_8, r2_8, cw_ref)
        u_ref[...] = u[tm - SUBLANE:tm, :]
    else:
        pre = pc_ref[...]
        t = lax.broadcasted_iota(jnp.int32, (tm, D_CONV), 0) & (per_seq - 1)
        r1 = jnp.where(t == 0, pltpu.roll(pre, tm - 1, axis=0), r1)
        r2 = jnp.where(t < 2, pre, r2)
        yb_scr[...] = b_ref[...] * _conv_taps(u, r1, r2, cw_ref)
        u_ref[...] = u

    yb = yb_scr[...].astype(BF16)
    ob = o_ref[...].astype(BF16)
    tn = D_MODEL // n_chunks
    for c in range(n_chunks):
        cols = slice(c * tn, (c + 1) * tn)
        y_conv = _dot(yb, wc_ref[:, cols])
        y_att = _dot(ob, wa_ref[:, cols])
        out_ref[:, cols] = (jax.nn.sigmoid(gc_refs[c][...]) * y_conv
                            + jax.nn.sigmoid(ga_refs[c][...]) * y_att).astype(BF16)


def _merge(z, o_att, conv_w, wc, wa, tm, per_seq=None, prefix=None):
    m = z.shape[0]
    tn = ATT_OUT_WIDTH
    n_chunks = D_MODEL // tn
    if per_seq is None:
        prev = lambda i, c: (jnp.maximum(i * (tm // SUBLANE) - 1, 0), c)
        pc_spec = pl.BlockSpec((SUBLANE, D_CONV), lambda i: prev(i, COL_C // D_CONV))
        ph_spec = pl.BlockSpec((SUBLANE, D_CONV), lambda i: prev(i, COL_H // D_CONV))
        pc_arg, ph_arg = z, z
        u_rows = SUBLANE
    else:
        pc_spec = pl.BlockSpec((tm, D_CONV), lambda i: (i, 0))
        ph_spec = pl.BlockSpec((SUBLANE, D_CONV), lambda i: (0, 0))
        pc_arg, ph_arg = prefix, prefix
        u_rows = tm
    gate_specs = [pl.BlockSpec((tm, tn), lambda i, base=base, c=c: (i, base // tn + c))
                  for base in (COL_GC, COL_GA) for c in range(n_chunks)]
    return pl.pallas_call(
        functools.partial(_merge_kernel, per_seq=per_seq, n_chunks=n_chunks),
        grid=(m // tm,),
        in_specs=[
            pl.BlockSpec((tm, D_CONV), lambda i: (i, COL_B // D_CONV)),
            pl.BlockSpec((tm, D_CONV), lambda i: (i, COL_C // D_CONV)),
            pl.BlockSpec((tm, D_CONV), lambda i: (i, COL_H // D_CONV)),
            pc_spec,
            ph_spec,
            pl.BlockSpec((3, D_CONV), lambda i: (0, 0)),
            pl.BlockSpec((tm, ATT_OUT_WIDTH), lambda i: (i, 0)),
            *gate_specs,
            pl.BlockSpec((D_CONV, D_MODEL), lambda i: (0, 0)),
            pl.BlockSpec((ATT_OUT_WIDTH, D_MODEL), lambda i: (0, 0)),
        ],
        out_specs=[
            pl.BlockSpec((tm, D_MODEL), lambda i: (i, 0)),
            pl.BlockSpec((u_rows, D_CONV), lambda i: (i, 0)),
        ],
        out_shape=[
            jax.ShapeDtypeStruct((m, D_MODEL), BF16),
            jax.ShapeDtypeStruct((m // tm * u_rows, D_CONV), F32),
        ],
        scratch_shapes=[pltpu.VMEM((tm, D_CONV), F32)],
        compiler_params=_params(("parallel",)),
    )(z, z, z, pc_arg, ph_arg, conv_w, o_att, *([z] * (2 * n_chunks)), wc, wa)


def _outproj_router_kernel(xp_ref, mgp_ref, xs_ref, mgs_ref, wo_ref, g2_ref, wr_hi_ref, wr_lo_ref,
                           br_ref, x1_ref, x1t_ref, route_ref):
    is_sample = pl.program_id(0) == pl.num_programs(0) - 1

    @pl.when(jnp.logical_not(is_sample))
    def _():
        _outproj_router_tile(xp_ref, mgp_ref, wo_ref, g2_ref, wr_hi_ref, wr_lo_ref, br_ref,
                             x1_ref, x1t_ref, route_ref)

    @pl.when(is_sample)
    def _():
        _outproj_router_tile(xs_ref, mgs_ref, wo_ref, g2_ref, wr_hi_ref, wr_lo_ref, br_ref,
                             x1_ref, x1t_ref, route_ref)


def _outproj_router_tile(x_ref, mg_ref, wo_ref, g2_ref, wr_hi_ref, wr_lo_ref, br_ref, x1_ref,
                         x1t_ref, route_ref):
    x1 = x_ref[...] + _dot(mg_ref[...], wo_ref[...])
    x1_ref[...] = x1
    _to_slabs(x1t_ref, x1)
    h = _rms(x1, g2_ref[...])
    h_hi = h.astype(BF16)
    h_lo = (h - h_hi.astype(F32)).astype(BF16)
    logits = (_dot(h_hi, wr_hi_ref[...]) + _dot(h_hi, wr_lo_ref[...]) + _dot(h_lo, wr_hi_ref[...])
              + br_ref[...])
    lane = lax.broadcasted_iota(jnp.int32, logits.shape, 1)
    is_coarse = lane < N_EXPERT_GROUPS
    coarse = jnp.where(is_coarse, logits, NEG)
    cmax = jnp.max(coarse, axis=-1, keepdims=True)
    grp = jnp.min(jnp.where(coarse == cmax, lane, LANE), axis=-1, keepdims=True)
    p_grp = 1.0 / jnp.sum(jnp.where(is_coarse, jnp.exp(coarse - cmax), 0.0), axis=-1, keepdims=True)
    eid = lane - N_EXPERT_GROUPS
    in_grp = (eid >= 0) & (eid < N_EXPERTS) & ((eid >> 3) == grp)
    fine = jnp.where(in_grp, logits, NEG)
    v1 = jnp.max(fine, axis=-1, keepdims=True)
    i1 = jnp.min(jnp.where(fine == v1, lane, LANE), axis=-1, keepdims=True)
    fine2 = jnp.where(lane == i1, NEG, fine)
    v2 = jnp.max(fine2, axis=-1, keepdims=True)
    i2 = jnp.min(jnp.where(fine2 == v2, lane, LANE), axis=-1, keepdims=True)
    e = jnp.exp(v2 - v1)
    gate1 = p_grp / (1.0 + e)
    gate2 = p_grp * e / (1.0 + e)
    route = jnp.where(lane == 0, (i1 - N_EXPERT_GROUPS).astype(F32),
                      jnp.where(lane == 1, (i2 - N_EXPERT_GROUPS).astype(F32),
                                jnp.where(lane == 2, gate1, jnp.where(lane == 3, gate2, 0.0))))
    route_ref[...] = route


def _outproj_router(xp, mg_p, xs, mg_s, wo, norm2_g, wr_hi, wr_lo, br):
    m_p, tm = xp.shape[0], xs.shape[0]
    assert m_p % tm == 0
    n_p = m_p // tm
    prompt_blk = lambda i: (jnp.minimum(i, n_p - 1), 0)
    fixed = lambda i: (0, 0)
    return pl.pallas_call(
        _outproj_router_kernel,
        grid=(n_p + 1,),
        in_specs=[
            pl.BlockSpec((tm, D_MODEL), prompt_blk),
            pl.BlockSpec((tm, D_MODEL), prompt_blk),
            pl.BlockSpec((tm, D_MODEL), fixed),
            pl.BlockSpec((tm, D_MODEL), fixed),
            pl.BlockSpec((D_MODEL, D_MODEL), fixed),
            pl.BlockSpec((1, D_MODEL), fixed),
            pl.BlockSpec((D_MODEL, LANE), fixed),
            pl.BlockSpec((D_MODEL, LANE), fixed),
            pl.BlockSpec((1, LANE), fixed),
        ],
        out_specs=[
            pl.BlockSpec((tm, D_MODEL), lambda i: (i, 0)),
            pl.BlockSpec((tm * TOK_ROWS, LANE), lambda i: (i, 0)),
            pl.BlockSpec((tm, LANE), lambda i: (i, 0)),
        ],
        out_shape=[
            jax.ShapeDtypeStruct((m_p + tm, D_MODEL), F32),
            jax.ShapeDtypeStruct(((m_p + tm) * TOK_ROWS, LANE), F32),
            jax.ShapeDtypeStruct((m_p + tm, LANE), F32),
        ],
        compiler_params=_params(("arbitrary",)),
    )(xp, mg_p, xs, mg_s, wo, norm2_g.reshape(1, D_MODEL), wr_hi, wr_lo, br)


def _to_slabs(dst_ref, x):
    rows = x.shape[0]
    for s in range(TOK_ROWS):
        dst_ref[pl.ds(s, rows, stride=TOK_ROWS), :] = x[:, s * LANE:(s + 1) * LANE]


def _from_slabs(src_ref, rows):
    return jnp.concatenate(
        [src_ref[pl.ds(s, rows, stride=TOK_ROWS), :] for s in range(TOK_ROWS)], axis=1)


def _dispatch_kernel(slots_ref, pad0_ref, padn_ref, nused_ref, x1t_hbm, xst_hbm, buf, zero, lsem,
                     ssem, zsem, *, tm, n_blocks):
    i = pl.program_id(0)
    n_steps = pl.num_programs(0)
    slab = tm * TOK_ROWS

    def load(step):
        return pltpu.make_async_copy(
            x1t_hbm.at[pl.ds(pl.multiple_of(step * slab, slab), slab), :], buf.at[step % 3],
            lsem.at[step % 3])

    def wait_scatter(step):
        for _ in range(TOP_K):
            pltpu.make_async_copy(buf.at[step % 3], xst_hbm.at[pl.ds(0, slab), :],
                                  ssem.at[step % 3]).wait()

    def pad_rows(start):
        def expert(e, carry):
            def row(j, c):
                dst = pl.multiple_of((pad0_ref[e] + j) * TOK_ROWS, TOK_ROWS)
                cp = pltpu.make_async_copy(zero, xst_hbm.at[pl.ds(dst, TOK_ROWS), :], zsem.at[0])
                if start:
                    cp.start()
                else:
                    cp.wait()
                return c
            return lax.fori_loop(0, padn_ref[e], row, carry)
        lax.fori_loop(0, N_EXPERTS, expert, 0)

    @pl.when(i == 0)
    def _():
        load(0).start()
        zero[...] = jnp.zeros(zero.shape, F32)
        pad_rows(start=True)

    @pl.when(i >= 2)
    def _():
        wait_scatter(i - 2)

    @pl.when(i + 1 < n_steps)
    def _():
        load(i + 1).start()

    load(i).wait()
    cur = i % 3

    def body(r, carry):
        src = buf.at[cur, pl.ds(pl.multiple_of(r * TOK_ROWS, TOK_ROWS), TOK_ROWS), :]
        for k in range(TOP_K):
            dst = pl.multiple_of(slots_ref[(i * tm + r) * TOP_K + k] * TOK_ROWS, TOK_ROWS)
            pltpu.make_async_copy(src, xst_hbm.at[pl.ds(dst, TOK_ROWS), :],
                                  ssem.at[cur]).start(priority=k)
        return carry
    lax.fori_loop(0, tm, body, 0, unroll=4)

    @pl.when(i == n_steps - 1)
    def _():
        @pl.when(i >= 1)
        def _():
            wait_scatter(i - 1)
        wait_scatter(i)
        pad_rows(start=False)
        chunk = math.gcd(tm, MOE_TM)
        rows = chunk * TOK_ROWS
        buf[0, 0:rows, :] = jnp.zeros((rows, LANE), F32)
        first = nused_ref[0] * MOE_TM
        n_tail = (n_blocks - nused_ref[0]) * (MOE_TM // chunk)

        def tail(j, carry, start):
            dst = pl.multiple_of((first + j * chunk) * TOK_ROWS, TOK_ROWS)
            cp = pltpu.make_async_copy(buf.at[0, pl.ds(0, rows), :],
                                       xst_hbm.at[pl.ds(dst, rows), :], zsem.at[0])
            if start:
                cp.start()
            else:
                cp.wait()
            return carry
        lax.fori_loop(0, n_tail, functools.partial(tail, start=True), 0)
        lax.fori_loop(0, n_tail, functools.partial(tail, start=False), 0)


def _dispatch(x1t, tok_slots, pad_start, pad_count, n_used, n_blocks, tm):
    n_tok = x1t.shape[0] // TOK_ROWS
    assert n_tok % tm == 0
    grid_spec = pltpu.PrefetchScalarGridSpec(
        num_scalar_prefetch=4,
        grid=(n_tok // tm,),
        in_specs=[pl.BlockSpec(memory_space=pl.ANY)],
        out_specs=pl.BlockSpec(memory_space=pl.ANY),
        scratch_shapes=[
            pltpu.VMEM((3, tm * TOK_ROWS, LANE), F32),
            pltpu.VMEM((TOK_ROWS, LANE), F32),
            pltpu.SemaphoreType.DMA((3,)),
            pltpu.SemaphoreType.DMA((3,)),
            pltpu.SemaphoreType.DMA((1,)),
        ],
    )
    return pl.pallas_call(
        functools.partial(_dispatch_kernel, tm=tm, n_blocks=n_blocks),
        grid_spec=grid_spec,
        out_shape=jax.ShapeDtypeStruct((n_blocks * MOE_TM * TOK_ROWS, LANE), F32),
        compiler_params=_params(("arbitrary",)),
    )(tok_slots, pad_start, pad_count, n_used, x1t)


def _moe_kernel(blk0_ref, nblk_ref, nused_ref, xst_hbm, g2_ref, wg_ref, wu_ref, wd_ref,
                yt_hbm, xbuf, ybuf, gsem, osem, wg_b, wu_b, wd_b):
    e = pl.program_id(0)
    n_used = nused_ref[0]
    slab = MOE_TM * TOK_ROWS

    def in_copy(blk, to_slot):
        return pltpu.make_async_copy(
            xst_hbm.at[pl.ds(pl.multiple_of(blk * slab, slab), slab), :], xbuf.at[to_slot],
            gsem.at[to_slot])

    def out_copy(blk, from_slot):
        return pltpu.make_async_copy(
            ybuf.at[from_slot], yt_hbm.at[pl.ds(pl.multiple_of(blk * slab, slab), slab), :],
            osem.at[from_slot])

    wg_b[...] = wg_ref[0].astype(BF16)
    wu_b[...] = wu_ref[0].astype(BF16)
    wd_b[...] = wd_ref[0].astype(BF16)

    @pl.when(e == 0)
    def _():
        in_copy(0, 0).start()

    def block(b, carry):
        blk = blk0_ref[e] + b
        slot = blk & 1

        @pl.when(blk + 1 < n_used)
        def _():
            in_copy(blk + 1, 1 - slot).start(priority=1)

        in_copy(blk, slot).wait()
        h = _rms(_from_slabs(xbuf.at[slot], MOE_TM), g2_ref[...]).astype(BF16)
        a = _dot(h, wg_b[...])
        u = _dot(h, wu_b[...])
        hm = (a * jax.nn.sigmoid(a) * u).astype(BF16)
        y = _dot(hm, wd_b[...])

        @pl.when(blk >= 2)
        def _():
            out_copy(blk - 2, slot).wait()

        _to_slabs(ybuf.at[slot], y)
        out_copy(blk, slot).start()
        return carry

    lax.fori_loop(0, nblk_ref[e], block, 0)

    @pl.when(e == pl.num_programs(0) - 1)
    def _():
        @pl.when(n_used >= 2)
        def _():
            out_copy(n_used - 2, (n_used - 2) & 1).wait()
        out_copy(n_used - 1, (n_used - 1) & 1).wait()


def _moe(xst, norm2_g, w_gate_e, w_up_e, w_down_e, blk_start, nblk, n_used):
    w_idx = lambda e, *_: (e, 0, 0)
    slab = MOE_TM * TOK_ROWS
    grid_spec = pltpu.PrefetchScalarGridSpec(
        num_scalar_prefetch=3,
        grid=(N_EXPERTS,),
        in_specs=[
            pl.BlockSpec(memory_space=pl.ANY),
            pl.BlockSpec((1, D_MODEL), lambda e, *_: (0, 0)),
            pl.BlockSpec((1, D_MODEL, D_EXPERT), w_idx),
            pl.BlockSpec((1, D_MODEL, D_EXPERT), w_idx),
            pl.BlockSpec((1, D_EXPERT, D_MODEL), w_idx),
        ],
        out_specs=pl.BlockSpec(memory_space=pl.ANY),
        scratch_shapes=[
            pltpu.VMEM((2, slab, LANE), F32),
            pltpu.VMEM((2, slab, LANE), F32),
            pltpu.SemaphoreType.DMA((2,)),
            pltpu.SemaphoreType.DMA((2,)),
            pltpu.VMEM((D_MODEL, D_EXPERT), BF16),
            pltpu.VMEM((D_MODEL, D_EXPERT), BF16),
            pltpu.VMEM((D_EXPERT, D_MODEL), BF16),
        ],
    )
    return pl.pallas_call(
        _moe_kernel,
        grid_spec=grid_spec,
        out_shape=jax.ShapeDtypeStruct(xst.shape, F32),
        input_output_aliases={3: 0},
        compiler_params=_params(("arbitrary",)),
    )(blk_start, nblk, n_used, xst, norm2_g.reshape(1, D_MODEL), w_gate_e, w_up_e, w_down_e)


def _combine_ple_kernel(slots_ref, x1_ref, route_ref, yt_hbm, p_ref, g3_ref, wpg_ref, wple_ref,
                        out_ref, ybuf, sem, *, tm, row0):
    i = pl.program_id(0)
    n_steps = pl.num_programs(0)
    slot = i & 1
    slab = tm * TOK_ROWS

    def start_gather(step, to_slot):
        base = (row0 + step * tm) * TOP_K

        def body(r, carry):
            for k in range(TOP_K):
                src = slots_ref[base + r * TOP_K + k]
                pltpu.make_async_copy(
                    yt_hbm.at[pl.ds(pl.multiple_of(src * TOK_ROWS, TOK_ROWS), TOK_ROWS), :],
                    ybuf.at[to_slot, k, pl.ds(pl.multiple_of(r * TOK_ROWS, TOK_ROWS), TOK_ROWS), :],
                    sem.at[to_slot]).start(priority=k)
            return carry
        lax.fori_loop(0, tm, body, 0, unroll=4)

    @pl.when(i == 0)
    def _():
        start_gather(0, 0)

    @pl.when(i + 1 < n_steps)
    def _():
        start_gather(i + 1, 1 - slot)

    for k in range(TOP_K):
        pltpu.make_async_copy(yt_hbm.at[pl.ds(0, slab), :], ybuf.at[slot, k], sem.at[slot]).wait()
    route = route_ref[...]
    x2 = x1_ref[...] + (route[:, 2:3] * _from_slabs(ybuf.at[slot, 0], tm)
                        + route[:, 3:4] * _from_slabs(ybuf.at[slot, 1], tm))
    h = _rms(x2, g3_ref[...]).astype(BF16)
    gate = jax.nn.sigmoid(_dot(h, wpg_ref[...]))
    out_ref[...] = x2 + gate * _dot(p_ref[...].astype(BF16), wple_ref[...])


def _combine_ple(x1_all, route_all, y_slots, tok_slots, p, norm3_g, wpg, wple, m, tm, row0):
    blk0 = row0 // tm
    grid_spec = pltpu.PrefetchScalarGridSpec(
        num_scalar_prefetch=1,
        grid=(m // tm,),
        in_specs=[
            pl.BlockSpec((tm, D_MODEL), lambda i, s: (blk0 + i, 0)),
            pl.BlockSpec((tm, LANE), lambda i, s: (blk0 + i, 0)),
            pl.BlockSpec(memory_space=pl.ANY),
            pl.BlockSpec((tm, PLE_DIM), lambda i, s: (i, 0)),
            pl.BlockSpec((1, D_MODEL), lambda i, s: (0, 0)),
            pl.BlockSpec((D_MODEL, D_MODEL), lambda i, s: (0, 0)),
            pl.BlockSpec((PLE_DIM, D_MODEL), lambda i, s: (0, 0)),
        ],
        out_specs=pl.BlockSpec((tm, D_MODEL), lambda i, s: (i, 0)),
        scratch_shapes=[
            pltpu.VMEM((2, TOP_K, tm * TOK_ROWS, LANE), F32),
            pltpu.SemaphoreType.DMA((2,)),
        ],
    )
    return pl.pallas_call(
        functools.partial(_combine_ple_kernel, tm=tm, row0=row0),
        grid_spec=grid_spec,
        out_shape=jax.ShapeDtypeStruct((m, D_MODEL), F32),
        compiler_params=_params(("arbitrary",)),
    )(tok_slots, x1_all, route_all, y_slots, p, norm3_g.reshape(1, D_MODEL), wpg, wple)


def _routing_tables(route_all):
    flat_e = route_all[:, 0:TOP_K].astype(jnp.int32).reshape(-1)
    onehot = (flat_e[:, None] == jnp.arange(N_EXPERTS, dtype=jnp.int32)[None, :]).astype(jnp.int32)
    csum = jnp.cumsum(onehot, axis=0)
    rank = jnp.take_along_axis(csum, flat_e[:, None], axis=1)[:, 0] - 1
    counts = csum[-1]
    nblk = (counts + MOE_TM - 1) // MOE_TM
    blk_end = jnp.cumsum(nblk)
    blk_start = blk_end - nblk
    slot = blk_start[flat_e] * MOE_TM + rank
    n_used = blk_end[-1].reshape(1)
    pad_start = blk_start * MOE_TM + counts
    pad_count = nblk * MOE_TM - counts
    i32 = lambda a: a.astype(jnp.int32)
    return i32(blk_start), i32(nblk), i32(n_used), i32(slot), i32(pad_start), i32(pad_count)


def kernel(x_prompt, x_sample, p_prompt, p_sample, cache_k_g0, cache_v_g0, cache_k_g1, cache_v_g1,
           cache_k_g2, cache_v_g2, state_conv, norm1_g, w_in, q_gain, k_gain, conv_w, w_conv_out,
           w_attn_out, w_o, norm2_g, w_coarse, b_coarse, w_fine, b_fine, w_gate_e, w_up_e, w_down_e,
           norm3_g, w_ple_gate, w_ple):
    seq = x_prompt.shape[1]
    assert x_prompt.shape == (1, seq, D_MODEL) and norm1_g.shape[0] == 1, "one prompt, one layer"
    n_seq, t_new, _ = x_sample.shape
    assert seq % ATT_UNIT == 0 and t_new == SUBLANE
    caches_k = (cache_k_g0, cache_k_g1, cache_k_g2)
    caches_v = (cache_v_g0, cache_v_g1, cache_v_g2)
    for g in range(N_GROUPS):
        assert caches_k[g].shape == (1, n_seq, WINDOWS[g], HEADS_PER_GROUP, HEAD_DIM)
    n_s = n_seq * t_new
    n_all = seq + n_s

    xp = x_prompt[0]
    xs = x_sample.reshape(n_s, D_MODEL)
    wc = w_conv_out[0].astype(BF16)
    wa = w_attn_out[0].astype(BF16)
    wo = w_o[0].astype(BF16)
    wpg = w_ple_gate[0].astype(BF16)
    wple = w_ple[0].astype(BF16)
    wr = jnp.concatenate(
        [w_coarse[0], jnp.transpose(w_fine[0], (1, 0, 2)).reshape(D_MODEL, N_EXPERTS)], axis=1)
    wr = jnp.pad(wr, ((0, 0), (0, LANE - wr.shape[1])))
    wr_hi = wr.astype(BF16)
    wr_lo = (wr - wr_hi.astype(F32)).astype(BF16)
    br = jnp.pad(jnp.concatenate([b_coarse[0], b_fine[0].reshape(-1)]), (0, LANE - 36)).reshape(1, LANE)

    z_p = _inproj(xp, norm1_g[0], w_in[0], tm=2048, tn=512)
    z_s = _inproj(xs, norm1_g[0], w_in[0], tm=n_s, tn=512)

    att_p = _prompt_attn(z_p, q_gain[0], k_gain[0])
    o_p, pks, pvs = att_p[0], att_p[1:4], att_p[4:7]
    ck = [c.reshape(n_seq, WINDOWS[g] * HEADS_PER_GROUP, HEAD_DIM) for g, c in enumerate(caches_k)]
    cv = [c.reshape(n_seq, WINDOWS[g] * HEADS_PER_GROUP, HEAD_DIM) for g, c in enumerate(caches_v)]
    att_s = _sample_attn(z_s, ck, cv, q_gain[0], k_gain[0], n_seq, t_new)
    o_s, sks, svs = att_s[0], att_s[1:4], att_s[4:7]

    mg_p, u_tail = _merge(z_p, o_p, conv_w[0], wc, wa, tm=512)
    prefix = jnp.pad(state_conv[0], ((0, 0), (0, t_new - 2), (0, 0))).reshape(n_s, D_CONV)
    mg_s, u_s = _merge(z_s, o_s, conv_w[0], wc, wa, tm=n_s, per_seq=t_new, prefix=prefix)
    x1_all, x1t_all, route_all = _outproj_router(xp, mg_p, xs, mg_s, wo, norm2_g[0], wr_hi, wr_lo, br)

    n_blocks = n_all * TOP_K // MOE_TM + N_EXPERTS
    blk_start, nblk, n_used, tok_slots, pad_start, pad_count = _routing_tables(route_all)
    xst = _dispatch(x1t_all, tok_slots, pad_start, pad_count, n_used, n_blocks, tm=n_s)
    y_slots = _moe(xst, norm2_g[0], w_gate_e[0], w_up_e[0], w_down_e[0], blk_start, nblk, n_used)

    y_p = _combine_ple(x1_all, route_all, y_slots, tok_slots, p_prompt[0, 0], norm3_g[0], wpg, wple,
                       m=seq, tm=256, row0=0)
    y_s = _combine_ple(x1_all, route_all, y_slots, tok_slots, p_sample[0].reshape(n_s, PLE_DIM),
                       norm3_g[0], wpg, wple, m=n_s, tm=n_s, row0=seq)

    def state(a, n, g):
        return a.reshape(1, n, WINDOWS[g], HEADS_PER_GROUP, HEAD_DIM)

    pk = [state(a, 1, g) for g, a in enumerate(pks)]
    pv = [state(a, 1, g) for g, a in enumerate(pvs)]
    sk = [state(a, n_seq, g) for g, a in enumerate(sks)]
    sv = [state(a, n_seq, g) for g, a in enumerate(svs)]
    pconv = u_tail[-2:].reshape(1, 1, 2, D_CONV)
    sconv = u_s.reshape(n_seq, t_new, D_CONV)[:, t_new - 2:].reshape(1, n_seq, 2, D_CONV)
    return (y_p.reshape(1, seq, D_MODEL), y_s.reshape(n_seq, t_new, D_MODEL),
            pk[0], pv[0], pk[1], pv[1], pk[2], pv[2], pconv,
            sk[0], sv[0], sk[1], sv[1], sk[2], sv[2], sconv)
```

```python
import functools
import math

import jax
import jax.numpy as jnp
from jax import lax
from jax.experimental import pallas as pl
from jax.experimental.pallas import tpu as pltpu

D_MODEL = 2048
HEAD_DIM = 128
HEADS_PER_GROUP = 4
WINDOWS = (128, 512, 2048)
DILATIONS = (1, 4, 16)
N_GROUPS = 3
N_HEADS = N_GROUPS * HEADS_PER_GROUP
ATT_WIDTH = N_HEADS * HEAD_DIM
ATT_OUT_WIDTH = HEADS_PER_GROUP * HEAD_DIM
SCALE = HEAD_DIM ** -0.5
D_CONV = D_MODEL // 2
PLE_DIM = 256
N_EXPERT_GROUPS = 4
EXPERTS_PER_GROUP = 8
N_EXPERTS = N_EXPERT_GROUPS * EXPERTS_PER_GROUP
TOP_K = 2
D_EXPERT = D_MODEL // 4
EPS = 1e-6

COL_B = 0
COL_C = D_CONV
COL_H = 2 * D_CONV
COL_Q = 3 * D_CONV
COL_K = COL_Q + ATT_WIDTH
COL_V = COL_K + ATT_WIDTH
COL_GC = COL_V + ATT_WIDTH
COL_GA = COL_GC + D_MODEL
IN_COLS = COL_GA + D_MODEL

LANE = 128
SUBLANE = 8
NEG = -1e30
ATT_UNIT = max(WINDOWS)
ATT_BLK = 128
ATT_TILE_UNROLL = 8
MXU_N = 256
TOK_ROWS = D_MODEL // LANE
MOE_TM = 288
VMEM_LIMIT = 56 * 1024 * 1024

BF16 = jnp.bfloat16
F32 = jnp.float32


def _params(sem):
    return pltpu.CompilerParams(dimension_semantics=sem, vmem_limit_bytes=VMEM_LIMIT)


def _rms(x, gain):
    return x * lax.rsqrt(jnp.mean(x * x, axis=-1, keepdims=True) + EPS) * gain


def _dot(a, b):
    return jnp.dot(a, b, preferred_element_type=F32)


def _dot_nt(a, b):
    return lax.dot_general(a, b, (((1,), (1,)), ((), ())), preferred_element_type=F32)


def _inproj_kernel(x_ref, xs_ref, g_ref, w_ref, z_ref, zs_ref, h_scr, hs_scr):
    i, j = pl.program_id(0), pl.program_id(1)
    with_sample = i == pl.num_programs(0) - 1

    @pl.when(j == 0)
    def _():
        h_scr[...] = _rms(x_ref[...], g_ref[...]).astype(BF16)

    @pl.when(with_sample & (j == 0))
    def _():
        hs_scr[...] = _rms(xs_ref[...], g_ref[...]).astype(BF16)

    n_chunks = w_ref.shape[1] // MXU_N
    for c in range(n_chunks):
        cols = slice(c * MXU_N, (c + 1) * MXU_N)
        z_ref[:, cols] = _dot(h_scr[...], w_ref[:, cols].astype(BF16))

    @pl.when(with_sample)
    def _():
        for c in range(n_chunks):
            cols = slice(c * MXU_N, (c + 1) * MXU_N)
            zs_ref[:, cols] = _dot(hs_scr[...], w_ref[:, cols].astype(BF16))


def _inproj(x, xs, norm_g, w_in, tm, tn):
    m, m_s = x.shape[0], xs.shape[0]
    n_i = m // tm
    return pl.pallas_call(
        _inproj_kernel,
        grid=(n_i, IN_COLS // tn),
        in_specs=[
            pl.BlockSpec((tm, D_MODEL), lambda i, j: (i, 0), pipeline_mode=pl.Buffered(1)),
            pl.BlockSpec((m_s, D_MODEL), lambda i, j: (0, 0), pipeline_mode=pl.Buffered(1)),
            pl.BlockSpec((1, D_MODEL), lambda i, j: (0, 0)),
            pl.BlockSpec((D_MODEL, tn), lambda i, j: (0, j)),
        ],
        out_specs=[
            pl.BlockSpec((tm, tn), lambda i, j: (i, j)),
            pl.BlockSpec((m_s, tn), lambda i, j: (0, jnp.where(i == n_i - 1, j, 0))),
        ],
        out_shape=[jax.ShapeDtypeStruct((m, IN_COLS), F32),
                   jax.ShapeDtypeStruct((m_s, IN_COLS), F32)],
        scratch_shapes=[pltpu.VMEM((tm, D_MODEL), BF16), pltpu.VMEM((m_s, D_MODEL), BF16)],
        compiler_params=_params(("arbitrary", "arbitrary")),
    )(x, xs, norm_g.reshape(1, D_MODEL), w_in)


def _prompt_attn_kernel(*refs):
    (q0, q1, q2, kc0, kc1, kc2, vc0, vc1, vc2, kp0, kp1, kp2, vp0, vp1, vp2, qg_ref, kg_ref,
     o_ref, pk0, pk1, pk2, pv0, pv1, pv2, kext, vext, acc_s, m_s, l_s) = refs
    q_refs, kc_refs, vc_refs = (q0, q1, q2), (kc0, kc1, kc2), (vc0, vc1, vc2)
    kp_refs, vp_refs = (kp0, kp1, kp2), (vp0, vp1, vp2)
    pk_refs, pv_refs = (pk0, pk1, pk2), (pv0, pv1, pv2)
    n = pl.program_id(1)
    last = pl.num_programs(1) - 1

    qi = lax.broadcasted_iota(jnp.int32, (ATT_BLK, 2 * ATT_BLK), 0) + ATT_BLK
    ki = lax.broadcasted_iota(jnp.int32, (ATT_BLK, 2 * ATT_BLK), 1)
    dist = qi - ki
    band = (dist >= 0) & (dist <= ATT_BLK)

    for g in range(N_GROUPS):
        win, dil = WINDOWS[g], DILATIONS[g]
        q_ref = q_refs[g]
        qgain = qg_ref[g, 0]
        kgain = kg_ref[g, 0]
        kext[0:win, :] = _rms(kp_refs[g][...], kgain)
        kext[win:win + ATT_UNIT, :] = _rms(kc_refs[g][...], kgain)
        vext[0:win, :] = vp_refs[g][...]
        vext[win:win + ATT_UNIT, :] = vc_refs[g][...]

        @pl.when(n == last)
        def _():
            pk_refs[g][...] = kext[ATT_UNIT:ATT_UNIT + win, :]
            pv_refs[g][...] = vext[ATT_UNIT:ATT_UNIT + win, :]

        def tile(t, carry, g=g, win=win, dil=dil, q_ref=q_ref, qgain=qgain):
            u = t // dil
            r = t % dil
            off = u * win + r
            if dil == 1:
                off = pl.multiple_of(off, ATT_BLK)
                rows_q = pl.ds(off, ATT_BLK)
                rows_k = pl.ds(off, 2 * ATT_BLK)
            else:
                rows_q = pl.ds(off, ATT_BLK, stride=dil)
                rows_k = pl.ds(off, 2 * ATT_BLK, stride=dil)
            q = _rms(q_ref[rows_q, :], qgain).astype(BF16)
            k = kext[rows_k, :].astype(BF16)
            v = vext[rows_k, :].astype(BF16)
            s = _dot_nt(q, k) * SCALE
            k_min = jnp.where((n > 0) | (u > 0), 0, ATT_BLK)
            s = jnp.where(band & (ki >= k_min), s, NEG)
            m_t = jnp.max(s, axis=-1, keepdims=True)
            p = jnp.exp(s - m_t)
            l_t = jnp.sum(p, axis=-1, keepdims=True)
            m_s[g, rows_q, :] = jnp.broadcast_to(m_t, (ATT_BLK, LANE))
            l_s[g, rows_q, :] = jnp.broadcast_to(l_t, (ATT_BLK, LANE))
            acc_s[g, rows_q, :] = _dot(p.astype(BF16), v)
            return carry

        lax.fori_loop(0, ATT_UNIT // ATT_BLK, tile, 0, unroll=ATT_TILE_UNROLL)

    def finish(c, carry):
        rows = pl.ds(pl.multiple_of(c * ATT_BLK, ATT_BLK), ATT_BLK)
        m = [m_s[g, rows, :] for g in range(N_GROUPS)]
        m_all = jnp.maximum(jnp.maximum(m[0], m[1]), m[2])
        num = jnp.zeros((ATT_BLK, LANE), F32)
        den = jnp.zeros((ATT_BLK, LANE), F32)
        for g in range(N_GROUPS):
            w = jnp.exp(m[g] - m_all)
            num = num + w * acc_s[g, rows, :]
            den = den + w * l_s[g, rows, :]
        o_ref[rows, :] = num / den
        return carry

    lax.fori_loop(0, ATT_UNIT // ATT_BLK, finish, 0, unroll=2)


def _prompt_attn(z, q_gain, k_gain):
    s_len = z.shape[0]
    n_units = s_len // ATT_UNIT
    in_specs, args = [], []

    def col(base, g):
        return lambda hh, n: (n, base // LANE + g * HEADS_PER_GROUP + hh)

    for g in range(N_GROUPS):
        in_specs.append(pl.BlockSpec((ATT_UNIT, LANE), col(COL_Q, g)))
    for base in (COL_K, COL_V):
        for g in range(N_GROUPS):
            in_specs.append(pl.BlockSpec((ATT_UNIT, LANE), col(base, g)))
    for base in (COL_K, COL_V):
        for g in range(N_GROUPS):
            per = ATT_UNIT // WINDOWS[g]
            in_specs.append(pl.BlockSpec(
                (WINDOWS[g], LANE),
                lambda hh, n, base=base, g=g, per=per: (
                    jnp.maximum(n * per - 1, 0), base // LANE + g * HEADS_PER_GROUP + hh)))
    args = [z] * 15
    gain_spec = pl.BlockSpec((N_GROUPS, 1, 1, LANE), lambda hh, n: (0, hh, 0, 0))
    in_specs += [gain_spec, gain_spec]
    args += [q_gain.reshape(N_GROUPS, HEADS_PER_GROUP, 1, LANE),
             k_gain.reshape(N_GROUPS, HEADS_PER_GROUP, 1, LANE)]

    out_specs = [pl.BlockSpec((ATT_UNIT, LANE), lambda hh, n: (n, hh))]
    out_shape = [jax.ShapeDtypeStruct((s_len, ATT_OUT_WIDTH), F32)]
    for _ in range(2):
        for g in range(N_GROUPS):
            out_specs.append(pl.BlockSpec((WINDOWS[g], LANE), lambda hh, n: (0, hh)))
            out_shape.append(jax.ShapeDtypeStruct((WINDOWS[g], ATT_OUT_WIDTH), F32))

    return pl.pallas_call(
        _prompt_attn_kernel,
        grid=(HEADS_PER_GROUP, n_units),
        in_specs=in_specs,
        out_specs=out_specs,
        out_shape=out_shape,
        scratch_shapes=[
            pltpu.VMEM((2 * ATT_UNIT, LANE), F32),
            pltpu.VMEM((2 * ATT_UNIT, LANE), F32),
            pltpu.VMEM((N_GROUPS, ATT_UNIT, LANE), F32),
            pltpu.VMEM((N_GROUPS, ATT_UNIT, LANE), F32),
            pltpu.VMEM((N_GROUPS, ATT_UNIT, LANE), F32),
        ],
        compiler_params=_params(("parallel", "arbitrary")),
    )(*args)


def _sample_attn_kernel(*refs):
    (q0, q1, q2, kn0, kn1, kn2, vn0, vn1, vn2, ck0, ck1, ck2, cv0, cv1, cv2, qg_ref, kg_ref,
     o_ref, sk0, sk1, sk2, sv0, sv1, sv2) = refs
    q_refs, kn_refs, vn_refs = (q0, q1, q2), (kn0, kn1, kn2), (vn0, vn1, vn2)
    ck_refs, cv_refs = (ck0, ck1, ck2), (cv0, cv1, cv2)
    sk_refs, sv_refs = (sk0, sk1, sk2), (sv0, sv1, sv2)
    t_new = q0.shape[0]
    nh = HEADS_PER_GROUP
    n_q = nh * t_new
    t_bits = t_new.bit_length() - 1

    def heads(ref):
        return [ref[:, h * LANE:(h + 1) * LANE] for h in range(nh)]

    parts = []
    for g in range(N_GROUPS):
        win, dil = WINDOWS[g], DILATIONS[g]
        q_all = jnp.concatenate(
            [_rms(x, qg_ref[g, h]) for h, x in enumerate(heads(q_refs[g]))], axis=0).astype(BF16)
        k_new = [_rms(x, kg_ref[g, h]) for h, x in enumerate(heads(kn_refs[g]))]
        v_new = heads(vn_refs[g])
        k_new_all = jnp.concatenate(k_new, axis=0).astype(BF16)
        v_new_all = jnp.concatenate(v_new, axis=0).astype(BF16)
        ck = ck_refs[g][0]
        cv = cv_refs[g][0]
        old = (win - t_new) * nh
        sk_refs[g][0, 0:old, :] = ck[t_new * nh:, :]
        sv_refs[g][0, 0:old, :] = cv[t_new * nh:, :]
        for h in range(nh):
            sk_refs[g][0, pl.ds(old + h, t_new, stride=nh), :] = k_new[h]
            sv_refs[g][0, pl.ds(old + h, t_new, stride=nh), :] = v_new[h]

        s_c = _dot_nt(q_all, ck.astype(BF16)) * SCALE
        s_n = _dot_nt(q_all, k_new_all) * SCALE
        row = lax.broadcasted_iota(jnp.int32, (n_q, win * nh), 0)
        col = lax.broadcasted_iota(jnp.int32, (n_q, win * nh), 1)
        i_q, pos = row & (t_new - 1), col >> 2
        mask_c = (((col & (nh - 1)) == (row >> t_bits)) & (pos >= i_q)
                  & (((pos - i_q) & (dil - 1)) == 0))
        row_n = lax.broadcasted_iota(jnp.int32, (n_q, n_q), 0)
        col_n = lax.broadcasted_iota(jnp.int32, (n_q, n_q), 1)
        d_n = (row_n & (t_new - 1)) - (col_n & (t_new - 1))
        mask_n = ((row_n >> t_bits) == (col_n >> t_bits)) & (d_n >= 0) & ((d_n & (dil - 1)) == 0)
        s_c = jnp.where(mask_c, s_c, NEG)
        s_n = jnp.where(mask_n, s_n, NEG)
        m = jnp.maximum(jnp.max(s_c, axis=-1, keepdims=True), jnp.max(s_n, axis=-1, keepdims=True))
        p_c = jnp.exp(s_c - m)
        p_n = jnp.exp(s_n - m)
        l = jnp.sum(p_c, axis=-1, keepdims=True) + jnp.sum(p_n, axis=-1, keepdims=True)
        acc = _dot(p_c.astype(BF16), cv.astype(BF16)) + _dot(p_n.astype(BF16), v_new_all)
        parts.append((m, l, acc))

    m_all = jnp.maximum(jnp.maximum(parts[0][0], parts[1][0]), parts[2][0])
    num = jnp.zeros((n_q, LANE), F32)
    den = jnp.zeros((n_q, 1), F32)
    for m, l, acc in parts:
        w = jnp.exp(m - m_all)
        num = num + w * acc
        den = den + w * l
    o_all = num / den
    for h in range(nh):
        o_ref[:, h * LANE:(h + 1) * LANE] = o_all[h * t_new:(h + 1) * t_new, :]


def _sample_attn(z, caches_k, caches_v, q_gain, k_gain, n_seq, t_new):
    assert HEADS_PER_GROUP == 4 and t_new & (t_new - 1) == 0
    in_specs, args = [], []
    for base in (COL_Q, COL_K, COL_V):
        for g in range(N_GROUPS):
            in_specs.append(pl.BlockSpec(
                (t_new, ATT_OUT_WIDTH), lambda b, base=base, g=g: (b, base // ATT_OUT_WIDTH + g)))
            args.append(z)
    for caches in (caches_k, caches_v):
        for g in range(N_GROUPS):
            in_specs.append(pl.BlockSpec((1, WINDOWS[g] * HEADS_PER_GROUP, LANE), lambda b: (b, 0, 0)))
            args.append(caches[g])
    gain_spec = pl.BlockSpec((N_GROUPS, HEADS_PER_GROUP, 1, LANE), lambda b: (0, 0, 0, 0))
    in_specs += [gain_spec, gain_spec]
    args += [q_gain.reshape(N_GROUPS, HEADS_PER_GROUP, 1, LANE),
             k_gain.reshape(N_GROUPS, HEADS_PER_GROUP, 1, LANE)]

    out_specs = [pl.BlockSpec((t_new, ATT_OUT_WIDTH), lambda b: (b, 0))]
    out_shape = [jax.ShapeDtypeStruct((n_seq * t_new, ATT_OUT_WIDTH), F32)]
    for _ in range(2):
        for g in range(N_GROUPS):
            rows = WINDOWS[g] * HEADS_PER_GROUP
            out_specs.append(pl.BlockSpec((1, rows, LANE), lambda b: (b, 0, 0)))
            out_shape.append(jax.ShapeDtypeStruct((n_seq, rows, LANE), F32))

    return pl.pallas_call(
        _sample_attn_kernel,
        grid=(n_seq,),
        in_specs=in_specs,
        out_specs=out_specs,
        out_shape=out_shape,
        compiler_params=_params(("parallel",)),
    )(*args)


def _conv_taps(u, r1, r2, cw_ref):
    return cw_ref[2:3, :] * u + cw_ref[1:2, :] * r1 + cw_ref[0:1, :] * r2


def _merge_kernel(*refs, per_seq, n_chunks):
    b_ref, c_ref, h_ref, pc_ref, ph_ref, cw_ref, o_ref = refs[:7]
    gc_refs = refs[7:7 + n_chunks]
    ga_refs = refs[7 + n_chunks:7 + 2 * n_chunks]
    wc_ref, wa_ref, out_ref, u_ref, yb_scr = refs[7 + 2 * n_chunks:]
    i = pl.program_id(0)
    tm = b_ref.shape[0]
    u = c_ref[...] * h_ref[...]
    r1 = pltpu.roll(u, 1, axis=0)
    r2 = pltpu.roll(u, 2, axis=0)
    if per_seq is None:
        yb_scr[...] = b_ref[...] * _conv_taps(u, r1, r2, cw_ref)
        up = jnp.where(i > 0, pc_ref[...] * ph_ref[...], 0.0)
        row = lax.broadcasted_iota(jnp.int32, (SUBLANE, D_CONV), 0)
        u8 = u[0:SUBLANE, :]
        r1_8 = jnp.where(row < 1, pltpu.roll(up, 1, axis=0), pltpu.roll(u8, 1, axis=0))
        r2_8 = jnp.where(row < 2, pltpu.roll(up, 2, axis=0), pltpu.roll(u8, 2, axis=0))
        yb_scr[0:SUBLANE, :] = b_ref[0:SUBLANE, :] * _conv_taps(u8, r1_8, r2_8, cw_ref)
        u_ref[...] = u[tm - SUBLANE:tm, :]
    else:
        pre = pc_ref[...]
        t = lax.broadcasted_iota(jnp.int32, (tm, D_CONV), 0) & (per_seq - 1)
        r1 = jnp.where(t == 0, pltpu.roll(pre, tm - 1, axis=0), r1)
        r2 = jnp.where(t < 2, pre, r2)
        yb_scr[...] = b_ref[...] * _conv_taps(u, r1, r2, cw_ref)
        u_ref[...] = u

    yb = yb_scr[...].astype(BF16)
    ob = o_ref[...].astype(BF16)
    tn = D_MODEL // n_chunks
    for c in range(n_chunks):
        cols = slice(c * tn, (c + 1) * tn)
        y_conv = _dot(yb, wc_ref[:, cols])
        y_att = _dot(ob, wa_ref[:, cols])
        out_ref[:, cols] = (jax.nn.sigmoid(gc_refs[c][...]) * y_conv
                            + jax.nn.sigmoid(ga_refs[c][...]) * y_att).astype(BF16)


def _merge(z, o_att, conv_w, wc, wa, tm, per_seq=None, prefix=None):
    m = z.shape[0]
    tn = ATT_OUT_WIDTH
    n_chunks = D_MODEL // tn
    if per_seq is None:
        prev = lambda i, c: (jnp.maximum(i * (tm // SUBLANE) - 1, 0), c)
        pc_spec = pl.BlockSpec((SUBLANE, D_CONV), lambda i: prev(i, COL_C // D_CONV))
        ph_spec = pl.BlockSpec((SUBLANE, D_CONV), lambda i: prev(i, COL_H // D_CONV))
        pc_arg, ph_arg = z, z
        u_rows = SUBLANE
    else:
        pc_spec = pl.BlockSpec((tm, D_CONV), lambda i: (i, 0))
        ph_spec = pl.BlockSpec((SUBLANE, D_CONV), lambda i: (0, 0))
        pc_arg, ph_arg = prefix, prefix
        u_rows = tm
    gate_specs = [pl.BlockSpec((tm, tn), lambda i, base=base, c=c: (i, base // tn + c))
                  for base in (COL_GC, COL_GA) for c in range(n_chunks)]
    return pl.pallas_call(
        functools.partial(_merge_kernel, per_seq=per_seq, n_chunks=n_chunks),
        grid=(m // tm,),
        in_specs=[
            pl.BlockSpec((tm, D_CONV), lambda i: (i, COL_B // D_CONV)),
            pl.BlockSpec((tm, D_CONV), lambda i: (i, COL_C // D_CONV)),
            pl.BlockSpec((tm, D_CONV), lambda i: (i, COL_H // D_CONV)),
            pc_spec,
            ph_spec,
            pl.BlockSpec((3, D_CONV), lambda i: (0, 0)),
            pl.BlockSpec((tm, ATT_OUT_WIDTH), lambda i: (i, 0)),
            *gate_specs,
            pl.BlockSpec((D_CONV, D_MODEL), lambda i: (0, 0)),
            pl.BlockSpec((ATT_OUT_WIDTH, D_MODEL), lambda i: (0, 0)),
        ],
        out_specs=[
            pl.BlockSpec((tm, D_MODEL), lambda i: (i, 0)),
            pl.BlockSpec((u_rows, D_CONV), lambda i: (i, 0)),
        ],
        out_shape=[
            jax.ShapeDtypeStruct((m, D_MODEL), BF16),
            jax.ShapeDtypeStruct((m // tm * u_rows, D_CONV), F32),
        ],
        scratch_shapes=[pltpu.VMEM((tm, D_CONV), F32)],
        compiler_params=_params(("parallel",)),
    )(z, z, z, pc_arg, ph_arg, conv_w, o_att, *([z] * (2 * n_chunks)), wc, wa)


def _outproj_router_kernel(xp_ref, mgp_ref, xs_ref, mgs_ref, wo_ref, g2_ref, wr_hi_ref, wr_lo_ref,
                           br_ref, x1_ref, x1t_ref, route_ref):
    is_sample = pl.program_id(0) == pl.num_programs(0) - 1

    @pl.when(jnp.logical_not(is_sample))
    def _():
        _outproj_router_tile(xp_ref, mgp_ref, wo_ref, g2_ref, wr_hi_ref, wr_lo_ref, br_ref,
                             x1_ref, x1t_ref, route_ref)

    @pl.when(is_sample)
    def _():
        _outproj_router_tile(xs_ref, mgs_ref, wo_ref, g2_ref, wr_hi_ref, wr_lo_ref, br_ref,
                             x1_ref, x1t_ref, route_ref)


def _outproj_router_tile(x_ref, mg_ref, wo_ref, g2_ref, wr_hi_ref, wr_lo_ref, br_ref, x1_ref,
                         x1t_ref, route_ref):
    x1 = x_ref[...] + _dot(mg_ref[...], wo_ref[...])
    x1_ref[...] = x1
    _to_slabs(x1t_ref, x1)
    h = _rms(x1, g2_ref[...])
    h_hi = h.astype(BF16)
    h_lo = (h - h_hi.astype(F32)).astype(BF16)
    logits = (_dot(h_hi, wr_hi_ref[...]) + _dot(h_hi, wr_lo_ref[...]) + _dot(h_lo, wr_hi_ref[...])
              + br_ref[...])
    lane = lax.broadcasted_iota(jnp.int32, logits.shape, 1)
    is_coarse = lane < N_EXPERT_GROUPS
    coarse = jnp.where(is_coarse, logits, NEG)
    cmax = jnp.max(coarse, axis=-1, keepdims=True)
    grp = jnp.min(jnp.where(coarse == cmax, lane, LANE), axis=-1, keepdims=True)
    p_grp = 1.0 / jnp.sum(jnp.where(is_coarse, jnp.exp(coarse - cmax), 0.0), axis=-1, keepdims=True)
    eid = lane - N_EXPERT_GROUPS
    in_grp = (eid >= 0) & (eid < N_EXPERTS) & ((eid >> 3) == grp)
    fine = jnp.where(in_grp, logits, NEG)
    v1 = jnp.max(fine, axis=-1, keepdims=True)
    i1 = jnp.min(jnp.where(fine == v1, lane, LANE), axis=-1, keepdims=True)
    fine2 = jnp.where(lane == i1, NEG, fine)
    v2 = jnp.max(fine2, axis=-1, keepdims=True)
    i2 = jnp.min(jnp.where(fine2 == v2, lane, LANE), axis=-1, keepdims=True)
    e = jnp.exp(v2 - v1)
    gate1 = p_grp / (1.0 + e)
    gate2 = p_grp * e / (1.0 + e)
    route = jnp.where(lane == 0, (i1 - N_EXPERT_GROUPS).astype(F32),
                      jnp.where(lane == 1, (i2 - N_EXPERT_GROUPS).astype(F32),
                                jnp.where(lane == 2, gate1, jnp.where(lane == 3, gate2, 0.0))))
    route_ref[...] = route


def _outproj_router(xp, mg_p, xs, mg_s, wo, norm2_g, wr_hi, wr_lo, br):
    m_p, tm = xp.shape[0], xs.shape[0]
    assert m_p % tm == 0
    n_p = m_p // tm
    prompt_blk = lambda i: (jnp.minimum(i, n_p - 1), 0)
    fixed = lambda i: (0, 0)
    return pl.pallas_call(
        _outproj_router_kernel,
        grid=(n_p + 1,),
        in_specs=[
            pl.BlockSpec((tm, D_MODEL), prompt_blk),
            pl.BlockSpec((tm, D_MODEL), prompt_blk),
            pl.BlockSpec((tm, D_MODEL), fixed),
            pl.BlockSpec((tm, D_MODEL), fixed),
            pl.BlockSpec((D_MODEL, D_MODEL), fixed),
            pl.BlockSpec((1, D_MODEL), fixed),
            pl.BlockSpec((D_MODEL, LANE), fixed),
            pl.BlockSpec((D_MODEL, LANE), fixed),
            pl.BlockSpec((1, LANE), fixed),
        ],
        out_specs=[
            pl.BlockSpec((tm, D_MODEL), lambda i: (i, 0)),
            pl.BlockSpec((tm * TOK_ROWS, LANE), lambda i: (i, 0)),
            pl.BlockSpec((tm, LANE), lambda i: (i, 0)),
        ],
        out_shape=[
            jax.ShapeDtypeStruct((m_p + tm, D_MODEL), F32),
            jax.ShapeDtypeStruct(((m_p + tm) * TOK_ROWS, LANE), F32),
            jax.ShapeDtypeStruct((m_p + tm, LANE), F32),
        ],
        compiler_params=_params(("arbitrary",)),
    )(xp, mg_p, xs, mg_s, wo, norm2_g.reshape(1, D_MODEL), wr_hi, wr_lo, br)


def _to_slabs(dst_ref, x):
    rows = x.shape[0]
    for s in range(TOK_ROWS):
        dst_ref[pl.ds(s, rows, stride=TOK_ROWS), :] = x[:, s * LANE:(s + 1) * LANE]


def _from_slabs(src_ref, rows):
    return jnp.concatenate(
        [src_ref[pl.ds(s, rows, stride=TOK_ROWS), :] for s in range(TOK_ROWS)], axis=1)


def _dispatch_kernel(slots_ref, pad0_ref, padn_ref, nused_ref, x1t_hbm, xst_hbm, buf, zero, lsem,
                     ssem, zsem, *, tm, n_blocks):
    i = pl.program_id(0)
    n_steps = pl.num_programs(0)
    slab = tm * TOK_ROWS

    def load(step):
        return pltpu.make_async_copy(
            x1t_hbm.at[pl.ds(pl.multiple_of(step * slab, slab), slab), :], buf.at[step % 3],
            lsem.at[step % 3])

    def wait_scatter(step):
        for _ in range(TOP_K):
            pltpu.make_async_copy(buf.at[step % 3], xst_hbm.at[pl.ds(0, slab), :],
                                  ssem.at[step % 3]).wait()

    def pad_rows(start):
        def expert(e, carry):
            def row(j, c):
                dst = pl.multiple_of((pad0_ref[e] + j) * TOK_ROWS, TOK_ROWS)
                cp = pltpu.make_async_copy(zero, xst_hbm.at[pl.ds(dst, TOK_ROWS), :], zsem.at[0])
                if start:
                    cp.start()
                else:
                    cp.wait()
                return c
            return lax.fori_loop(0, padn_ref[e], row, carry)
        lax.fori_loop(0, N_EXPERTS, expert, 0)

    @pl.when(i == 0)
    def _():
        load(0).start()
        zero[...] = jnp.zeros(zero.shape, F32)
        pad_rows(start=True)

    @pl.when(i >= 2)
    def _():
        wait_scatter(i - 2)

    @pl.when(i + 1 < n_steps)
    def _():
        load(i + 1).start()

    load(i).wait()
    cur = i % 3

    def body(r, carry):
        src = buf.at[cur, pl.ds(pl.multiple_of(r * TOK_ROWS, TOK_ROWS), TOK_ROWS), :]
        for k in range(TOP_K):
            dst = pl.multiple_of(slots_ref[(i * tm + r) * TOP_K + k] * TOK_ROWS, TOK_ROWS)
            pltpu.make_async_copy(src, xst_hbm.at[pl.ds(dst, TOK_ROWS), :],
                                  ssem.at[cur]).start(priority=k)
        return carry
    lax.fori_loop(0, tm, body, 0, unroll=4)

    @pl.when(i == n_steps - 1)
    def _():
        @pl.when(i >= 1)
        def _():
            wait_scatter(i - 1)
        wait_scatter(i)
        pad_rows(start=False)
        chunk = math.gcd(tm, MOE_TM)
        rows = chunk * TOK_ROWS
        buf[0, 0:rows, :] = jnp.zeros((rows, LANE), F32)
        first = nused_ref[0] * MOE_TM
        n_tail = (n_blocks - nused_ref[0]) * (MOE_TM // chunk)

        def tail(j, carry, start):
            dst = pl.multiple_of((first + j * chunk) * TOK_ROWS, TOK_ROWS)
            cp = pltpu.make_async_copy(buf.at[0, pl.ds(0, rows), :],
                                       xst_hbm.at[pl.ds(dst, rows), :], zsem.at[0])
            if start:
                cp.start()
            else:
                cp.wait()
            return carry
        lax.fori_loop(0, n_tail, functools.partial(tail, start=True), 0)
        lax.fori_loop(0, n_tail, functools.partial(tail, start=False), 0)


def _dispatch(x1t, tok_slots, pad_start, pad_count, n_used, n_blocks, tm):
    n_tok = x1t.shape[0] // TOK_ROWS
    assert n_tok % tm == 0
    grid_spec = pltpu.PrefetchScalarGridSpec(
        num_scalar_prefetch=4,
        grid=(n_tok // tm,),
        in_specs=[pl.BlockSpec(memory_space=pl.ANY)],
        out_specs=pl.BlockSpec(memory_space=pl.ANY),
        scratch_shapes=[
            pltpu.VMEM((3, tm * TOK_ROWS, LANE), F32),
            pltpu.VMEM((TOK_ROWS, LANE), F32),
            pltpu.SemaphoreType.DMA((3,)),
            pltpu.SemaphoreType.DMA((3,)),
            pltpu.SemaphoreType.DMA((1,)),
        ],
    )
    return pl.pallas_call(
        functools.partial(_dispatch_kernel, tm=tm, n_blocks=n_blocks),
        grid_spec=grid_spec,
        out_shape=jax.ShapeDtypeStruct((n_blocks * MOE_TM * TOK_ROWS, LANE), F32),
        compiler_params=_params(("arbitrary",)),
    )(tok_slots, pad_start, pad_count, n_used, x1t)


def _moe_kernel(blk0_ref, nblk_ref, nused_ref, xst_hbm, g2_ref, wg_ref, wu_ref, wd_ref,
                yt_hbm, xbuf, ybuf, gsem, osem, wg_b, wu_b, wd_b):
    e = pl.program_id(0)
    n_used = nused_ref[0]
    slab = MOE_TM * TOK_ROWS

    def in_copy(blk, to_slot):
        return pltpu.make_async_copy(
            xst_hbm.at[pl.ds(pl.multiple_of(blk * slab, slab), slab), :], xbuf.at[to_slot],
            gsem.at[to_slot])

    def out_copy(blk, from_slot):
        return pltpu.make_async_copy(
            ybuf.at[from_slot], yt_hbm.at[pl.ds(pl.multiple_of(blk * slab, slab), slab), :],
            osem.at[from_slot])

    wg_b[...] = wg_ref[0].astype(BF16)
    wu_b[...] = wu_ref[0].astype(BF16)
    wd_b[...] = wd_ref[0].astype(BF16)

    @pl.when(e == 0)
    def _():
        in_copy(0, 0).start()

    def block(b, carry):
        blk = blk0_ref[e] + b
        slot = blk & 1

        @pl.when(blk + 1 < n_used)
        def _():
            in_copy(blk + 1, 1 - slot).start(priority=1)

        in_copy(blk, slot).wait()
        h = _rms(_from_slabs(xbuf.at[slot], MOE_TM), g2_ref[...]).astype(BF16)
        a = _dot(h, wg_b[...])
        u = _dot(h, wu_b[...])
        hm = (a * jax.nn.sigmoid(a) * u).astype(BF16)
        y = _dot(hm, wd_b[...])

        @pl.when(blk >= 2)
        def _():
            out_copy(blk - 2, slot).wait()

        _to_slabs(ybuf.at[slot], y)
        out_copy(blk, slot).start()
        return carry

    lax.fori_loop(0, nblk_ref[e], block, 0)

    @pl.when(e == pl.num_programs(0) - 1)
    def _():
        @pl.when(n_used >= 2)
        def _():
            out_copy(n_used - 2, (n_used - 2) & 1).wait()
        out_copy(n_used - 1, (n_used - 1) & 1).wait()


def _moe(xst, norm2_g, w_gate_e, w_up_e, w_down_e, blk_start, nblk, n_used):
    w_idx = lambda e, *_: (e, 0, 0)
    slab = MOE_TM * TOK_ROWS
    grid_spec = pltpu.PrefetchScalarGridSpec(
        num_scalar_prefetch=3,
        grid=(N_EXPERTS,),
        in_specs=[
            pl.BlockSpec(memory_space=pl.ANY),
            pl.BlockSpec((1, D_MODEL), lambda e, *_: (0, 0)),
            pl.BlockSpec((1, D_MODEL, D_EXPERT), w_idx),
            pl.BlockSpec((1, D_MODEL, D_EXPERT), w_idx),
            pl.BlockSpec((1, D_EXPERT, D_MODEL), w_idx),
        ],
        out_specs=pl.BlockSpec(memory_space=pl.ANY),
        scratch_shapes=[
            pltpu.VMEM((2, slab, LANE), F32),
            pltpu.VMEM((2, slab, LANE), F32),
            pltpu.SemaphoreType.DMA((2,)),
            pltpu.SemaphoreType.DMA((2,)),
            pltpu.VMEM((D_MODEL, D_EXPERT), BF16),
            pltpu.VMEM((D_MODEL, D_EXPERT), BF16),
            pltpu.VMEM((D_EXPERT, D_MODEL), BF16),
        ],
    )
    return pl.pallas_call(
        _moe_kernel,
        grid_spec=grid_spec,
        out_shape=jax.ShapeDtypeStruct(xst.shape, F32),
        input_output_aliases={3: 0},
        compiler_params=_params(("arbitrary",)),
    )(blk_start, nblk, n_used, xst, norm2_g.reshape(1, D_MODEL), w_gate_e, w_up_e, w_down_e)


def _combine_ple_kernel(slots_ref, x1_ref, route_ref, yt_hbm, p_ref, g3_ref, wpg_ref, wple_ref,
                        out_ref, ybuf, sem, *, tm, row0):
    i = pl.program_id(0)
    n_steps = pl.num_programs(0)
    slot = i & 1
    slab = tm * TOK_ROWS

    def start_gather(step, to_slot):
        base = (row0 + step * tm) * TOP_K

        def body(r, carry):
            for k in range(TOP_K):
                src = slots_ref[base + r * TOP_K + k]
                pltpu.make_async_copy(
                    yt_hbm.at[pl.ds(pl.multiple_of(src * TOK_ROWS, TOK_ROWS), TOK_ROWS), :],
                    ybuf.at[to_slot, k, pl.ds(pl.multiple_of(r * TOK_ROWS, TOK_ROWS), TOK_ROWS), :],
                    sem.at[to_slot]).start(priority=k)
            return carry
        lax.fori_loop(0, tm, body, 0, unroll=4)

    @pl.when(i == 0)
    def _():
        start_gather(0, 0)

    @pl.when(i + 1 < n_steps)
    def _():
        start_gather(i + 1, 1 - slot)

    for k in range(TOP_K):
        pltpu.make_async_copy(yt_hbm.at[pl.ds(0, slab), :], ybuf.at[slot, k], sem.at[slot]).wait()
    route = route_ref[...]
    x2 = x1_ref[...] + (route[:, 2:3] * _from_slabs(ybuf.at[slot, 0], tm)
                        + route[:, 3:4] * _from_slabs(ybuf.at[slot, 1], tm))
    h = _rms(x2, g3_ref[...]).astype(BF16)
    gate = jax.nn.sigmoid(_dot(h, wpg_ref[...]))
    out_ref[...] = x2 + gate * _dot(p_ref[...].astype(BF16), wple_ref[...])


def _combine_ple(x1_all, route_all, y_slots, tok_slots, p, norm3_g, wpg, wple, m, tm, row0):
    blk0 = row0 // tm
    grid_spec = pltpu.PrefetchScalarGridSpec(
        num_scalar_prefetch=1,
        grid=(m // tm,),
        in_specs=[
            pl.BlockSpec((tm, D_MODEL), lambda i, s: (blk0 + i, 0)),
            pl.BlockSpec((tm, LANE), lambda i, s: (blk0 + i, 0)),
            pl.BlockSpec(memory_space=pl.ANY),
            pl.BlockSpec((tm, PLE_DIM), lambda i, s: (i, 0)),
            pl.BlockSpec((1, D_MODEL), lambda i, s: (0, 0)),
            pl.BlockSpec((D_MODEL, D_MODEL), lambda i, s: (0, 0)),
            pl.BlockSpec((PLE_DIM, D_MODEL), lambda i, s: (0, 0)),
        ],
        out_specs=pl.BlockSpec((tm, D_MODEL), lambda i, s: (i, 0)),
        scratch_shapes=[
            pltpu.VMEM((2, TOP_K, tm * TOK_ROWS, LANE), F32),
            pltpu.SemaphoreType.DMA((2,)),
        ],
    )
    return pl.pallas_call(
        functools.partial(_combine_ple_kernel, tm=tm, row0=row0),
        grid_spec=grid_spec,
        out_shape=jax.ShapeDtypeStruct((m, D_MODEL), F32),
        compiler_params=_params(("arbitrary",)),
    )(tok_slots, x1_all, route_all, y_slots, p, norm3_g.reshape(1, D_MODEL), wpg, wple)


def _routing_tables(route_all):
    flat_e = route_all[:, 0:TOP_K].astype(jnp.int32).reshape(-1)
    onehot = (flat_e[:, None] == jnp.arange(N_EXPERTS, dtype=jnp.int32)[None, :]).astype(jnp.int32)
    csum = jnp.cumsum(onehot, axis=0)
    rank = jnp.take_along_axis(csum, flat_e[:, None], axis=1)[:, 0] - 1
    counts = csum[-1]
    nblk = (counts + MOE_TM - 1) // MOE_TM
    blk_end = jnp.cumsum(nblk)
    blk_start = blk_end - nblk
    slot = blk_start[flat_e] * MOE_TM + rank
    n_used = blk_end[-1].reshape(1)
    pad_start = blk_start * MOE_TM + counts
    pad_count = nblk * MOE_TM - counts
    i32 = lambda a: a.astype(jnp.int32)
    return i32(blk_start), i32(nblk), i32(n_used), i32(slot), i32(pad_start), i32(pad_count)


def kernel(x_prompt, x_sample, p_prompt, p_sample, cache_k_g0, cache_v_g0, cache_k_g1, cache_v_g1,
           cache_k_g2, cache_v_g2, state_conv, norm1_g, w_in, q_gain, k_gain, conv_w, w_conv_out,
           w_attn_out, w_o, norm2_g, w_coarse, b_coarse, w_fine, b_fine, w_gate_e, w_up_e, w_down_e,
           norm3_g, w_ple_gate, w_ple):
    seq = x_prompt.shape[1]
    assert x_prompt.shape == (1, seq, D_MODEL) and norm1_g.shape[0] == 1, "one prompt, one layer"
    n_seq, t_new, _ = x_sample.shape
    assert seq % ATT_UNIT == 0 and t_new == SUBLANE
    caches_k = (cache_k_g0, cache_k_g1, cache_k_g2)
    caches_v = (cache_v_g0, cache_v_g1, cache_v_g2)
    for g in range(N_GROUPS):
        assert caches_k[g].shape == (1, n_seq, WINDOWS[g], HEADS_PER_GROUP, HEAD_DIM)
    n_s = n_seq * t_new
    n_all = seq + n_s

    xp = x_prompt[0]
    xs = x_sample.reshape(n_s, D_MODEL)
    wc = w_conv_out[0].astype(BF16)
    wa = w_attn_out[0].astype(BF16)
    wo = w_o[0].astype(BF16)
    wpg = w_ple_gate[0].astype(BF16)
    wple = w_ple[0].astype(BF16)
    wr = jnp.concatenate(
        [w_coarse[0], jnp.transpose(w_fine[0], (1, 0, 2)).reshape(D_MODEL, N_EXPERTS)], axis=1)
    wr = jnp.pad(wr, ((0, 0), (0, LANE - wr.shape[1])))
    wr_hi = wr.astype(BF16)
    wr_lo = (wr - wr_hi.astype(F32)).astype(BF16)
    br = jnp.pad(jnp.concatenate([b_coarse[0], b_fine[0].reshape(-1)]), (0, LANE - 36)).reshape(1, LANE)

    z_p, z_s = _inproj(xp, xs, norm1_g[0], w_in[0], tm=2048, tn=512)

    att_p = _prompt_attn(z_p, q_gain[0], k_gain[0])
    o_p, pks, pvs = att_p[0], att_p[1:4], att_p[4:7]
    ck = [c.reshape(n_seq, WINDOWS[g] * HEADS_PER_GROUP, HEAD_DIM) for g, c in enumerate(caches_k)]
    cv = [c.reshape(n_seq, WINDOWS[g] * HEADS_PER_GROUP, HEAD_DIM) for g, c in enumerate(caches_v)]
    att_s = _sample_attn(z_s, ck, cv, q_gain[0], k_gain[0], n_seq, t_new)
    o_s, sks, svs = att_s[0], att_s[1:4], att_s[4:7]

    mg_p, u_tail = _merge(z_p, o_p, conv_w[0], wc, wa, tm=512)
    prefix = jnp.pad(state_conv[0], ((0, 0), (0, t_new - 2), (0, 0))).reshape(n_s, D_CONV)
    mg_s, u_s = _merge(z_s, o_s, conv_w[0], wc, wa, tm=n_s, per_seq=t_new, prefix=prefix)
    x1_all, x1t_all, route_all = _outproj_router(xp, mg_p, xs, mg_s, wo, norm2_g[0], wr_hi, wr_lo, br)

    n_blocks = n_all * TOP_K // MOE_TM + N_EXPERTS
    blk_start, nblk, n_used, tok_slots, pad_start, pad_count = _routing_tables(route_all)
    xst = _dispatch(x1t_all, tok_slots, pad_start, pad_count, n_used, n_blocks, tm=n_s)
    y_slots = _moe(xst, norm2_g[0], w_gate_e[0], w_up_e[0], w_down_e[0], blk_start, nblk, n_used)

    y_p = _combine_ple(x1_all, route_all, y_slots, tok_slots, p_prompt[0, 0], norm3_g[0], wpg, wple,
                       m=seq, tm=256, row0=0)
    y_s = _combine_ple(x1_all, route_all, y_slots, tok_slots, p_sample[0].reshape(n_s, PLE_DIM),
                       norm3_g[0], wpg, wple, m=n_s, tm=n_s, row0=seq)

    def state(a, n, g):
        return a.reshape(1, n, WINDOWS[g], HEADS_PER_GROUP, HEAD_DIM)

    pk = [state(a, 1, g) for g, a in enumerate(pks)]
    pv = [state(a, 1, g) for g, a in enumerate(pvs)]
    sk = [state(a, n_seq, g) for g, a in enumerate(sks)]
    sv = [state(a, n_seq, g) for g, a in enumerate(svs)]
    pconv = u_tail[-2:].reshape(1, 1, 2, D_CONV)
    sconv = u_s.reshape(n_seq, t_new, D_CONV)[:, t_new - 2:].reshape(1, n_seq, 2, D_CONV)
    return (y_p.reshape(1, seq, D_MODEL), y_s.reshape(n_seq, t_new, D_MODEL),
            pk[0], pv[0], pk[1], pv[1], pk[2], pv[2], pconv,
            sk[0], sv[0], sk[1], sv[1], sk[2], sv[2], sconv)
```

```python
import functools
import math

import jax
import jax.numpy as jnp
from jax import lax
from jax.experimental import pallas as pl
from jax.experimental.pallas import tpu as pltpu

D_MODEL = 2048
HEAD_DIM = 128
HEADS_PER_GROUP = 4
WINDOWS = (128, 512, 2048)
DILATIONS = (1, 4, 16)
N_GROUPS = 3
N_HEADS = N_GROUPS * HEADS_PER_GROUP
ATT_WIDTH = N_HEADS * HEAD_DIM
ATT_OUT_WIDTH = HEADS_PER_GROUP * HEAD_DIM
SCALE = HEAD_DIM ** -0.5
D_CONV = D_MODEL // 2
PLE_DIM = 256
N_EXPERT_GROUPS = 4
EXPERTS_PER_GROUP = 8
N_EXPERTS = N_EXPERT_GROUPS * EXPERTS_PER_GROUP
TOP_K = 2
D_EXPERT = D_MODEL // 4
EPS = 1e-6

COL_B = 0
COL_C = D_CONV
COL_H = 2 * D_CONV
COL_Q = 3 * D_CONV
COL_K = COL_Q + ATT_WIDTH
COL_V = COL_K + ATT_WIDTH
COL_GC = COL_V + ATT_WIDTH
COL_GA = COL_GC + D_MODEL
IN_COLS = COL_GA + D_MODEL

LANE = 128
SUBLANE = 8
NEG = -1e30
ATT_UNIT = max(WINDOWS)
ATT_BLK = 128
ATT_TILE_UNROLL = 8
MXU_N = 256
TOK_ROWS = D_MODEL // LANE
MOE_TM = 288
VMEM_LIMIT = 56 * 1024 * 1024

BF16 = jnp.bfloat16
F32 = jnp.float32


def _params(sem):
    return pltpu.CompilerParams(dimension_semantics=sem, vmem_limit_bytes=VMEM_LIMIT)


def _rms(x, gain):
    return x * lax.rsqrt(jnp.mean(x * x, axis=-1, keepdims=True) + EPS) * gain


def _dot(a, b):
    return jnp.dot(a, b, preferred_element_type=F32)


def _dot_nt(a, b):
    return lax.dot_general(a, b, (((1,), (1,)), ((), ())), preferred_element_type=F32)


def _inproj_kernel(x_ref, xs_ref, g_ref, w_ref, z_ref, zs_ref, h_scr, hs_scr):
    i, j = pl.program_id(0), pl.program_id(1)
    with_sample = i == pl.num_programs(0) - 1

    @pl.when(j == 0)
    def _():
        h_scr[...] = _rms(x_ref[...], g_ref[...]).astype(BF16)

    @pl.when(with_sample & (j == 0))
    def _():
        hs_scr[...] = _rms(xs_ref[...], g_ref[...]).astype(BF16)

    n_chunks = w_ref.shape[1] // MXU_N
    for c in range(n_chunks):
        cols = slice(c * MXU_N, (c + 1) * MXU_N)
        z_ref[:, cols] = _dot(h_scr[...], w_ref[:, cols].astype(BF16))

    @pl.when(with_sample)
    def _():
        for c in range(n_chunks):
            cols = slice(c * MXU_N, (c + 1) * MXU_N)
            zs_ref[:, cols] = _dot(hs_scr[...], w_ref[:, cols].astype(BF16))


def _inproj(x, xs, norm_g, w_in, tm, tn):
    m, m_s = x.shape[0], xs.shape[0]
    n_i = m // tm
    return pl.pallas_call(
        _inproj_kernel,
        grid=(n_i, IN_COLS // tn),
        in_specs=[
            pl.BlockSpec((tm, D_MODEL), lambda i, j: (i, 0), pipeline_mode=pl.Buffered(1)),
            pl.BlockSpec((m_s, D_MODEL), lambda i, j: (0, 0), pipeline_mode=pl.Buffered(1)),
            pl.BlockSpec((1, D_MODEL), lambda i, j: (0, 0)),
            pl.BlockSpec((D_MODEL, tn), lambda i, j: (0, j)),
        ],
        out_specs=[
            pl.BlockSpec((tm, tn), lambda i, j: (i, j)),
            pl.BlockSpec((m_s, tn), lambda i, j: (0, jnp.where(i == n_i - 1, j, 0))),
        ],
        out_shape=[jax.ShapeDtypeStruct((m, IN_COLS), F32),
                   jax.ShapeDtypeStruct((m_s, IN_COLS), F32)],
        scratch_shapes=[pltpu.VMEM((tm, D_MODEL), BF16), pltpu.VMEM((m_s, D_MODEL), BF16)],
        compiler_params=_params(("arbitrary", "arbitrary")),
    )(x, xs, norm_g.reshape(1, D_MODEL), w_in)


def _prompt_attn_kernel(*refs):
    (q0, q1, q2, kc0, kc1, kc2, vc0, vc1, vc2, kp0, kp1, kp2, vp0, vp1, vp2, qg_ref, kg_ref,
     o_ref, pk0, pk1, pk2, pv0, pv1, pv2, kext, vext, acc_s, m_s, l_s) = refs
    q_refs, kc_refs, vc_refs = (q0, q1, q2), (kc0, kc1, kc2), (vc0, vc1, vc2)
    kp_refs, vp_refs = (kp0, kp1, kp2), (vp0, vp1, vp2)
    pk_refs, pv_refs = (pk0, pk1, pk2), (pv0, pv1, pv2)
    n = pl.program_id(1)
    last = pl.num_programs(1) - 1

    qi = lax.broadcasted_iota(jnp.int32, (ATT_BLK, 2 * ATT_BLK), 0) + ATT_BLK
    ki = lax.broadcasted_iota(jnp.int32, (ATT_BLK, 2 * ATT_BLK), 1)
    dist = qi - ki
    band = (dist >= 0) & (dist <= ATT_BLK)

    for g in range(N_GROUPS):
        win, dil = WINDOWS[g], DILATIONS[g]
        q_ref = q_refs[g]
        qgain = qg_ref[g, 0]
        kgain = kg_ref[g, 0]
        kext[0:win, :] = _rms(kp_refs[g][...], kgain)
        kext[win:win + ATT_UNIT, :] = _rms(kc_refs[g][...], kgain)
        vext[0:win, :] = vp_refs[g][...]
        vext[win:win + ATT_UNIT, :] = vc_refs[g][...]

        @pl.when(n == last)
        def _():
            pk_refs[g][...] = kext[ATT_UNIT:ATT_UNIT + win, :]
            pv_refs[g][...] = vext[ATT_UNIT:ATT_UNIT + win, :]

        def tile(t, carry, g=g, win=win, dil=dil, q_ref=q_ref, qgain=qgain):
            u = t // dil
            r = t % dil
            off = u * win + r
            if dil == 1:
                off = pl.multiple_of(off, ATT_BLK)
                rows_q = pl.ds(off, ATT_BLK)
                rows_k = pl.ds(off, 2 * ATT_BLK)
            else:
                rows_q = pl.ds(off, ATT_BLK, stride=dil)
                rows_k = pl.ds(off, 2 * ATT_BLK, stride=dil)
            q = _rms(q_ref[rows_q, :], qgain).astype(BF16)
            k = kext[rows_k, :].astype(BF16)
            v = vext[rows_k, :].astype(BF16)
            s = _dot_nt(q, k) * SCALE
            k_min = jnp.where((n > 0) | (u > 0), 0, ATT_BLK)
            s = jnp.where(band & (ki >= k_min), s, NEG)
            m_t = jnp.max(s, axis=-1, keepdims=True)
            p = jnp.exp(s - m_t)
            l_t = jnp.sum(p, axis=-1, keepdims=True)
            m_s[g, rows_q, :] = jnp.broadcast_to(m_t, (ATT_BLK, LANE))
            l_s[g, rows_q, :] = jnp.broadcast_to(l_t, (ATT_BLK, LANE))
            acc_s[g, rows_q, :] = _dot(p.astype(BF16), v)
            return carry

        lax.fori_loop(0, ATT_UNIT // ATT_BLK, tile, 0, unroll=ATT_TILE_UNROLL)

    def finish(c, carry):
        rows = pl.ds(pl.multiple_of(c * ATT_BLK, ATT_BLK), ATT_BLK)
        m = [m_s[g, rows, :] for g in range(N_GROUPS)]
        m_all = jnp.maximum(jnp.maximum(m[0], m[1]), m[2])
        num = jnp.zeros((ATT_BLK, LANE), F32)
        den = jnp.zeros((ATT_BLK, LANE), F32)
        for g in range(N_GROUPS):
            w = jnp.exp(m[g] - m_all)
            num = num + w * acc_s[g, rows, :]
            den = den + w * l_s[g, rows, :]
        o_ref[rows, :] = num / den
        return carry

    lax.fori_loop(0, ATT_UNIT // ATT_BLK, finish, 0, unroll=2)


def _prompt_attn(z, q_gain, k_gain):
    s_len = z.shape[0]
    n_units = s_len // ATT_UNIT
    in_specs, args = [], []

    def col(base, g):
        return lambda hh, n: (n, base // LANE + g * HEADS_PER_GROUP + hh)

    for g in range(N_GROUPS):
        in_specs.append(pl.BlockSpec((ATT_UNIT, LANE), col(COL_Q, g)))
    for base in (COL_K, COL_V):
        for g in range(N_GROUPS):
            in_specs.append(pl.BlockSpec((ATT_UNIT, LANE), col(base, g)))
    for base in (COL_K, COL_V):
        for g in range(N_GROUPS):
            per = ATT_UNIT // WINDOWS[g]
            in_specs.append(pl.BlockSpec(
                (WINDOWS[g], LANE),
                lambda hh, n, base=base, g=g, per=per: (
                    jnp.maximum(n * per - 1, 0), base // LANE + g * HEADS_PER_GROUP + hh)))
    args = [z] * 15
    gain_spec = pl.BlockSpec((N_GROUPS, 1, 1, LANE), lambda hh, n: (0, hh, 0, 0))
    in_specs += [gain_spec, gain_spec]
    args += [q_gain.reshape(N_GROUPS, HEADS_PER_GROUP, 1, LANE),
             k_gain.reshape(N_GROUPS, HEADS_PER_GROUP, 1, LANE)]

    out_specs = [pl.BlockSpec((ATT_UNIT, LANE), lambda hh, n: (n, hh))]
    out_shape = [jax.ShapeDtypeStruct((s_len, ATT_OUT_WIDTH), F32)]
    for _ in range(2):
        for g in range(N_GROUPS):
            out_specs.append(pl.BlockSpec((WINDOWS[g], LANE), lambda hh, n: (0, hh)))
            out_shape.append(jax.ShapeDtypeStruct((WINDOWS[g], ATT_OUT_WIDTH), F32))

    return pl.pallas_call(
        _prompt_attn_kernel,
        grid=(HEADS_PER_GROUP, n_units),
        in_specs=in_specs,
        out_specs=out_specs,
        out_shape=out_shape,
        scratch_shapes=[
            pltpu.VMEM((2 * ATT_UNIT, LANE), F32),
            pltpu.VMEM((2 * ATT_UNIT, LANE), F32),
            pltpu.VMEM((N_GROUPS, ATT_UNIT, LANE), F32),
            pltpu.VMEM((N_GROUPS, ATT_UNIT, LANE), F32),
            pltpu.VMEM((N_GROUPS, ATT_UNIT, LANE), F32),
        ],
        compiler_params=_params(("parallel", "arbitrary")),
    )(*args)


def _sample_attn_kernel(*refs):
    (q0, q1, q2, kn0, kn1, kn2, vn0, vn1, vn2, ck0, ck1, ck2, cv0, cv1, cv2, qg_ref, kg_ref,
     o_ref, sk0, sk1, sk2, sv0, sv1, sv2) = refs
    q_refs, kn_refs, vn_refs = (q0, q1, q2), (kn0, kn1, kn2), (vn0, vn1, vn2)
    ck_refs, cv_refs = (ck0, ck1, ck2), (cv0, cv1, cv2)
    sk_refs, sv_refs = (sk0, sk1, sk2), (sv0, sv1, sv2)
    t_new = q0.shape[0]
    nh = HEADS_PER_GROUP
    n_q = nh * t_new
    t_bits = t_new.bit_length() - 1

    def heads(ref):
        return [ref[:, h * LANE:(h + 1) * LANE] for h in range(nh)]

    parts = []
    for g in range(N_GROUPS):
        win, dil = WINDOWS[g], DILATIONS[g]
        q_all = jnp.concatenate(
            [_rms(x, qg_ref[g, h]) for h, x in enumerate(heads(q_refs[g]))], axis=0).astype(BF16)
        k_new = [_rms(x, kg_ref[g, h]) for h, x in enumerate(heads(kn_refs[g]))]
        v_new = heads(vn_refs[g])
        k_new_all = jnp.concatenate(k_new, axis=0).astype(BF16)
        v_new_all = jnp.concatenate(v_new, axis=0).astype(BF16)
        ck = ck_refs[g][0]
        cv = cv_refs[g][0]
        old = (win - t_new) * nh
        sk_refs[g][0, 0:old, :] = ck[t_new * nh:, :]
        sv_refs[g][0, 0:old, :] = cv[t_new * nh:, :]
        for h in range(nh):
            sk_refs[g][0, pl.ds(old + h, t_new, stride=nh), :] = k_new[h]
            sv_refs[g][0, pl.ds(old + h, t_new, stride=nh), :] = v_new[h]

        s_c = _dot_nt(q_all, ck.astype(BF16)) * SCALE
        s_n = _dot_nt(q_all, k_new_all) * SCALE
        row = lax.broadcasted_iota(jnp.int32, (n_q, win * nh), 0)
        col = lax.broadcasted_iota(jnp.int32, (n_q, win * nh), 1)
        i_q, pos = row & (t_new - 1), col >> 2
        mask_c = (((col & (nh - 1)) == (row >> t_bits)) & (pos >= i_q)
                  & (((pos - i_q) & (dil - 1)) == 0))
        row_n = lax.broadcasted_iota(jnp.int32, (n_q, n_q), 0)
        col_n = lax.broadcasted_iota(jnp.int32, (n_q, n_q), 1)
        d_n = (row_n & (t_new - 1)) - (col_n & (t_new - 1))
        mask_n = ((row_n >> t_bits) == (col_n >> t_bits)) & (d_n >= 0) & ((d_n & (dil - 1)) == 0)
        s_c = jnp.where(mask_c, s_c, NEG)
        s_n = jnp.where(mask_n, s_n, NEG)
        m = jnp.maximum(jnp.max(s_c, axis=-1, keepdims=True), jnp.max(s_n, axis=-1, keepdims=True))
        p_c = jnp.exp(s_c - m)
        p_n = jnp.exp(s_n - m)
        l = jnp.sum(p_c, axis=-1, keepdims=True) + jnp.sum(p_n, axis=-1, keepdims=True)
        acc = _dot(p_c.astype(BF16), cv.astype(BF16)) + _dot(p_n.astype(BF16), v_new_all)
        parts.append((m, l, acc))

    m_all = jnp.maximum(jnp.maximum(parts[0][0], parts[1][0]), parts[2][0])
    num = jnp.zeros((n_q, LANE), F32)
    den = jnp.zeros((n_q, 1), F32)
    for m, l, acc in parts:
        w = jnp.exp(m - m_all)
        num = num + w * acc
        den = den + w * l
    o_all = num / den
    for h in range(nh):
        o_ref[:, h * LANE:(h + 1) * LANE] = o_all[h * t_new:(h + 1) * t_new, :]


def _sample_attn(z, caches_k, caches_v, q_gain, k_gain, n_seq, t_new):
    assert HEADS_PER_GROUP == 4 and t_new & (t_new - 1) == 0
    in_specs, args = [], []
    for base in (COL_Q, COL_K, COL_V):
        for g in range(N_GROUPS):
            in_specs.append(pl.BlockSpec(
                (t_new, ATT_OUT_WIDTH), lambda b, base=base, g=g: (b, base // ATT_OUT_WIDTH + g)))
            args.append(z)
    for caches in (caches_k, caches_v):
        for g in range(N_GROUPS):
            in_specs.append(pl.BlockSpec((1, WINDOWS[g] * HEADS_PER_GROUP, LANE), lambda b: (b, 0, 0)))
            args.append(caches[g])
    gain_spec = pl.BlockSpec((N_GROUPS, HEADS_PER_GROUP, 1, LANE), lambda b: (0, 0, 0, 0))
    in_specs += [gain_spec, gain_spec]
    args += [q_gain.reshape(N_GROUPS, HEADS_PER_GROUP, 1, LANE),
             k_gain.reshape(N_GROUPS, HEADS_PER_GROUP, 1, LANE)]

    out_specs = [pl.BlockSpec((t_new, ATT_OUT_WIDTH), lambda b: (b, 0))]
    out_shape = [jax.ShapeDtypeStruct((n_seq * t_new, ATT_OUT_WIDTH), F32)]
    for _ in range(2):
        for g in range(N_GROUPS):
            rows = WINDOWS[g] * HEADS_PER_GROUP
            out_specs.append(pl.BlockSpec((1, rows, LANE), lambda b: (b, 0, 0)))
            out_shape.append(jax.ShapeDtypeStruct((n_seq, rows, LANE), F32))

    return pl.pallas_call(
        _sample_attn_kernel,
        grid=(n_seq,),
        in_specs=in_specs,
        out_specs=out_specs,
        out_shape=out_shape,
        compiler_params=_params(("parallel",)),
    )(*args)


def _conv_taps(u, r1, r2, cw_ref):
    return cw_ref[2:3, :] * u + cw_ref[1:2, :] * r1 + cw_ref[0:1, :] * r2


def _merge_kernel(*refs, per_seq, n_chunks):
    b_ref, c_ref, h_ref, pc_ref, ph_ref, cw_ref, o_ref = refs[:7]
    gc_refs = refs[7:7 + n_chunks]
    ga_refs = refs[7 + n_chunks:7 + 2 * n_chunks]
    wc_ref, wa_ref, out_ref, u_ref, yb_scr = refs[7 + 2 * n_chunks:]
    i = pl.program_id(0)
    tm = b_ref.shape[0]
    u = c_ref[...] * h_ref[...]
    r1 = pltpu.roll(u, 1, axis=0)
    r2 = pltpu.roll(u, 2, axis=0)
    if per_seq is None:
        yb_scr[...] = b_ref[...] * _conv_taps(u, r1, r2, cw_ref)
        up = jnp.where(i > 0, pc_ref[...] * ph_ref[...], 0.0)
        row = lax.broadcasted_iota(jnp.int32, (SUBLANE, D_CONV), 0)
        u8 = u[0:SUBLANE, :]
        r1_8 = jnp.where(row < 1, pltpu.roll(up, 1, axis=0), pltpu.roll(u8, 1, axis=0))
        r2_8 = jnp.where(row < 2, pltpu.roll(up, 2, axis=0), pltpu.roll(u8, 2, axis=0))
        yb_scr[0:SUBLANE, :] = b_ref[0:SUBLANE, :] * _conv_taps(u8, r1_8, r2_8, cw_ref)
        u_ref[...] = u[tm - SUBLANE:tm, :]
    else:
        pre = pc_ref[...]
        t = lax.broadcasted_iota(jnp.int32, (tm, D_CONV), 0) & (per_seq - 1)
        r1 = jnp.where(t == 0, pltpu.roll(pre, tm - 1, axis=0), r1)
        r2 = jnp.where(t < 2, pre, r2)
        yb_scr[...] = b_ref[...] * _conv_taps(u, r1, r2, cw_ref)
        u_ref[...] = u

    yb = yb_scr[...].astype(BF16)
    ob = o_ref[...].astype(BF16)
    tn = D_MODEL // n_chunks
    for c in range(n_chunks):
        cols = slice(c * tn, (c + 1) * tn)
        y_conv = _dot(yb, wc_ref[:, cols])
        y_att = _dot(ob, wa_ref[:, cols])
        out_ref[:, cols] = (jax.nn.sigmoid(gc_refs[c][...]) * y_conv
                            + jax.nn.sigmoid(ga_refs[c][...]) * y_att).astype(BF16)


def _merge(z, o_att, conv_w, wc, wa, tm, per_seq=None, prefix=None):
    m = z.shape[0]
    tn = ATT_OUT_WIDTH
    n_chunks = D_MODEL // tn
    if per_seq is None:
        prev = lambda i, c: (jnp.maximum(i * (tm // SUBLANE) - 1, 0), c)
        pc_spec = pl.BlockSpec((SUBLANE, D_CONV), lambda i: prev(i, COL_C // D_CONV))
        ph_spec = pl.BlockSpec((SUBLANE, D_CONV), lambda i: prev(i, COL_H // D_CONV))
        pc_arg, ph_arg = z, z
        u_rows = SUBLANE
    else:
        pc_spec = pl.BlockSpec((tm, D_CONV), lambda i: (i, 0))
        ph_spec = pl.BlockSpec((SUBLANE, D_CONV), lambda i: (0, 0))
        pc_arg, ph_arg = prefix, prefix
        u_rows = tm
    gate_specs = [pl.BlockSpec((tm, tn), lambda i, base=base, c=c: (i, base // tn + c))
                  for base in (COL_GC, COL_GA) for c in range(n_chunks)]
    return pl.pallas_call(
        functools.partial(_merge_kernel, per_seq=per_seq, n_chunks=n_chunks),
        grid=(m // tm,),
        in_specs=[
            pl.BlockSpec((tm, D_CONV), lambda i: (i, COL_B // D_CONV)),
            pl.BlockSpec((tm, D_CONV), lambda i: (i, COL_C // D_CONV)),
            pl.BlockSpec((tm, D_CONV), lambda i: (i, COL_H // D_CONV)),
            pc_spec,
            ph_spec,
            pl.BlockSpec((3, D_CONV), lambda i: (0, 0)),
            pl.BlockSpec((tm, ATT_OUT_WIDTH), lambda i: (i, 0)),
            *gate_specs,
            pl.BlockSpec((D_CONV, D_MODEL), lambda i: (0, 0)),
            pl.BlockSpec((ATT_OUT_WIDTH, D_MODEL), lambda i: (0, 0)),
        ],
        out_specs=[
            pl.BlockSpec((tm, D_MODEL), lambda i: (i, 0)),
            pl.BlockSpec((u_rows, D_CONV), lambda i: (i, 0)),
        ],
        out_shape=[
            jax.ShapeDtypeStruct((m, D_MODEL), BF16),
            jax.ShapeDtypeStruct((m // tm * u_rows, D_CONV), F32),
        ],
        scratch_shapes=[pltpu.VMEM((tm, D_CONV), F32)],
        compiler_params=_params(("parallel",)),
    )(z, z, z, pc_arg, ph_arg, conv_w, o_att, *([z] * (2 * n_chunks)), wc, wa)


def _outproj_router_kernel(xp_ref, mgp_ref, xs_ref, mgs_ref, wo_ref, g2_ref, wr_hi_ref, wr_lo_ref,
                           br_ref, x1_ref, x1t_ref, route_ref):
    is_sample = pl.program_id(0) == pl.num_programs(0) - 1

    @pl.when(jnp.logical_not(is_sample))
    def _():
        _outproj_router_tile(xp_ref, mgp_ref, wo_ref, g2_ref, wr_hi_ref, wr_lo_ref, br_ref,
                             x1_ref, x1t_ref, route_ref)

    @pl.when(is_sample)
    def _():
        _outproj_router_tile(xs_ref, mgs_ref, wo_ref, g2_ref, wr_hi_ref, wr_lo_ref, br_ref,
                             x1_ref, x1t_ref, route_ref)


def _outproj_router_tile(x_ref, mg_ref, wo_ref, g2_ref, wr_hi_ref, wr_lo_ref, br_ref, x1_ref,
                         x1t_ref, route_ref):
    x1 = x_ref[...] + _dot(mg_ref[...], wo_ref[...])
    x1_ref[...] = x1
    _to_slabs(x1t_ref, x1)
    h = _rms(x1, g2_ref[...])
    h_hi = h.astype(BF16)
    h_lo = (h - h_hi.astype(F32)).astype(BF16)
    logits = (_dot(h_hi, wr_hi_ref[...]) + _dot(h_hi, wr_lo_ref[...]) + _dot(h_lo, wr_hi_ref[...])
              + br_ref[...])
    lane = lax.broadcasted_iota(jnp.int32, logits.shape, 1)
    is_coarse = lane < N_EXPERT_GROUPS
    coarse = jnp.where(is_coarse, logits, NEG)
    cmax = jnp.max(coarse, axis=-1, keepdims=True)
    grp = jnp.min(jnp.where(coarse == cmax, lane, LANE), axis=-1, keepdims=True)
    p_grp = 1.0 / jnp.sum(jnp.where(is_coarse, jnp.exp(coarse - cmax), 0.0), axis=-1, keepdims=True)
    eid = lane - N_EXPERT_GROUPS
    in_grp = (eid >= 0) & (eid < N_EXPERTS) & ((eid >> 3) == grp)
    fine = jnp.where(in_grp, logits, NEG)
    v1 = jnp.max(fine, axis=-1, keepdims=True)
    i1 = jnp.min(jnp.where(fine == v1, lane, LANE), axis=-1, keepdims=True)
    fine2 = jnp.where(lane == i1, NEG, fine)
    v2 = jnp.max(fine2, axis=-1, keepdims=True)
    i2 = jnp.min(jnp.where(fine2 == v2, lane, LANE), axis=-1, keepdims=True)
    e = jnp.exp(v2 - v1)
    gate1 = p_grp / (1.0 + e)
    gate2 = p_grp * e / (1.0 + e)
    route = jnp.where(lane == 0, (i1 - N_EXPERT_GROUPS).astype(F32),
                      jnp.where(lane == 1, (i2 - N_EXPERT_GROUPS).astype(F32),
                                jnp.where(lane == 2, gate1, jnp.where(lane == 3, gate2, 0.0))))
    route_ref[...] = route


def _outproj_router(xp, mg_p, xs, mg_s, wo, norm2_g, wr_hi, wr_lo, br):
    m_p, tm = xp.shape[0], xs.shape[0]
    assert m_p % tm == 0
    n_p = m_p // tm
    prompt_blk = lambda i: (jnp.minimum(i, n_p - 1), 0)
    fixed = lambda i: (0, 0)
    return pl.pallas_call(
        _outproj_router_kernel,
        grid=(n_p + 1,),
        in_specs=[
            pl.BlockSpec((tm, D_MODEL), prompt_blk),
            pl.BlockSpec((tm, D_MODEL), prompt_blk),
            pl.BlockSpec((tm, D_MODEL), fixed),
            pl.BlockSpec((tm, D_MODEL), fixed),
            pl.BlockSpec((D_MODEL, D_MODEL), fixed),
            pl.BlockSpec((1, D_MODEL), fixed),
            pl.BlockSpec((D_MODEL, LANE), fixed),
            pl.BlockSpec((D_MODEL, LANE), fixed),
            pl.BlockSpec((1, LANE), fixed),
        ],
        out_specs=[
            pl.BlockSpec((tm, D_MODEL), lambda i: (i, 0)),
            pl.BlockSpec((tm * TOK_ROWS, LANE), lambda i: (i, 0)),
            pl.BlockSpec((tm, LANE), lambda i: (i, 0)),
        ],
        out_shape=[
            jax.ShapeDtypeStruct((m_p + tm, D_MODEL), F32),
            jax.ShapeDtypeStruct(((m_p + tm) * TOK_ROWS, LANE), F32),
            jax.ShapeDtypeStruct((m_p + tm, LANE), F32),
        ],
        compiler_params=_params(("arbitrary",)),
    )(xp, mg_p, xs, mg_s, wo, norm2_g.reshape(1, D_MODEL), wr_hi, wr_lo, br)


def _to_slabs(dst_ref, x):
    rows = x.shape[0]
    for s in range(TOK_ROWS):
        dst_ref[pl.ds(s, rows, stride=TOK_ROWS), :] = x[:, s * LANE:(s + 1) * LANE]


def _from_slabs(src_ref, rows):
    return jnp.concatenate(
        [src_ref[pl.ds(s, rows, stride=TOK_ROWS), :] for s in range(TOK_ROWS)], axis=1)


def _dispatch_kernel(slots_ref, pad0_ref, padn_ref, nused_ref, x1t_hbm, xst_hbm, buf, zero, lsem,
                     ssem, zsem, *, tm, n_blocks):
    i = pl.program_id(0)
    n_steps = pl.num_programs(0)
    slab = tm * TOK_ROWS

    def load(step):
        return pltpu.make_async_copy(
            x1t_hbm.at[pl.ds(pl.multiple_of(step * slab, slab), slab), :], buf.at[step % 3],
            lsem.at[step % 3])

    def wait_scatter(step):
        for _ in range(TOP_K):
            pltpu.make_async_copy(buf.at[step % 3], xst_hbm.at[pl.ds(0, slab), :],
                                  ssem.at[step % 3]).wait()

    def pad_rows(start):
        def expert(e, carry):
            def row(j, c):
                dst = pl.multiple_of((pad0_ref[e] + j) * TOK_ROWS, TOK_ROWS)
                cp = pltpu.make_async_copy(zero, xst_hbm.at[pl.ds(dst, TOK_ROWS), :], zsem.at[0])
                if start:
                    cp.start()
                else:
                    cp.wait()
                return c
            return lax.fori_loop(0, padn_ref[e], row, carry)
        lax.fori_loop(0, N_EXPERTS, expert, 0)

    @pl.when(i == 0)
    def _():
        load(0).start()
        zero[...] = jnp.zeros(zero.shape, F32)
        pad_rows(start=True)

    @pl.when(i >= 2)
    def _():
        wait_scatter(i - 2)

    @pl.when(i + 1 < n_steps)
    def _():
        load(i + 1).start()

    load(i).wait()
    cur = i % 3

    def body(r, carry):
        src = buf.at[cur, pl.ds(pl.multiple_of(r * TOK_ROWS, TOK_ROWS), TOK_ROWS), :]
        for k in range(TOP_K):
            dst = pl.multiple_of(slots_ref[(i * tm + r) * TOP_K + k] * TOK_ROWS, TOK_ROWS)
            pltpu.make_async_copy(src, xst_hbm.at[pl.ds(dst, TOK_ROWS), :],
                                  ssem.at[cur]).start(priority=k)
        return carry
    lax.fori_loop(0, tm, body, 0, unroll=4)

    @pl.when(i == n_steps - 1)
    def _():
        @pl.when(i >= 1)
        def _():
            wait_scatter(i - 1)
        wait_scatter(i)
        pad_rows(start=False)
        chunk = math.gcd(tm, MOE_TM)
        rows = chunk * TOK_ROWS
        buf[0, 0:rows, :] = jnp.zeros((rows, LANE), F32)
        first = nused_ref[0] * MOE_TM
        n_tail = (n_blocks - nused_ref[0]) * (MOE_TM // chunk)

        def tail(j, carry, start):
            dst = pl.multiple_of((first + j * chunk) * TOK_ROWS, TOK_ROWS)
            cp = pltpu.make_async_copy(buf.at[0, pl.ds(0, rows), :],
                                       xst_hbm.at[pl.ds(dst, rows), :], zsem.at[0])
            if start:
                cp.start()
            else:
                cp.wait()
            return carry
        lax.fori_loop(0, n_tail, functools.partial(tail, start=True), 0)
        lax.fori_loop(0, n_tail, functools.partial(tail, start=False), 0)


def _dispatch(x1t, tok_slots, pad_start, pad_count, n_used, n_blocks, tm):
    n_tok = x1t.shape[0] // TOK_ROWS
    assert n_tok % tm == 0
    grid_spec = pltpu.PrefetchScalarGridSpec(
        num_scalar_prefetch=4,
        grid=(n_tok // tm,),
        in_specs=[pl.BlockSpec(memory_space=pl.ANY)],
        out_specs=pl.BlockSpec(memory_space=pl.ANY),
        scratch_shapes=[
            pltpu.VMEM((3, tm * TOK_ROWS, LANE), F32),
            pltpu.VMEM((TOK_ROWS, LANE), F32),
            pltpu.SemaphoreType.DMA((3,)),
            pltpu.SemaphoreType.DMA((3,)),
            pltpu.SemaphoreType.DMA((1,)),
        ],
    )
    return pl.pallas_call(
        functools.partial(_dispatch_kernel, tm=tm, n_blocks=n_blocks),
        grid_spec=grid_spec,
        out_shape=jax.ShapeDtypeStruct((n_blocks * MOE_TM * TOK_ROWS, LANE), F32),
        compiler_params=_params(("arbitrary",)),
    )(tok_slots, pad_start, pad_count, n_used, x1t)


def _moe_kernel(blk0_ref, nblk_ref, nused_ref, xst_hbm, g2_ref, wg_ref, wu_ref, wd_ref,
                yt_hbm, xbuf, ybuf, gsem, osem):
    e = pl.program_id(0)
    n_used = nused_ref[0]
    slab = MOE_TM * TOK_ROWS

    def in_copy(blk, to_slot):
        return pltpu.make_async_copy(
            xst_hbm.at[pl.ds(pl.multiple_of(blk * slab, slab), slab), :], xbuf.at[to_slot],
            gsem.at[to_slot])

    def out_copy(blk, from_slot):
        return pltpu.make_async_copy(
            ybuf.at[from_slot], yt_hbm.at[pl.ds(pl.multiple_of(blk * slab, slab), slab), :],
            osem.at[from_slot])


    @pl.when(e == 0)
    def _():
        in_copy(0, 0).start()

    def block(b, carry):
        blk = blk0_ref[e] + b
        slot = blk & 1

        @pl.when(blk + 1 < n_used)
        def _():
            in_copy(blk + 1, 1 - slot).start(priority=1)

        in_copy(blk, slot).wait()
        h = _rms(_from_slabs(xbuf.at[slot], MOE_TM), g2_ref[...]).astype(BF16)
        def mm(lhs, w_ref, c):
            return _dot(lhs, w_ref[0, :, c * MXU_N:(c + 1) * MXU_N].astype(BF16))
        a = jnp.concatenate([mm(h, wg_ref, c) for c in range(D_EXPERT // MXU_N)], axis=1)
        u = jnp.concatenate([mm(h, wu_ref, c) for c in range(D_EXPERT // MXU_N)], axis=1)
        hm = (a * jax.nn.sigmoid(a) * u).astype(BF16)
        y = jnp.concatenate([mm(hm, wd_ref, c) for c in range(D_MODEL // MXU_N)], axis=1)

        @pl.when(blk >= 2)
        def _():
            out_copy(blk - 2, slot).wait()

        _to_slabs(ybuf.at[slot], y)
        out_copy(blk, slot).start()
        return carry

    lax.fori_loop(0, nblk_ref[e], block, 0)

    @pl.when(e == pl.num_programs(0) - 1)
    def _():
        @pl.when(n_used >= 2)
        def _():
            out_copy(n_used - 2, (n_used - 2) & 1).wait()
        out_copy(n_used - 1, (n_used - 1) & 1).wait()


def _moe(xst, norm2_g, w_gate_e, w_up_e, w_down_e, blk_start, nblk, n_used):
    w_idx = lambda e, *_: (e, 0, 0)
    slab = MOE_TM * TOK_ROWS
    grid_spec = pltpu.PrefetchScalarGridSpec(
        num_scalar_prefetch=3,
        grid=(N_EXPERTS,),
        in_specs=[
            pl.BlockSpec(memory_space=pl.ANY),
            pl.BlockSpec((1, D_MODEL), lambda e, *_: (0, 0)),
            pl.BlockSpec((1, D_MODEL, D_EXPERT), w_idx),
            pl.BlockSpec((1, D_MODEL, D_EXPERT), w_idx),
            pl.BlockSpec((1, D_EXPERT, D_MODEL), w_idx),
        ],
        out_specs=pl.BlockSpec(memory_space=pl.ANY),
        scratch_shapes=[
            pltpu.VMEM((2, slab, LANE), F32),
            pltpu.VMEM((2, slab, LANE), F32),
            pltpu.SemaphoreType.DMA((2,)),
            pltpu.SemaphoreType.DMA((2,)),
        ],
    )
    return pl.pallas_call(
        _moe_kernel,
        grid_spec=grid_spec,
        out_shape=jax.ShapeDtypeStruct(xst.shape, F32),
        input_output_aliases={3: 0},
        compiler_params=_params(("arbitrary",)),
    )(blk_start, nblk, n_used, xst, norm2_g.reshape(1, D_MODEL), w_gate_e, w_up_e, w_down_e)


def _combine_ple_kernel(slots_ref, x1_ref, route_ref, yt_hbm, p_ref, g3_ref, wpg_ref, wple_ref,
                        out_ref, ybuf, sem, *, tm, row0):
    i = pl.program_id(0)
    n_steps = pl.num_programs(0)
    slot = i & 1
    slab = tm * TOK_ROWS

    def start_gather(step, to_slot):
        base = (row0 + step * tm) * TOP_K

        def body(r, carry):
            for k in range(TOP_K):
                src = slots_ref[base + r * TOP_K + k]
                pltpu.make_async_copy(
                    yt_hbm.at[pl.ds(pl.multiple_of(src * TOK_ROWS, TOK_ROWS), TOK_ROWS), :],
                    ybuf.at[to_slot, k, pl.ds(pl.multiple_of(r * TOK_ROWS, TOK_ROWS), TOK_ROWS), :],
                    sem.at[to_slot]).start(priority=k)
            return carry
        lax.fori_loop(0, tm, body, 0, unroll=4)

    @pl.when(i == 0)
    def _():
        start_gather(0, 0)

    @pl.when(i + 1 < n_steps)
    def _():
        start_gather(i + 1, 1 - slot)

    for k in range(TOP_K):
        pltpu.make_async_copy(yt_hbm.at[pl.ds(0, slab), :], ybuf.at[slot, k], sem.at[slot]).wait()
    route = route_ref[...]
    x2 = x1_ref[...] + (route[:, 2:3] * _from_slabs(ybuf.at[slot, 0], tm)
                        + route[:, 3:4] * _from_slabs(ybuf.at[slot, 1], tm))
    h = _rms(x2, g3_ref[...]).astype(BF16)
    gate = jax.nn.sigmoid(_dot(h, wpg_ref[...]))
    out_ref[...] = x2 + gate * _dot(p_ref[...].astype(BF16), wple_ref[...])


def _combine_ple(x1_all, route_all, y_slots, tok_slots, p, norm3_g, wpg, wple, m, tm, row0):
    blk0 = row0 // tm
    grid_spec = pltpu.PrefetchScalarGridSpec(
        num_scalar_prefetch=1,
        grid=(m // tm,),
        in_specs=[
            pl.BlockSpec((tm, D_MODEL), lambda i, s: (blk0 + i, 0)),
            pl.BlockSpec((tm, LANE), lambda i, s: (blk0 + i, 0)),
            pl.BlockSpec(memory_space=pl.ANY),
            pl.BlockSpec((tm, PLE_DIM), lambda i, s: (i, 0)),
            pl.BlockSpec((1, D_MODEL), lambda i, s: (0, 0)),
            pl.BlockSpec((D_MODEL, D_MODEL), lambda i, s: (0, 0)),
            pl.BlockSpec((PLE_DIM, D_MODEL), lambda i, s: (0, 0)),
        ],
        out_specs=pl.BlockSpec((tm, D_MODEL), lambda i, s: (i, 0)),
        scratch_shapes=[
            pltpu.VMEM((2, TOP_K, tm * TOK_ROWS, LANE), F32),
            pltpu.SemaphoreType.DMA((2,)),
        ],
    )
    return pl.pallas_call(
        functools.partial(_combine_ple_kernel, tm=tm, row0=row0),
        grid_spec=grid_spec,
        out_shape=jax.ShapeDtypeStruct((m, D_MODEL), F32),
        compiler_params=_params(("arbitrary",)),
    )(tok_slots, x1_all, route_all, y_slots, p, norm3_g.reshape(1, D_MODEL), wpg, wple)


def _routing_tables(route_all):
    flat_e = route_all[:, 0:TOP_K].astype(jnp.int32).reshape(-1)
    onehot = (flat_e[:, None] == jnp.arange(N_EXPERTS, dtype=jnp.int32)[None, :]).astype(jnp.int32)
    csum = jnp.cumsum(onehot, axis=0)
    rank = jnp.take_along_axis(csum, flat_e[:, None], axis=1)[:, 0] - 1
    counts = csum[-1]
    nblk = (counts + MOE_TM - 1) // MOE_TM
    blk_end = jnp.cumsum(nblk)
    blk_start = blk_end - nblk
    slot = blk_start[flat_e] * MOE_TM + rank
    n_used = blk_end[-1].reshape(1)
    pad_start = blk_start * MOE_TM + counts
    pad_count = nblk * MOE_TM - counts
    i32 = lambda a: a.astype(jnp.int32)
    return i32(blk_start), i32(nblk), i32(n_used), i32(slot), i32(pad_start), i32(pad_count)


def kernel(x_prompt, x_sample, p_prompt, p_sample, cache_k_g0, cache_v_g0, cache_k_g1, cache_v_g1,
           cache_k_g2, cache_v_g2, state_conv, norm1_g, w_in, q_gain, k_gain, conv_w, w_conv_out,
           w_attn_out, w_o, norm2_g, w_coarse, b_coarse, w_fine, b_fine, w_gate_e, w_up_e, w_down_e,
           norm3_g, w_ple_gate, w_ple):
    seq = x_prompt.shape[1]
    assert x_prompt.shape == (1, seq, D_MODEL) and norm1_g.shape[0] == 1, "one prompt, one layer"
    n_seq, t_new, _ = x_sample.shape
    assert seq % ATT_UNIT == 0 and t_new == SUBLANE
    caches_k = (cache_k_g0, cache_k_g1, cache_k_g2)
    caches_v = (cache_v_g0, cache_v_g1, cache_v_g2)
    for g in range(N_GROUPS):
        assert caches_k[g].shape == (1, n_seq, WINDOWS[g], HEADS_PER_GROUP, HEAD_DIM)
    n_s = n_seq * t_new
    n_all = seq + n_s

    xp = x_prompt[0]
    xs = x_sample.reshape(n_s, D_MODEL)
    wc = w_conv_out[0].astype(BF16)
    wa = w_attn_out[0].astype(BF16)
    wo = w_o[0].astype(BF16)
    wpg = w_ple_gate[0].astype(BF16)
    wple = w_ple[0].astype(BF16)
    wr = jnp.concatenate(
        [w_coarse[0], jnp.transpose(w_fine[0], (1, 0, 2)).reshape(D_MODEL, N_EXPERTS)], axis=1)
    wr = jnp.pad(wr, ((0, 0), (0, LANE - wr.shape[1])))
    wr_hi = wr.astype(BF16)
    wr_lo = (wr - wr_hi.astype(F32)).astype(BF16)
    br = jnp.pad(jnp.concatenate([b_coarse[0], b_fine[0].reshape(-1)]), (0, LANE - 36)).reshape(1, LANE)

    z_p, z_s = _inproj(xp, xs, norm1_g[0], w_in[0], tm=2048, tn=512)

    att_p = _prompt_attn(z_p, q_gain[0], k_gain[0])
    o_p, pks, pvs = att_p[0], att_p[1:4], att_p[4:7]
    ck = [c.reshape(n_seq, WINDOWS[g] * HEADS_PER_GROUP, HEAD_DIM) for g, c in enumerate(caches_k)]
    cv = [c.reshape(n_seq, WINDOWS[g] * HEADS_PER_GROUP, HEAD_DIM) for g, c in enumerate(caches_v)]
    att_s = _sample_attn(z_s, ck, cv, q_gain[0], k_gain[0], n_seq, t_new)
    o_s, sks, svs = att_s[0], att_s[1:4], att_s[4:7]

    mg_p, u_tail = _merge(z_p, o_p, conv_w[0], wc, wa, tm=512)
    prefix = jnp.pad(state_conv[0], ((0, 0), (0, t_new - 2), (0, 0))).reshape(n_s, D_CONV)
    mg_s, u_s = _merge(z_s, o_s, conv_w[0], wc, wa, tm=n_s, per_seq=t_new, prefix=prefix)
    x1_all, x1t_all, route_all = _outproj_router(xp, mg_p, xs, mg_s, wo, norm2_g[0], wr_hi, wr_lo, br)

    n_blocks = n_all * TOP_K // MOE_TM + N_EXPERTS
    blk_start, nblk, n_used, tok_slots, pad_start, pad_count = _routing_tables(route_all)
    xst = _dispatch(x1t_all, tok_slots, pad_start, pad_count, n_used, n_blocks, tm=n_s)
    y_slots = _moe(xst, norm2_g[0], w_gate_e[0], w_up_e[0], w_down_e[0], blk_start, nblk, n_used)

    y_p = _combine_ple(x1_all, route_all, y_slots, tok_slots, p_prompt[0, 0], norm3_g[0], wpg, wple,
                       m=seq, tm=256, row0=0)
    y_s = _combine_ple(x1_all, route_all, y_slots, tok_slots, p_sample[0].reshape(n_s, PLE_DIM),
                       norm3_g[0], wpg, wple, m=n_s, tm=n_s, row0=seq)

    def state(a, n, g):
        return a.reshape(1, n, WINDOWS[g], HEADS_PER_GROUP, HEAD_DIM)

    pk = [state(a, 1, g) for g, a in enumerate(pks)]
    pv = [state(a, 1, g) for g, a in enumerate(pvs)]
    sk = [state(a, n_seq, g) for g, a in enumerate(sks)]
    sv = [state(a, n_seq, g) for g, a in enumerate(svs)]
    pconv = u_tail[-2:].reshape(1, 1, 2, D_CONV)
    sconv = u_s.reshape(n_seq, t_new, D_CONV)[:, t_new - 2:].reshape(1, n_seq, 2, D_CONV)
    return (y_p.reshape(1, seq, D_MODEL), y_s.reshape(n_seq, t_new, D_MODEL),
            pk[0], pv[0], pk[1], pv[1], pk[2], pv[2], pconv,
            sk[0], sv[0], sk[1], sv[1], sk[2], sv[2], sconv)
```

```python
import functools
import math

import jax
import jax.numpy as jnp
from jax import lax
from jax.experimental import pallas as pl
from jax.experimental.pallas import tpu as pltpu

D_MODEL = 2048
HEAD_DIM = 128
HEADS_PER_GROUP = 4
WINDOWS = (128, 512, 2048)
DILATIONS = (1, 4, 16)
N_GROUPS = 3
N_HEADS = N_GROUPS * HEADS_PER_GROUP
ATT_WIDTH = N_HEADS * HEAD_DIM
ATT_OUT_WIDTH = HEADS_PER_GROUP * HEAD_DIM
SCALE = HEAD_DIM ** -0.5
D_CONV = D_MODEL // 2
PLE_DIM = 256
N_EXPERT_GROUPS = 4
EXPERTS_PER_GROUP = 8
N_EXPERTS = N_EXPERT_GROUPS * EXPERTS_PER_GROUP
TOP_K = 2
D_EXPERT = D_MODEL // 4
EPS = 1e-6

COL_B = 0
COL_C = D_CONV
COL_H = 2 * D_CONV
COL_Q = 3 * D_CONV
COL_K = COL_Q + ATT_WIDTH
COL_V = COL_K + ATT_WIDTH
COL_GC = COL_V + ATT_WIDTH
COL_GA = COL_GC + D_MODEL
IN_COLS = COL_GA + D_MODEL

LANE = 128
SUBLANE = 8
NEG = -1e30
ATT_UNIT = max(WINDOWS)
ATT_BLK = 128
ATT_TILE_UNROLL = 16
MXU_N = 256
TOK_ROWS = D_MODEL // LANE
MOE_TM = 288
VMEM_LIMIT = 56 * 1024 * 1024

BF16 = jnp.bfloat16
F32 = jnp.float32


def _params(sem):
    return pltpu.CompilerParams(dimension_semantics=sem, vmem_limit_bytes=VMEM_LIMIT)


def _rms(x, gain):
    return x * lax.rsqrt(jnp.mean(x * x, axis=-1, keepdims=True) + EPS) * gain


def _dot(a, b):
    return jnp.dot(a, b, preferred_element_type=F32)


def _dot_nt(a, b):
    return lax.dot_general(a, b, (((1,), (1,)), ((), ())), preferred_element_type=F32)


def _inproj_kernel(x_ref, xs_ref, g_ref, w_ref, z_ref, zs_ref, h_scr, hs_scr):
    i, j = pl.program_id(0), pl.program_id(1)
    with_sample = i == pl.num_programs(0) - 1

    @pl.when(j == 0)
    def _():
        h_scr[...] = _rms(x_ref[...], g_ref[...]).astype(BF16)

    @pl.when(with_sample & (j == 0))
    def _():
        hs_scr[...] = _rms(xs_ref[...], g_ref[...]).astype(BF16)

    n_chunks = w_ref.shape[1] // MXU_N
    for c in range(n_chunks):
        cols = slice(c * MXU_N, (c + 1) * MXU_N)
        z_ref[:, cols] = _dot(h_scr[...], w_ref[:, cols].astype(BF16))

    @pl.when(with_sample)
    def _():
        for c in range(n_chunks):
            cols = slice(c * MXU_N, (c + 1) * MXU_N)
            zs_ref[:, cols] = _dot(hs_scr[...], w_ref[:, cols].astype(BF16))


def _inproj(x, xs, norm_g, w_in, tm, tn):
    m, m_s = x.shape[0], xs.shape[0]
    n_i = m // tm
    return pl.pallas_call(
        _inproj_kernel,
        grid=(n_i, IN_COLS // tn),
        in_specs=[
            pl.BlockSpec((tm, D_MODEL), lambda i, j: (i, 0), pipeline_mode=pl.Buffered(1)),
            pl.BlockSpec((m_s, D_MODEL), lambda i, j: (0, 0), pipeline_mode=pl.Buffered(1)),
            pl.BlockSpec((1, D_MODEL), lambda i, j: (0, 0)),
            pl.BlockSpec((D_MODEL, tn), lambda i, j: (0, j)),
        ],
        out_specs=[
            pl.BlockSpec((tm, tn), lambda i, j: (i, j)),
            pl.BlockSpec((m_s, tn), lambda i, j: (0, jnp.where(i == n_i - 1, j, 0))),
        ],
        out_shape=[jax.ShapeDtypeStruct((m, IN_COLS), F32),
                   jax.ShapeDtypeStruct((m_s, IN_COLS), F32)],
        scratch_shapes=[pltpu.VMEM((tm, D_MODEL), BF16), pltpu.VMEM((m_s, D_MODEL), BF16)],
        compiler_params=_params(("arbitrary", "arbitrary")),
    )(x, xs, norm_g.reshape(1, D_MODEL), w_in)


def _prompt_attn_kernel(*refs):
    (q0, q1, q2, kc0, kc1, kc2, vc0, vc1, vc2, kp0, kp1, kp2, vp0, vp1, vp2, qg_ref, kg_ref,
     o_ref, pk0, pk1, pk2, pv0, pv1, pv2, kext, vext, acc_s, m_s, l_s) = refs
    q_refs, kc_refs, vc_refs = (q0, q1, q2), (kc0, kc1, kc2), (vc0, vc1, vc2)
    kp_refs, vp_refs = (kp0, kp1, kp2), (vp0, vp1, vp2)
    pk_refs, pv_refs = (pk0, pk1, pk2), (pv0, pv1, pv2)
    n = pl.program_id(1)
    last = pl.num_programs(1) - 1

    qi = lax.broadcasted_iota(jnp.int32, (ATT_BLK, 2 * ATT_BLK), 0) + ATT_BLK
    ki = lax.broadcasted_iota(jnp.int32, (ATT_BLK, 2 * ATT_BLK), 1)
    dist = qi - ki
    band = (dist >= 0) & (dist <= ATT_BLK)

    for g in range(N_GROUPS):
        win, dil = WINDOWS[g], DILATIONS[g]
        q_ref = q_refs[g]
        qgain = qg_ref[g, 0]
        kgain = kg_ref[g, 0]
        kext[0:win, :] = _rms(kp_refs[g][...], kgain)
        kext[win:win + ATT_UNIT, :] = _rms(kc_refs[g][...], kgain)
        vext[0:win, :] = vp_refs[g][...]
        vext[win:win + ATT_UNIT, :] = vc_refs[g][...]

        @pl.when(n == last)
        def _():
            pk_refs[g][...] = kext[ATT_UNIT:ATT_UNIT + win, :]
            pv_refs[g][...] = vext[ATT_UNIT:ATT_UNIT + win, :]

        def tile(t, carry, g=g, win=win, dil=dil, q_ref=q_ref, qgain=qgain):
            u = t // dil
            r = t % dil
            off = u * win + r
            if dil == 1:
                off = pl.multiple_of(off, ATT_BLK)
                rows_q = pl.ds(off, ATT_BLK)
                rows_k = pl.ds(off, 2 * ATT_BLK)
            else:
                rows_q = pl.ds(off, ATT_BLK, stride=dil)
                rows_k = pl.ds(off, 2 * ATT_BLK, stride=dil)
            q = _rms(q_ref[rows_q, :], qgain).astype(BF16)
            k = kext[rows_k, :].astype(BF16)
            v = vext[rows_k, :].astype(BF16)
            s = _dot_nt(q, k) * SCALE
            k_min = jnp.where((n > 0) | (u > 0), 0, ATT_BLK)
            s = jnp.where(band & (ki >= k_min), s, NEG)
            m_t = jnp.max(s, axis=-1, keepdims=True)
            p = jnp.exp(s - m_t)
            l_t = jnp.sum(p, axis=-1, keepdims=True)
            m_s[g, rows_q, :] = jnp.broadcast_to(m_t, (ATT_BLK, LANE))
            l_s[g, rows_q, :] = jnp.broadcast_to(l_t, (ATT_BLK, LANE))
            acc_s[g, rows_q, :] = _dot(p.astype(BF16), v)
            return carry

        lax.fori_loop(0, ATT_UNIT // ATT_BLK, tile, 0, unroll=ATT_TILE_UNROLL)

    def finish(c, carry):
        rows = pl.ds(pl.multiple_of(c * ATT_BLK, ATT_BLK), ATT_BLK)
        m = [m_s[g, rows, :] for g in range(N_GROUPS)]
        m_all = jnp.maximum(jnp.maximum(m[0], m[1]), m[2])
        num = jnp.zeros((ATT_BLK, LANE), F32)
        den = jnp.zeros((ATT_BLK, LANE), F32)
        for g in range(N_GROUPS):
            w = jnp.exp(m[g] - m_all)
            num = num + w * acc_s[g, rows, :]
            den = den + w * l_s[g, rows, :]
        o_ref[rows, :] = num / den
        return carry

    lax.fori_loop(0, ATT_UNIT // ATT_BLK, finish, 0, unroll=2)


def _prompt_attn(z, q_gain, k_gain):
    s_len = z.shape[0]
    n_units = s_len // ATT_UNIT
    in_specs, args = [], []

    def col(base, g):
        return lambda hh, n: (n, base // LANE + g * HEADS_PER_GROUP + hh)

    for g in range(N_GROUPS):
        in_specs.append(pl.BlockSpec((ATT_UNIT, LANE), col(COL_Q, g)))
    for base in (COL_K, COL_V):
        for g in range(N_GROUPS):
            in_specs.append(pl.BlockSpec((ATT_UNIT, LANE), col(base, g)))
    for base in (COL_K, COL_V):
        for g in range(N_GROUPS):
            per = ATT_UNIT // WINDOWS[g]
            in_specs.append(pl.BlockSpec(
                (WINDOWS[g], LANE),
                lambda hh, n, base=base, g=g, per=per: (
                    jnp.maximum(n * per - 1, 0), base // LANE + g * HEADS_PER_GROUP + hh)))
    args = [z] * 15
    gain_spec = pl.BlockSpec((N_GROUPS, 1, 1, LANE), lambda hh, n: (0, hh, 0, 0))
    in_specs += [gain_spec, gain_spec]
    args += [q_gain.reshape(N_GROUPS, HEADS_PER_GROUP, 1, LANE),
             k_gain.reshape(N_GROUPS, HEADS_PER_GROUP, 1, LANE)]

    out_specs = [pl.BlockSpec((ATT_UNIT, LANE), lambda hh, n: (n, hh))]
    out_shape = [jax.ShapeDtypeStruct((s_len, ATT_OUT_WIDTH), F32)]
    for _ in range(2):
        for g in range(N_GROUPS):
            out_specs.append(pl.BlockSpec((WINDOWS[g], LANE), lambda hh, n: (0, hh)))
            out_shape.append(jax.ShapeDtypeStruct((WINDOWS[g], ATT_OUT_WIDTH), F32))

    return pl.pallas_call(
        _prompt_attn_kernel,
        grid=(HEADS_PER_GROUP, n_units),
        in_specs=in_specs,
        out_specs=out_specs,
        out_shape=out_shape,
        scratch_shapes=[
            pltpu.VMEM((2 * ATT_UNIT, LANE), F32),
            pltpu.VMEM((2 * ATT_UNIT, LANE), F32),
            pltpu.VMEM((N_GROUPS, ATT_UNIT, LANE), F32),
            pltpu.VMEM((N_GROUPS, ATT_UNIT, LANE), F32),
            pltpu.VMEM((N_GROUPS, ATT_UNIT, LANE), F32),
        ],
        compiler_params=_params(("parallel", "arbitrary")),
    )(*args)


def _sample_attn_kernel(*refs):
    (q0, q1, q2, kn0, kn1, kn2, vn0, vn1, vn2, ck0, ck1, ck2, cv0, cv1, cv2, qg_ref, kg_ref,
     o_ref, sk0, sk1, sk2, sv0, sv1, sv2) = refs
    q_refs, kn_refs, vn_refs = (q0, q1, q2), (kn0, kn1, kn2), (vn0, vn1, vn2)
    ck_refs, cv_refs = (ck0, ck1, ck2), (cv0, cv1, cv2)
    sk_refs, sv_refs = (sk0, sk1, sk2), (sv0, sv1, sv2)
    t_new = q0.shape[0]
    nh = HEADS_PER_GROUP
    n_q = nh * t_new
    t_bits = t_new.bit_length() - 1

    def heads(ref):
        return [ref[:, h * LANE:(h + 1) * LANE] for h in range(nh)]

    parts = []
    for g in range(N_GROUPS):
        win, dil = WINDOWS[g], DILATIONS[g]
        q_all = jnp.concatenate(
            [_rms(x, qg_ref[g, h]) for h, x in enumerate(heads(q_refs[g]))], axis=0).astype(BF16)
        k_new = [_rms(x, kg_ref[g, h]) for h, x in enumerate(heads(kn_refs[g]))]
        v_new = heads(vn_refs[g])
        k_new_all = jnp.concatenate(k_new, axis=0).astype(BF16)
        v_new_all = jnp.concatenate(v_new, axis=0).astype(BF16)
        ck = ck_refs[g][0]
        cv = cv_refs[g][0]
        old = (win - t_new) * nh
        sk_refs[g][0, 0:old, :] = ck[t_new * nh:, :]
        sv_refs[g][0, 0:old, :] = cv[t_new * nh:, :]
        for h in range(nh):
            sk_refs[g][0, pl.ds(old + h, t_new, stride=nh), :] = k_new[h]
            sv_refs[g][0, pl.ds(old + h, t_new, stride=nh), :] = v_new[h]

        s_c = _dot_nt(q_all, ck.astype(BF16)) * SCALE
        s_n = _dot_nt(q_all, k_new_all) * SCALE
        row = lax.broadcasted_iota(jnp.int32, (n_q, win * nh), 0)
        col = lax.broadcasted_iota(jnp.int32, (n_q, win * nh), 1)
        i_q, pos = row & (t_new - 1), col >> 2
        mask_c = (((col & (nh - 1)) == (row >> t_bits)) & (pos >= i_q)
                  & (((pos - i_q) & (dil - 1)) == 0))
        row_n = lax.broadcasted_iota(jnp.int32, (n_q, n_q), 0)
        col_n = lax.broadcasted_iota(jnp.int32, (n_q, n_q), 1)
        d_n = (row_n & (t_new - 1)) - (col_n & (t_new - 1))
        mask_n = ((row_n >> t_bits) == (col_n >> t_bits)) & (d_n >= 0) & ((d_n & (dil - 1)) == 0)
        s_c = jnp.where(mask_c, s_c, NEG)
        s_n = jnp.where(mask_n, s_n, NEG)
        m = jnp.maximum(jnp.max(s_c, axis=-1, keepdims=True), jnp.max(s_n, axis=-1, keepdims=True))
        p_c = jnp.exp(s_c - m)
        p_n = jnp.exp(s_n - m)
        l = jnp.sum(p_c, axis=-1, keepdims=True) + jnp.sum(p_n, axis=-1, keepdims=True)
        acc = _dot(p_c.astype(BF16), cv.astype(BF16)) + _dot(p_n.astype(BF16), v_new_all)
        parts.append((m, l, acc))

    m_all = jnp.maximum(jnp.maximum(parts[0][0], parts[1][0]), parts[2][0])
    num = jnp.zeros((n_q, LANE), F32)
    den = jnp.zeros((n_q, 1), F32)
    for m, l, acc in parts:
        w = jnp.exp(m - m_all)
        num = num + w * acc
        den = den + w * l
    o_all = num / den
    for h in range(nh):
        o_ref[:, h * LANE:(h + 1) * LANE] = o_all[h * t_new:(h + 1) * t_new, :]


def _sample_attn(z, caches_k, caches_v, q_gain, k_gain, n_seq, t_new):
    assert HEADS_PER_GROUP == 4 and t_new & (t_new - 1) == 0
    in_specs, args = [], []
    for base in (COL_Q, COL_K, COL_V):
        for g in range(N_GROUPS):
            in_specs.append(pl.BlockSpec(
                (t_new, ATT_OUT_WIDTH), lambda b, base=base, g=g: (b, base // ATT_OUT_WIDTH + g)))
            args.append(z)
    for caches in (caches_k, caches_v):
        for g in range(N_GROUPS):
            in_specs.append(pl.BlockSpec((1, WINDOWS[g] * HEADS_PER_GROUP, LANE), lambda b: (b, 0, 0)))
            args.append(caches[g])
    gain_spec = pl.BlockSpec((N_GROUPS, HEADS_PER_GROUP, 1, LANE), lambda b: (0, 0, 0, 0))
    in_specs += [gain_spec, gain_spec]
    args += [q_gain.reshape(N_GROUPS, HEADS_PER_GROUP, 1, LANE),
             k_gain.reshape(N_GROUPS, HEADS_PER_GROUP, 1, LANE)]

    out_specs = [pl.BlockSpec((t_new, ATT_OUT_WIDTH), lambda b: (b, 0))]
    out_shape = [jax.ShapeDtypeStruct((n_seq * t_new, ATT_OUT_WIDTH), F32)]
    for _ in range(2):
        for g in range(N_GROUPS):
            rows = WINDOWS[g] * HEADS_PER_GROUP
            out_specs.append(pl.BlockSpec((1, rows, LANE), lambda b: (b, 0, 0)))
            out_shape.append(jax.ShapeDtypeStruct((n_seq, rows, LANE), F32))

    return pl.pallas_call(
        _sample_attn_kernel,
        grid=(n_seq,),
        in_specs=in_specs,
        out_specs=out_specs,
        out_shape=out_shape,
        compiler_params=_params(("parallel",)),
    )(*args)


def _conv_taps(u, r1, r2, cw_ref):
    return cw_ref[2:3, :] * u + cw_ref[1:2, :] * r1 + cw_ref[0:1, :] * r2


def _merge_kernel(*refs, per_seq, n_chunks):
    b_ref, c_ref, h_ref, pc_ref, ph_ref, cw_ref, o_ref = refs[:7]
    gc_refs = refs[7:7 + n_chunks]
    ga_refs = refs[7 + n_chunks:7 + 2 * n_chunks]
    wc_ref, wa_ref, out_ref, u_ref, yb_scr = refs[7 + 2 * n_chunks:]
    i = pl.program_id(0)
    tm = b_ref.shape[0]
    u = c_ref[...] * h_ref[...]
    r1 = pltpu.roll(u, 1, axis=0)
    r2 = pltpu.roll(u, 2, axis=0)
    if per_seq is None:
        yb_scr[...] = b_ref[...] * _conv_taps(u, r1, r2, cw_ref)
        up = jnp.where(i > 0, pc_ref[...] * ph_ref[...], 0.0)
        row = lax.broadcasted_iota(jnp.int32, (SUBLANE, D_CONV), 0)
        u8 = u[0:SUBLANE, :]
        r1_8 = jnp.where(row < 1, pltpu.roll(up, 1, axis=0), pltpu.roll(u8, 1, axis=0))
        r2_8 = jnp.where(row < 2, pltpu.roll(up, 2, axis=0), pltpu.roll(u8, 2, axis=0))
        yb_scr[0:SUBLANE, :] = b_ref[0:SUBLANE, :] * _conv_taps(u8, r1_8, r2_8, cw_ref)
        u_ref[...] = u[tm - SUBLANE:tm, :]
    else:
        pre = pc_ref[...]
        t = lax.broadcasted_iota(jnp.int32, (tm, D_CONV), 0) & (per_seq - 1)
        r1 = jnp.where(t == 0, pltpu.roll(pre, tm - 1, axis=0), r1)
        r2 = jnp.where(t < 2, pre, r2)
        yb_scr[...] = b_ref[...] * _conv_taps(u, r1, r2, cw_ref)
        u_ref[...] = u

    yb = yb_scr[...].astype(BF16)
    ob = o_ref[...].astype(BF16)
    tn = D_MODEL // n_chunks
    for c in range(n_chunks):
        cols = slice(c * tn, (c + 1) * tn)
        y_conv = _dot(yb, wc_ref[:, cols])
        y_att = _dot(ob, wa_ref[:, cols])
        out_ref[:, cols] = (jax.nn.sigmoid(gc_refs[c][...]) * y_conv
                            + jax.nn.sigmoid(ga_refs[c][...]) * y_att).astype(BF16)


def _merge(z, o_att, conv_w, wc, wa, tm, per_seq=None, prefix=None):
    m = z.shape[0]
    tn = ATT_OUT_WIDTH
    n_chunks = D_MODEL // tn
    if per_seq is None:
        prev = lambda i, c: (jnp.maximum(i * (tm // SUBLANE) - 1, 0), c)
        pc_spec = pl.BlockSpec((SUBLANE, D_CONV), lambda i: prev(i, COL_C // D_CONV))
        ph_spec = pl.BlockSpec((SUBLANE, D_CONV), lambda i: prev(i, COL_H // D_CONV))
        pc_arg, ph_arg = z, z
        u_rows = SUBLANE
    else:
        pc_spec = pl.BlockSpec((tm, D_CONV), lambda i: (i, 0))
        ph_spec = pl.BlockSpec((SUBLANE, D_CONV), lambda i: (0, 0))
        pc_arg, ph_arg = prefix, prefix
        u_rows = tm
    gate_specs = [pl.BlockSpec((tm, tn), lambda i, base=base, c=c: (i, base // tn + c))
                  for base in (COL_GC, COL_GA) for c in range(n_chunks)]
    return pl.pallas_call(
        functools.partial(_merge_kernel, per_seq=per_seq, n_chunks=n_chunks),
        grid=(m // tm,),
        in_specs=[
            pl.BlockSpec((tm, D_CONV), lambda i: (i, COL_B // D_CONV)),
            pl.BlockSpec((tm, D_CONV), lambda i: (i, COL_C // D_CONV)),
            pl.BlockSpec((tm, D_CONV), lambda i: (i, COL_H // D_CONV)),
            pc_spec,
            ph_spec,
            pl.BlockSpec((3, D_CONV), lambda i: (0, 0)),
            pl.BlockSpec((tm, ATT_OUT_WIDTH), lambda i: (i, 0)),
            *gate_specs,
            pl.BlockSpec((D_CONV, D_MODEL), lambda i: (0, 0)),
            pl.BlockSpec((ATT_OUT_WIDTH, D_MODEL), lambda i: (0, 0)),
        ],
        out_specs=[
            pl.BlockSpec((tm, D_MODEL), lambda i: (i, 0)),
            pl.BlockSpec((u_rows, D_CONV), lambda i: (i, 0)),
        ],
        out_shape=[
            jax.ShapeDtypeStruct((m, D_MODEL), BF16),
            jax.ShapeDtypeStruct((m // tm * u_rows, D_CONV), F32),
        ],
        scratch_shapes=[pltpu.VMEM((tm, D_CONV), F32)],
        compiler_params=_params(("parallel",)),
    )(z, z, z, pc_arg, ph_arg, conv_w, o_att, *([z] * (2 * n_chunks)), wc, wa)


def _outproj_router_kernel(xp_ref, mgp_ref, xs_ref, mgs_ref, wo_ref, g2_ref, wr_hi_ref, wr_lo_ref,
                           br_ref, x1_ref, x1t_ref, route_ref):
    is_sample = pl.program_id(0) == pl.num_programs(0) - 1

    @pl.when(jnp.logical_not(is_sample))
    def _():
        _outproj_router_tile(xp_ref, mgp_ref, wo_ref, g2_ref, wr_hi_ref, wr_lo_ref, br_ref,
                             x1_ref, x1t_ref, route_ref)

    @pl.when(is_sample)
    def _():
        _outproj_router_tile(xs_ref, mgs_ref, wo_ref, g2_ref, wr_hi_ref, wr_lo_ref, br_ref,
                             x1_ref, x1t_ref, route_ref)


def _outproj_router_tile(x_ref, mg_ref, wo_ref, g2_ref, wr_hi_ref, wr_lo_ref, br_ref, x1_ref,
                         x1t_ref, route_ref):
    x1 = x_ref[...] + _dot(mg_ref[...], wo_ref[...])
    x1_ref[...] = x1
    _to_slabs(x1t_ref, x1)
    h = _rms(x1, g2_ref[...])
    h_hi = h.astype(BF16)
    h_lo = (h - h_hi.astype(F32)).astype(BF16)
    logits = (_dot(h_hi, wr_hi_ref[...]) + _dot(h_hi, wr_lo_ref[...]) + _dot(h_lo, wr_hi_ref[...])
              + br_ref[...])
    lane = lax.broadcasted_iota(jnp.int32, logits.shape, 1)
    is_coarse = lane < N_EXPERT_GROUPS
    coarse = jnp.where(is_coarse, logits, NEG)
    cmax = jnp.max(coarse, axis=-1, keepdims=True)
    grp = jnp.min(jnp.where(coarse == cmax, lane, LANE), axis=-1, keepdims=True)
    p_grp = 1.0 / jnp.sum(jnp.where(is_coarse, jnp.exp(coarse - cmax), 0.0), axis=-1, keepdims=True)
    eid = lane - N_EXPERT_GROUPS
    in_grp = (eid >= 0) & (eid < N_EXPERTS) & ((eid >> 3) == grp)
    fine = jnp.where(in_grp, logits, NEG)
    v1 = jnp.max(fine, axis=-1, keepdims=True)
    i1 = jnp.min(jnp.where(fine == v1, lane, LANE), axis=-1, keepdims=True)
    fine2 = jnp.where(lane == i1, NEG, fine)
    v2 = jnp.max(fine2, axis=-1, keepdims=True)
    i2 = jnp.min(jnp.where(fine2 == v2, lane, LANE), axis=-1, keepdims=True)
    e = jnp.exp(v2 - v1)
    gate1 = p_grp / (1.0 + e)
    gate2 = p_grp * e / (1.0 + e)
    route = jnp.where(lane == 0, (i1 - N_EXPERT_GROUPS).astype(F32),
                      jnp.where(lane == 1, (i2 - N_EXPERT_GROUPS).astype(F32),
                                jnp.where(lane == 2, gate1, jnp.where(lane == 3, gate2, 0.0))))
    route_ref[...] = route


def _outproj_router(xp, mg_p, xs, mg_s, wo, norm2_g, wr_hi, wr_lo, br):
    m_p, tm = xp.shape[0], xs.shape[0]
    assert m_p % tm == 0
    n_p = m_p // tm
    prompt_blk = lambda i: (jnp.minimum(i, n_p - 1), 0)
    fixed = lambda i: (0, 0)
    return pl.pallas_call(
        _outproj_router_kernel,
        grid=(n_p + 1,),
        in_specs=[
            pl.BlockSpec((tm, D_MODEL), prompt_blk),
            pl.BlockSpec((tm, D_MODEL), prompt_blk),
            pl.BlockSpec((tm, D_MODEL), fixed),
            pl.BlockSpec((tm, D_MODEL), fixed),
            pl.BlockSpec((D_MODEL, D_MODEL), fixed),
            pl.BlockSpec((1, D_MODEL), fixed),
            pl.BlockSpec((D_MODEL, LANE), fixed),
            pl.BlockSpec((D_MODEL, LANE), fixed),
            pl.BlockSpec((1, LANE), fixed),
        ],
        out_specs=[
            pl.BlockSpec((tm, D_MODEL), lambda i: (i, 0)),
            pl.BlockSpec((tm * TOK_ROWS, LANE), lambda i: (i, 0)),
            pl.BlockSpec((tm, LANE), lambda i: (i, 0)),
        ],
        out_shape=[
            jax.ShapeDtypeStruct((m_p + tm, D_MODEL), F32),
            jax.ShapeDtypeStruct(((m_p + tm) * TOK_ROWS, LANE), F32),
            jax.ShapeDtypeStruct((m_p + tm, LANE), F32),
        ],
        compiler_params=_params(("arbitrary",)),
    )(xp, mg_p, xs, mg_s, wo, norm2_g.reshape(1, D_MODEL), wr_hi, wr_lo, br)


def _to_slabs(dst_ref, x):
    rows = x.shape[0]
    for s in range(TOK_ROWS):
        dst_ref[pl.ds(s, rows, stride=TOK_ROWS), :] = x[:, s * LANE:(s + 1) * LANE]


def _from_slabs(src_ref, rows):
    return jnp.concatenate(
        [src_ref[pl.ds(s, rows, stride=TOK_ROWS), :] for s in range(TOK_ROWS)], axis=1)


def _dispatch_kernel(slots_ref, pad0_ref, padn_ref, nused_ref, x1t_hbm, xst_hbm, buf, zero, lsem,
                     ssem, zsem, *, tm, n_blocks):
    i = pl.program_id(0)
    n_steps = pl.num_programs(0)
    slab = tm * TOK_ROWS

    def load(step):
        return pltpu.make_async_copy(
            x1t_hbm.at[pl.ds(pl.multiple_of(step * slab, slab), slab), :], buf.at[step % 3],
            lsem.at[step % 3])

    def wait_scatter(step):
        for _ in range(TOP_K):
            pltpu.make_async_copy(buf.at[step % 3], xst_hbm.at[pl.ds(0, slab), :],
                                  ssem.at[step % 3]).wait()

    def pad_rows(start):
        def expert(e, carry):
            def row(j, c):
                dst = pl.multiple_of((pad0_ref[e] + j) * TOK_ROWS, TOK_ROWS)
                cp = pltpu.make_async_copy(zero, xst_hbm.at[pl.ds(dst, TOK_ROWS), :], zsem.at[0])
                if start:
                    cp.start()
                else:
                    cp.wait()
                return c
            return lax.fori_loop(0, padn_ref[e], row, carry)
        lax.fori_loop(0, N_EXPERTS, expert, 0)

    @pl.when(i == 0)
    def _():
        load(0).start()
        zero[...] = jnp.zeros(zero.shape, F32)
        pad_rows(start=True)

    @pl.when(i >= 2)
    def _():
        wait_scatter(i - 2)

    @pl.when(i + 1 < n_steps)
    def _():
        load(i + 1).start()

    load(i).wait()
    cur = i % 3

    def body(r, carry):
        src = buf.at[cur, pl.ds(pl.multiple_of(r * TOK_ROWS, TOK_ROWS), TOK_ROWS), :]
        for k in range(TOP_K):
            dst = pl.multiple_of(slots_ref[(i * tm + r) * TOP_K + k] * TOK_ROWS, TOK_ROWS)
            pltpu.make_async_copy(src, xst_hbm.at[pl.ds(dst, TOK_ROWS), :],
                                  ssem.at[cur]).start(priority=k)
        return carry
    lax.fori_loop(0, tm, body, 0, unroll=4)

    @pl.when(i == n_steps - 1)
    def _():
        @pl.when(i >= 1)
        def _():
            wait_scatter(i - 1)
        wait_scatter(i)
        pad_rows(start=False)
        chunk = math.gcd(tm, MOE_TM)
        rows = chunk * TOK_ROWS
        buf[0, 0:rows, :] = jnp.zeros((rows, LANE), F32)
        first = nused_ref[0] * MOE_TM
        n_tail = (n_blocks - nused_ref[0]) * (MOE_TM // chunk)

        def tail(j, carry, start):
            dst = pl.multiple_of((first + j * chunk) * TOK_ROWS, TOK_ROWS)
            cp = pltpu.make_async_copy(buf.at[0, pl.ds(0, rows), :],
                                       xst_hbm.at[pl.ds(dst, rows), :], zsem.at[0])
            if start:
                cp.start()
            else:
                cp.wait()
            return carry
        lax.fori_loop(0, n_tail, functools.partial(tail, start=True), 0)
        lax.fori_loop(0, n_tail, functools.partial(tail, start=False), 0)


def _dispatch(x1t, tok_slots, pad_start, pad_count, n_used, n_blocks, tm):
    n_tok = x1t.shape[0] // TOK_ROWS
    assert n_tok % tm == 0
    grid_spec = pltpu.PrefetchScalarGridSpec(
        num_scalar_prefetch=4,
        grid=(n_tok // tm,),
        in_specs=[pl.BlockSpec(memory_space=pl.ANY)],
        out_specs=pl.BlockSpec(memory_space=pl.ANY),
        scratch_shapes=[
            pltpu.VMEM((3, tm * TOK_ROWS, LANE), F32),
            pltpu.VMEM((TOK_ROWS, LANE), F32),
            pltpu.SemaphoreType.DMA((3,)),
            pltpu.SemaphoreType.DMA((3,)),
            pltpu.SemaphoreType.DMA((1,)),
        ],
    )
    return pl.pallas_call(
        functools.partial(_dispatch_kernel, tm=tm, n_blocks=n_blocks),
        grid_spec=grid_spec,
        out_shape=jax.ShapeDtypeStruct((n_blocks * MOE_TM * TOK_ROWS, LANE), F32),
        compiler_params=_params(("arbitrary",)),
    )(tok_slots, pad_start, pad_count, n_used, x1t)


def _moe_kernel(blk0_ref, nblk_ref, nused_ref, xst_hbm, g2_ref, wg_ref, wu_ref, wd_ref,
                yt_hbm, xbuf, ybuf, gsem, osem, wg_b, wu_b, wd_b):
    e = pl.program_id(0)
    n_used = nused_ref[0]
    slab = MOE_TM * TOK_ROWS

    def in_copy(blk, to_slot):
        return pltpu.make_async_copy(
            xst_hbm.at[pl.ds(pl.multiple_of(blk * slab, slab), slab), :], xbuf.at[to_slot],
            gsem.at[to_slot])

    def out_copy(blk, from_slot):
        return pltpu.make_async_copy(
            ybuf.at[from_slot], yt_hbm.at[pl.ds(pl.multiple_of(blk * slab, slab), slab), :],
            osem.at[from_slot])

    wg_b[...] = wg_ref[0].astype(BF16)
    wu_b[...] = wu_ref[0].astype(BF16)
    wd_b[...] = wd_ref[0].astype(BF16)

    @pl.when(e == 0)
    def _():
        in_copy(0, 0).start()

    def block(b, carry):
        blk = blk0_ref[e] + b
        slot = blk & 1

        @pl.when(blk + 1 < n_used)
        def _():
            in_copy(blk + 1, 1 - slot).start(priority=1)

        in_copy(blk, slot).wait()
        h = _rms(_from_slabs(xbuf.at[slot], MOE_TM), g2_ref[...]).astype(BF16)
        a = _dot(h, wg_b[...])
        u = _dot(h, wu_b[...])
        hm = (a * jax.nn.sigmoid(a) * u).astype(BF16)
        y = _dot(hm, wd_b[...])

        @pl.when(blk >= 2)
        def _():
            out_copy(blk - 2, slot).wait()

        _to_slabs(ybuf.at[slot], y)
        out_copy(blk, slot).start()
        return carry

    lax.fori_loop(0, nblk_ref[e], block, 0)

    @pl.when(e == pl.num_programs(0) - 1)
    def _():
        @pl.when(n_used >= 2)
        def _():
            out_copy(n_used - 2, (n_used - 2) & 1).wait()
        out_copy(n_used - 1, (n_used - 1) & 1).wait()


def _moe(xst, norm2_g, w_gate_e, w_up_e, w_down_e, blk_start, nblk, n_used):
    w_idx = lambda e, *_: (e, 0, 0)
    slab = MOE_TM * TOK_ROWS
    grid_spec = pltpu.PrefetchScalarGridSpec(
        num_scalar_prefetch=3,
        grid=(N_EXPERTS,),
        in_specs=[
            pl.BlockSpec(memory_space=pl.ANY),
            pl.BlockSpec((1, D_MODEL), lambda e, *_: (0, 0)),
            pl.BlockSpec((1, D_MODEL, D_EXPERT), w_idx),
            pl.BlockSpec((1, D_MODEL, D_EXPERT), w_idx),
            pl.BlockSpec((1, D_EXPERT, D_MODEL), w_idx),
        ],
        out_specs=pl.BlockSpec(memory_space=pl.ANY),
        scratch_shapes=[
            pltpu.VMEM((2, slab, LANE), F32),
            pltpu.VMEM((2, slab, LANE), F32),
            pltpu.SemaphoreType.DMA((2,)),
            pltpu.SemaphoreType.DMA((2,)),
            pltpu.VMEM((D_MODEL, D_EXPERT), BF16),
            pltpu.VMEM((D_MODEL, D_EXPERT), BF16),
            pltpu.VMEM((D_EXPERT, D_MODEL), BF16),
        ],
    )
    return pl.pallas_call(
        _moe_kernel,
        grid_spec=grid_spec,
        out_shape=jax.ShapeDtypeStruct(xst.shape, F32),
        input_output_aliases={3: 0},
        compiler_params=_params(("arbitrary",)),
    )(blk_start, nblk, n_used, xst, norm2_g.reshape(1, D_MODEL), w_gate_e, w_up_e, w_down_e)


def _combine_ple_kernel(slots_ref, x1_ref, route_ref, yt_hbm, p_ref, g3_ref, wpg_ref, wple_ref,
                        out_ref, ybuf, sem, *, tm, row0):
    i = pl.program_id(0)
    n_steps = pl.num_programs(0)
    slot = i & 1
    slab = tm * TOK_ROWS

    def start_gather(step, to_slot):
        base = (row0 + step * tm) * TOP_K

        def body(r, carry):
            for k in range(TOP_K):
                src = slots_ref[base + r * TOP_K + k]
                pltpu.make_async_copy(
                    yt_hbm.at[pl.ds(pl.multiple_of(src * TOK_ROWS, TOK_ROWS), TOK_ROWS), :],
                    ybuf.at[to_slot, k, pl.ds(pl.multiple_of(r * TOK_ROWS, TOK_ROWS), TOK_ROWS), :],
                    sem.at[to_slot]).start(priority=k)
            return carry
        lax.fori_loop(0, tm, body, 0, unroll=4)

    @pl.when(i == 0)
    def _():
        start_gather(0, 0)

    @pl.when(i + 1 < n_steps)
    def _():
        start_gather(i + 1, 1 - slot)

    for k in range(TOP_K):
        pltpu.make_async_copy(yt_hbm.at[pl.ds(0, slab), :], ybuf.at[slot, k], sem.at[slot]).wait()
    route = route_ref[...]
    x2 = x1_ref[...] + (route[:, 2:3] * _from_slabs(ybuf.at[slot, 0], tm)
                        + route[:, 3:4] * _from_slabs(ybuf.at[slot, 1], tm))
    h = _rms(x2, g3_ref[...]).astype(BF16)
    gate = jax.nn.sigmoid(_dot(h, wpg_ref[...]))
    out_ref[...] = x2 + gate * _dot(p_ref[...].astype(BF16), wple_ref[...])


def _combine_ple(x1_all, route_all, y_slots, tok_slots, p, norm3_g, wpg, wple, m, tm, row0):
    blk0 = row0 // tm
    grid_spec = pltpu.PrefetchScalarGridSpec(
        num_scalar_prefetch=1,
        grid=(m // tm,),
        in_specs=[
            pl.BlockSpec((tm, D_MODEL), lambda i, s: (blk0 + i, 0)),
            pl.BlockSpec((tm, LANE), lambda i, s: (blk0 + i, 0)),
            pl.BlockSpec(memory_space=pl.ANY),
            pl.BlockSpec((tm, PLE_DIM), lambda i, s: (i, 0)),
            pl.BlockSpec((1, D_MODEL), lambda i, s: (0, 0)),
            pl.BlockSpec((D_MODEL, D_MODEL), lambda i, s: (0, 0)),
            pl.BlockSpec((PLE_DIM, D_MODEL), lambda i, s: (0, 0)),
        ],
        out_specs=pl.BlockSpec((tm, D_MODEL), lambda i, s: (i, 0)),
        scratch_shapes=[
            pltpu.VMEM((2, TOP_K, tm * TOK_ROWS, LANE), F32),
            pltpu.SemaphoreType.DMA((2,)),
        ],
    )
    return pl.pallas_call(
        functools.partial(_combine_ple_kernel, tm=tm, row0=row0),
        grid_spec=grid_spec,
        out_shape=jax.ShapeDtypeStruct((m, D_MODEL), F32),
        compiler_params=_params(("arbitrary",)),
    )(tok_slots, x1_all, route_all, y_slots, p, norm3_g.reshape(1, D_MODEL), wpg, wple)


def _routing_tables(route_all):
    flat_e = route_all[:, 0:TOP_K].astype(jnp.int32).reshape(-1)
    onehot = (flat_e[:, None] == jnp.arange(N_EXPERTS, dtype=jnp.int32)[None, :]).astype(jnp.int32)
    csum = jnp.cumsum(onehot, axis=0)
    rank = jnp.take_along_axis(csum, flat_e[:, None], axis=1)[:, 0] - 1
    counts = csum[-1]
    nblk = (counts + MOE_TM - 1) // MOE_TM
    blk_end = jnp.cumsum(nblk)
    blk_start = blk_end - nblk
    slot = blk_start[flat_e] * MOE_TM + rank
    n_used = blk_end[-1].reshape(1)
    pad_start = blk_start * MOE_TM + counts
    pad_count = nblk * MOE_TM - counts
    i32 = lambda a: a.astype(jnp.int32)
    return i32(blk_start), i32(nblk), i32(n_used), i32(slot), i32(pad_start), i32(pad_count)


def kernel(x_prompt, x_sample, p_prompt, p_sample, cache_k_g0, cache_v_g0, cache_k_g1, cache_v_g1,
           cache_k_g2, cache_v_g2, state_conv, norm1_g, w_in, q_gain, k_gain, conv_w, w_conv_out,
           w_attn_out, w_o, norm2_g, w_coarse, b_coarse, w_fine, b_fine, w_gate_e, w_up_e, w_down_e,
           norm3_g, w_ple_gate, w_ple):
    seq = x_prompt.shape[1]
    assert x_prompt.shape == (1, seq, D_MODEL) and norm1_g.shape[0] == 1, "one prompt, one layer"
    n_seq, t_new, _ = x_sample.shape
    assert seq % ATT_UNIT == 0 and t_new == SUBLANE
    caches_k = (cache_k_g0, cache_k_g1, cache_k_g2)
    caches_v = (cache_v_g0, cache_v_g1, cache_v_g2)
    for g in range(N_GROUPS):
        assert caches_k[g].shape == (1, n_seq, WINDOWS[g], HEADS_PER_GROUP, HEAD_DIM)
    n_s = n_seq * t_new
    n_all = seq + n_s

    xp = x_prompt[0]
    xs = x_sample.reshape(n_s, D_MODEL)
    wc = w_conv_out[0].astype(BF16)
    wa = w_attn_out[0].astype(BF16)
    wo = w_o[0].astype(BF16)
    wpg = w_ple_gate[0].astype(BF16)
    wple = w_ple[0].astype(BF16)
    wr = jnp.concatenate(
        [w_coarse[0], jnp.transpose(w_fine[0], (1, 0, 2)).reshape(D_MODEL, N_EXPERTS)], axis=1)
    wr = jnp.pad(wr, ((0, 0), (0, LANE - wr.shape[1])))
    wr_hi = wr.astype(BF16)
    wr_lo = (wr - wr_hi.astype(F32)).astype(BF16)
    br = jnp.pad(jnp.concatenate([b_coarse[0], b_fine[0].reshape(-1)]), (0, LANE - 36)).reshape(1, LANE)

    z_p, z_s = _inproj(xp, xs, norm1_g[0], w_in[0], tm=2048, tn=512)

    att_p = _prompt_attn(z_p, q_gain[0], k_gain[0])
    o_p, pks, pvs = att_p[0], att_p[1:4], att_p[4:7]
    ck = [c.reshape(n_seq, WINDOWS[g] * HEADS_PER_GROUP, HEAD_DIM) for g, c in enumerate(caches_k)]
    cv = [c.reshape(n_seq, WINDOWS[g] * HEADS_PER_GROUP, HEAD_DIM) for g, c in enumerate(caches_v)]
    att_s = _sample_attn(z_s, ck, cv, q_gain[0], k_gain[0], n_seq, t_new)
    o_s, sks, svs = att_s[0], att_s[1:4], att_s[4:7]

    mg_p, u_tail = _merge(z_p, o_p, conv_w[0], wc, wa, tm=512)
    prefix = jnp.pad(state_conv[0], ((0, 0), (0, t_new - 2), (0, 0))).reshape(n_s, D_CONV)
    mg_s, u_s = _merge(z_s, o_s, conv_w[0], wc, wa, tm=n_s, per_seq=t_new, prefix=prefix)
    x1_all, x1t_all, route_all = _outproj_router(xp, mg_p, xs, mg_s, wo, norm2_g[0], wr_hi, wr_lo, br)

    n_blocks = n_all * TOP_K // MOE_TM + N_EXPERTS
    blk_start, nblk, n_used, tok_slots, pad_start, pad_count = _routing_tables(route_all)
    xst = _dispatch(x1t_all, tok_slots, pad_start, pad_count, n_used, n_blocks, tm=n_s)
    y_slots = _moe(xst, norm2_g[0], w_gate_e[0], w_up_e[0], w_down_e[0], blk_start, nblk, n_used)

    y_p = _combine_ple(x1_all, route_all, y_slots, tok_slots, p_prompt[0, 0], norm3_g[0], wpg, wple,
                       m=seq, tm=256, row0=0)
    y_s = _combine_ple(x1_all, route_all, y_slots, tok_slots, p_sample[0].reshape(n_s, PLE_DIM),
                       norm3_g[0], wpg, wple, m=n_s, tm=n_s, row0=seq)

    def state(a, n, g):
        return a.reshape(1, n, WINDOWS[g], HEADS_PER_GROUP, HEAD_DIM)

    pk = [state(a, 1, g) for g, a in enumerate(pks)]
    pv = [state(a, 1, g) for g, a in enumerate(pvs)]
    sk = [state(a, n_seq, g) for g, a in enumerate(sks)]
    sv = [state(a, n_seq, g) for g, a in enumerate(svs)]
    pconv = u_tail[-2:].reshape(1, 1, 2, D_CONV)
    sconv = u_s.reshape(n_seq, t_new, D_CONV)[:, t_new - 2:].reshape(1, n_seq, 2, D_CONV)
    return (y_p.reshape(1, seq, D_MODEL), y_s.reshape(n_seq, t_new, D_MODEL),
            pk[0], pv[0], pk[1], pv[1], pk[2], pv[2], pconv,
            sk[0], sv[0], sk[1], sv[1], sk[2], sv[2], sconv)
```

```python
import functools
import math

import jax
import jax.numpy as jnp
from jax import lax
from jax.experimental import pallas as pl
from jax.experimental.pallas import tpu as pltpu

D_MODEL = 2048
HEAD_DIM = 128
HEADS_PER_GROUP = 4
WINDOWS = (128, 512, 2048)
DILATIONS = (1, 4, 16)
N_GROUPS = 3
N_HEADS = N_GROUPS * HEADS_PER_GROUP
ATT_WIDTH = N_HEADS * HEAD_DIM
ATT_OUT_WIDTH = HEADS_PER_GROUP * HEAD_DIM
SCALE = HEAD_DIM ** -0.5
D_CONV = D_MODEL // 2
PLE_DIM = 256
N_EXPERT_GROUPS = 4
EXPERTS_PER_GROUP = 8
N_EXPERTS = N_EXPERT_GROUPS * EXPERTS_PER_GROUP
TOP_K = 2
D_EXPERT = D_MODEL // 4
EPS = 1e-6

COL_B = 0
COL_C = D_CONV
COL_H = 2 * D_CONV
COL_Q = 3 * D_CONV
COL_K = COL_Q + ATT_WIDTH
COL_V = COL_K + ATT_WIDTH
COL_GC = COL_V + ATT_WIDTH
COL_GA = COL_GC + D_MODEL
IN_COLS = COL_GA + D_MODEL

LANE = 128
SUBLANE = 8
NEG = -1e30
ATT_UNIT = max(WINDOWS)
ATT_BLK = 128
ATT_TILE_UNROLL = 16
MXU_N = 256
TOK_ROWS = D_MODEL // LANE
MOE_TM = 288
VMEM_LIMIT = 56 * 1024 * 1024

BF16 = jnp.bfloat16
F32 = jnp.float32


def _params(sem):
    return pltpu.CompilerParams(dimension_semantics=sem, vmem_limit_bytes=VMEM_LIMIT)


def _rms(x, gain):
    return x * lax.rsqrt(jnp.mean(x * x, axis=-1, keepdims=True) + EPS) * gain


def _dot(a, b):
    return jnp.dot(a, b, preferred_element_type=F32)


def _dot_nt(a, b):
    return lax.dot_general(a, b, (((1,), (1,)), ((), ())), preferred_element_type=F32)


def _inproj_kernel(x_ref, xs_ref, g_ref, w_ref, z_ref, zs_ref, h_scr, hs_scr):
    i, j = pl.program_id(0), pl.program_id(1)
    with_sample = i == pl.num_programs(0) - 1

    @pl.when(j == 0)
    def _():
        h_scr[...] = _rms(x_ref[...], g_ref[...]).astype(BF16)

    @pl.when(with_sample & (j == 0))
    def _():
        hs_scr[...] = _rms(xs_ref[...], g_ref[...]).astype(BF16)

    def project(sample_too):
        for c in range(w_ref.shape[1] // MXU_N):
            cols = slice(c * MXU_N, (c + 1) * MXU_N)
            w_c = w_ref[:, cols].astype(BF16)
            z_ref[:, cols] = _dot(h_scr[...], w_c)
            if sample_too:
                zs_ref[:, cols] = _dot(hs_scr[...], w_c)

    @pl.when(jnp.logical_not(with_sample))
    def _():
        project(False)

    @pl.when(with_sample)
    def _():
        project(True)


def _inproj(x, xs, norm_g, w_in, tm, tn):
    m, m_s = x.shape[0], xs.shape[0]
    n_i = m // tm
    return pl.pallas_call(
        _inproj_kernel,
        grid=(n_i, IN_COLS // tn),
        in_specs=[
            pl.BlockSpec((tm, D_MODEL), lambda i, j: (i, 0), pipeline_mode=pl.Buffered(1)),
            pl.BlockSpec((m_s, D_MODEL), lambda i, j: (0, 0), pipeline_mode=pl.Buffered(1)),
            pl.BlockSpec((1, D_MODEL), lambda i, j: (0, 0)),
            pl.BlockSpec((D_MODEL, tn), lambda i, j: (0, j)),
        ],
        out_specs=[
            pl.BlockSpec((tm, tn), lambda i, j: (i, j)),
            pl.BlockSpec((m_s, tn), lambda i, j: (0, jnp.where(i == n_i - 1, j, 0))),
        ],
        out_shape=[jax.ShapeDtypeStruct((m, IN_COLS), F32),
                   jax.ShapeDtypeStruct((m_s, IN_COLS), F32)],
        scratch_shapes=[pltpu.VMEM((tm, D_MODEL), BF16), pltpu.VMEM((m_s, D_MODEL), BF16)],
        compiler_params=_params(("arbitrary", "arbitrary")),
    )(x, xs, norm_g.reshape(1, D_MODEL), w_in)


def _prompt_attn_kernel(*refs):
    (q0, q1, q2, kc0, kc1, kc2, vc0, vc1, vc2, kp0, kp1, kp2, vp0, vp1, vp2, qg_ref, kg_ref,
     o_ref, pk0, pk1, pk2, pv0, pv1, pv2, kext, vext, acc_s, m_s, l_s) = refs
    q_refs, kc_refs, vc_refs = (q0, q1, q2), (kc0, kc1, kc2), (vc0, vc1, vc2)
    kp_refs, vp_refs = (kp0, kp1, kp2), (vp0, vp1, vp2)
    pk_refs, pv_refs = (pk0, pk1, pk2), (pv0, pv1, pv2)
    n = pl.program_id(1)
    last = pl.num_programs(1) - 1

    qi = lax.broadcasted_iota(jnp.int32, (ATT_BLK, 2 * ATT_BLK), 0) + ATT_BLK
    ki = lax.broadcasted_iota(jnp.int32, (ATT_BLK, 2 * ATT_BLK), 1)
    dist = qi - ki
    band = (dist >= 0) & (dist <= ATT_BLK)

    for g in range(N_GROUPS):
        win, dil = WINDOWS[g], DILATIONS[g]
        q_ref = q_refs[g]
        qgain = qg_ref[g, 0]
        kgain = kg_ref[g, 0]
        kext[0:win, :] = _rms(kp_refs[g][...], kgain)
        kext[win:win + ATT_UNIT, :] = _rms(kc_refs[g][...], kgain)
        vext[0:win, :] = vp_refs[g][...]
        vext[win:win + ATT_UNIT, :] = vc_refs[g][...]

        @pl.when(n == last)
        def _():
            pk_refs[g][...] = kext[ATT_UNIT:ATT_UNIT + win, :]
            pv_refs[g][...] = vext[ATT_UNIT:ATT_UNIT + win, :]

        def tile(t, carry, g=g, win=win, dil=dil, q_ref=q_ref, qgain=qgain):
            u = t // dil
            r = t % dil
            off = u * win + r
            if dil == 1:
                off = pl.multiple_of(off, ATT_BLK)
                rows_q = pl.ds(off, ATT_BLK)
                rows_k = pl.ds(off, 2 * ATT_BLK)
            else:
                rows_q = pl.ds(off, ATT_BLK, stride=dil)
                rows_k = pl.ds(off, 2 * ATT_BLK, stride=dil)
            q = _rms(q_ref[rows_q, :], qgain).astype(BF16)
            k = kext[rows_k, :].astype(BF16)
            v = vext[rows_k, :].astype(BF16)
            s = _dot_nt(q, k) * SCALE
            k_min = jnp.where((n > 0) | (u > 0), 0, ATT_BLK)
            s = jnp.where(band & (ki >= k_min), s, NEG)
            m_t = jnp.max(s, axis=-1, keepdims=True)
            p = jnp.exp(s - m_t)
            l_t = jnp.sum(p, axis=-1, keepdims=True)
            m_s[g, rows_q, :] = jnp.broadcast_to(m_t, (ATT_BLK, LANE))
            l_s[g, rows_q, :] = jnp.broadcast_to(l_t, (ATT_BLK, LANE))
            acc_s[g, rows_q, :] = _dot(p.astype(BF16), v)
            return carry

        lax.fori_loop(0, ATT_UNIT // ATT_BLK, tile, 0, unroll=ATT_TILE_UNROLL)

    def finish(c, carry):
        rows = pl.ds(pl.multiple_of(c * ATT_BLK, ATT_BLK), ATT_BLK)
        m = [m_s[g, rows, :] for g in range(N_GROUPS)]
        m_all = jnp.maximum(jnp.maximum(m[0], m[1]), m[2])
        num = jnp.zeros((ATT_BLK, LANE), F32)
        den = jnp.zeros((ATT_BLK, LANE), F32)
        for g in range(N_GROUPS):
            w = jnp.exp(m[g] - m_all)
            num = num + w * acc_s[g, rows, :]
            den = den + w * l_s[g, rows, :]
        o_ref[rows, :] = num / den
        return carry

    lax.fori_loop(0, ATT_UNIT // ATT_BLK, finish, 0, unroll=2)


def _prompt_attn(z, q_gain, k_gain):
    s_len = z.shape[0]
    n_units = s_len // ATT_UNIT
    in_specs, args = [], []

    def col(base, g):
        return lambda hh, n: (n, base // LANE + g * HEADS_PER_GROUP + hh)

    for g in range(N_GROUPS):
        in_specs.append(pl.BlockSpec((ATT_UNIT, LANE), col(COL_Q, g)))
    for base in (COL_K, COL_V):
        for g in range(N_GROUPS):
            in_specs.append(pl.BlockSpec((ATT_UNIT, LANE), col(base, g)))
    for base in (COL_K, COL_V):
        for g in range(N_GROUPS):
            per = ATT_UNIT // WINDOWS[g]
            in_specs.append(pl.BlockSpec(
                (WINDOWS[g], LANE),
                lambda hh, n, base=base, g=g, per=per: (
                    jnp.maximum(n * per - 1, 0), base // LANE + g * HEADS_PER_GROUP + hh)))
    args = [z] * 15
    gain_spec = pl.BlockSpec((N_GROUPS, 1, 1, LANE), lambda hh, n: (0, hh, 0, 0))
    in_specs += [gain_spec, gain_spec]
    args += [q_gain.reshape(N_GROUPS, HEADS_PER_GROUP, 1, LANE),
             k_gain.reshape(N_GROUPS, HEADS_PER_GROUP, 1, LANE)]

    out_specs = [pl.BlockSpec((ATT_UNIT, LANE), lambda hh, n: (n, hh))]
    out_shape = [jax.ShapeDtypeStruct((s_len, ATT_OUT_WIDTH), F32)]
    for _ in range(2):
        for g in range(N_GROUPS):
            out_specs.append(pl.BlockSpec((WINDOWS[g], LANE), lambda hh, n: (0, hh)))
            out_shape.append(jax.ShapeDtypeStruct((WINDOWS[g], ATT_OUT_WIDTH), F32))

    return pl.pallas_call(
        _prompt_attn_kernel,
        grid=(HEADS_PER_GROUP, n_units),
        in_specs=in_specs,
        out_specs=out_specs,
        out_shape=out_shape,
        scratch_shapes=[
            pltpu.VMEM((2 * ATT_UNIT, LANE), F32),
            pltpu.VMEM((2 * ATT_UNIT, LANE), F32),
            pltpu.VMEM((N_GROUPS, ATT_UNIT, LANE), F32),
            pltpu.VMEM((N_GROUPS, ATT_UNIT, LANE), F32),
            pltpu.VMEM((N_GROUPS, ATT_UNIT, LANE), F32),
        ],
        compiler_params=_params(("parallel", "arbitrary")),
    )(*args)


def _sample_attn_kernel(*refs):
    (q0, q1, q2, kn0, kn1, kn2, vn0, vn1, vn2, ck0, ck1, ck2, cv0, cv1, cv2, qg_ref, kg_ref,
     o_ref, sk0, sk1, sk2, sv0, sv1, sv2) = refs
    q_refs, kn_refs, vn_refs = (q0, q1, q2), (kn0, kn1, kn2), (vn0, vn1, vn2)
    ck_refs, cv_refs = (ck0, ck1, ck2), (cv0, cv1, cv2)
    sk_refs, sv_refs = (sk0, sk1, sk2), (sv0, sv1, sv2)
    t_new = q0.shape[0]
    nh = HEADS_PER_GROUP
    n_q = nh * t_new
    t_bits = t_new.bit_length() - 1

    def heads(ref):
        return [ref[:, h * LANE:(h + 1) * LANE] for h in range(nh)]

    parts = []
    for g in range(N_GROUPS):
        win, dil = WINDOWS[g], DILATIONS[g]
        q_all = jnp.concatenate(
            [_rms(x, qg_ref[g, h]) for h, x in enumerate(heads(q_refs[g]))], axis=0).astype(BF16)
        k_new = [_rms(x, kg_ref[g, h]) for h, x in enumerate(heads(kn_refs[g]))]
        v_new = heads(vn_refs[g])
        k_new_all = jnp.concatenate(k_new, axis=0).astype(BF16)
        v_new_all = jnp.concatenate(v_new, axis=0).astype(BF16)
        ck = ck_refs[g][0]
        cv = cv_refs[g][0]
        old = (win - t_new) * nh
        sk_refs[g][0, 0:old, :] = ck[t_new * nh:, :]
        sv_refs[g][0, 0:old, :] = cv[t_new * nh:, :]
        for h in range(nh):
            sk_refs[g][0, pl.ds(old + h, t_new, stride=nh), :] = k_new[h]
            sv_refs[g][0, pl.ds(old + h, t_new, stride=nh), :] = v_new[h]

        s_c = _dot_nt(q_all, ck.astype(BF16)) * SCALE
        s_n = _dot_nt(q_all, k_new_all) * SCALE
        row = lax.broadcasted_iota(jnp.int32, (n_q, win * nh), 0)
        col = lax.broadcasted_iota(jnp.int32, (n_q, win * nh), 1)
        i_q, pos = row & (t_new - 1), col >> 2
        mask_c = (((col & (nh - 1)) == (row >> t_bits)) & (pos >= i_q)
                  & (((pos - i_q) & (dil - 1)) == 0))
        row_n = lax.broadcasted_iota(jnp.int32, (n_q, n_q), 0)
        col_n = lax.broadcasted_iota(jnp.int32, (n_q, n_q), 1)
        d_n = (row_n & (t_new - 1)) - (col_n & (t_new - 1))
        mask_n = ((row_n >> t_bits) == (col_n >> t_bits)) & (d_n >= 0) & ((d_n & (dil - 1)) == 0)
        s_c = jnp.where(mask_c, s_c, NEG)
        s_n = jnp.where(mask_n, s_n, NEG)
        m = jnp.maximum(jnp.max(s_c, axis=-1, keepdims=True), jnp.max(s_n, axis=-1, keepdims=True))
        p_c = jnp.exp(s_c - m)
        p_n = jnp.exp(s_n - m)
        l = jnp.sum(p_c, axis=-1, keepdims=True) + jnp.sum(p_n, axis=-1, keepdims=True)
        acc = _dot(p_c.astype(BF16), cv.astype(BF16)) + _dot(p_n.astype(BF16), v_new_all)
        parts.append((m, l, acc))

    m_all = jnp.maximum(jnp.maximum(parts[0][0], parts[1][0]), parts[2][0])
    num = jnp.zeros((n_q, LANE), F32)
    den = jnp.zeros((n_q, 1), F32)
    for m, l, acc in parts:
        w = jnp.exp(m - m_all)
        num = num + w * acc
        den = den + w * l
    o_all = num / den
    for h in range(nh):
        o_ref[:, h * LANE:(h + 1) * LANE] = o_all[h * t_new:(h + 1) * t_new, :]


def _sample_attn(z, caches_k, caches_v, q_gain, k_gain, n_seq, t_new):
    assert HEADS_PER_GROUP == 4 and t_new & (t_new - 1) == 0
    in_specs, args = [], []
    for base in (COL_Q, COL_K, COL_V):
        for g in range(N_GROUPS):
            in_specs.append(pl.BlockSpec(
                (t_new, ATT_OUT_WIDTH), lambda b, base=base, g=g: (b, base // ATT_OUT_WIDTH + g)))
            args.append(z)
    for caches in (caches_k, caches_v):
        for g in range(N_GROUPS):
            in_specs.append(pl.BlockSpec((1, WINDOWS[g] * HEADS_PER_GROUP, LANE), lambda b: (b, 0, 0)))
            args.append(caches[g])
    gain_spec = pl.BlockSpec((N_GROUPS, HEADS_PER_GROUP, 1, LANE), lambda b: (0, 0, 0, 0))
    in_specs += [gain_spec, gain_spec]
    args += [q_gain.reshape(N_GROUPS, HEADS_PER_GROUP, 1, LANE),
             k_gain.reshape(N_GROUPS, HEADS_PER_GROUP, 1, LANE)]

    out_specs = [pl.BlockSpec((t_new, ATT_OUT_WIDTH), lambda b: (b, 0))]
    out_shape = [jax.ShapeDtypeStruct((n_seq * t_new, ATT_OUT_WIDTH), F32)]
    for _ in range(2):
        for g in range(N_GROUPS):
            rows = WINDOWS[g] * HEADS_PER_GROUP
            out_specs.append(pl.BlockSpec((1, rows, LANE), lambda b: (b, 0, 0)))
            out_shape.append(jax.ShapeDtypeStruct((n_seq, rows, LANE), F32))

    return pl.pallas_call(
        _sample_attn_kernel,
        grid=(n_seq,),
        in_specs=in_specs,
        out_specs=out_specs,
        out_shape=out_shape,
        compiler_params=_params(("parallel",)),
    )(*args)


def _conv_taps(u, r1, r2, cw_ref):
    return cw_ref[2:3, :] * u + cw_ref[1:2, :] * r1 + cw_ref[0:1, :] * r2


def _merge_kernel(*refs, per_seq, n_chunks):
    b_ref, c_ref, h_ref, pc_ref, ph_ref, cw_ref, o_ref = refs[:7]
    gc_refs = refs[7:7 + n_chunks]
    ga_refs = refs[7 + n_chunks:7 + 2 * n_chunks]
    wc_ref, wa_ref, out_ref, u_ref, yb_scr = refs[7 + 2 * n_chunks:]
    i = pl.program_id(0)
    tm = b_ref.shape[0]
    u = c_ref[...] * h_ref[...]
    r1 = pltpu.roll(u, 1, axis=0)
    r2 = pltpu.roll(u, 2, axis=0)
    if per_seq is None:
        yb_scr[...] = b_ref[...] * _conv_taps(u, r1, r2, cw_ref)
        up = jnp.where(i > 0, pc_ref[...] * ph_ref[...], 0.0)
        row = lax.broadcasted_iota(jnp.int32, (SUBLANE, D_CONV), 0)
        u8 = u[0:SUBLANE, :]
        r1_8 = jnp.where(row < 1, pltpu.roll(up, 1, axis=0), pltpu.roll(u8, 1, axis=0))
        r2_8 = jnp.where(row < 2, pltpu.roll(up, 2, axis=0), pltpu.roll(u8, 2, axis=0))
        yb_scr[0:SUBLANE, :] = b_ref[0:SUBLANE, :] * _conv_taps(u8, r1_8, r2_8, cw_ref)
        u_ref[...] = u[tm - SUBLANE:tm, :]
    else:
        pre = pc_ref[...]
        t = lax.broadcasted_iota(jnp.int32, (tm, D_CONV), 0) & (per_seq - 1)
        r1 = jnp.where(t == 0, pltpu.roll(pre, tm - 1, axis=0), r1)
        r2 = jnp.where(t < 2, pre, r2)
        yb_scr[...] = b_ref[...] * _conv_taps(u, r1, r2, cw_ref)
        u_ref[...] = u

    yb = yb_scr[...].astype(BF16)
    ob = o_ref[...].astype(BF16)
    tn = D_MODEL // n_chunks
    for c in range(n_chunks):
        cols = slice(c * tn, (c + 1) * tn)
        y_conv = _dot(yb, wc_ref[:, cols])
        y_att = _dot(ob, wa_ref[:, cols])
        out_ref[:, cols] = (jax.nn.sigmoid(gc_refs[c][...]) * y_conv
                            + jax.nn.sigmoid(ga_refs[c][...]) * y_att).astype(BF16)


def _merge(z, o_att, conv_w, wc, wa, tm, per_seq=None, prefix=None):
    m = z.shape[0]
    tn = ATT_OUT_WIDTH
    n_chunks = D_MODEL // tn
    if per_seq is None:
        prev = lambda i, c: (jnp.maximum(i * (tm // SUBLANE) - 1, 0), c)
        pc_spec = pl.BlockSpec((SUBLANE, D_CONV), lambda i: prev(i, COL_C // D_CONV))
        ph_spec = pl.BlockSpec((SUBLANE, D_CONV), lambda i: prev(i, COL_H // D_CONV))
        pc_arg, ph_arg = z, z
        u_rows = SUBLANE
    else:
        pc_spec = pl.BlockSpec((tm, D_CONV), lambda i: (i, 0))
        ph_spec = pl.BlockSpec((SUBLANE, D_CONV), lambda i: (0, 0))
        pc_arg, ph_arg = prefix, prefix
        u_rows = tm
    gate_specs = [pl.BlockSpec((tm, tn), lambda i, base=base, c=c: (i, base // tn + c))
                  for base in (COL_GC, COL_GA) for c in range(n_chunks)]
    return pl.pallas_call(
        functools.partial(_merge_kernel, per_seq=per_seq, n_chunks=n_chunks),
        grid=(m // tm,),
        in_specs=[
            pl.BlockSpec((tm, D_CONV), lambda i: (i, COL_B // D_CONV)),
            pl.BlockSpec((tm, D_CONV), lambda i: (i, COL_C // D_CONV)),
            pl.BlockSpec((tm, D_CONV), lambda i: (i, COL_H // D_CONV)),
            pc_spec,
            ph_spec,
            pl.BlockSpec((3, D_CONV), lambda i: (0, 0)),
            pl.BlockSpec((tm, ATT_OUT_WIDTH), lambda i: (i, 0)),
            *gate_specs,
            pl.BlockSpec((D_CONV, D_MODEL), lambda i: (0, 0)),
            pl.BlockSpec((ATT_OUT_WIDTH, D_MODEL), lambda i: (0, 0)),
        ],
        out_specs=[
            pl.BlockSpec((tm, D_MODEL), lambda i: (i, 0)),
            pl.BlockSpec((u_rows, D_CONV), lambda i: (i, 0)),
        ],
        out_shape=[
            jax.ShapeDtypeStruct((m, D_MODEL), BF16),
            jax.ShapeDtypeStruct((m // tm * u_rows, D_CONV), F32),
        ],
        scratch_shapes=[pltpu.VMEM((tm, D_CONV), F32)],
        compiler_params=_params(("parallel",)),
    )(z, z, z, pc_arg, ph_arg, conv_w, o_att, *([z] * (2 * n_chunks)), wc, wa)


def _outproj_router_kernel(xp_ref, mgp_ref, xs_ref, mgs_ref, wo_ref, g2_ref, wr_hi_ref, wr_lo_ref,
                           br_ref, x1_ref, x1t_ref, route_ref):
    is_sample = pl.program_id(0) == pl.num_programs(0) - 1

    @pl.when(jnp.logical_not(is_sample))
    def _():
        _outproj_router_tile(xp_ref, mgp_ref, wo_ref, g2_ref, wr_hi_ref, wr_lo_ref, br_ref,
                             x1_ref, x1t_ref, route_ref)

    @pl.when(is_sample)
    def _():
        _outproj_router_tile(xs_ref, mgs_ref, wo_ref, g2_ref, wr_hi_ref, wr_lo_ref, br_ref,
                             x1_ref, x1t_ref, route_ref)


def _outproj_router_tile(x_ref, mg_ref, wo_ref, g2_ref, wr_hi_ref, wr_lo_ref, br_ref, x1_ref,
                         x1t_ref, route_ref):
    x1 = x_ref[...] + _dot(mg_ref[...], wo_ref[...])
    x1_ref[...] = x1
    _to_slabs(x1t_ref, x1)
    h = _rms(x1, g2_ref[...])
    h_hi = h.astype(BF16)
    h_lo = (h - h_hi.astype(F32)).astype(BF16)
    logits = (_dot(h_hi, wr_hi_ref[...]) + _dot(h_hi, wr_lo_ref[...]) + _dot(h_lo, wr_hi_ref[...])
              + br_ref[...])
    lane = lax.broadcasted_iota(jnp.int32, logits.shape, 1)
    is_coarse = lane < N_EXPERT_GROUPS
    coarse = jnp.where(is_coarse, logits, NEG)
    cmax = jnp.max(coarse, axis=-1, keepdims=True)
    grp = jnp.min(jnp.where(coarse == cmax, lane, LANE), axis=-1, keepdims=True)
    p_grp = 1.0 / jnp.sum(jnp.where(is_coarse, jnp.exp(coarse - cmax), 0.0), axis=-1, keepdims=True)
    eid = lane - N_EXPERT_GROUPS
    in_grp = (eid >= 0) & (eid < N_EXPERTS) & ((eid >> 3) == grp)
    fine = jnp.where(in_grp, logits, NEG)
    v1 = jnp.max(fine, axis=-1, keepdims=True)
    i1 = jnp.min(jnp.where(fine == v1, lane, LANE), axis=-1, keepdims=True)
    fine2 = jnp.where(lane == i1, NEG, fine)
    v2 = jnp.max(fine2, axis=-1, keepdims=True)
    i2 = jnp.min(jnp.where(fine2 == v2, lane, LANE), axis=-1, keepdims=True)
    e = jnp.exp(v2 - v1)
    gate1 = p_grp / (1.0 + e)
    gate2 = p_grp * e / (1.0 + e)
    route = jnp.where(lane == 0, (i1 - N_EXPERT_GROUPS).astype(F32),
                      jnp.where(lane == 1, (i2 - N_EXPERT_GROUPS).astype(F32),
                                jnp.where(lane == 2, gate1, jnp.where(lane == 3, gate2, 0.0))))
    route_ref[...] = route


def _outproj_router(xp, mg_p, xs, mg_s, wo, norm2_g, wr_hi, wr_lo, br):
    m_p, tm = xp.shape[0], xs.shape[0]
    assert m_p % tm == 0
    n_p = m_p // tm
    prompt_blk = lambda i: (jnp.minimum(i, n_p - 1), 0)
    fixed = lambda i: (0, 0)
    return pl.pallas_call(
        _outproj_router_kernel,
        grid=(n_p + 1,),
        in_specs=[
            pl.BlockSpec((tm, D_MODEL), prompt_blk),
            pl.BlockSpec((tm, D_MODEL), prompt_blk),
            pl.BlockSpec((tm, D_MODEL), fixed),
            pl.BlockSpec((tm, D_MODEL), fixed),
            pl.BlockSpec((D_MODEL, D_MODEL), fixed),
            pl.BlockSpec((1, D_MODEL), fixed),
            pl.BlockSpec((D_MODEL, LANE), fixed),
            pl.BlockSpec((D_MODEL, LANE), fixed),
            pl.BlockSpec((1, LANE), fixed),
        ],
        out_specs=[
            pl.BlockSpec((tm, D_MODEL), lambda i: (i, 0)),
            pl.BlockSpec((tm * TOK_ROWS, LANE), lambda i: (i, 0)),
            pl.BlockSpec((tm, LANE), lambda i: (i, 0)),
        ],
        out_shape=[
            jax.ShapeDtypeStruct((m_p + tm, D_MODEL), F32),
            jax.ShapeDtypeStruct(((m_p + tm) * TOK_ROWS, LANE), F32),
            jax.ShapeDtypeStruct((m_p + tm, LANE), F32),
        ],
        compiler_params=_params(("arbitrary",)),
    )(xp, mg_p, xs, mg_s, wo, norm2_g.reshape(1, D_MODEL), wr_hi, wr_lo, br)


def _to_slabs(dst_ref, x):
    rows = x.shape[0]
    for s in range(TOK_ROWS):
        dst_ref[pl.ds(s, rows, stride=TOK_ROWS), :] = x[:, s * LANE:(s + 1) * LANE]


def _from_slabs(src_ref, rows):
    return jnp.concatenate(
        [src_ref[pl.ds(s, rows, stride=TOK_ROWS), :] for s in range(TOK_ROWS)], axis=1)


def _dispatch_kernel(slots_ref, pad0_ref, padn_ref, nused_ref, x1t_hbm, xst_hbm, buf, zero, lsem,
                     ssem, zsem, *, tm, n_blocks):
    i = pl.program_id(0)
    n_steps = pl.num_programs(0)
    slab = tm * TOK_ROWS

    def load(step):
        return pltpu.make_async_copy(
            x1t_hbm.at[pl.ds(pl.multiple_of(step * slab, slab), slab), :], buf.at[step % 3],
            lsem.at[step % 3])

    def wait_scatter(step):
        for _ in range(TOP_K):
            pltpu.make_async_copy(buf.at[step % 3], xst_hbm.at[pl.ds(0, slab), :],
                                  ssem.at[step % 3]).wait()

    def pad_rows(start):
        def expert(e, carry):
            def row(j, c):
                dst = pl.multiple_of((pad0_ref[e] + j) * TOK_ROWS, TOK_ROWS)
                cp = pltpu.make_async_copy(zero, xst_hbm.at[pl.ds(dst, TOK_ROWS), :], zsem.at[0])
                if start:
                    cp.start()
                else:
                    cp.wait()
                return c
            return lax.fori_loop(0, padn_ref[e], row, carry)
        lax.fori_loop(0, N_EXPERTS, expert, 0)

    @pl.when(i == 0)
    def _():
        load(0).start()
        zero[...] = jnp.zeros(zero.shape, F32)
        pad_rows(start=True)

    @pl.when(i >= 2)
    def _():
        wait_scatter(i - 2)

    @pl.when(i + 1 < n_steps)
    def _():
        load(i + 1).start()

    load(i).wait()
    cur = i % 3

    def body(r, carry):
        src = buf.at[cur, pl.ds(pl.multiple_of(r * TOK_ROWS, TOK_ROWS), TOK_ROWS), :]
        for k in range(TOP_K):
            dst = pl.multiple_of(slots_ref[(i * tm + r) * TOP_K + k] * TOK_ROWS, TOK_ROWS)
            pltpu.make_async_copy(src, xst_hbm.at[pl.ds(dst, TOK_ROWS), :],
                                  ssem.at[cur]).start(priority=k)
        return carry
    lax.fori_loop(0, tm, body, 0, unroll=4)

    @pl.when(i == n_steps - 1)
    def _():
        @pl.when(i >= 1)
        def _():
            wait_scatter(i - 1)
        wait_scatter(i)
        pad_rows(start=False)
        chunk = math.gcd(tm, MOE_TM)
        rows = chunk * TOK_ROWS
        buf[0, 0:rows, :] = jnp.zeros((rows, LANE), F32)
        first = nused_ref[0] * MOE_TM
        n_tail = (n_blocks - nused_ref[0]) * (MOE_TM // chunk)

        def tail(j, carry, start):
            dst = pl.multiple_of((first + j * chunk) * TOK_ROWS, TOK_ROWS)
            cp = pltpu.make_async_copy(buf.at[0, pl.ds(0, rows), :],
                                       xst_hbm.at[pl.ds(dst, rows), :], zsem.at[0])
            if start:
                cp.start()
            else:
                cp.wait()
            return carry
        lax.fori_loop(0, n_tail, functools.partial(tail, start=True), 0)
        lax.fori_loop(0, n_tail, functools.partial(tail, start=False), 0)


def _dispatch(x1t, tok_slots, pad_start, pad_count, n_used, n_blocks, tm):
    n_tok = x1t.shape[0] // TOK_ROWS
    assert n_tok % tm == 0
    grid_spec = pltpu.PrefetchScalarGridSpec(
        num_scalar_prefetch=4,
        grid=(n_tok // tm,),
        in_specs=[pl.BlockSpec(memory_space=pl.ANY)],
        out_specs=pl.BlockSpec(memory_space=pl.ANY),
        scratch_shapes=[
            pltpu.VMEM((3, tm * TOK_ROWS, LANE), F32),
            pltpu.VMEM((TOK_ROWS, LANE), F32),
            pltpu.SemaphoreType.DMA((3,)),
            pltpu.SemaphoreType.DMA((3,)),
            pltpu.SemaphoreType.DMA((1,)),
        ],
    )
    return pl.pallas_call(
        functools.partial(_dispatch_kernel, tm=tm, n_blocks=n_blocks),
        grid_spec=grid_spec,
        out_shape=jax.ShapeDtypeStruct((n_blocks * MOE_TM * TOK_ROWS, LANE), F32),
        compiler_params=_params(("arbitrary",)),
    )(tok_slots, pad_start, pad_count, n_used, x1t)


def _moe_kernel(blk0_ref, nblk_ref, nused_ref, xst_hbm, g2_ref, wg_ref, wu_ref, wd_ref,
                yt_hbm, xbuf, ybuf, gsem, osem, wg_b, wu_b, wd_b):
    e = pl.program_id(0)
    n_used = nused_ref[0]
    slab = MOE_TM * TOK_ROWS

    def in_copy(blk, to_slot):
        return pltpu.make_async_copy(
            xst_hbm.at[pl.ds(pl.multiple_of(blk * slab, slab), slab), :], xbuf.at[to_slot],
            gsem.at[to_slot])

    def out_copy(blk, from_slot):
        return pltpu.make_async_copy(
            ybuf.at[from_slot], yt_hbm.at[pl.ds(pl.multiple_of(blk * slab, slab), slab), :],
            osem.at[from_slot])

    wg_b[...] = wg_ref[0].astype(BF16)
    wu_b[...] = wu_ref[0].astype(BF16)
    wd_b[...] = wd_ref[0].astype(BF16)

    @pl.when(e == 0)
    def _():
        in_copy(0, 0).start()

    def block(b, carry):
        blk = blk0_ref[e] + b
        slot = blk & 1

        @pl.when(blk + 1 < n_used)
        def _():
            in_copy(blk + 1, 1 - slot).start(priority=1)

        in_copy(blk, slot).wait()
        h = _rms(_from_slabs(xbuf.at[slot], MOE_TM), g2_ref[...]).astype(BF16)
        a = _dot(h, wg_b[...])
        u = _dot(h, wu_b[...])
        hm = (a * jax.nn.sigmoid(a) * u).astype(BF16)
        y = _dot(hm, wd_b[...])

        @pl.when(blk >= 2)
        def _():
            out_copy(blk - 2, slot).wait()

        _to_slabs(ybuf.at[slot], y)
        out_copy(blk, slot).start()
        return carry

    lax.fori_loop(0, nblk_ref[e], block, 0)

    @pl.when(e == pl.num_programs(0) - 1)
    def _():
        @pl.when(n_used >= 2)
        def _():
            out_copy(n_used - 2, (n_used - 2) & 1).wait()
        out_copy(n_used - 1, (n_used - 1) & 1).wait()


def _moe(xst, norm2_g, w_gate_e, w_up_e, w_down_e, blk_start, nblk, n_used):
    w_idx = lambda e, *_: (e, 0, 0)
    slab = MOE_TM * TOK_ROWS
    grid_spec = pltpu.PrefetchScalarGridSpec(
        num_scalar_prefetch=3,
        grid=(N_EXPERTS,),
        in_specs=[
            pl.BlockSpec(memory_space=pl.ANY),
            pl.BlockSpec((1, D_MODEL), lambda e, *_: (0, 0)),
            pl.BlockSpec((1, D_MODEL, D_EXPERT), w_idx),
            pl.BlockSpec((1, D_MODEL, D_EXPERT), w_idx),
            pl.BlockSpec((1, D_EXPERT, D_MODEL), w_idx),
        ],
        out_specs=pl.BlockSpec(memory_space=pl.ANY),
        scratch_shapes=[
            pltpu.VMEM((2, slab, LANE), F32),
            pltpu.VMEM((2, slab, LANE), F32),
            pltpu.SemaphoreType.DMA((2,)),
            pltpu.SemaphoreType.DMA((2,)),
            pltpu.VMEM((D_MODEL, D_EXPERT), BF16),
            pltpu.VMEM((D_MODEL, D_EXPERT), BF16),
            pltpu.VMEM((D_EXPERT, D_MODEL), BF16),
        ],
    )
    return pl.pallas_call(
        _moe_kernel,
        grid_spec=grid_spec,
        out_shape=jax.ShapeDtypeStruct(xst.shape, F32),
        input_output_aliases={3: 0},
        compiler_params=_params(("arbitrary",)),
    )(blk_start, nblk, n_used, xst, norm2_g.reshape(1, D_MODEL), w_gate_e, w_up_e, w_down_e)


def _combine_ple_kernel(slots_ref, x1_ref, route_ref, yt_hbm, p_ref, g3_ref, wpg_ref, wple_ref,
                        out_ref, ybuf, sem, *, tm, row0):
    i = pl.program_id(0)
    n_steps = pl.num_programs(0)
    slot = i & 1
    slab = tm * TOK_ROWS

    def start_gather(step, to_slot):
        base = (row0 + step * tm) * TOP_K

        def body(r, carry):
            for k in range(TOP_K):
                src = slots_ref[base + r * TOP_K + k]
                pltpu.make_async_copy(
                    yt_hbm.at[pl.ds(pl.multiple_of(src * TOK_ROWS, TOK_ROWS), TOK_ROWS), :],
                    ybuf.at[to_slot, k, pl.ds(pl.multiple_of(r * TOK_ROWS, TOK_ROWS), TOK_ROWS), :],
                    sem.at[to_slot]).start(priority=k)
            return carry
        lax.fori_loop(0, tm, body, 0, unroll=4)

    @pl.when(i == 0)
    def _():
        start_gather(0, 0)

    @pl.when(i + 1 < n_steps)
    def _():
        start_gather(i + 1, 1 - slot)

    for k in range(TOP_K):
        pltpu.make_async_copy(yt_hbm.at[pl.ds(0, slab), :], ybuf.at[slot, k], sem.at[slot]).wait()
    route = route_ref[...]
    x2 = x1_ref[...] + (route[:, 2:3] * _from_slabs(ybuf.at[slot, 0], tm)
                        + route[:, 3:4] * _from_slabs(ybuf.at[slot, 1], tm))
    h = _rms(x2, g3_ref[...]).astype(BF16)
    gate = jax.nn.sigmoid(_dot(h, wpg_ref[...]))
    out_ref[...] = x2 + gate * _dot(p_ref[...].astype(BF16), wple_ref[...])


def _combine_ple(x1_all, route_all, y_slots, tok_slots, p, norm3_g, wpg, wple, m, tm, row0):
    blk0 = row0 // tm
    grid_spec = pltpu.PrefetchScalarGridSpec(
        num_scalar_prefetch=1,
        grid=(m // tm,),
        in_specs=[
            pl.BlockSpec((tm, D_MODEL), lambda i, s: (blk0 + i, 0)),
            pl.BlockSpec((tm, LANE), lambda i, s: (blk0 + i, 0)),
            pl.BlockSpec(memory_space=pl.ANY),
            pl.BlockSpec((tm, PLE_DIM), lambda i, s: (i, 0)),
            pl.BlockSpec((1, D_MODEL), lambda i, s: (0, 0)),
            pl.BlockSpec((D_MODEL, D_MODEL), lambda i, s: (0, 0)),
            pl.BlockSpec((PLE_DIM, D_MODEL), lambda i, s: (0, 0)),
        ],
        out_specs=pl.BlockSpec((tm, D_MODEL), lambda i, s: (i, 0)),
        scratch_shapes=[
            pltpu.VMEM((2, TOP_K, tm * TOK_ROWS, LANE), F32),
            pltpu.SemaphoreType.DMA((2,)),
        ],
    )
    return pl.pallas_call(
        functools.partial(_combine_ple_kernel, tm=tm, row0=row0),
        grid_spec=grid_spec,
        out_shape=jax.ShapeDtypeStruct((m, D_MODEL), F32),
        compiler_params=_params(("arbitrary",)),
    )(tok_slots, x1_all, route_all, y_slots, p, norm3_g.reshape(1, D_MODEL), wpg, wple)


def _routing_tables(route_all):
    flat_e = route_all[:, 0:TOP_K].astype(jnp.int32).reshape(-1)
    n_asg = flat_e.shape[0]
    onehot = (flat_e[:, None] == jnp.arange(N_EXPERTS, dtype=jnp.int32)[None, :]).astype(F32)
    chunk = math.gcd(n_asg, 2 * LANE)
    within = jnp.einsum('ij,cjk->cik', jnp.tril(jnp.ones((chunk, chunk), F32)),
                        onehot.reshape(n_asg // chunk, chunk, N_EXPERTS))
    totals = within[:, -1, :]
    before = jnp.cumsum(totals, axis=0) - totals
    csum = (within + before[:, None, :]).reshape(n_asg, N_EXPERTS)
    rank = jnp.sum(csum * onehot, axis=1).astype(jnp.int32) - 1
    counts = (before[-1] + totals[-1]).astype(jnp.int32)
    nblk = (counts + MOE_TM - 1) // MOE_TM
    blk_end = jnp.cumsum(nblk)
    blk_start = blk_end - nblk
    first_blk = jnp.sum(onehot * blk_start.astype(F32)[None, :], axis=1).astype(jnp.int32)
    slot = first_blk * MOE_TM + rank
    n_used = blk_end[-1].reshape(1)
    pad_start = blk_start * MOE_TM + counts
    pad_count = nblk * MOE_TM - counts
    i32 = lambda a: a.astype(jnp.int32)
    return i32(blk_start), i32(nblk), i32(n_used), i32(slot), i32(pad_start), i32(pad_count)


def kernel(x_prompt, x_sample, p_prompt, p_sample, cache_k_g0, cache_v_g0, cache_k_g1, cache_v_g1,
           cache_k_g2, cache_v_g2, state_conv, norm1_g, w_in, q_gain, k_gain, conv_w, w_conv_out,
           w_attn_out, w_o, norm2_g, w_coarse, b_coarse, w_fine, b_fine, w_gate_e, w_up_e, w_down_e,
           norm3_g, w_ple_gate, w_ple):
    seq = x_prompt.shape[1]
    assert x_prompt.shape == (1, seq, D_MODEL) and norm1_g.shape[0] == 1, "one prompt, one layer"
    n_seq, t_new, _ = x_sample.shape
    assert seq % ATT_UNIT == 0 and t_new == SUBLANE
    caches_k = (cache_k_g0, cache_k_g1, cache_k_g2)
    caches_v = (cache_v_g0, cache_v_g1, cache_v_g2)
    for g in range(N_GROUPS):
        assert caches_k[g].shape == (1, n_seq, WINDOWS[g], HEADS_PER_GROUP, HEAD_DIM)
    n_s = n_seq * t_new
    n_all = seq + n_s

    xp = x_prompt[0]
    xs = x_sample.reshape(n_s, D_MODEL)
    wc = w_conv_out[0].astype(BF16)
    wa = w_attn_out[0].astype(BF16)
    wo = w_o[0].astype(BF16)
    wpg = w_ple_gate[0].astype(BF16)
    wple = w_ple[0].astype(BF16)
    wr = jnp.concatenate(
        [w_coarse[0], jnp.transpose(w_fine[0], (1, 0, 2)).reshape(D_MODEL, N_EXPERTS)], axis=1)
    wr = jnp.pad(wr, ((0, 0), (0, LANE - wr.shape[1])))
    wr_hi = wr.astype(BF16)
    wr_lo = (wr - wr_hi.astype(F32)).astype(BF16)
    br = jnp.pad(jnp.concatenate([b_coarse[0], b_fine[0].reshape(-1)]), (0, LANE - 36)).reshape(1, LANE)

    z_p, z_s = _inproj(xp, xs, norm1_g[0], w_in[0], tm=2048, tn=512)

    att_p = _prompt_attn(z_p, q_gain[0], k_gain[0])
    o_p, pks, pvs = att_p[0], att_p[1:4], att_p[4:7]
    ck = [c.reshape(n_seq, WINDOWS[g] * HEADS_PER_GROUP, HEAD_DIM) for g, c in enumerate(caches_k)]
    cv = [c.reshape(n_seq, WINDOWS[g] * HEADS_PER_GROUP, HEAD_DIM) for g, c in enumerate(caches_v)]
    att_s = _sample_attn(z_s, ck, cv, q_gain[0], k_gain[0], n_seq, t_new)
    o_s, sks, svs = att_s[0], att_s[1:4], att_s[4:7]

    mg_p, u_tail = _merge(z_p, o_p, conv_w[0], wc, wa, tm=512)
    prefix = jnp.pad(state_conv[0], ((0, 0), (0, t_new - 2), (0, 0))).reshape(n_s, D_CONV)
    mg_s, u_s = _merge(z_s, o_s, conv_w[0], wc, wa, tm=n_s, per_seq=t_new, prefix=prefix)
    x1_all, x1t_all, route_all = _outproj_router(xp, mg_p, xs, mg_s, wo, norm2_g[0], wr_hi, wr_lo, br)

    n_blocks = n_all * TOP_K // MOE_TM + N_EXPERTS
    blk_start, nblk, n_used, tok_slots, pad_start, pad_count = _routing_tables(route_all)
    xst = _dispatch(x1t_all, tok_slots, pad_start, pad_count, n_used, n_blocks, tm=n_s)
    y_slots = _moe(xst, norm2_g[0], w_gate_e[0], w_up_e[0], w_down_e[0], blk_start, nblk, n_used)

    y_p = _combine_ple(x1_all, route_all, y_slots, tok_slots, p_prompt[0, 0], norm3_g[0], wpg, wple,
                       m=seq, tm=256, row0=0)
    y_s = _combine_ple(x1_all, route_all, y_slots, tok_slots, p_sample[0].reshape(n_s, PLE_DIM),
                       norm3_g[0], wpg, wple, m=n_s, tm=n_s, row0=seq)

    def state(a, n, g):
        return a.reshape(1, n, WINDOWS[g], HEADS_PER_GROUP, HEAD_DIM)

    pk = [state(a, 1, g) for g, a in enumerate(pks)]
    pv = [state(a, 1, g) for g, a in enumerate(pvs)]
    sk = [state(a, n_seq, g) for g, a in enumerate(sks)]
    sv = [state(a, n_seq, g) for g, a in enumerate(svs)]
    pconv = u_tail[-2:].reshape(1, 1, 2, D_CONV)
    sconv = u_s.reshape(n_seq, t_new, D_CONV)[:, t_new - 2:].reshape(1, n_seq, 2, D_CONV)
    return (y_p.reshape(1, seq, D_MODEL), y_s.reshape(n_seq, t_new, D_MODEL),
            pk[0], pv[0], pk[1], pv[1], pk[2], pv[2], pconv,
            sk[0], sv[0], sk[1], sv[1], sk[2], sv[2], sconv)
```

```python
import functools
import math

import jax
import jax.numpy as jnp
from jax import lax
from jax.experimental import pallas as pl
from jax.experimental.pallas import tpu as pltpu

D_MODEL = 2048
HEAD_DIM = 128
HEADS_PER_GROUP = 4
WINDOWS = (128, 512, 2048)
DILATIONS = (1, 4, 16)
N_GROUPS = 3
N_HEADS = N_GROUPS * HEADS_PER_GROUP
ATT_WIDTH = N_HEADS * HEAD_DIM
ATT_OUT_WIDTH = HEADS_PER_GROUP * HEAD_DIM
SCALE = HEAD_DIM ** -0.5
D_CONV = D_MODEL // 2
PLE_DIM = 256
N_EXPERT_GROUPS = 4
EXPERTS_PER_GROUP = 8
N_EXPERTS = N_EXPERT_GROUPS * EXPERTS_PER_GROUP
TOP_K = 2
D_EXPERT = D_MODEL // 4
EPS = 1e-6

COL_B = 0
COL_C = D_CONV
COL_H = 2 * D_CONV
COL_Q = 3 * D_CONV
COL_K = COL_Q + ATT_WIDTH
COL_V = COL_K + ATT_WIDTH
COL_GC = COL_V + ATT_WIDTH
COL_GA = COL_GC + D_MODEL
IN_COLS = COL_GA + D_MODEL

LANE = 128
SUBLANE = 8
NEG = -1e30
ATT_UNIT = max(WINDOWS)
ATT_BLK = 128
ATT_TILE_UNROLL = 16
MXU_N = 256
TOK_ROWS = D_MODEL // LANE
MOE_TM = 288
MOE_IN_BUFS = 4
VMEM_LIMIT = 56 * 1024 * 1024

BF16 = jnp.bfloat16
F32 = jnp.float32


def _params(sem):
    return pltpu.CompilerParams(dimension_semantics=sem, vmem_limit_bytes=VMEM_LIMIT)


def _rms(x, gain):
    return x * lax.rsqrt(jnp.mean(x * x, axis=-1, keepdims=True) + EPS) * gain


def _dot(a, b):
    return jnp.dot(a, b, preferred_element_type=F32)


def _dot_nt(a, b):
    return lax.dot_general(a, b, (((1,), (1,)), ((), ())), preferred_element_type=F32)


def _inproj_kernel(x_ref, xs_ref, g_ref, w_ref, z_ref, zs_ref, h_scr, hs_scr):
    i, j = pl.program_id(0), pl.program_id(1)
    with_sample = i == pl.num_programs(0) - 1

    @pl.when(j == 0)
    def _():
        h_scr[...] = _rms(x_ref[...], g_ref[...]).astype(BF16)

    @pl.when(with_sample & (j == 0))
    def _():
        hs_scr[...] = _rms(xs_ref[...], g_ref[...]).astype(BF16)

    def project(sample_too):
        for c in range(w_ref.shape[1] // MXU_N):
            cols = slice(c * MXU_N, (c + 1) * MXU_N)
            w_c = w_ref[:, cols].astype(BF16)
            z_ref[:, cols] = _dot(h_scr[...], w_c)
            if sample_too:
                zs_ref[:, cols] = _dot(hs_scr[...], w_c)

    @pl.when(jnp.logical_not(with_sample))
    def _():
        project(False)

    @pl.when(with_sample)
    def _():
        project(True)


def _inproj(x, xs, norm_g, w_in, tm, tn):
    m, m_s = x.shape[0], xs.shape[0]
    n_i = m // tm
    return pl.pallas_call(
        _inproj_kernel,
        grid=(n_i, IN_COLS // tn),
        in_specs=[
            pl.BlockSpec((tm, D_MODEL), lambda i, j: (i, 0), pipeline_mode=pl.Buffered(1)),
            pl.BlockSpec((m_s, D_MODEL), lambda i, j: (0, 0), pipeline_mode=pl.Buffered(1)),
            pl.BlockSpec((1, D_MODEL), lambda i, j: (0, 0)),
            pl.BlockSpec((D_MODEL, tn), lambda i, j: (0, j)),
        ],
        out_specs=[
            pl.BlockSpec((tm, tn), lambda i, j: (i, j)),
            pl.BlockSpec((m_s, tn), lambda i, j: (0, jnp.where(i == n_i - 1, j, 0))),
        ],
        out_shape=[jax.ShapeDtypeStruct((m, IN_COLS), F32),
                   jax.ShapeDtypeStruct((m_s, IN_COLS), F32)],
        scratch_shapes=[pltpu.VMEM((tm, D_MODEL), BF16), pltpu.VMEM((m_s, D_MODEL), BF16)],
        compiler_params=_params(("arbitrary", "arbitrary")),
    )(x, xs, norm_g.reshape(1, D_MODEL), w_in)


def _prompt_attn_kernel(*refs):
    (q0, q1, q2, kc0, kc1, kc2, vc0, vc1, vc2, kp0, kp1, kp2, vp0, vp1, vp2, qg_ref, kg_ref,
     o_ref, pk0, pk1, pk2, pv0, pv1, pv2, kext, vext, acc_s, m_s, l_s) = refs
    q_refs, kc_refs, vc_refs = (q0, q1, q2), (kc0, kc1, kc2), (vc0, vc1, vc2)
    kp_refs, vp_refs = (kp0, kp1, kp2), (vp0, vp1, vp2)
    pk_refs, pv_refs = (pk0, pk1, pk2), (pv0, pv1, pv2)
    n = pl.program_id(1)
    last = pl.num_programs(1) - 1

    qi = lax.broadcasted_iota(jnp.int32, (ATT_BLK, 2 * ATT_BLK), 0) + ATT_BLK
    ki = lax.broadcasted_iota(jnp.int32, (ATT_BLK, 2 * ATT_BLK), 1)
    dist = qi - ki
    band = (dist >= 0) & (dist <= ATT_BLK)

    for g in range(N_GROUPS):
        win, dil = WINDOWS[g], DILATIONS[g]
        q_ref = q_refs[g]
        qgain = qg_ref[g, 0]
        kgain = kg_ref[g, 0]
        kext[0:win, :] = _rms(kp_refs[g][...], kgain)
        kext[win:win + ATT_UNIT, :] = _rms(kc_refs[g][...], kgain)
        vext[0:win, :] = vp_refs[g][...]
        vext[win:win + ATT_UNIT, :] = vc_refs[g][...]

        @pl.when(n == last)
        def _():
            pk_refs[g][...] = kext[ATT_UNIT:ATT_UNIT + win, :]
            pv_refs[g][...] = vext[ATT_UNIT:ATT_UNIT + win, :]

        def tile(t, carry, g=g, win=win, dil=dil, q_ref=q_ref, qgain=qgain):
            u = t // dil
            r = t % dil
            off = u * win + r
            if dil == 1:
                off = pl.multiple_of(off, ATT_BLK)
                rows_q = pl.ds(off, ATT_BLK)
                rows_k = pl.ds(off, 2 * ATT_BLK)
            else:
                rows_q = pl.ds(off, ATT_BLK, stride=dil)
                rows_k = pl.ds(off, 2 * ATT_BLK, stride=dil)
            q = _rms(q_ref[rows_q, :], qgain).astype(BF16)
            k = kext[rows_k, :].astype(BF16)
            v = vext[rows_k, :].astype(BF16)
            s = _dot_nt(q, k) * SCALE
            k_min = jnp.where((n > 0) | (u > 0), 0, ATT_BLK)
            s = jnp.where(band & (ki >= k_min), s, NEG)
            m_t = jnp.max(s, axis=-1, keepdims=True)
            p = jnp.exp(s - m_t)
            l_t = jnp.sum(p, axis=-1, keepdims=True)
            m_s[g, rows_q, :] = jnp.broadcast_to(m_t, (ATT_BLK, LANE))
            l_s[g, rows_q, :] = jnp.broadcast_to(l_t, (ATT_BLK, LANE))
            acc_s[g, rows_q, :] = _dot(p.astype(BF16), v)
            return carry

        lax.fori_loop(0, ATT_UNIT // ATT_BLK, tile, 0, unroll=ATT_TILE_UNROLL)

    def finish(c, carry):
        rows = pl.ds(pl.multiple_of(c * ATT_BLK, ATT_BLK), ATT_BLK)
        m = [m_s[g, rows, :] for g in range(N_GROUPS)]
        m_all = jnp.maximum(jnp.maximum(m[0], m[1]), m[2])
        num = jnp.zeros((ATT_BLK, LANE), F32)
        den = jnp.zeros((ATT_BLK, LANE), F32)
        for g in range(N_GROUPS):
            w = jnp.exp(m[g] - m_all)
            num = num + w * acc_s[g, rows, :]
            den = den + w * l_s[g, rows, :]
        o_ref[rows, :] = num / den
        return carry

    lax.fori_loop(0, ATT_UNIT // ATT_BLK, finish, 0, unroll=2)


def _prompt_attn(z, q_gain, k_gain):
    s_len = z.shape[0]
    n_units = s_len // ATT_UNIT
    in_specs, args = [], []

    def col(base, g):
        return lambda hh, n: (n, base // LANE + g * HEADS_PER_GROUP + hh)

    for g in range(N_GROUPS):
        in_specs.append(pl.BlockSpec((ATT_UNIT, LANE), col(COL_Q, g)))
    for base in (COL_K, COL_V):
        for g in range(N_GROUPS):
            in_specs.append(pl.BlockSpec((ATT_UNIT, LANE), col(base, g)))
    for base in (COL_K, COL_V):
        for g in range(N_GROUPS):
            per = ATT_UNIT // WINDOWS[g]
            in_specs.append(pl.BlockSpec(
                (WINDOWS[g], LANE),
                lambda hh, n, base=base, g=g, per=per: (
                    jnp.maximum(n * per - 1, 0), base // LANE + g * HEADS_PER_GROUP + hh)))
    args = [z] * 15
    gain_spec = pl.BlockSpec((N_GROUPS, 1, 1, LANE), lambda hh, n: (0, hh, 0, 0))
    in_specs += [gain_spec, gain_spec]
    args += [q_gain.reshape(N_GROUPS, HEADS_PER_GROUP, 1, LANE),
             k_gain.reshape(N_GROUPS, HEADS_PER_GROUP, 1, LANE)]

    out_specs = [pl.BlockSpec((ATT_UNIT, LANE), lambda hh, n: (n, hh))]
    out_shape = [jax.ShapeDtypeStruct((s_len, ATT_OUT_WIDTH), F32)]
    for _ in range(2):
        for g in range(N_GROUPS):
            out_specs.append(pl.BlockSpec((WINDOWS[g], LANE), lambda hh, n: (0, hh)))
            out_shape.append(jax.ShapeDtypeStruct((WINDOWS[g], ATT_OUT_WIDTH), F32))

    return pl.pallas_call(
        _prompt_attn_kernel,
        grid=(HEADS_PER_GROUP, n_units),
        in_specs=in_specs,
        out_specs=out_specs,
        out_shape=out_shape,
        scratch_shapes=[
            pltpu.VMEM((2 * ATT_UNIT, LANE), F32),
            pltpu.VMEM((2 * ATT_UNIT, LANE), F32),
            pltpu.VMEM((N_GROUPS, ATT_UNIT, LANE), F32),
            pltpu.VMEM((N_GROUPS, ATT_UNIT, LANE), F32),
            pltpu.VMEM((N_GROUPS, ATT_UNIT, LANE), F32),
        ],
        compiler_params=_params(("parallel", "arbitrary")),
    )(*args)


def _sample_attn_kernel(*refs):
    (q0, q1, q2, kn0, kn1, kn2, vn0, vn1, vn2, ck0, ck1, ck2, cv0, cv1, cv2, qg_ref, kg_ref,
     o_ref, sk0, sk1, sk2, sv0, sv1, sv2) = refs
    q_refs, kn_refs, vn_refs = (q0, q1, q2), (kn0, kn1, kn2), (vn0, vn1, vn2)
    ck_refs, cv_refs = (ck0, ck1, ck2), (cv0, cv1, cv2)
    sk_refs, sv_refs = (sk0, sk1, sk2), (sv0, sv1, sv2)
    t_new = q0.shape[0]
    nh = HEADS_PER_GROUP
    n_q = nh * t_new
    t_bits = t_new.bit_length() - 1

    def heads(ref):
        return [ref[:, h * LANE:(h + 1) * LANE] for h in range(nh)]

    parts = []
    for g in range(N_GROUPS):
        win, dil = WINDOWS[g], DILATIONS[g]
        q_all = jnp.concatenate(
            [_rms(x, qg_ref[g, h]) for h, x in enumerate(heads(q_refs[g]))], axis=0).astype(BF16)
        k_new = [_rms(x, kg_ref[g, h]) for h, x in enumerate(heads(kn_refs[g]))]
        v_new = heads(vn_refs[g])
        k_new_all = jnp.concatenate(k_new, axis=0).astype(BF16)
        v_new_all = jnp.concatenate(v_new, axis=0).astype(BF16)
        ck = ck_refs[g][0]
        cv = cv_refs[g][0]
        old = (win - t_new) * nh
        sk_refs[g][0, 0:old, :] = ck[t_new * nh:, :]
        sv_refs[g][0, 0:old, :] = cv[t_new * nh:, :]
        for h in range(nh):
            sk_refs[g][0, pl.ds(old + h, t_new, stride=nh), :] = k_new[h]
            sv_refs[g][0, pl.ds(old + h, t_new, stride=nh), :] = v_new[h]

        s_c = _dot_nt(q_all, ck.astype(BF16)) * SCALE
        s_n = _dot_nt(q_all, k_new_all) * SCALE
        row = lax.broadcasted_iota(jnp.int32, (n_q, win * nh), 0)
        col = lax.broadcasted_iota(jnp.int32, (n_q, win * nh), 1)
        i_q, pos = row & (t_new - 1), col >> 2
        mask_c = (((col & (nh - 1)) == (row >> t_bits)) & (pos >= i_q)
                  & (((pos - i_q) & (dil - 1)) == 0))
        row_n = lax.broadcasted_iota(jnp.int32, (n_q, n_q), 0)
        col_n = lax.broadcasted_iota(jnp.int32, (n_q, n_q), 1)
        d_n = (row_n & (t_new - 1)) - (col_n & (t_new - 1))
        mask_n = ((row_n >> t_bits) == (col_n >> t_bits)) & (d_n >= 0) & ((d_n & (dil - 1)) == 0)
        s_c = jnp.where(mask_c, s_c, NEG)
        s_n = jnp.where(mask_n, s_n, NEG)
        m = jnp.maximum(jnp.max(s_c, axis=-1, keepdims=True), jnp.max(s_n, axis=-1, keepdims=True))
        p_c = jnp.exp(s_c - m)
        p_n = jnp.exp(s_n - m)
        l = jnp.sum(p_c, axis=-1, keepdims=True) + jnp.sum(p_n, axis=-1, keepdims=True)
        acc = _dot(p_c.astype(BF16), cv.astype(BF16)) + _dot(p_n.astype(BF16), v_new_all)
        parts.append((m, l, acc))

    m_all = jnp.maximum(jnp.maximum(parts[0][0], parts[1][0]), parts[2][0])
    num = jnp.zeros((n_q, LANE), F32)
    den = jnp.zeros((n_q, 1), F32)
    for m, l, acc in parts:
        w = jnp.exp(m - m_all)
        num = num + w * acc
        den = den + w * l
    o_all = num / den
    for h in range(nh):
        o_ref[:, h * LANE:(h + 1) * LANE] = o_all[h * t_new:(h + 1) * t_new, :]


def _sample_attn(z, caches_k, caches_v, q_gain, k_gain, n_seq, t_new):
    assert HEADS_PER_GROUP == 4 and t_new & (t_new - 1) == 0
    in_specs, args = [], []
    for base in (COL_Q, COL_K, COL_V):
        for g in range(N_GROUPS):
            in_specs.append(pl.BlockSpec(
                (t_new, ATT_OUT_WIDTH), lambda b, base=base, g=g: (b, base // ATT_OUT_WIDTH + g)))
            args.append(z)
    for caches in (caches_k, caches_v):
        for g in range(N_GROUPS):
            in_specs.append(pl.BlockSpec((1, WINDOWS[g] * HEADS_PER_GROUP, LANE), lambda b: (b, 0, 0)))
            args.append(caches[g])
    gain_spec = pl.BlockSpec((N_GROUPS, HEADS_PER_GROUP, 1, LANE), lambda b: (0, 0, 0, 0))
    in_specs += [gain_spec, gain_spec]
    args += [q_gain.reshape(N_GROUPS, HEADS_PER_GROUP, 1, LANE),
             k_gain.reshape(N_GROUPS, HEADS_PER_GROUP, 1, LANE)]

    out_specs = [pl.BlockSpec((t_new, ATT_OUT_WIDTH), lambda b: (b, 0))]
    out_shape = [jax.ShapeDtypeStruct((n_seq * t_new, ATT_OUT_WIDTH), F32)]
    for _ in range(2):
        for g in range(N_GROUPS):
            rows = WINDOWS[g] * HEADS_PER_GROUP
            out_specs.append(pl.BlockSpec((1, rows, LANE), lambda b: (b, 0, 0)))
            out_shape.append(jax.ShapeDtypeStruct((n_seq, rows, LANE), F32))

    return pl.pallas_call(
        _sample_attn_kernel,
        grid=(n_seq,),
        in_specs=in_specs,
        out_specs=out_specs,
        out_shape=out_shape,
        compiler_params=_params(("parallel",)),
    )(*args)


def _conv_taps(u, r1, r2, cw_ref):
    return cw_ref[2:3, :] * u + cw_ref[1:2, :] * r1 + cw_ref[0:1, :] * r2


def _merge_kernel(*refs, per_seq, n_chunks):
    b_ref, c_ref, h_ref, pc_ref, ph_ref, cw_ref, o_ref = refs[:7]
    gc_refs = refs[7:7 + n_chunks]
    ga_refs = refs[7 + n_chunks:7 + 2 * n_chunks]
    wc_ref, wa_ref, out_ref, u_ref, yb_scr = refs[7 + 2 * n_chunks:]
    i = pl.program_id(0)
    tm = b_ref.shape[0]
    u = c_ref[...] * h_ref[...]
    r1 = pltpu.roll(u, 1, axis=0)
    r2 = pltpu.roll(u, 2, axis=0)
    if per_seq is None:
        yb_scr[...] = b_ref[...] * _conv_taps(u, r1, r2, cw_ref)
        up = jnp.where(i > 0, pc_ref[...] * ph_ref[...], 0.0)
        row = lax.broadcasted_iota(jnp.int32, (SUBLANE, D_CONV), 0)
        u8 = u[0:SUBLANE, :]
        r1_8 = jnp.where(row < 1, pltpu.roll(up, 1, axis=0), pltpu.roll(u8, 1, axis=0))
        r2_8 = jnp.where(row < 2, pltpu.roll(up, 2, axis=0), pltpu.roll(u8, 2, axis=0))
        yb_scr[0:SUBLANE, :] = b_ref[0:SUBLANE, :] * _conv_taps(u8, r1_8, r2_8, cw_ref)
        u_ref[...] = u[tm - SUBLANE:tm, :]
    else:
        pre = pc_ref[...]
        t = lax.broadcasted_iota(jnp.int32, (tm, D_CONV), 0) & (per_seq - 1)
        r1 = jnp.where(t == 0, pltpu.roll(pre, tm - 1, axis=0), r1)
        r2 = jnp.where(t < 2, pre, r2)
        yb_scr[...] = b_ref[...] * _conv_taps(u, r1, r2, cw_ref)
        u_ref[...] = u

    yb = yb_scr[...].astype(BF16)
    ob = o_ref[...].astype(BF16)
    tn = D_MODEL // n_chunks
    for c in range(n_chunks):
        cols = slice(c * tn, (c + 1) * tn)
        y_conv = _dot(yb, wc_ref[:, cols])
        y_att = _dot(ob, wa_ref[:, cols])
        out_ref[:, cols] = (jax.nn.sigmoid(gc_refs[c][...]) * y_conv
                            + jax.nn.sigmoid(ga_refs[c][...]) * y_att).astype(BF16)


def _merge(z, o_att, conv_w, wc, wa, tm, per_seq=None, prefix=None):
    m = z.shape[0]
    tn = ATT_OUT_WIDTH
    n_chunks = D_MODEL // tn
    if per_seq is None:
        prev = lambda i, c: (jnp.maximum(i * (tm // SUBLANE) - 1, 0), c)
        pc_spec = pl.BlockSpec((SUBLANE, D_CONV), lambda i: prev(i, COL_C // D_CONV))
        ph_spec = pl.BlockSpec((SUBLANE, D_CONV), lambda i: prev(i, COL_H // D_CONV))
        pc_arg, ph_arg = z, z
        u_rows = SUBLANE
    else:
        pc_spec = pl.BlockSpec((tm, D_CONV), lambda i: (i, 0))
        ph_spec = pl.BlockSpec((SUBLANE, D_CONV), lambda i: (0, 0))
        pc_arg, ph_arg = prefix, prefix
        u_rows = tm
    gate_specs = [pl.BlockSpec((tm, tn), lambda i, base=base, c=c: (i, base // tn + c))
                  for base in (COL_GC, COL_GA) for c in range(n_chunks)]
    return pl.pallas_call(
        functools.partial(_merge_kernel, per_seq=per_seq, n_chunks=n_chunks),
        grid=(m // tm,),
        in_specs=[
            pl.BlockSpec((tm, D_CONV), lambda i: (i, COL_B // D_CONV)),
            pl.BlockSpec((tm, D_CONV), lambda i: (i, COL_C // D_CONV)),
            pl.BlockSpec((tm, D_CONV), lambda i: (i, COL_H // D_CONV)),
            pc_spec,
            ph_spec,
            pl.BlockSpec((3, D_CONV), lambda i: (0, 0)),
            pl.BlockSpec((tm, ATT_OUT_WIDTH), lambda i: (i, 0)),
            *gate_specs,
            pl.BlockSpec((D_CONV, D_MODEL), lambda i: (0, 0)),
            pl.BlockSpec((ATT_OUT_WIDTH, D_MODEL), lambda i: (0, 0)),
        ],
        out_specs=[
            pl.BlockSpec((tm, D_MODEL), lambda i: (i, 0)),
            pl.BlockSpec((u_rows, D_CONV), lambda i: (i, 0)),
        ],
        out_shape=[
            jax.ShapeDtypeStruct((m, D_MODEL), BF16),
            jax.ShapeDtypeStruct((m // tm * u_rows, D_CONV), F32),
        ],
        scratch_shapes=[pltpu.VMEM((tm, D_CONV), F32)],
        compiler_params=_params(("parallel",)),
    )(z, z, z, pc_arg, ph_arg, conv_w, o_att, *([z] * (2 * n_chunks)), wc, wa)


def _outproj_router_kernel(xp_ref, mgp_ref, xs_ref, mgs_ref, wo_ref, g2_ref, wr_hi_ref, wr_lo_ref,
                           br_ref, x1_ref, x1t_ref, route_ref):
    is_sample = pl.program_id(0) == pl.num_programs(0) - 1

    @pl.when(jnp.logical_not(is_sample))
    def _():
        _outproj_router_tile(xp_ref, mgp_ref, wo_ref, g2_ref, wr_hi_ref, wr_lo_ref, br_ref,
                             x1_ref, x1t_ref, route_ref)

    @pl.when(is_sample)
    def _():
        _outproj_router_tile(xs_ref, mgs_ref, wo_ref, g2_ref, wr_hi_ref, wr_lo_ref, br_ref,
                             x1_ref, x1t_ref, route_ref)


def _outproj_router_tile(x_ref, mg_ref, wo_ref, g2_ref, wr_hi_ref, wr_lo_ref, br_ref, x1_ref,
                         x1t_ref, route_ref):
    x1 = x_ref[...] + _dot(mg_ref[...], wo_ref[...])
    x1_ref[...] = x1
    _to_slabs(x1t_ref, x1)
    h = _rms(x1, g2_ref[...])
    h_hi = h.astype(BF16)
    h_lo = (h - h_hi.astype(F32)).astype(BF16)
    logits = (_dot(h_hi, wr_hi_ref[...]) + _dot(h_hi, wr_lo_ref[...]) + _dot(h_lo, wr_hi_ref[...])
              + br_ref[...])
    lane = lax.broadcasted_iota(jnp.int32, logits.shape, 1)
    is_coarse = lane < N_EXPERT_GROUPS
    coarse = jnp.where(is_coarse, logits, NEG)
    cmax = jnp.max(coarse, axis=-1, keepdims=True)
    grp = jnp.min(jnp.where(coarse == cmax, lane, LANE), axis=-1, keepdims=True)
    p_grp = 1.0 / jnp.sum(jnp.where(is_coarse, jnp.exp(coarse - cmax), 0.0), axis=-1, keepdims=True)
    eid = lane - N_EXPERT_GROUPS
    in_grp = (eid >= 0) & (eid < N_EXPERTS) & ((eid >> 3) == grp)
    fine = jnp.where(in_grp, logits, NEG)
    v1 = jnp.max(fine, axis=-1, keepdims=True)
    i1 = jnp.min(jnp.where(fine == v1, lane, LANE), axis=-1, keepdims=True)
    fine2 = jnp.where(lane == i1, NEG, fine)
    v2 = jnp.max(fine2, axis=-1, keepdims=True)
    i2 = jnp.min(jnp.where(fine2 == v2, lane, LANE), axis=-1, keepdims=True)
    e = jnp.exp(v2 - v1)
    gate1 = p_grp / (1.0 + e)
    gate2 = p_grp * e / (1.0 + e)
    route = jnp.where(lane == 0, (i1 - N_EXPERT_GROUPS).astype(F32),
                      jnp.where(lane == 1, (i2 - N_EXPERT_GROUPS).astype(F32),
                                jnp.where(lane == 2, gate1, jnp.where(lane == 3, gate2, 0.0))))
    route_ref[...] = route


def _outproj_router(xp, mg_p, xs, mg_s, wo, norm2_g, wr_hi, wr_lo, br):
    m_p, tm = xp.shape[0], xs.shape[0]
    assert m_p % tm == 0
    n_p = m_p // tm
    prompt_blk = lambda i: (jnp.minimum(i, n_p - 1), 0)
    fixed = lambda i: (0, 0)
    return pl.pallas_call(
        _outproj_router_kernel,
        grid=(n_p + 1,),
        in_specs=[
            pl.BlockSpec((tm, D_MODEL), prompt_blk),
            pl.BlockSpec((tm, D_MODEL), prompt_blk),
            pl.BlockSpec((tm, D_MODEL), fixed),
            pl.BlockSpec((tm, D_MODEL), fixed),
            pl.BlockSpec((D_MODEL, D_MODEL), fixed),
            pl.BlockSpec((1, D_MODEL), fixed),
            pl.BlockSpec((D_MODEL, LANE), fixed),
            pl.BlockSpec((D_MODEL, LANE), fixed),
            pl.BlockSpec((1, LANE), fixed),
        ],
        out_specs=[
            pl.BlockSpec((tm, D_MODEL), lambda i: (i, 0)),
            pl.BlockSpec((tm * TOK_ROWS, LANE), lambda i: (i, 0)),
            pl.BlockSpec((tm, LANE), lambda i: (i, 0)),
        ],
        out_shape=[
            jax.ShapeDtypeStruct((m_p + tm, D_MODEL), F32),
            jax.ShapeDtypeStruct(((m_p + tm) * TOK_ROWS, LANE), F32),
            jax.ShapeDtypeStruct((m_p + tm, LANE), F32),
        ],
        compiler_params=_params(("arbitrary",)),
    )(xp, mg_p, xs, mg_s, wo, norm2_g.reshape(1, D_MODEL), wr_hi, wr_lo, br)


def _to_slabs(dst_ref, x):
    rows = x.shape[0]
    for s in range(TOK_ROWS):
        dst_ref[pl.ds(s, rows, stride=TOK_ROWS), :] = x[:, s * LANE:(s + 1) * LANE]


def _from_slabs(src_ref, rows):
    return jnp.concatenate(
        [src_ref[pl.ds(s, rows, stride=TOK_ROWS), :] for s in range(TOK_ROWS)], axis=1)


def _dispatch_kernel(slots_ref, pad0_ref, padn_ref, nused_ref, x1t_hbm, xst_hbm, buf, zero, lsem,
                     ssem, zsem, *, tm, n_blocks):
    i = pl.program_id(0)
    n_steps = pl.num_programs(0)
    slab = tm * TOK_ROWS

    def load(step):
        return pltpu.make_async_copy(
            x1t_hbm.at[pl.ds(pl.multiple_of(step * slab, slab), slab), :], buf.at[step % 3],
            lsem.at[step % 3])

    def wait_scatter(step):
        for _ in range(TOP_K):
            pltpu.make_async_copy(buf.at[step % 3], xst_hbm.at[pl.ds(0, slab), :],
                                  ssem.at[step % 3]).wait()

    def pad_rows(start):
        def expert(e, carry):
            def row(j, c):
                dst = pl.multiple_of((pad0_ref[e] + j) * TOK_ROWS, TOK_ROWS)
                cp = pltpu.make_async_copy(zero, xst_hbm.at[pl.ds(dst, TOK_ROWS), :], zsem.at[0])
                if start:
                    cp.start()
                else:
                    cp.wait()
                return c
            return lax.fori_loop(0, padn_ref[e], row, carry)
        lax.fori_loop(0, N_EXPERTS, expert, 0)

    @pl.when(i == 0)
    def _():
        load(0).start()
        zero[...] = jnp.zeros(zero.shape, F32)
        pad_rows(start=True)

    @pl.when(i >= 2)
    def _():
        wait_scatter(i - 2)

    @pl.when(i + 1 < n_steps)
    def _():
        load(i + 1).start()

    load(i).wait()
    cur = i % 3

    def body(r, carry):
        src = buf.at[cur, pl.ds(pl.multiple_of(r * TOK_ROWS, TOK_ROWS), TOK_ROWS), :]
        for k in range(TOP_K):
            dst = pl.multiple_of(slots_ref[(i * tm + r) * TOP_K + k] * TOK_ROWS, TOK_ROWS)
            pltpu.make_async_copy(src, xst_hbm.at[pl.ds(dst, TOK_ROWS), :],
                                  ssem.at[cur]).start(priority=k)
        return carry
    lax.fori_loop(0, tm, body, 0, unroll=4)

    @pl.when(i == n_steps - 1)
    def _():
        @pl.when(i >= 1)
        def _():
            wait_scatter(i - 1)
        wait_scatter(i)
        pad_rows(start=False)
        chunk = math.gcd(tm, MOE_TM)
        rows = chunk * TOK_ROWS
        buf[0, 0:rows, :] = jnp.zeros((rows, LANE), F32)
        first = nused_ref[0] * MOE_TM
        n_tail = (n_blocks - nused_ref[0]) * (MOE_TM // chunk)

        def tail(j, carry, start):
            dst = pl.multiple_of((first + j * chunk) * TOK_ROWS, TOK_ROWS)
            cp = pltpu.make_async_copy(buf.at[0, pl.ds(0, rows), :],
                                       xst_hbm.at[pl.ds(dst, rows), :], zsem.at[0])
            if start:
                cp.start()
            else:
                cp.wait()
            return carry
        lax.fori_loop(0, n_tail, functools.partial(tail, start=True), 0)
        lax.fori_loop(0, n_tail, functools.partial(tail, start=False), 0)


def _dispatch(x1t, tok_slots, pad_start, pad_count, n_used, n_blocks, tm):
    n_tok = x1t.shape[0] // TOK_ROWS
    assert n_tok % tm == 0
    grid_spec = pltpu.PrefetchScalarGridSpec(
        num_scalar_prefetch=4,
        grid=(n_tok // tm,),
        in_specs=[pl.BlockSpec(memory_space=pl.ANY)],
        out_specs=pl.BlockSpec(memory_space=pl.ANY),
        scratch_shapes=[
            pltpu.VMEM((3, tm * TOK_ROWS, LANE), F32),
            pltpu.VMEM((TOK_ROWS, LANE), F32),
            pltpu.SemaphoreType.DMA((3,)),
            pltpu.SemaphoreType.DMA((3,)),
            pltpu.SemaphoreType.DMA((1,)),
        ],
    )
    return pl.pallas_call(
        functools.partial(_dispatch_kernel, tm=tm, n_blocks=n_blocks),
        grid_spec=grid_spec,
        out_shape=jax.ShapeDtypeStruct((n_blocks * MOE_TM * TOK_ROWS, LANE), F32),
        compiler_params=_params(("arbitrary",)),
    )(tok_slots, pad_start, pad_count, n_used, x1t)


def _moe_kernel(blk0_ref, nblk_ref, nused_ref, xst_hbm, g2_ref, wg_ref, wu_ref, wd_ref,
                yt_hbm, xbuf, ybuf, gsem, osem, wg_b, wu_b, wd_b):
    e = pl.program_id(0)
    n_used = nused_ref[0]
    slab = MOE_TM * TOK_ROWS

    def in_copy(blk):
        to_slot = lax.rem(blk, MOE_IN_BUFS)
        return pltpu.make_async_copy(
            xst_hbm.at[pl.ds(pl.multiple_of(blk * slab, slab), slab), :], xbuf.at[to_slot],
            gsem.at[to_slot])

    def out_copy(blk, from_slot):
        return pltpu.make_async_copy(
            ybuf.at[from_slot], yt_hbm.at[pl.ds(pl.multiple_of(blk * slab, slab), slab), :],
            osem.at[from_slot])

    wg_b[...] = wg_ref[0].astype(BF16)
    wu_b[...] = wu_ref[0].astype(BF16)
    wd_b[...] = wd_ref[0].astype(BF16)

    @pl.when(e == 0)
    def _():
        for first in range(MOE_IN_BUFS - 1):
            @pl.when(first < n_used)
            def _():
                in_copy(first).start()

    def block(b, carry):
        blk = blk0_ref[e] + b
        slot = blk & 1
        ahead = blk + MOE_IN_BUFS - 1

        @pl.when(ahead < n_used)
        def _():
            in_copy(ahead).start()

        in_copy(blk).wait()
        h = _rms(_from_slabs(xbuf.at[lax.rem(blk, MOE_IN_BUFS)], MOE_TM), g2_ref[...]).astype(BF16)
        a = _dot(h, wg_b[...])
        u = _dot(h, wu_b[...])
        hm = (a * jax.nn.sigmoid(a) * u).astype(BF16)
        y = _dot(hm, wd_b[...])

        @pl.when(blk >= 2)
        def _():
            out_copy(blk - 2, slot).wait()

        _to_slabs(ybuf.at[slot], y)
        out_copy(blk, slot).start()
        return carry

    lax.fori_loop(0, nblk_ref[e], block, 0)

    @pl.when(e == pl.num_programs(0) - 1)
    def _():
        @pl.when(n_used >= 2)
        def _():
            out_copy(n_used - 2, (n_used - 2) & 1).wait()
        out_copy(n_used - 1, (n_used - 1) & 1).wait()


def _moe(xst, norm2_g, w_gate_e, w_up_e, w_down_e, blk_start, nblk, n_used):
    w_idx = lambda e, *_: (e, 0, 0)
    slab = MOE_TM * TOK_ROWS
    grid_spec = pltpu.PrefetchScalarGridSpec(
        num_scalar_prefetch=3,
        grid=(N_EXPERTS,),
        in_specs=[
            pl.BlockSpec(memory_space=pl.ANY),
            pl.BlockSpec((1, D_MODEL), lambda e, *_: (0, 0)),
            pl.BlockSpec((1, D_MODEL, D_EXPERT), w_idx),
            pl.BlockSpec((1, D_MODEL, D_EXPERT), w_idx),
            pl.BlockSpec((1, D_EXPERT, D_MODEL), w_idx),
        ],
        out_specs=pl.BlockSpec(memory_space=pl.ANY),
        scratch_shapes=[
            pltpu.VMEM((MOE_IN_BUFS, slab, LANE), F32),
            pltpu.VMEM((2, slab, LANE), F32),
            pltpu.SemaphoreType.DMA((MOE_IN_BUFS,)),
            pltpu.SemaphoreType.DMA((2,)),
            pltpu.VMEM((D_MODEL, D_EXPERT), BF16),
            pltpu.VMEM((D_MODEL, D_EXPERT), BF16),
            pltpu.VMEM((D_EXPERT, D_MODEL), BF16),
        ],
    )
    return pl.pallas_call(
        _moe_kernel,
        grid_spec=grid_spec,
        out_shape=jax.ShapeDtypeStruct(xst.shape, F32),
        input_output_aliases={3: 0},
        compiler_params=_params(("arbitrary",)),
    )(blk_start, nblk, n_used, xst, norm2_g.reshape(1, D_MODEL), w_gate_e, w_up_e, w_down_e)


def _combine_ple_kernel(slots_ref, x1_ref, route_ref, yt_hbm, p_ref, g3_ref, wpg_ref, wple_ref,
                        out_ref, ybuf, sem, *, tm, row0):
    i = pl.program_id(0)
    n_steps = pl.num_programs(0)
    slot = i & 1
    slab = tm * TOK_ROWS

    def start_gather(step, to_slot):
        base = (row0 + step * tm) * TOP_K

        def body(r, carry):
            for k in range(TOP_K):
                src = slots_ref[base + r * TOP_K + k]
                pltpu.make_async_copy(
                    yt_hbm.at[pl.ds(pl.multiple_of(src * TOK_ROWS, TOK_ROWS), TOK_ROWS), :],
                    ybuf.at[to_slot, k, pl.ds(pl.multiple_of(r * TOK_ROWS, TOK_ROWS), TOK_ROWS), :],
                    sem.at[to_slot]).start(priority=k)
            return carry
        lax.fori_loop(0, tm, body, 0, unroll=4)

    @pl.when(i == 0)
    def _():
        start_gather(0, 0)

    @pl.when(i + 1 < n_steps)
    def _():
        start_gather(i + 1, 1 - slot)

    for k in range(TOP_K):
        pltpu.make_async_copy(yt_hbm.at[pl.ds(0, slab), :], ybuf.at[slot, k], sem.at[slot]).wait()
    route = route_ref[...]
    x2 = x1_ref[...] + (route[:, 2:3] * _from_slabs(ybuf.at[slot, 0], tm)
                        + route[:, 3:4] * _from_slabs(ybuf.at[slot, 1], tm))
    h = _rms(x2, g3_ref[...]).astype(BF16)
    gate = jax.nn.sigmoid(_dot(h, wpg_ref[...]))
    out_ref[...] = x2 + gate * _dot(p_ref[...].astype(BF16), wple_ref[...])


def _combine_ple(x1_all, route_all, y_slots, tok_slots, p, norm3_g, wpg, wple, m, tm, row0):
    blk0 = row0 // tm
    grid_spec = pltpu.PrefetchScalarGridSpec(
        num_scalar_prefetch=1,
        grid=(m // tm,),
        in_specs=[
            pl.BlockSpec((tm, D_MODEL), lambda i, s: (blk0 + i, 0)),
            pl.BlockSpec((tm, LANE), lambda i, s: (blk0 + i, 0)),
            pl.BlockSpec(memory_space=pl.ANY),
            pl.BlockSpec((tm, PLE_DIM), lambda i, s: (i, 0)),
            pl.BlockSpec((1, D_MODEL), lambda i, s: (0, 0)),
            pl.BlockSpec((D_MODEL, D_MODEL), lambda i, s: (0, 0)),
            pl.BlockSpec((PLE_DIM, D_MODEL), lambda i, s: (0, 0)),
        ],
        out_specs=pl.BlockSpec((tm, D_MODEL), lambda i, s: (i, 0)),
        scratch_shapes=[
            pltpu.VMEM((2, TOP_K, tm * TOK_ROWS, LANE), F32),
            pltpu.SemaphoreType.DMA((2,)),
        ],
    )
    return pl.pallas_call(
        functools.partial(_combine_ple_kernel, tm=tm, row0=row0),
        grid_spec=grid_spec,
        out_shape=jax.ShapeDtypeStruct((m, D_MODEL), F32),
        compiler_params=_params(("arbitrary",)),
    )(tok_slots, x1_all, route_all, y_slots, p, norm3_g.reshape(1, D_MODEL), wpg, wple)


def _routing_tables(route_all):
    flat_e = route_all[:, 0:TOP_K].astype(jnp.int32).reshape(-1)
    n_asg = flat_e.shape[0]
    onehot = (flat_e[:, None] == jnp.arange(N_EXPERTS, dtype=jnp.int32)[None, :]).astype(F32)
    chunk = math.gcd(n_asg, 2 * LANE)
    within = jnp.einsum('ij,cjk->cik', jnp.tril(jnp.ones((chunk, chunk), F32)),
                        onehot.reshape(n_asg // chunk, chunk, N_EXPERTS))
    totals = within[:, -1, :]
    before = jnp.cumsum(totals, axis=0) - totals
    csum = (within + before[:, None, :]).reshape(n_asg, N_EXPERTS)
    rank = jnp.sum(csum * onehot, axis=1).astype(jnp.int32) - 1
    counts = (before[-1] + totals[-1]).astype(jnp.int32)
    nblk = (counts + MOE_TM - 1) // MOE_TM
    blk_end = jnp.cumsum(nblk)
    blk_start = blk_end - nblk
    first_blk = jnp.sum(onehot * blk_start.astype(F32)[None, :], axis=1).astype(jnp.int32)
    slot = first_blk * MOE_TM + rank
    n_used = blk_end[-1].reshape(1)
    pad_start = blk_start * MOE_TM + counts
    pad_count = nblk * MOE_TM - counts
    i32 = lambda a: a.astype(jnp.int32)
    return i32(blk_start), i32(nblk), i32(n_used), i32(slot), i32(pad_start), i32(pad_count)


def kernel(x_prompt, x_sample, p_prompt, p_sample, cache_k_g0, cache_v_g0, cache_k_g1, cache_v_g1,
           cache_k_g2, cache_v_g2, state_conv, norm1_g, w_in, q_gain, k_gain, conv_w, w_conv_out,
           w_attn_out, w_o, norm2_g, w_coarse, b_coarse, w_fine, b_fine, w_gate_e, w_up_e, w_down_e,
           norm3_g, w_ple_gate, w_ple):
    seq = x_prompt.shape[1]
    assert x_prompt.shape == (1, seq, D_MODEL) and norm1_g.shape[0] == 1, "one prompt, one layer"
    n_seq, t_new, _ = x_sample.shape
    assert seq % ATT_UNIT == 0 and t_new == SUBLANE
    caches_k = (cache_k_g0, cache_k_g1, cache_k_g2)
    caches_v = (cache_v_g0, cache_v_g1, cache_v_g2)
    for g in range(N_GROUPS):
        assert caches_k[g].shape == (1, n_seq, WINDOWS[g], HEADS_PER_GROUP, HEAD_DIM)
    n_s = n_seq * t_new
    n_all = seq + n_s

    xp = x_prompt[0]
    xs = x_sample.reshape(n_s, D_MODEL)
    wc = w_conv_out[0].astype(BF16)
    wa = w_attn_out[0].astype(BF16)
    wo = w_o[0].astype(BF16)
    wpg = w_ple_gate[0].astype(BF16)
    wple = w_ple[0].astype(BF16)
    wr = jnp.concatenate(
        [w_coarse[0], jnp.transpose(w_fine[0], (1, 0, 2)).reshape(D_MODEL, N_EXPERTS)], axis=1)
    wr = jnp.pad(wr, ((0, 0), (0, LANE - wr.shape[1])))
    wr_hi = wr.astype(BF16)
    wr_lo = (wr - wr_hi.astype(F32)).astype(BF16)
    br = jnp.pad(jnp.concatenate([b_coarse[0], b_fine[0].reshape(-1)]), (0, LANE - 36)).reshape(1, LANE)

    z_p, z_s = _inproj(xp, xs, norm1_g[0], w_in[0], tm=2048, tn=512)

    att_p = _prompt_attn(z_p, q_gain[0], k_gain[0])
    o_p, pks, pvs = att_p[0], att_p[1:4], att_p[4:7]
    ck = [c.reshape(n_seq, WINDOWS[g] * HEADS_PER_GROUP, HEAD_DIM) for g, c in enumerate(caches_k)]
    cv = [c.reshape(n_seq, WINDOWS[g] * HEADS_PER_GROUP, HEAD_DIM) for g, c in enumerate(caches_v)]
    att_s = _sample_attn(z_s, ck, cv, q_gain[0], k_gain[0], n_seq, t_new)
    o_s, sks, svs = att_s[0], att_s[1:4], att_s[4:7]

    mg_p, u_tail = _merge(z_p, o_p, conv_w[0], wc, wa, tm=512)
    prefix = jnp.pad(state_conv[0], ((0, 0), (0, t_new - 2), (0, 0))).reshape(n_s, D_CONV)
    mg_s, u_s = _merge(z_s, o_s, conv_w[0], wc, wa, tm=n_s, per_seq=t_new, prefix=prefix)
    x1_all, x1t_all, route_all = _outproj_router(xp, mg_p, xs, mg_s, wo, norm2_g[0], wr_hi, wr_lo, br)

    n_blocks = n_all * TOP_K // MOE_TM + N_EXPERTS
    blk_start, nblk, n_used, tok_slots, pad_start, pad_count = _routing_tables(route_all)
    xst = _dispatch(x1t_all, tok_slots, pad_start, pad_count, n_used, n_blocks, tm=n_s)
    y_slots = _moe(xst, norm2_g[0], w_gate_e[0], w_up_e[0], w_down_e[0], blk_start, nblk, n_used)

    y_p = _combine_ple(x1_all, route_all, y_slots, tok_slots, p_prompt[0, 0], norm3_g[0], wpg, wple,
                       m=seq, tm=256, row0=0)
    y_s = _combine_ple(x1_all, route_all, y_slots, tok_slots, p_sample[0].reshape(n_s, PLE_DIM),
                       norm3_g[0], wpg, wple, m=n_s, tm=n_s, row0=seq)

    def state(a, n, g):
        return a.reshape(1, n, WINDOWS[g], HEADS_PER_GROUP, HEAD_DIM)

    pk = [state(a, 1, g) for g, a in enumerate(pks)]
    pv = [state(a, 1, g) for g, a in enumerate(pvs)]
    sk = [state(a, n_seq, g) for g, a in enumerate(sks)]
    sv = [state(a, n_seq, g) for g, a in enumerate(svs)]
    pconv = u_tail[-2:].reshape(1, 1, 2, D_CONV)
    sconv = u_s.reshape(n_seq, t_new, D_CONV)[:, t_new - 2:].reshape(1, n_seq, 2, D_CONV)
    return (y_p.reshape(1, seq, D_MODEL), y_s.reshape(n_seq, t_new, D_MODEL),
            pk[0], pv[0], pk[1], pv[1], pk[2], pv[2], pconv,
            sk[0], sv[0], sk[1], sv[1], sk[2], sv[2], sconv)
```

```python
import functools
import math

import jax
import jax.numpy as jnp
from jax import lax
from jax.experimental import pallas as pl
from jax.experimental.pallas import tpu as pltpu

D_MODEL = 2048
HEAD_DIM = 128
HEADS_PER_GROUP = 4
WINDOWS = (128, 512, 2048)
DILATIONS = (1, 4, 16)
N_GROUPS = 3
N_HEADS = N_GROUPS * HEADS_PER_GROUP
ATT_WIDTH = N_HEADS * HEAD_DIM
ATT_OUT_WIDTH = HEADS_PER_GROUP * HEAD_DIM
SCALE = HEAD_DIM ** -0.5
D_CONV = D_MODEL // 2
PLE_DIM = 256
N_EXPERT_GROUPS = 4
EXPERTS_PER_GROUP = 8
N_EXPERTS = N_EXPERT_GROUPS * EXPERTS_PER_GROUP
TOP_K = 2
D_EXPERT = D_MODEL // 4
EPS = 1e-6

COL_B = 0
COL_C = D_CONV
COL_H = 2 * D_CONV
COL_Q = 3 * D_CONV
COL_K = COL_Q + ATT_WIDTH
COL_V = COL_K + ATT_WIDTH
COL_GC = COL_V + ATT_WIDTH
COL_GA = COL_GC + D_MODEL
IN_COLS = COL_GA + D_MODEL

LANE = 128
SUBLANE = 8
NEG = -1e30
ATT_UNIT = max(WINDOWS)
ATT_BLK = 128
ATT_TILE_UNROLL = 16
MXU_N = 256
TOK_ROWS = D_MODEL // LANE
MOE_TM = 288
MOE_IN_BUFS = 4
VMEM_LIMIT = 56 * 1024 * 1024

BF16 = jnp.bfloat16
F32 = jnp.float32


def _params(sem):
    return pltpu.CompilerParams(dimension_semantics=sem, vmem_limit_bytes=VMEM_LIMIT)


def _rms(x, gain):
    return x * lax.rsqrt(jnp.mean(x * x, axis=-1, keepdims=True) + EPS) * gain


def _dot(a, b):
    return jnp.dot(a, b, preferred_element_type=F32)


def _dot_nt(a, b):
    return lax.dot_general(a, b, (((1,), (1,)), ((), ())), preferred_element_type=F32)


def _inproj_kernel(x_ref, xs_ref, g_ref, w_ref, z_ref, zs_ref, h_scr, hs_scr):
    i, j = pl.program_id(0), pl.program_id(1)
    with_sample = i == pl.num_programs(0) - 1

    @pl.when(j == 0)
    def _():
        h_scr[...] = _rms(x_ref[...], g_ref[...]).astype(BF16)

    @pl.when(with_sample & (j == 0))
    def _():
        hs_scr[...] = _rms(xs_ref[...], g_ref[...]).astype(BF16)

    def project(sample_too):
        for c in range(w_ref.shape[1] // MXU_N):
            cols = slice(c * MXU_N, (c + 1) * MXU_N)
            w_c = w_ref[:, cols].astype(BF16)
            z_ref[:, cols] = _dot(h_scr[...], w_c)
            if sample_too:
                zs_ref[:, cols] = _dot(hs_scr[...], w_c)

    @pl.when(jnp.logical_not(with_sample))
    def _():
        project(False)

    @pl.when(with_sample)
    def _():
        project(True)


def _inproj(x, xs, norm_g, w_in, tm, tn):
    m, m_s = x.shape[0], xs.shape[0]
    n_i = m // tm
    return pl.pallas_call(
        _inproj_kernel,
        grid=(n_i, IN_COLS // tn),
        in_specs=[
            pl.BlockSpec((tm, D_MODEL), lambda i, j: (i, 0), pipeline_mode=pl.Buffered(1)),
            pl.BlockSpec((m_s, D_MODEL), lambda i, j: (0, 0), pipeline_mode=pl.Buffered(1)),
            pl.BlockSpec((1, D_MODEL), lambda i, j: (0, 0)),
            pl.BlockSpec((D_MODEL, tn), lambda i, j: (0, j)),
        ],
        out_specs=[
            pl.BlockSpec((tm, tn), lambda i, j: (i, j)),
            pl.BlockSpec((m_s, tn), lambda i, j: (0, jnp.where(i == n_i - 1, j, 0))),
        ],
        out_shape=[jax.ShapeDtypeStruct((m, IN_COLS), F32),
                   jax.ShapeDtypeStruct((m_s, IN_COLS), F32)],
        scratch_shapes=[pltpu.VMEM((tm, D_MODEL), BF16), pltpu.VMEM((m_s, D_MODEL), BF16)],
        compiler_params=_params(("arbitrary", "arbitrary")),
    )(x, xs, norm_g.reshape(1, D_MODEL), w_in)


def _prompt_attn_kernel(*refs):
    (q0, q1, q2, kc0, kc1, kc2, vc0, vc1, vc2, kp0, kp1, kp2, vp0, vp1, vp2, qg_ref, kg_ref,
     o_ref, pk0, pk1, pk2, pv0, pv1, pv2, kext, vext, acc_s, m_s, l_s) = refs
    q_refs, kc_refs, vc_refs = (q0, q1, q2), (kc0, kc1, kc2), (vc0, vc1, vc2)
    kp_refs, vp_refs = (kp0, kp1, kp2), (vp0, vp1, vp2)
    pk_refs, pv_refs = (pk0, pk1, pk2), (pv0, pv1, pv2)
    n = pl.program_id(1)
    last = pl.num_programs(1) - 1

    qi = lax.broadcasted_iota(jnp.int32, (ATT_BLK, 2 * ATT_BLK), 0) + ATT_BLK
    ki = lax.broadcasted_iota(jnp.int32, (ATT_BLK, 2 * ATT_BLK), 1)
    dist = qi - ki
    band = (dist >= 0) & (dist <= ATT_BLK)

    for g in range(N_GROUPS):
        win, dil = WINDOWS[g], DILATIONS[g]
        q_ref = q_refs[g]
        qgain = qg_ref[g, 0]
        kgain = kg_ref[g, 0]
        kext[0:win, :] = _rms(kp_refs[g][...], kgain)
        kext[win:win + ATT_UNIT, :] = _rms(kc_refs[g][...], kgain)
        vext[0:win, :] = vp_refs[g][...]
        vext[win:win + ATT_UNIT, :] = vc_refs[g][...]

        @pl.when(n == last)
        def _():
            pk_refs[g][...] = kext[ATT_UNIT:ATT_UNIT + win, :]
            pv_refs[g][...] = vext[ATT_UNIT:ATT_UNIT + win, :]

        def tile(t, carry, g=g, win=win, dil=dil, q_ref=q_ref, qgain=qgain):
            u = t // dil
            r = t % dil
            off = u * win + r
            if dil == 1:
                off = pl.multiple_of(off, ATT_BLK)
                rows_q = pl.ds(off, ATT_BLK)
                rows_k = pl.ds(off, 2 * ATT_BLK)
            else:
                rows_q = pl.ds(off, ATT_BLK, stride=dil)
                rows_k = pl.ds(off, 2 * ATT_BLK, stride=dil)
            q = _rms(q_ref[rows_q, :], qgain).astype(BF16)
            k = kext[rows_k, :].astype(BF16)
            v = vext[rows_k, :].astype(BF16)
            s = _dot_nt(q, k) * SCALE
            k_min = jnp.where((n > 0) | (u > 0), 0, ATT_BLK)
            s = jnp.where(band & (ki >= k_min), s, NEG)
            m_t = jnp.max(s, axis=-1, keepdims=True)
            p = jnp.exp(s - m_t)
            l_t = jnp.sum(p, axis=-1, keepdims=True)
            m_s[g, rows_q, :] = jnp.broadcast_to(m_t, (ATT_BLK, LANE))
            l_s[g, rows_q, :] = jnp.broadcast_to(l_t, (ATT_BLK, LANE))
            acc_s[g, rows_q, :] = _dot(p.astype(BF16), v)
            return carry

        lax.fori_loop(0, ATT_UNIT // ATT_BLK, tile, 0, unroll=ATT_TILE_UNROLL)

    def finish(c, carry):
        rows = pl.ds(pl.multiple_of(c * ATT_BLK, ATT_BLK), ATT_BLK)
        m = [m_s[g, rows, :] for g in range(N_GROUPS)]
        m_all = jnp.maximum(jnp.maximum(m[0], m[1]), m[2])
        num = jnp.zeros((ATT_BLK, LANE), F32)
        den = jnp.zeros((ATT_BLK, LANE), F32)
        for g in range(N_GROUPS):
            w = jnp.exp(m[g] - m_all)
            num = num + w * acc_s[g, rows, :]
            den = den + w * l_s[g, rows, :]
        o_ref[rows, :] = num / den
        return carry

    lax.fori_loop(0, ATT_UNIT // ATT_BLK, finish, 0, unroll=2)


def _prompt_attn(z, q_gain, k_gain):
    s_len = z.shape[0]
    n_units = s_len // ATT_UNIT
    in_specs, args = [], []

    def col(base, g):
        return lambda hh, n: (n, base // LANE + g * HEADS_PER_GROUP + hh)

    for g in range(N_GROUPS):
        in_specs.append(pl.BlockSpec((ATT_UNIT, LANE), col(COL_Q, g)))
    for base in (COL_K, COL_V):
        for g in range(N_GROUPS):
            in_specs.append(pl.BlockSpec((ATT_UNIT, LANE), col(base, g)))
    for base in (COL_K, COL_V):
        for g in range(N_GROUPS):
            per = ATT_UNIT // WINDOWS[g]
            in_specs.append(pl.BlockSpec(
                (WINDOWS[g], LANE),
                lambda hh, n, base=base, g=g, per=per: (
                    jnp.maximum(n * per - 1, 0), base // LANE + g * HEADS_PER_GROUP + hh)))
    args = [z] * 15
    gain_spec = pl.BlockSpec((N_GROUPS, 1, 1, LANE), lambda hh, n: (0, hh, 0, 0))
    in_specs += [gain_spec, gain_spec]
    args += [q_gain.reshape(N_GROUPS, HEADS_PER_GROUP, 1, LANE),
             k_gain.reshape(N_GROUPS, HEADS_PER_GROUP, 1, LANE)]

    out_specs = [pl.BlockSpec((ATT_UNIT, LANE), lambda hh, n: (n, hh))]
    out_shape = [jax.ShapeDtypeStruct((s_len, ATT_OUT_WIDTH), F32)]
    for _ in range(2):
        for g in range(N_GROUPS):
            out_specs.append(pl.BlockSpec((WINDOWS[g], LANE), lambda hh, n: (0, hh)))
            out_shape.append(jax.ShapeDtypeStruct((WINDOWS[g], ATT_OUT_WIDTH), F32))

    return pl.pallas_call(
        _prompt_attn_kernel,
        grid=(HEADS_PER_GROUP, n_units),
        in_specs=in_specs,
        out_specs=out_specs,
        out_shape=out_shape,
        scratch_shapes=[
            pltpu.VMEM((2 * ATT_UNIT, LANE), F32),
            pltpu.VMEM((2 * ATT_UNIT, LANE), F32),
            pltpu.VMEM((N_GROUPS, ATT_UNIT, LANE), F32),
            pltpu.VMEM((N_GROUPS, ATT_UNIT, LANE), F32),
            pltpu.VMEM((N_GROUPS, ATT_UNIT, LANE), F32),
        ],
        compiler_params=_params(("parallel", "arbitrary")),
    )(*args)


def _sample_attn_kernel(*refs):
    (q0, q1, q2, kn0, kn1, kn2, vn0, vn1, vn2, ck0, ck1, ck2, cv0, cv1, cv2, qg_ref, kg_ref,
     o_ref, sk0, sk1, sk2, sv0, sv1, sv2) = refs
    q_refs, kn_refs, vn_refs = (q0, q1, q2), (kn0, kn1, kn2), (vn0, vn1, vn2)
    ck_refs, cv_refs = (ck0, ck1, ck2), (cv0, cv1, cv2)
    sk_refs, sv_refs = (sk0, sk1, sk2), (sv0, sv1, sv2)
    t_new = q0.shape[0]
    nh = HEADS_PER_GROUP
    n_q = nh * t_new
    t_bits = t_new.bit_length() - 1

    def heads(ref):
        return [ref[:, h * LANE:(h + 1) * LANE] for h in range(nh)]

    parts = []
    for g in range(N_GROUPS):
        win, dil = WINDOWS[g], DILATIONS[g]
        q_all = jnp.concatenate(
            [_rms(x, qg_ref[g, h]) for h, x in enumerate(heads(q_refs[g]))], axis=0).astype(BF16)
        k_new = [_rms(x, kg_ref[g, h]) for h, x in enumerate(heads(kn_refs[g]))]
        v_new = heads(vn_refs[g])
        k_new_all = jnp.concatenate(k_new, axis=0).astype(BF16)
        v_new_all = jnp.concatenate(v_new, axis=0).astype(BF16)
        ck = ck_refs[g][0]
        cv = cv_refs[g][0]
        old = (win - t_new) * nh
        sk_refs[g][0, 0:old, :] = ck[t_new * nh:, :]
        sv_refs[g][0, 0:old, :] = cv[t_new * nh:, :]
        for h in range(nh):
            sk_refs[g][0, pl.ds(old + h, t_new, stride=nh), :] = k_new[h]
            sv_refs[g][0, pl.ds(old + h, t_new, stride=nh), :] = v_new[h]

        s_c = _dot_nt(q_all, ck.astype(BF16)) * SCALE
        s_n = _dot_nt(q_all, k_new_all) * SCALE
        row = lax.broadcasted_iota(jnp.int32, (n_q, win * nh), 0)
        col = lax.broadcasted_iota(jnp.int32, (n_q, win * nh), 1)
        i_q, pos = row & (t_new - 1), col >> 2
        mask_c = (((col & (nh - 1)) == (row >> t_bits)) & (pos >= i_q)
                  & (((pos - i_q) & (dil - 1)) == 0))
        row_n = lax.broadcasted_iota(jnp.int32, (n_q, n_q), 0)
        col_n = lax.broadcasted_iota(jnp.int32, (n_q, n_q), 1)
        d_n = (row_n & (t_new - 1)) - (col_n & (t_new - 1))
        mask_n = ((row_n >> t_bits) == (col_n >> t_bits)) & (d_n >= 0) & ((d_n & (dil - 1)) == 0)
        s_c = jnp.where(mask_c, s_c, NEG)
        s_n = jnp.where(mask_n, s_n, NEG)
        m = jnp.maximum(jnp.max(s_c, axis=-1, keepdims=True), jnp.max(s_n, axis=-1, keepdims=True))
        p_c = jnp.exp(s_c - m)
        p_n = jnp.exp(s_n - m)
        l = jnp.sum(p_c, axis=-1, keepdims=True) + jnp.sum(p_n, axis=-1, keepdims=True)
        acc = _dot(p_c.astype(BF16), cv.astype(BF16)) + _dot(p_n.astype(BF16), v_new_all)
        parts.append((m, l, acc))

    m_all = jnp.maximum(jnp.maximum(parts[0][0], parts[1][0]), parts[2][0])
    num = jnp.zeros((n_q, LANE), F32)
    den = jnp.zeros((n_q, 1), F32)
    for m, l, acc in parts:
        w = jnp.exp(m - m_all)
        num = num + w * acc
        den = den + w * l
    o_all = num / den
    for h in range(nh):
        o_ref[:, h * LANE:(h + 1) * LANE] = o_all[h * t_new:(h + 1) * t_new, :]


def _sample_attn(z, caches_k, caches_v, q_gain, k_gain, n_seq, t_new):
    assert HEADS_PER_GROUP == 4 and t_new & (t_new - 1) == 0
    in_specs, args = [], []
    for base in (COL_Q, COL_K, COL_V):
        for g in range(N_GROUPS):
            in_specs.append(pl.BlockSpec(
                (t_new, ATT_OUT_WIDTH), lambda b, base=base, g=g: (b, base // ATT_OUT_WIDTH + g)))
            args.append(z)
    for caches in (caches_k, caches_v):
        for g in range(N_GROUPS):
            in_specs.append(pl.BlockSpec((1, WINDOWS[g] * HEADS_PER_GROUP, LANE), lambda b: (b, 0, 0)))
            args.append(caches[g])
    gain_spec = pl.BlockSpec((N_GROUPS, HEADS_PER_GROUP, 1, LANE), lambda b: (0, 0, 0, 0))
    in_specs += [gain_spec, gain_spec]
    args += [q_gain.reshape(N_GROUPS, HEADS_PER_GROUP, 1, LANE),
             k_gain.reshape(N_GROUPS, HEADS_PER_GROUP, 1, LANE)]

    out_specs = [pl.BlockSpec((t_new, ATT_OUT_WIDTH), lambda b: (b, 0))]
    out_shape = [jax.ShapeDtypeStruct((n_seq * t_new, ATT_OUT_WIDTH), F32)]
    for _ in range(2):
        for g in range(N_GROUPS):
            rows = WINDOWS[g] * HEADS_PER_GROUP
            out_specs.append(pl.BlockSpec((1, rows, LANE), lambda b: (b, 0, 0)))
            out_shape.append(jax.ShapeDtypeStruct((n_seq, rows, LANE), F32))

    return pl.pallas_call(
        _sample_attn_kernel,
        grid=(n_seq,),
        in_specs=in_specs,
        out_specs=out_specs,
        out_shape=out_shape,
        compiler_params=_params(("parallel",)),
    )(*args)


def _conv_taps(u, r1, r2, cw_ref):
    return cw_ref[2:3, :] * u + cw_ref[1:2, :] * r1 + cw_ref[0:1, :] * r2


def _merge_kernel(*refs, per_seq, n_chunks):
    b_ref, c_ref, h_ref, pc_ref, ph_ref, cw_ref, o_ref = refs[:7]
    gc_refs = refs[7:7 + n_chunks]
    ga_refs = refs[7 + n_chunks:7 + 2 * n_chunks]
    wc_ref, wa_ref, out_ref, u_ref, yb_scr = refs[7 + 2 * n_chunks:]
    i = pl.program_id(0)
    tm = b_ref.shape[0]
    u = c_ref[...] * h_ref[...]
    r1 = pltpu.roll(u, 1, axis=0)
    r2 = pltpu.roll(u, 2, axis=0)
    if per_seq is None:
        yb_scr[...] = b_ref[...] * _conv_taps(u, r1, r2, cw_ref)
        up = jnp.where(i > 0, pc_ref[...] * ph_ref[...], 0.0)
        row = lax.broadcasted_iota(jnp.int32, (SUBLANE, D_CONV), 0)
        u8 = u[0:SUBLANE, :]
        r1_8 = jnp.where(row < 1, pltpu.roll(up, 1, axis=0), pltpu.roll(u8, 1, axis=0))
        r2_8 = jnp.where(row < 2, pltpu.roll(up, 2, axis=0), pltpu.roll(u8, 2, axis=0))
        yb_scr[0:SUBLANE, :] = b_ref[0:SUBLANE, :] * _conv_taps(u8, r1_8, r2_8, cw_ref)
        u_ref[...] = u[tm - SUBLANE:tm, :]
    else:
        pre = pc_ref[...]
        t = lax.broadcasted_iota(jnp.int32, (tm, D_CONV), 0) & (per_seq - 1)
        r1 = jnp.where(t == 0, pltpu.roll(pre, tm - 1, axis=0), r1)
        r2 = jnp.where(t < 2, pre, r2)
        yb_scr[...] = b_ref[...] * _conv_taps(u, r1, r2, cw_ref)
        u_ref[...] = u

    yb = yb_scr[...].astype(BF16)
    ob = o_ref[...].astype(BF16)
    tn = D_MODEL // n_chunks
    for c in range(n_chunks):
        cols = slice(c * tn, (c + 1) * tn)
        y_conv = _dot(yb, wc_ref[:, cols])
        y_att = _dot(ob, wa_ref[:, cols])
        out_ref[:, cols] = (jax.nn.sigmoid(gc_refs[c][...]) * y_conv
                            + jax.nn.sigmoid(ga_refs[c][...]) * y_att).astype(BF16)


def _merge(z, o_att, conv_w, wc, wa, tm, per_seq=None, prefix=None):
    m = z.shape[0]
    tn = ATT_OUT_WIDTH
    n_chunks = D_MODEL // tn
    if per_seq is None:
        prev = lambda i, c: (jnp.maximum(i * (tm // SUBLANE) - 1, 0), c)
        pc_spec = pl.BlockSpec((SUBLANE, D_CONV), lambda i: prev(i, COL_C // D_CONV))
        ph_spec = pl.BlockSpec((SUBLANE, D_CONV), lambda i: prev(i, COL_H // D_CONV))
        pc_arg, ph_arg = z, z
        u_rows = SUBLANE
    else:
        pc_spec = pl.BlockSpec((tm, D_CONV), lambda i: (i, 0))
        ph_spec = pl.BlockSpec((SUBLANE, D_CONV), lambda i: (0, 0))
        pc_arg, ph_arg = prefix, prefix
        u_rows = tm
    gate_specs = [pl.BlockSpec((tm, tn), lambda i, base=base, c=c: (i, base // tn + c))
                  for base in (COL_GC, COL_GA) for c in range(n_chunks)]
    return pl.pallas_call(
        functools.partial(_merge_kernel, per_seq=per_seq, n_chunks=n_chunks),
        grid=(m // tm,),
        in_specs=[
            pl.BlockSpec((tm, D_CONV), lambda i: (i, COL_B // D_CONV)),
            pl.BlockSpec((tm, D_CONV), lambda i: (i, COL_C // D_CONV)),
            pl.BlockSpec((tm, D_CONV), lambda i: (i, COL_H // D_CONV)),
            pc_spec,
            ph_spec,
            pl.BlockSpec((3, D_CONV), lambda i: (0, 0)),
            pl.BlockSpec((tm, ATT_OUT_WIDTH), lambda i: (i, 0)),
            *gate_specs,
            pl.BlockSpec((D_CONV, D_MODEL), lambda i: (0, 0)),
            pl.BlockSpec((ATT_OUT_WIDTH, D_MODEL), lambda i: (0, 0)),
        ],
        out_specs=[
            pl.BlockSpec((tm, D_MODEL), lambda i: (i, 0)),
            pl.BlockSpec((u_rows, D_CONV), lambda i: (i, 0)),
        ],
        out_shape=[
            jax.ShapeDtypeStruct((m, D_MODEL), BF16),
            jax.ShapeDtypeStruct((m // tm * u_rows, D_CONV), F32),
        ],
        scratch_shapes=[pltpu.VMEM((tm, D_CONV), F32)],
        compiler_params=_params(("parallel",)),
    )(z, z, z, pc_arg, ph_arg, conv_w, o_att, *([z] * (2 * n_chunks)), wc, wa)


def _outproj_router_kernel(xp_ref, mgp_ref, xs_ref, mgs_ref, wo_ref, g2_ref, wr_hi_ref, wr_lo_ref,
                           br_ref, x1_ref, x1t_ref, route_ref):
    is_sample = pl.program_id(0) == pl.num_programs(0) - 1

    @pl.when(jnp.logical_not(is_sample))
    def _():
        _outproj_router_tile(xp_ref, mgp_ref, wo_ref, g2_ref, wr_hi_ref, wr_lo_ref, br_ref,
                             x1_ref, x1t_ref, route_ref)

    @pl.when(is_sample)
    def _():
        _outproj_router_tile(xs_ref, mgs_ref, wo_ref, g2_ref, wr_hi_ref, wr_lo_ref, br_ref,
                             x1_ref, x1t_ref, route_ref)


def _outproj_router_tile(x_ref, mg_ref, wo_ref, g2_ref, wr_hi_ref, wr_lo_ref, br_ref, x1_ref,
                         x1t_ref, route_ref):
    x1 = x_ref[...] + _dot(mg_ref[...], wo_ref[...])
    x1_ref[...] = x1
    _to_slabs(x1t_ref, x1)
    h = _rms(x1, g2_ref[...])
    h_hi = h.astype(BF16)
    h_lo = (h - h_hi.astype(F32)).astype(BF16)
    logits = (_dot(h_hi, wr_hi_ref[...]) + _dot(h_hi, wr_lo_ref[...]) + _dot(h_lo, wr_hi_ref[...])
              + br_ref[...])
    lane = lax.broadcasted_iota(jnp.int32, logits.shape, 1)
    is_coarse = lane < N_EXPERT_GROUPS
    coarse = jnp.where(is_coarse, logits, NEG)
    cmax = jnp.max(coarse, axis=-1, keepdims=True)
    grp = jnp.min(jnp.where(coarse == cmax, lane, LANE), axis=-1, keepdims=True)
    p_grp = 1.0 / jnp.sum(jnp.where(is_coarse, jnp.exp(coarse - cmax), 0.0), axis=-1, keepdims=True)
    eid = lane - N_EXPERT_GROUPS
    in_grp = (eid >= 0) & (eid < N_EXPERTS) & ((eid >> 3) == grp)
    fine = jnp.where(in_grp, logits, NEG)
    v1 = jnp.max(fine, axis=-1, keepdims=True)
    i1 = jnp.min(jnp.where(fine == v1, lane, LANE), axis=-1, keepdims=True)
    fine2 = jnp.where(lane == i1, NEG, fine)
    v2 = jnp.max(fine2, axis=-1, keepdims=True)
    i2 = jnp.min(jnp.where(fine2 == v2, lane, LANE), axis=-1, keepdims=True)
    e = jnp.exp(v2 - v1)
    gate1 = p_grp / (1.0 + e)
    gate2 = p_grp * e / (1.0 + e)
    route = jnp.where(lane == 0, (i1 - N_EXPERT_GROUPS).astype(F32),
                      jnp.where(lane == 1, (i2 - N_EXPERT_GROUPS).astype(F32),
                                jnp.where(lane == 2, gate1, jnp.where(lane == 3, gate2, 0.0))))
    route_ref[...] = route


def _outproj_router(xp, mg_p, xs, mg_s, wo, norm2_g, wr_hi, wr_lo, br):
    m_p, tm = xp.shape[0], xs.shape[0]
    assert m_p % tm == 0
    n_p = m_p // tm
    prompt_blk = lambda i: (jnp.minimum(i, n_p - 1), 0)
    fixed = lambda i: (0, 0)
    return pl.pallas_call(
        _outproj_router_kernel,
        grid=(n_p + 1,),
        in_specs=[
            pl.BlockSpec((tm, D_MODEL), prompt_blk),
            pl.BlockSpec((tm, D_MODEL), prompt_blk),
            pl.BlockSpec((tm, D_MODEL), fixed),
            pl.BlockSpec((tm, D_MODEL), fixed),
            pl.BlockSpec((D_MODEL, D_MODEL), fixed),
            pl.BlockSpec((1, D_MODEL), fixed),
            pl.BlockSpec((D_MODEL, LANE), fixed),
            pl.BlockSpec((D_MODEL, LANE), fixed),
            pl.BlockSpec((1, LANE), fixed),
        ],
        out_specs=[
            pl.BlockSpec((tm, D_MODEL), lambda i: (i, 0)),
            pl.BlockSpec((tm * TOK_ROWS, LANE), lambda i: (i, 0)),
            pl.BlockSpec((tm, LANE), lambda i: (i, 0)),
        ],
        out_shape=[
            jax.ShapeDtypeStruct((m_p + tm, D_MODEL), F32),
            jax.ShapeDtypeStruct(((m_p + tm) * TOK_ROWS, LANE), F32),
            jax.ShapeDtypeStruct((m_p + tm, LANE), F32),
        ],
        compiler_params=_params(("arbitrary",)),
    )(xp, mg_p, xs, mg_s, wo, norm2_g.reshape(1, D_MODEL), wr_hi, wr_lo, br)


def _to_slabs(dst_ref, x):
    rows = x.shape[0]
    for s in range(TOK_ROWS):
        dst_ref[pl.ds(s, rows, stride=TOK_ROWS), :] = x[:, s * LANE:(s + 1) * LANE]


def _from_slabs(src_ref, rows):
    return jnp.concatenate(
        [src_ref[pl.ds(s, rows, stride=TOK_ROWS), :] for s in range(TOK_ROWS)], axis=1)


def _dispatch_kernel(slots_ref, pad0_ref, padn_ref, nused_ref, x1t_hbm, xst_hbm, buf, zero, lsem,
                     ssem, zsem, *, tm, n_blocks):
    i = pl.program_id(0)
    n_steps = pl.num_programs(0)
    slab = tm * TOK_ROWS

    def load(step):
        return pltpu.make_async_copy(
            x1t_hbm.at[pl.ds(pl.multiple_of(step * slab, slab), slab), :], buf.at[step % 3],
            lsem.at[step % 3])

    def wait_scatter(step):
        for _ in range(TOP_K):
            pltpu.make_async_copy(buf.at[step % 3], xst_hbm.at[pl.ds(0, slab), :],
                                  ssem.at[step % 3]).wait()

    def pad_rows(start):
        def expert(e, carry):
            def row(j, c):
                dst = pl.multiple_of((pad0_ref[e] + j) * TOK_ROWS, TOK_ROWS)
                cp = pltpu.make_async_copy(zero, xst_hbm.at[pl.ds(dst, TOK_ROWS), :], zsem.at[0])
                if start:
                    cp.start()
                else:
                    cp.wait()
                return c
            return lax.fori_loop(0, padn_ref[e], row, carry)
        lax.fori_loop(0, N_EXPERTS, expert, 0)

    @pl.when(i == 0)
    def _():
        load(0).start()
        zero[...] = jnp.zeros(zero.shape, F32)
        pad_rows(start=True)

    @pl.when(i >= 2)
    def _():
        wait_scatter(i - 2)

    @pl.when(i + 1 < n_steps)
    def _():
        load(i + 1).start()

    load(i).wait()
    cur = i % 3

    def body(r, carry):
        src = buf.at[cur, pl.ds(pl.multiple_of(r * TOK_ROWS, TOK_ROWS), TOK_ROWS), :]
        for k in range(TOP_K):
            dst = pl.multiple_of(slots_ref[(i * tm + r) * TOP_K + k] * TOK_ROWS, TOK_ROWS)
            pltpu.make_async_copy(src, xst_hbm.at[pl.ds(dst, TOK_ROWS), :],
                                  ssem.at[cur]).start(priority=k)
        return carry
    lax.fori_loop(0, tm, body, 0, unroll=4)

    @pl.when(i == n_steps - 1)
    def _():
        @pl.when(i >= 1)
        def _():
            wait_scatter(i - 1)
        wait_scatter(i)
        pad_rows(start=False)
        chunk = math.gcd(tm, MOE_TM)
        rows = chunk * TOK_ROWS
        buf[0, 0:rows, :] = jnp.zeros((rows, LANE), F32)
        first = nused_ref[0] * MOE_TM
        n_tail = (n_blocks - nused_ref[0]) * (MOE_TM // chunk)

        def tail(j, carry, start):
            dst = pl.multiple_of((first + j * chunk) * TOK_ROWS, TOK_ROWS)
            cp = pltpu.make_async_copy(buf.at[0, pl.ds(0, rows), :],
                                       xst_hbm.at[pl.ds(dst, rows), :], zsem.at[0])
            if start:
                cp.start()
            else:
                cp.wait()
            return carry
        lax.fori_loop(0, n_tail, functools.partial(tail, start=True), 0)
        lax.fori_loop(0, n_tail, functools.partial(tail, start=False), 0)


def _dispatch(x1t, tok_slots, pad_start, pad_count, n_used, n_blocks, tm):
    n_tok = x1t.shape[0] // TOK_ROWS
    assert n_tok % tm == 0
    grid_spec = pltpu.PrefetchScalarGridSpec(
        num_scalar_prefetch=4,
        grid=(n_tok // tm,),
        in_specs=[pl.BlockSpec(memory_space=pl.ANY)],
        out_specs=pl.BlockSpec(memory_space=pl.ANY),
        scratch_shapes=[
            pltpu.VMEM((3, tm * TOK_ROWS, LANE), F32),
            pltpu.VMEM((TOK_ROWS, LANE), F32),
            pltpu.SemaphoreType.DMA((3,)),
            pltpu.SemaphoreType.DMA((3,)),
            pltpu.SemaphoreType.DMA((1,)),
        ],
    )
    return pl.pallas_call(
        functools.partial(_dispatch_kernel, tm=tm, n_blocks=n_blocks),
        grid_spec=grid_spec,
        out_shape=jax.ShapeDtypeStruct((n_blocks * MOE_TM * TOK_ROWS, LANE), F32),
        compiler_params=_params(("arbitrary",)),
    )(tok_slots, pad_start, pad_count, n_used, x1t)


def _moe_kernel(blk0_ref, nblk_ref, nused_ref, xst_hbm, g2_ref, wg_ref, wu_ref, wd_ref,
                yt_hbm, xbuf, ybuf, gsem, osem):
    e = pl.program_id(0)
    n_used = nused_ref[0]
    slab = MOE_TM * TOK_ROWS

    def in_copy(blk):
        to_slot = lax.rem(blk, MOE_IN_BUFS)
        return pltpu.make_async_copy(
            xst_hbm.at[pl.ds(pl.multiple_of(blk * slab, slab), slab), :], xbuf.at[to_slot],
            gsem.at[to_slot])

    def out_copy(blk, from_slot):
        return pltpu.make_async_copy(
            ybuf.at[from_slot], yt_hbm.at[pl.ds(pl.multiple_of(blk * slab, slab), slab), :],
            osem.at[from_slot])


    @pl.when(e == 0)
    def _():
        for first in range(MOE_IN_BUFS - 1):
            @pl.when(first < n_used)
            def _():
                in_copy(first).start()

    def block(b, carry):
        blk = blk0_ref[e] + b
        slot = blk & 1
        ahead = blk + MOE_IN_BUFS - 1

        @pl.when(ahead < n_used)
        def _():
            in_copy(ahead).start()

        in_copy(blk).wait()
        h = _rms(_from_slabs(xbuf.at[lax.rem(blk, MOE_IN_BUFS)], MOE_TM), g2_ref[...]).astype(BF16)
        def mm(lhs, w_ref, c):
            return _dot(lhs, w_ref[0, :, c * MXU_N:(c + 1) * MXU_N].astype(BF16))
        a = jnp.concatenate([mm(h, wg_ref, c) for c in range(D_EXPERT // MXU_N)], axis=1)
        u = jnp.concatenate([mm(h, wu_ref, c) for c in range(D_EXPERT // MXU_N)], axis=1)
        hm = (a * jax.nn.sigmoid(a) * u).astype(BF16)
        y = jnp.concatenate([mm(hm, wd_ref, c) for c in range(D_MODEL // MXU_N)], axis=1)

        @pl.when(blk >= 2)
        def _():
            out_copy(blk - 2, slot).wait()

        _to_slabs(ybuf.at[slot], y)
        out_copy(blk, slot).start()
        return carry

    lax.fori_loop(0, nblk_ref[e], block, 0)

    @pl.when(e == pl.num_programs(0) - 1)
    def _():
        @pl.when(n_used >= 2)
        def _():
            out_copy(n_used - 2, (n_used - 2) & 1).wait()
        out_copy(n_used - 1, (n_used - 1) & 1).wait()


def _moe(xst, norm2_g, w_gate_e, w_up_e, w_down_e, blk_start, nblk, n_used):
    w_idx = lambda e, *_: (e, 0, 0)
    slab = MOE_TM * TOK_ROWS
    grid_spec = pltpu.PrefetchScalarGridSpec(
        num_scalar_prefetch=3,
        grid=(N_EXPERTS,),
        in_specs=[
            pl.BlockSpec(memory_space=pl.ANY),
            pl.BlockSpec((1, D_MODEL), lambda e, *_: (0, 0)),
            pl.BlockSpec((1, D_MODEL, D_EXPERT), w_idx),
            pl.BlockSpec((1, D_MODEL, D_EXPERT), w_idx),
            pl.BlockSpec((1, D_EXPERT, D_MODEL), w_idx),
        ],
        out_specs=pl.BlockSpec(memory_space=pl.ANY),
        scratch_shapes=[
            pltpu.VMEM((MOE_IN_BUFS, slab, LANE), F32),
            pltpu.VMEM((2, slab, LANE), F32),
            pltpu.SemaphoreType.DMA((MOE_IN_BUFS,)),
            pltpu.SemaphoreType.DMA((2,)),
        ],
    )
    return pl.pallas_call(
        _moe_kernel,
        grid_spec=grid_spec,
        out_shape=jax.ShapeDtypeStruct(xst.shape, F32),
        input_output_aliases={3: 0},
        compiler_params=_params(("arbitrary",)),
    )(blk_start, nblk, n_used, xst, norm2_g.reshape(1, D_MODEL), w_gate_e, w_up_e, w_down_e)


def _combine_ple_kernel(slots_ref, x1_ref, route_ref, yt_hbm, p_ref, g3_ref, wpg_ref, wple_ref,
                        out_ref, ybuf, sem, *, tm, row0):
    i = pl.program_id(0)
    n_steps = pl.num_programs(0)
    slot = i & 1
    slab = tm * TOK_ROWS

    def start_gather(step, to_slot):
        base = (row0 + step * tm) * TOP_K

        def body(r, carry):
            for k in range(TOP_K):
                src = slots_ref[base + r * TOP_K + k]
                pltpu.make_async_copy(
                    yt_hbm.at[pl.ds(pl.multiple_of(src * TOK_ROWS, TOK_ROWS), TOK_ROWS), :],
                    ybuf.at[to_slot, k, pl.ds(pl.multiple_of(r * TOK_ROWS, TOK_ROWS), TOK_ROWS), :],
                    sem.at[to_slot]).start(priority=k)
            return carry
        lax.fori_loop(0, tm, body, 0, unroll=4)

    @pl.when(i == 0)
    def _():
        start_gather(0, 0)

    @pl.when(i + 1 < n_steps)
    def _():
        start_gather(i + 1, 1 - slot)

    for k in range(TOP_K):
        pltpu.make_async_copy(yt_hbm.at[pl.ds(0, slab), :], ybuf.at[slot, k], sem.at[slot]).wait()
    route = route_ref[...]
    x2 = x1_ref[...] + (route[:, 2:3] * _from_slabs(ybuf.at[slot, 0], tm)
                        + route[:, 3:4] * _from_slabs(ybuf.at[slot, 1], tm))
    h = _rms(x2, g3_ref[...]).astype(BF16)
    gate = jax.nn.sigmoid(_dot(h, wpg_ref[...]))
    out_ref[...] = x2 + gate * _dot(p_ref[...].astype(BF16), wple_ref[...])


def _combine_ple(x1_all, route_all, y_slots, tok_slots, p, norm3_g, wpg, wple, m, tm, row0):
    blk0 = row0 // tm
    grid_spec = pltpu.PrefetchScalarGridSpec(
        num_scalar_prefetch=1,
        grid=(m // tm,),
        in_specs=[
            pl.BlockSpec((tm, D_MODEL), lambda i, s: (blk0 + i, 0)),
            pl.BlockSpec((tm, LANE), lambda i, s: (blk0 + i, 0)),
            pl.BlockSpec(memory_space=pl.ANY),
            pl.BlockSpec((tm, PLE_DIM), lambda i, s: (i, 0)),
            pl.BlockSpec((1, D_MODEL), lambda i, s: (0, 0)),
            pl.BlockSpec((D_MODEL, D_MODEL), lambda i, s: (0, 0)),
            pl.BlockSpec((PLE_DIM, D_MODEL), lambda i, s: (0, 0)),
        ],
        out_specs=pl.BlockSpec((tm, D_MODEL), lambda i, s: (i, 0)),
        scratch_shapes=[
            pltpu.VMEM((2, TOP_K, tm * TOK_ROWS, LANE), F32),
            pltpu.SemaphoreType.DMA((2,)),
        ],
    )
    return pl.pallas_call(
        functools.partial(_combine_ple_kernel, tm=tm, row0=row0),
        grid_spec=grid_spec,
        out_shape=jax.ShapeDtypeStruct((m, D_MODEL), F32),
        compiler_params=_params(("arbitrary",)),
    )(tok_slots, x1_all, route_all, y_slots, p, norm3_g.reshape(1, D_MODEL), wpg, wple)


def _routing_tables(route_all):
    flat_e = route_all[:, 0:TOP_K].astype(jnp.int32).reshape(-1)
    n_asg = flat_e.shape[0]
    onehot = (flat_e[:, None] == jnp.arange(N_EXPERTS, dtype=jnp.int32)[None, :]).astype(F32)
    chunk = math.gcd(n_asg, 2 * LANE)
    within = jnp.einsum('ij,cjk->cik', jnp.tril(jnp.ones((chunk, chunk), F32)),
                        onehot.reshape(n_asg // chunk, chunk, N_EXPERTS))
    totals = within[:, -1, :]
    before = jnp.cumsum(totals, axis=0) - totals
    csum = (within + before[:, None, :]).reshape(n_asg, N_EXPERTS)
    rank = jnp.sum(csum * onehot, axis=1).astype(jnp.int32) - 1
    counts = (before[-1] + totals[-1]).astype(jnp.int32)
    nblk = (counts + MOE_TM - 1) // MOE_TM
    blk_end = jnp.cumsum(nblk)
    blk_start = blk_end - nblk
    first_blk = jnp.sum(onehot * blk_start.astype(F32)[None, :], axis=1).astype(jnp.int32)
    slot = first_blk * MOE_TM + rank
    n_used = blk_end[-1].reshape(1)
    pad_start = blk_start * MOE_TM + counts
    pad_count = nblk * MOE_TM - counts
    i32 = lambda a: a.astype(jnp.int32)
    return i32(blk_start), i32(nblk), i32(n_used), i32(slot), i32(pad_start), i32(pad_count)


def kernel(x_prompt, x_sample, p_prompt, p_sample, cache_k_g0, cache_v_g0, cache_k_g1, cache_v_g1,
           cache_k_g2, cache_v_g2, state_conv, norm1_g, w_in, q_gain, k_gain, conv_w, w_conv_out,
           w_attn_out, w_o, norm2_g, w_coarse, b_coarse, w_fine, b_fine, w_gate_e, w_up_e, w_down_e,
           norm3_g, w_ple_gate, w_ple):
    seq = x_prompt.shape[1]
    assert x_prompt.shape == (1, seq, D_MODEL) and norm1_g.shape[0] == 1, "one prompt, one layer"
    n_seq, t_new, _ = x_sample.shape
    assert seq % ATT_UNIT == 0 and t_new == SUBLANE
    caches_k = (cache_k_g0, cache_k_g1, cache_k_g2)
    caches_v = (cache_v_g0, cache_v_g1, cache_v_g2)
    for g in range(N_GROUPS):
        assert caches_k[g].shape == (1, n_seq, WINDOWS[g], HEADS_PER_GROUP, HEAD_DIM)
    n_s = n_seq * t_new
    n_all = seq + n_s

    xp = x_prompt[0]
    xs = x_sample.reshape(n_s, D_MODEL)
    wc = w_conv_out[0].astype(BF16)
    wa = w_attn_out[0].astype(BF16)
    wo = w_o[0].astype(BF16)
    wpg = w_ple_gate[0].astype(BF16)
    wple = w_ple[0].astype(BF16)
    wr = jnp.concatenate(
        [w_coarse[0], jnp.transpose(w_fine[0], (1, 0, 2)).reshape(D_MODEL, N_EXPERTS)], axis=1)
    wr = jnp.pad(wr, ((0, 0), (0, LANE - wr.shape[1])))
    wr_hi = wr.astype(BF16)
    wr_lo = (wr - wr_hi.astype(F32)).astype(BF16)
    br = jnp.pad(jnp.concatenate([b_coarse[0], b_fine[0].reshape(-1)]), (0, LANE - 36)).reshape(1, LANE)

    z_p, z_s = _inproj(xp, xs, norm1_g[0], w_in[0], tm=2048, tn=512)

    att_p = _prompt_attn(z_p, q_gain[0], k_gain[0])
    o_p, pks, pvs = att_p[0], att_p[1:4], att_p[4:7]
    ck = [c.reshape(n_seq, WINDOWS[g] * HEADS_PER_GROUP, HEAD_DIM) for g, c in enumerate(caches_k)]
    cv = [c.reshape(n_seq, WINDOWS[g] * HEADS_PER_GROUP, HEAD_DIM) for g, c in enumerate(caches_v)]
    att_s = _sample_attn(z_s, ck, cv, q_gain[0], k_gain[0], n_seq, t_new)
    o_s, sks, svs = att_s[0], att_s[1:4], att_s[4:7]

    mg_p, u_tail = _merge(z_p, o_p, conv_w[0], wc, wa, tm=512)
    prefix = jnp.pad(state_conv[0], ((0, 0), (0, t_new - 2), (0, 0))).reshape(n_s, D_CONV)
    mg_s, u_s = _merge(z_s, o_s, conv_w[0], wc, wa, tm=n_s, per_seq=t_new, prefix=prefix)
    x1_all, x1t_all, route_all = _outproj_router(xp, mg_p, xs, mg_s, wo, norm2_g[0], wr_hi, wr_lo, br)

    n_blocks = n_all * TOP_K // MOE_TM + N_EXPERTS
    blk_start, nblk, n_used, tok_slots, pad_start, pad_count = _routing_tables(route_all)
    xst = _dispatch(x1t_all, tok_slots, pad_start, pad_count, n_used, n_blocks, tm=n_s)
    y_slots = _moe(xst, norm2_g[0], w_gate_e[0], w_up_e[0], w_down_e[0], blk_start, nblk, n_used)

    y_p = _combine_ple(x1_all, route_all, y_slots, tok_slots, p_prompt[0, 0], norm3_g[0], wpg, wple,
                       m=seq, tm=256, row0=0)
    y_s = _combine_ple(x1_all, route_all, y_slots, tok_slots, p_sample[0].reshape(n_s, PLE_DIM),
                       norm3_g[0], wpg, wple, m=n_s, tm=n_s, row0=seq)

    def state(a, n, g):
        return a.reshape(1, n, WINDOWS[g], HEADS_PER_GROUP, HEAD_DIM)

    pk = [state(a, 1, g) for g, a in enumerate(pks)]
    pv = [state(a, 1, g) for g, a in enumerate(pvs)]
    sk = [state(a, n_seq, g) for g, a in enumerate(sks)]
    sv = [state(a, n_seq, g) for g, a in enumerate(svs)]
    pconv = u_tail[-2:].reshape(1, 1, 2, D_CONV)
    sconv = u_s.reshape(n_seq, t_new, D_CONV)[:, t_new - 2:].reshape(1, n_seq, 2, D_CONV)
    return (y_p.reshape(1, seq, D_MODEL), y_s.reshape(n_seq, t_new, D_MODEL),
            pk[0], pv[0], pk[1], pv[1], pk[2], pv[2], pconv,
            sk[0], sv[0], sk[1], sv[1], sk[2], sv[2], sconv)
```

```python
import functools
import math

import jax
import jax.numpy as jnp
from jax import lax
from jax.experimental import pallas as pl
from jax.experimental.pallas import tpu as pltpu

D_MODEL = 2048
HEAD_DIM = 128
HEADS_PER_GROUP = 4
WINDOWS = (128, 512, 2048)
DILATIONS = (1, 4, 16)
N_GROUPS = 3
N_HEADS = N_GROUPS * HEADS_PER_GROUP
ATT_WIDTH = N_HEADS * HEAD_DIM
ATT_OUT_WIDTH = HEADS_PER_GROUP * HEAD_DIM
SCALE = HEAD_DIM ** -0.5
D_CONV = D_MODEL // 2
PLE_DIM = 256
N_EXPERT_GROUPS = 4
EXPERTS_PER_GROUP = 8
N_EXPERTS = N_EXPERT_GROUPS * EXPERTS_PER_GROUP
TOP_K = 2
D_EXPERT = D_MODEL // 4
EPS = 1e-6

COL_B = 0
COL_C = D_CONV
COL_H = 2 * D_CONV
COL_Q = 3 * D_CONV
COL_K = COL_Q + ATT_WIDTH
COL_V = COL_K + ATT_WIDTH
COL_GC = COL_V + ATT_WIDTH
COL_GA = COL_GC + D_MODEL
IN_COLS = COL_GA + D_MODEL

LANE = 128
SUBLANE = 8
NEG = -1e30
ATT_UNIT = max(WINDOWS)
ATT_BLK = 128
ATT_TILE_UNROLL = 16
MXU_N = 256
TOK_ROWS = D_MODEL // LANE
MOE_TM = 288
MOE_IN_BUFS = 4
VMEM_LIMIT = 56 * 1024 * 1024

BF16 = jnp.bfloat16
F32 = jnp.float32


def _params(sem):
    return pltpu.CompilerParams(dimension_semantics=sem, vmem_limit_bytes=VMEM_LIMIT)


def _rms(x, gain):
    return x * lax.rsqrt(jnp.mean(x * x, axis=-1, keepdims=True) + EPS) * gain


def _dot(a, b):
    return jnp.dot(a, b, preferred_element_type=F32)


def _dot_nt(a, b):
    return lax.dot_general(a, b, (((1,), (1,)), ((), ())), preferred_element_type=F32)


def _inproj_kernel(x_ref, xs_ref, g_ref, w_ref, z_ref, zs_ref, h_scr, hs_scr):
    i, j = pl.program_id(0), pl.program_id(1)
    with_sample = i == pl.num_programs(0) - 1

    @pl.when(j == 0)
    def _():
        h_scr[...] = _rms(x_ref[...], g_ref[...]).astype(BF16)

    @pl.when(with_sample & (j == 0))
    def _():
        hs_scr[...] = _rms(xs_ref[...], g_ref[...]).astype(BF16)

    def project(sample_too):
        for c in range(w_ref.shape[1] // MXU_N):
            cols = slice(c * MXU_N, (c + 1) * MXU_N)
            w_c = w_ref[:, cols].astype(BF16)
            z_ref[:, cols] = _dot(h_scr[...], w_c)
            if sample_too:
                zs_ref[:, cols] = _dot(hs_scr[...], w_c)

    @pl.when(jnp.logical_not(with_sample))
    def _():
        project(False)

    @pl.when(with_sample)
    def _():
        project(True)


def _inproj(x, xs, norm_g, w_in, tm, tn):
    m, m_s = x.shape[0], xs.shape[0]
    n_i = m // tm
    return pl.pallas_call(
        _inproj_kernel,
        grid=(n_i, IN_COLS // tn),
        in_specs=[
            pl.BlockSpec((tm, D_MODEL), lambda i, j: (i, 0), pipeline_mode=pl.Buffered(1)),
            pl.BlockSpec((m_s, D_MODEL), lambda i, j: (0, 0), pipeline_mode=pl.Buffered(1)),
            pl.BlockSpec((1, D_MODEL), lambda i, j: (0, 0)),
            pl.BlockSpec((D_MODEL, tn), lambda i, j: (0, j)),
        ],
        out_specs=[
            pl.BlockSpec((tm, tn), lambda i, j: (i, j)),
            pl.BlockSpec((m_s, tn), lambda i, j: (0, jnp.where(i == n_i - 1, j, 0))),
        ],
        out_shape=[jax.ShapeDtypeStruct((m, IN_COLS), F32),
                   jax.ShapeDtypeStruct((m_s, IN_COLS), F32)],
        scratch_shapes=[pltpu.VMEM((tm, D_MODEL), BF16), pltpu.VMEM((m_s, D_MODEL), BF16)],
        compiler_params=_params(("arbitrary", "arbitrary")),
    )(x, xs, norm_g.reshape(1, D_MODEL), w_in)


def _prompt_attn_kernel(*refs):
    (q0, q1, q2, kc0, kc1, kc2, vc0, vc1, vc2, kp0, kp1, kp2, vp0, vp1, vp2, qg_ref, kg_ref,
     o_ref, pk0, pk1, pk2, pv0, pv1, pv2, kext, vext, acc_s, m_s, l_s) = refs
    q_refs, kc_refs, vc_refs = (q0, q1, q2), (kc0, kc1, kc2), (vc0, vc1, vc2)
    kp_refs, vp_refs = (kp0, kp1, kp2), (vp0, vp1, vp2)
    pk_refs, pv_refs = (pk0, pk1, pk2), (pv0, pv1, pv2)
    n = pl.program_id(1)
    last = pl.num_programs(1) - 1

    qi = lax.broadcasted_iota(jnp.int32, (ATT_BLK, 2 * ATT_BLK), 0) + ATT_BLK
    ki = lax.broadcasted_iota(jnp.int32, (ATT_BLK, 2 * ATT_BLK), 1)
    dist = qi - ki
    band = (dist >= 0) & (dist <= ATT_BLK)

    for g in range(N_GROUPS):
        win, dil = WINDOWS[g], DILATIONS[g]
        q_ref = q_refs[g]
        qgain = qg_ref[g, 0]
        kgain = kg_ref[g, 0]
        kext[0:win, :] = _rms(kp_refs[g][...], kgain)
        kext[win:win + ATT_UNIT, :] = _rms(kc_refs[g][...], kgain)
        vext[0:win, :] = vp_refs[g][...]
        vext[win:win + ATT_UNIT, :] = vc_refs[g][...]

        @pl.when(n == last)
        def _():
            pk_refs[g][...] = kext[ATT_UNIT:ATT_UNIT + win, :]
            pv_refs[g][...] = vext[ATT_UNIT:ATT_UNIT + win, :]

        def tile(t, carry, g=g, win=win, dil=dil, q_ref=q_ref, qgain=qgain):
            u = t // dil
            r = t % dil
            off = u * win + r
            if dil == 1:
                off = pl.multiple_of(off, ATT_BLK)
                rows_q = pl.ds(off, ATT_BLK)
                rows_k = pl.ds(off, 2 * ATT_BLK)
            else:
                rows_q = pl.ds(off, ATT_BLK, stride=dil)
                rows_k = pl.ds(off, 2 * ATT_BLK, stride=dil)
            q = _rms(q_ref[rows_q, :], qgain).astype(BF16)
            k = kext[rows_k, :].astype(BF16)
            v = vext[rows_k, :].astype(BF16)
            s = _dot_nt(q, k) * SCALE
            k_min = jnp.where((n > 0) | (u > 0), 0, ATT_BLK)
            s = jnp.where(band & (ki >= k_min), s, NEG)
            m_t = jnp.max(s, axis=-1, keepdims=True)
            p = jnp.exp(s - m_t)
            l_t = jnp.sum(p, axis=-1, keepdims=True)
            m_s[g, rows_q, :] = jnp.broadcast_to(m_t, (ATT_BLK, LANE))
            l_s[g, rows_q, :] = jnp.broadcast_to(l_t, (ATT_BLK, LANE))
            acc_s[g, rows_q, :] = _dot(p.astype(BF16), v)
            return carry

        lax.fori_loop(0, ATT_UNIT // ATT_BLK, tile, 0, unroll=ATT_TILE_UNROLL)

    def finish(c, carry):
        rows = pl.ds(pl.multiple_of(c * ATT_BLK, ATT_BLK), ATT_BLK)
        m = [m_s[g, rows, :] for g in range(N_GROUPS)]
        m_all = jnp.maximum(jnp.maximum(m[0], m[1]), m[2])
        num = jnp.zeros((ATT_BLK, LANE), F32)
        den = jnp.zeros((ATT_BLK, LANE), F32)
        for g in range(N_GROUPS):
            w = jnp.exp(m[g] - m_all)
            num = num + w * acc_s[g, rows, :]
            den = den + w * l_s[g, rows, :]
        o_ref[rows, :] = num / den
        return carry

    lax.fori_loop(0, ATT_UNIT // ATT_BLK, finish, 0, unroll=2)


def _prompt_attn(z, q_gain, k_gain):
    s_len = z.shape[0]
    n_units = s_len // ATT_UNIT
    in_specs, args = [], []

    def col(base, g):
        return lambda hh, n: (n, base // LANE + g * HEADS_PER_GROUP + hh)

    for g in range(N_GROUPS):
        in_specs.append(pl.BlockSpec((ATT_UNIT, LANE), col(COL_Q, g)))
    for base in (COL_K, COL_V):
        for g in range(N_GROUPS):
            in_specs.append(pl.BlockSpec((ATT_UNIT, LANE), col(base, g)))
    for base in (COL_K, COL_V):
        for g in range(N_GROUPS):
            per = ATT_UNIT // WINDOWS[g]
            in_specs.append(pl.BlockSpec(
                (WINDOWS[g], LANE),
                lambda hh, n, base=base, g=g, per=per: (
                    jnp.maximum(n * per - 1, 0), base // LANE + g * HEADS_PER_GROUP + hh)))
    args = [z] * 15
    gain_spec = pl.BlockSpec((N_GROUPS, 1, 1, LANE), lambda hh, n: (0, hh, 0, 0))
    in_specs += [gain_spec, gain_spec]
    args += [q_gain.reshape(N_GROUPS, HEADS_PER_GROUP, 1, LANE),
             k_gain.reshape(N_GROUPS, HEADS_PER_GROUP, 1, LANE)]

    out_specs = [pl.BlockSpec((ATT_UNIT, LANE), lambda hh, n: (n, hh))]
    out_shape = [jax.ShapeDtypeStruct((s_len, ATT_OUT_WIDTH), F32)]
    for _ in range(2):
        for g in range(N_GROUPS):
            out_specs.append(pl.BlockSpec((WINDOWS[g], LANE), lambda hh, n: (0, hh)))
            out_shape.append(jax.ShapeDtypeStruct((WINDOWS[g], ATT_OUT_WIDTH), F32))

    return pl.pallas_call(
        _prompt_attn_kernel,
        grid=(HEADS_PER_GROUP, n_units),
        in_specs=in_specs,
        out_specs=out_specs,
        out_shape=out_shape,
        scratch_shapes=[
            pltpu.VMEM((2 * ATT_UNIT, LANE), F32),
            pltpu.VMEM((2 * ATT_UNIT, LANE), F32),
            pltpu.VMEM((N_GROUPS, ATT_UNIT, LANE), F32),
            pltpu.VMEM((N_GROUPS, ATT_UNIT, LANE), F32),
            pltpu.VMEM((N_GROUPS, ATT_UNIT, LANE), F32),
        ],
        compiler_params=_params(("parallel", "arbitrary")),
    )(*args)


def _sample_attn_kernel(*refs):
    (q0, q1, q2, kn0, kn1, kn2, vn0, vn1, vn2, ck0, ck1, ck2, cv0, cv1, cv2, qg_ref, kg_ref,
     o_ref, sk0, sk1, sk2, sv0, sv1, sv2) = refs
    q_refs, kn_refs, vn_refs = (q0, q1, q2), (kn0, kn1, kn2), (vn0, vn1, vn2)
    ck_refs, cv_refs = (ck0, ck1, ck2), (cv0, cv1, cv2)
    sk_refs, sv_refs = (sk0, sk1, sk2), (sv0, sv1, sv2)
    t_new = q0.shape[0]
    nh = HEADS_PER_GROUP
    n_q = nh * t_new
    t_bits = t_new.bit_length() - 1

    def heads(ref):
        return [ref[:, h * LANE:(h + 1) * LANE] for h in range(nh)]

    parts = []
    for g in range(N_GROUPS):
        win, dil = WINDOWS[g], DILATIONS[g]
        q_all = jnp.concatenate(
            [_rms(x, qg_ref[g, h]) for h, x in enumerate(heads(q_refs[g]))], axis=0).astype(BF16)
        k_new = [_rms(x, kg_ref[g, h]) for h, x in enumerate(heads(kn_refs[g]))]
        v_new = heads(vn_refs[g])
        k_new_all = jnp.concatenate(k_new, axis=0).astype(BF16)
        v_new_all = jnp.concatenate(v_new, axis=0).astype(BF16)
        ck = ck_refs[g][0]
        cv = cv_refs[g][0]
        old = (win - t_new) * nh
        sk_refs[g][0, 0:old, :] = ck[t_new * nh:, :]
        sv_refs[g][0, 0:old, :] = cv[t_new * nh:, :]
        for h in range(nh):
            sk_refs[g][0, pl.ds(old + h, t_new, stride=nh), :] = k_new[h]
            sv_refs[g][0, pl.ds(old + h, t_new, stride=nh), :] = v_new[h]

        s_c = _dot_nt(q_all, ck.astype(BF16)) * SCALE
        s_n = _dot_nt(q_all, k_new_all) * SCALE
        row = lax.broadcasted_iota(jnp.int32, (n_q, win * nh), 0)
        col = lax.broadcasted_iota(jnp.int32, (n_q, win * nh), 1)
        i_q, pos = row & (t_new - 1), col >> 2
        mask_c = (((col & (nh - 1)) == (row >> t_bits)) & (pos >= i_q)
                  & (((pos - i_q) & (dil - 1)) == 0))
        row_n = lax.broadcasted_iota(jnp.int32, (n_q, n_q), 0)
        col_n = lax.broadcasted_iota(jnp.int32, (n_q, n_q), 1)
        d_n = (row_n & (t_new - 1)) - (col_n & (t_new - 1))
        mask_n = ((row_n >> t_bits) == (col_n >> t_bits)) & (d_n >= 0) & ((d_n & (dil - 1)) == 0)
        s_c = jnp.where(mask_c, s_c, NEG)
        s_n = jnp.where(mask_n, s_n, NEG)
        m = jnp.maximum(jnp.max(s_c, axis=-1, keepdims=True), jnp.max(s_n, axis=-1, keepdims=True))
        p_c = jnp.exp(s_c - m)
        p_n = jnp.exp(s_n - m)
        l = jnp.sum(p_c, axis=-1, keepdims=True) + jnp.sum(p_n, axis=-1, keepdims=True)
        acc = _dot(p_c.astype(BF16), cv.astype(BF16)) + _dot(p_n.astype(BF16), v_new_all)
        parts.append((m, l, acc))

    m_all = jnp.maximum(jnp.maximum(parts[0][0], parts[1][0]), parts[2][0])
    num = jnp.zeros((n_q, LANE), F32)
    den = jnp.zeros((n_q, 1), F32)
    for m, l, acc in parts:
        w = jnp.exp(m - m_all)
        num = num + w * acc
        den = den + w * l
    o_all = num / den
    for h in range(nh):
        o_ref[:, h * LANE:(h + 1) * LANE] = o_all[h * t_new:(h + 1) * t_new, :]


def _sample_attn(z, caches_k, caches_v, q_gain, k_gain, n_seq, t_new):
    assert HEADS_PER_GROUP == 4 and t_new & (t_new - 1) == 0
    in_specs, args = [], []
    for base in (COL_Q, COL_K, COL_V):
        for g in range(N_GROUPS):
            in_specs.append(pl.BlockSpec(
                (t_new, ATT_OUT_WIDTH), lambda b, base=base, g=g: (b, base // ATT_OUT_WIDTH + g)))
            args.append(z)
    for caches in (caches_k, caches_v):
        for g in range(N_GROUPS):
            in_specs.append(pl.BlockSpec((1, WINDOWS[g] * HEADS_PER_GROUP, LANE), lambda b: (b, 0, 0)))
            args.append(caches[g])
    gain_spec = pl.BlockSpec((N_GROUPS, HEADS_PER_GROUP, 1, LANE), lambda b: (0, 0, 0, 0))
    in_specs += [gain_spec, gain_spec]
    args += [q_gain.reshape(N_GROUPS, HEADS_PER_GROUP, 1, LANE),
             k_gain.reshape(N_GROUPS, HEADS_PER_GROUP, 1, LANE)]

    out_specs = [pl.BlockSpec((t_new, ATT_OUT_WIDTH), lambda b: (b, 0))]
    out_shape = [jax.ShapeDtypeStruct((n_seq * t_new, ATT_OUT_WIDTH), F32)]
    for _ in range(2):
        for g in range(N_GROUPS):
            rows = WINDOWS[g] * HEADS_PER_GROUP
            out_specs.append(pl.BlockSpec((1, rows, LANE), lambda b: (b, 0, 0)))
            out_shape.append(jax.ShapeDtypeStruct((n_seq, rows, LANE), F32))

    return pl.pallas_call(
        _sample_attn_kernel,
        grid=(n_seq,),
        in_specs=in_specs,
        out_specs=out_specs,
        out_shape=out_shape,
        compiler_params=_params(("parallel",)),
    )(*args)


def _conv_taps(u, r1, r2, cw_ref):
    return cw_ref[2:3, :] * u + cw_ref[1:2, :] * r1 + cw_ref[0:1, :] * r2


def _merge_kernel(*refs, per_seq, n_chunks):
    b_ref, c_ref, h_ref, pc_ref, ph_ref, cw_ref, o_ref = refs[:7]
    gc_refs = refs[7:7 + n_chunks]
    ga_refs = refs[7 + n_chunks:7 + 2 * n_chunks]
    wc_ref, wa_ref, out_ref, u_ref, yb_scr = refs[7 + 2 * n_chunks:]
    i = pl.program_id(0)
    tm = b_ref.shape[0]
    u = c_ref[...] * h_ref[...]
    r1 = pltpu.roll(u, 1, axis=0)
    r2 = pltpu.roll(u, 2, axis=0)
    if per_seq is None:
        yb_scr[...] = b_ref[...] * _conv_taps(u, r1, r2, cw_ref)
        up = jnp.where(i > 0, pc_ref[...] * ph_ref[...], 0.0)
        row = lax.broadcasted_iota(jnp.int32, (SUBLANE, D_CONV), 0)
        u8 = u[0:SUBLANE, :]
        r1_8 = jnp.where(row < 1, pltpu.roll(up, 1, axis=0), pltpu.roll(u8, 1, axis=0))
        r2_8 = jnp.where(row < 2, pltpu.roll(up, 2, axis=0), pltpu.roll(u8, 2, axis=0))
        yb_scr[0:SUBLANE, :] = b_ref[0:SUBLANE, :] * _conv_taps(u8, r1_8, r2_8, cw_ref)
        u_ref[...] = u[tm - SUBLANE:tm, :]
    else:
        pre = pc_ref[...]
        t = lax.broadcasted_iota(jnp.int32, (tm, D_CONV), 0) & (per_seq - 1)
        r1 = jnp.where(t == 0, pltpu.roll(pre, tm - 1, axis=0), r1)
        r2 = jnp.where(t < 2, pre, r2)
        yb_scr[...] = b_ref[...] * _conv_taps(u, r1, r2, cw_ref)
        u_ref[...] = u

    yb = yb_scr[...].astype(BF16)
    ob = o_ref[...].astype(BF16)
    tn = D_MODEL // n_chunks
    for c in range(n_chunks):
        cols = slice(c * tn, (c + 1) * tn)
        y_conv = _dot(yb, wc_ref[:, cols])
        y_att = _dot(ob, wa_ref[:, cols])
        out_ref[:, cols] = (jax.nn.sigmoid(gc_refs[c][...]) * y_conv
                            + jax.nn.sigmoid(ga_refs[c][...]) * y_att).astype(BF16)


def _merge(z, o_att, conv_w, wc, wa, tm, per_seq=None, prefix=None):
    m = z.shape[0]
    tn = ATT_OUT_WIDTH
    n_chunks = D_MODEL // tn
    if per_seq is None:
        prev = lambda i, c: (jnp.maximum(i * (tm // SUBLANE) - 1, 0), c)
        pc_spec = pl.BlockSpec((SUBLANE, D_CONV), lambda i: prev(i, COL_C // D_CONV))
        ph_spec = pl.BlockSpec((SUBLANE, D_CONV), lambda i: prev(i, COL_H // D_CONV))
        pc_arg, ph_arg = z, z
        u_rows = SUBLANE
    else:
        pc_spec = pl.BlockSpec((tm, D_CONV), lambda i: (i, 0))
        ph_spec = pl.BlockSpec((SUBLANE, D_CONV), lambda i: (0, 0))
        pc_arg, ph_arg = prefix, prefix
        u_rows = tm
    gate_specs = [pl.BlockSpec((tm, tn), lambda i, base=base, c=c: (i, base // tn + c))
                  for base in (COL_GC, COL_GA) for c in range(n_chunks)]
    return pl.pallas_call(
        functools.partial(_merge_kernel, per_seq=per_seq, n_chunks=n_chunks),
        grid=(m // tm,),
        in_specs=[
            pl.BlockSpec((tm, D_CONV), lambda i: (i, COL_B // D_CONV)),
            pl.BlockSpec((tm, D_CONV), lambda i: (i, COL_C // D_CONV)),
            pl.BlockSpec((tm, D_CONV), lambda i: (i, COL_H // D_CONV)),
            pc_spec,
            ph_spec,
            pl.BlockSpec((3, D_CONV), lambda i: (0, 0)),
            pl.BlockSpec((tm, ATT_OUT_WIDTH), lambda i: (i, 0)),
            *gate_specs,
            pl.BlockSpec((D_CONV, D_MODEL), lambda i: (0, 0)),
            pl.BlockSpec((ATT_OUT_WIDTH, D_MODEL), lambda i: (0, 0)),
        ],
        out_specs=[
            pl.BlockSpec((tm, D_MODEL), lambda i: (i, 0)),
            pl.BlockSpec((u_rows, D_CONV), lambda i: (i, 0)),
        ],
        out_shape=[
            jax.ShapeDtypeStruct((m, D_MODEL), BF16),
            jax.ShapeDtypeStruct((m // tm * u_rows, D_CONV), F32),
        ],
        scratch_shapes=[pltpu.VMEM((tm, D_CONV), F32)],
        compiler_params=_params(("parallel",)),
    )(z, z, z, pc_arg, ph_arg, conv_w, o_att, *([z] * (2 * n_chunks)), wc, wa)


def _outproj_router_kernel(xp_ref, mgp_ref, xs_ref, mgs_ref, wo_ref, g2_ref, wr_hi_ref, wr_lo_ref,
                           br_ref, x1_ref, route_ref):
    is_sample = pl.program_id(0) == pl.num_programs(0) - 1

    @pl.when(jnp.logical_not(is_sample))
    def _():
        _outproj_router_tile(xp_ref, mgp_ref, wo_ref, g2_ref, wr_hi_ref, wr_lo_ref, br_ref,
                             x1_ref, route_ref)

    @pl.when(is_sample)
    def _():
        _outproj_router_tile(xs_ref, mgs_ref, wo_ref, g2_ref, wr_hi_ref, wr_lo_ref, br_ref,
                             x1_ref, route_ref)


def _outproj_router_tile(x_ref, mg_ref, wo_ref, g2_ref, wr_hi_ref, wr_lo_ref, br_ref, x1_ref,
                         route_ref):
    x1 = x_ref[...] + _dot(mg_ref[...], wo_ref[...])
    x1_ref[...] = x1
    h = _rms(x1, g2_ref[...])
    h_hi = h.astype(BF16)
    h_lo = (h - h_hi.astype(F32)).astype(BF16)
    logits = (_dot(h_hi, wr_hi_ref[...]) + _dot(h_hi, wr_lo_ref[...]) + _dot(h_lo, wr_hi_ref[...])
              + br_ref[...])
    lane = lax.broadcasted_iota(jnp.int32, logits.shape, 1)
    is_coarse = lane < N_EXPERT_GROUPS
    coarse = jnp.where(is_coarse, logits, NEG)
    cmax = jnp.max(coarse, axis=-1, keepdims=True)
    grp = jnp.min(jnp.where(coarse == cmax, lane, LANE), axis=-1, keepdims=True)
    p_grp = 1.0 / jnp.sum(jnp.where(is_coarse, jnp.exp(coarse - cmax), 0.0), axis=-1, keepdims=True)
    eid = lane - N_EXPERT_GROUPS
    in_grp = (eid >= 0) & (eid < N_EXPERTS) & ((eid >> 3) == grp)
    fine = jnp.where(in_grp, logits, NEG)
    v1 = jnp.max(fine, axis=-1, keepdims=True)
    i1 = jnp.min(jnp.where(fine == v1, lane, LANE), axis=-1, keepdims=True)
    fine2 = jnp.where(lane == i1, NEG, fine)
    v2 = jnp.max(fine2, axis=-1, keepdims=True)
    i2 = jnp.min(jnp.where(fine2 == v2, lane, LANE), axis=-1, keepdims=True)
    e = jnp.exp(v2 - v1)
    gate1 = p_grp / (1.0 + e)
    gate2 = p_grp * e / (1.0 + e)
    route = jnp.where(lane == 0, (i1 - N_EXPERT_GROUPS).astype(F32),
                      jnp.where(lane == 1, (i2 - N_EXPERT_GROUPS).astype(F32),
                                jnp.where(lane == 2, gate1, jnp.where(lane == 3, gate2, 0.0))))
    route_ref[...] = route


def _outproj_router(xp, mg_p, xs, mg_s, wo, norm2_g, wr_hi, wr_lo, br):
    m_p, tm = xp.shape[0], xs.shape[0]
    assert m_p % tm == 0
    n_p = m_p // tm
    prompt_blk = lambda i: (jnp.minimum(i, n_p - 1), 0)
    fixed = lambda i: (0, 0)
    return pl.pallas_call(
        _outproj_router_kernel,
        grid=(n_p + 1,),
        in_specs=[
            pl.BlockSpec((tm, D_MODEL), prompt_blk),
            pl.BlockSpec((tm, D_MODEL), prompt_blk),
            pl.BlockSpec((tm, D_MODEL), fixed),
            pl.BlockSpec((tm, D_MODEL), fixed),
            pl.BlockSpec((D_MODEL, D_MODEL), fixed),
            pl.BlockSpec((1, D_MODEL), fixed),
            pl.BlockSpec((D_MODEL, LANE), fixed),
            pl.BlockSpec((D_MODEL, LANE), fixed),
            pl.BlockSpec((1, LANE), fixed),
        ],
        out_specs=[
            pl.BlockSpec((tm, D_MODEL), lambda i: (i, 0)),
            pl.BlockSpec((tm, LANE), lambda i: (i, 0)),
        ],
        out_shape=[
            jax.ShapeDtypeStruct((m_p + tm, D_MODEL), F32),
            jax.ShapeDtypeStruct((m_p + tm, LANE), F32),
        ],
        compiler_params=_params(("arbitrary",)),
    )(xp, mg_p, xs, mg_s, wo, norm2_g.reshape(1, D_MODEL), wr_hi, wr_lo, br)


def _to_slabs(dst_ref, x):
    rows = x.shape[0]
    for s in range(TOK_ROWS):
        dst_ref[pl.ds(s, rows, stride=TOK_ROWS), :] = x[:, s * LANE:(s + 1) * LANE]


def _from_slabs(src_ref, rows):
    return jnp.concatenate(
        [src_ref[pl.ds(s, rows, stride=TOK_ROWS), :] for s in range(TOK_ROWS)], axis=1)


def _dispatch_kernel(slots_ref, pad0_ref, padn_ref, nused_ref, x1_hbm, xst_hbm, rows, buf, zero,
                     lsem, ssem, zsem, *, tm, n_blocks):
    i = pl.program_id(0)
    n_steps = pl.num_programs(0)
    slab = tm * TOK_ROWS

    def load(step):
        return pltpu.make_async_copy(
            x1_hbm.at[pl.ds(pl.multiple_of(step * tm, tm), tm), :], rows.at[step % 3],
            lsem.at[step % 3])

    def wait_scatter(step):
        for _ in range(TOP_K):
            pltpu.make_async_copy(buf.at[step % 3], xst_hbm.at[pl.ds(0, slab), :],
                                  ssem.at[step % 3]).wait()

    def pad_rows(start):
        def expert(e, carry):
            def row(j, c):
                dst = pl.multiple_of((pad0_ref[e] + j) * TOK_ROWS, TOK_ROWS)
                cp = pltpu.make_async_copy(zero, xst_hbm.at[pl.ds(dst, TOK_ROWS), :], zsem.at[0])
                if start:
                    cp.start()
                else:
                    cp.wait()
                return c
            return lax.fori_loop(0, padn_ref[e], row, carry)
        lax.fori_loop(0, N_EXPERTS, expert, 0)

    @pl.when(i == 0)
    def _():
        load(0).start()
        zero[...] = jnp.zeros(zero.shape, F32)
        pad_rows(start=True)

    @pl.when(i >= 2)
    def _():
        wait_scatter(i - 2)

    @pl.when(i + 1 < n_steps)
    def _():
        load(i + 1).start()

    load(i).wait()
    cur = i % 3
    _to_slabs(buf.at[cur], rows[cur])

    def body(r, carry):
        src = buf.at[cur, pl.ds(pl.multiple_of(r * TOK_ROWS, TOK_ROWS), TOK_ROWS), :]
        for k in range(TOP_K):
            dst = pl.multiple_of(slots_ref[(i * tm + r) * TOP_K + k] * TOK_ROWS, TOK_ROWS)
            pltpu.make_async_copy(src, xst_hbm.at[pl.ds(dst, TOK_ROWS), :],
                                  ssem.at[cur]).start(priority=k)
        return carry
    lax.fori_loop(0, tm, body, 0, unroll=4)

    @pl.when(i == n_steps - 1)
    def _():
        @pl.when(i >= 1)
        def _():
            wait_scatter(i - 1)
        wait_scatter(i)
        pad_rows(start=False)
        chunk = math.gcd(tm, MOE_TM)
        rows = chunk * TOK_ROWS
        buf[0, 0:rows, :] = jnp.zeros((rows, LANE), F32)
        first = nused_ref[0] * MOE_TM
        n_tail = (n_blocks - nused_ref[0]) * (MOE_TM // chunk)

        def tail(j, carry, start):
            dst = pl.multiple_of((first + j * chunk) * TOK_ROWS, TOK_ROWS)
            cp = pltpu.make_async_copy(buf.at[0, pl.ds(0, rows), :],
                                       xst_hbm.at[pl.ds(dst, rows), :], zsem.at[0])
            if start:
                cp.start()
            else:
                cp.wait()
            return carry
        lax.fori_loop(0, n_tail, functools.partial(tail, start=True), 0)
        lax.fori_loop(0, n_tail, functools.partial(tail, start=False), 0)


def _dispatch(x1, tok_slots, pad_start, pad_count, n_used, n_blocks, tm):
    n_tok = x1.shape[0]
    assert n_tok % tm == 0
    grid_spec = pltpu.PrefetchScalarGridSpec(
        num_scalar_prefetch=4,
        grid=(n_tok // tm,),
        in_specs=[pl.BlockSpec(memory_space=pl.ANY)],
        out_specs=pl.BlockSpec(memory_space=pl.ANY),
        scratch_shapes=[
            pltpu.VMEM((3, tm, D_MODEL), F32),
            pltpu.VMEM((3, tm * TOK_ROWS, LANE), F32),
            pltpu.VMEM((TOK_ROWS, LANE), F32),
            pltpu.SemaphoreType.DMA((3,)),
            pltpu.SemaphoreType.DMA((3,)),
            pltpu.SemaphoreType.DMA((1,)),
        ],
    )
    return pl.pallas_call(
        functools.partial(_dispatch_kernel, tm=tm, n_blocks=n_blocks),
        grid_spec=grid_spec,
        out_shape=jax.ShapeDtypeStruct((n_blocks * MOE_TM * TOK_ROWS, LANE), F32),
        compiler_params=_params(("arbitrary",)),
    )(tok_slots, pad_start, pad_count, n_used, x1)


def _moe_kernel(blk0_ref, nblk_ref, nused_ref, xst_hbm, g2_ref, wg_ref, wu_ref, wd_ref,
                yt_hbm, xbuf, ybuf, gsem, osem):
    e = pl.program_id(0)
    n_used = nused_ref[0]
    slab = MOE_TM * TOK_ROWS

    def in_copy(blk):
        to_slot = lax.rem(blk, MOE_IN_BUFS)
        return pltpu.make_async_copy(
            xst_hbm.at[pl.ds(pl.multiple_of(blk * slab, slab), slab), :], xbuf.at[to_slot],
            gsem.at[to_slot])

    def out_copy(blk, from_slot):
        return pltpu.make_async_copy(
            ybuf.at[from_slot], yt_hbm.at[pl.ds(pl.multiple_of(blk * slab, slab), slab), :],
            osem.at[from_slot])


    @pl.when(e == 0)
    def _():
        for first in range(MOE_IN_BUFS - 1):
            @pl.when(first < n_used)
            def _():
                in_copy(first).start()

    def block(b, carry):
        blk = blk0_ref[e] + b
        slot = blk & 1
        ahead = blk + MOE_IN_BUFS - 1

        @pl.when(ahead < n_used)
        def _():
            in_copy(ahead).start()

        in_copy(blk).wait()
        h = _rms(_from_slabs(xbuf.at[lax.rem(blk, MOE_IN_BUFS)], MOE_TM), g2_ref[...]).astype(BF16)
        def mm(lhs, w_ref, c):
            return _dot(lhs, w_ref[0, :, c * MXU_N:(c + 1) * MXU_N].astype(BF16))
        a = jnp.concatenate([mm(h, wg_ref, c) for c in range(D_EXPERT // MXU_N)], axis=1)
        u = jnp.concatenate([mm(h, wu_ref, c) for c in range(D_EXPERT // MXU_N)], axis=1)
        hm = (a * jax.nn.sigmoid(a) * u).astype(BF16)
        y = jnp.concatenate([mm(hm, wd_ref, c) for c in range(D_MODEL // MXU_N)], axis=1)

        @pl.when(blk >= 2)
        def _():
            out_copy(blk - 2, slot).wait()

        _to_slabs(ybuf.at[slot], y)
        out_copy(blk, slot).start()
        return carry

    lax.fori_loop(0, nblk_ref[e], block, 0)

    @pl.when(e == pl.num_programs(0) - 1)
    def _():
        @pl.when(n_used >= 2)
        def _():
            out_copy(n_used - 2, (n_used - 2) & 1).wait()
        out_copy(n_used - 1, (n_used - 1) & 1).wait()


def _moe(xst, norm2_g, w_gate_e, w_up_e, w_down_e, blk_start, nblk, n_used):
    w_idx = lambda e, *_: (e, 0, 0)
    slab = MOE_TM * TOK_ROWS
    grid_spec = pltpu.PrefetchScalarGridSpec(
        num_scalar_prefetch=3,
        grid=(N_EXPERTS,),
        in_specs=[
            pl.BlockSpec(memory_space=pl.ANY),
            pl.BlockSpec((1, D_MODEL), lambda e, *_: (0, 0)),
            pl.BlockSpec((1, D_MODEL, D_EXPERT), w_idx),
            pl.BlockSpec((1, D_MODEL, D_EXPERT), w_idx),
            pl.BlockSpec((1, D_EXPERT, D_MODEL), w_idx),
        ],
        out_specs=pl.BlockSpec(memory_space=pl.ANY),
        scratch_shapes=[
            pltpu.VMEM((MOE_IN_BUFS, slab, LANE), F32),
            pltpu.VMEM((2, slab, LANE), F32),
            pltpu.SemaphoreType.DMA((MOE_IN_BUFS,)),
            pltpu.SemaphoreType.DMA((2,)),
        ],
    )
    return pl.pallas_call(
        _moe_kernel,
        grid_spec=grid_spec,
        out_shape=jax.ShapeDtypeStruct(xst.shape, F32),
        input_output_aliases={3: 0},
        compiler_params=_params(("arbitrary",)),
    )(blk_start, nblk, n_used, xst, norm2_g.reshape(1, D_MODEL), w_gate_e, w_up_e, w_down_e)


def _combine_ple_kernel(slots_ref, x1_ref, route_ref, yt_hbm, p_ref, g3_ref, wpg_ref, wple_ref,
                        out_ref, ybuf, sem, *, tm, row0):
    i = pl.program_id(0)
    n_steps = pl.num_programs(0)
    slot = i & 1
    slab = tm * TOK_ROWS

    def start_gather(step, to_slot):
        base = (row0 + step * tm) * TOP_K

        def body(r, carry):
            for k in range(TOP_K):
                src = slots_ref[base + r * TOP_K + k]
                pltpu.make_async_copy(
                    yt_hbm.at[pl.ds(pl.multiple_of(src * TOK_ROWS, TOK_ROWS), TOK_ROWS), :],
                    ybuf.at[to_slot, k, pl.ds(pl.multiple_of(r * TOK_ROWS, TOK_ROWS), TOK_ROWS), :],
                    sem.at[to_slot]).start(priority=k)
            return carry
        lax.fori_loop(0, tm, body, 0, unroll=4)

    @pl.when(i == 0)
    def _():
        start_gather(0, 0)

    @pl.when(i + 1 < n_steps)
    def _():
        start_gather(i + 1, 1 - slot)

    for k in range(TOP_K):
        pltpu.make_async_copy(yt_hbm.at[pl.ds(0, slab), :], ybuf.at[slot, k], sem.at[slot]).wait()
    route = route_ref[...]
    x2 = x1_ref[...] + (route[:, 2:3] * _from_slabs(ybuf.at[slot, 0], tm)
                        + route[:, 3:4] * _from_slabs(ybuf.at[slot, 1], tm))
    h = _rms(x2, g3_ref[...]).astype(BF16)
    gate = jax.nn.sigmoid(_dot(h, wpg_ref[...]))
    out_ref[...] = x2 + gate * _dot(p_ref[...].astype(BF16), wple_ref[...])


def _combine_ple(x1_all, route_all, y_slots, tok_slots, p, norm3_g, wpg, wple, m, tm, row0):
    blk0 = row0 // tm
    grid_spec = pltpu.PrefetchScalarGridSpec(
        num_scalar_prefetch=1,
        grid=(m // tm,),
        in_specs=[
            pl.BlockSpec((tm, D_MODEL), lambda i, s: (blk0 + i, 0)),
            pl.BlockSpec((tm, LANE), lambda i, s: (blk0 + i, 0)),
            pl.BlockSpec(memory_space=pl.ANY),
            pl.BlockSpec((tm, PLE_DIM), lambda i, s: (i, 0)),
            pl.BlockSpec((1, D_MODEL), lambda i, s: (0, 0)),
            pl.BlockSpec((D_MODEL, D_MODEL), lambda i, s: (0, 0)),
            pl.BlockSpec((PLE_DIM, D_MODEL), lambda i, s: (0, 0)),
        ],
        out_specs=pl.BlockSpec((tm, D_MODEL), lambda i, s: (i, 0)),
        scratch_shapes=[
            pltpu.VMEM((2, TOP_K, tm * TOK_ROWS, LANE), F32),
            pltpu.SemaphoreType.DMA((2,)),
        ],
    )
    return pl.pallas_call(
        functools.partial(_combine_ple_kernel, tm=tm, row0=row0),
        grid_spec=grid_spec,
        out_shape=jax.ShapeDtypeStruct((m, D_MODEL), F32),
        compiler_params=_params(("arbitrary",)),
    )(tok_slots, x1_all, route_all, y_slots, p, norm3_g.reshape(1, D_MODEL), wpg, wple)


def _routing_tables(route_all):
    flat_e = route_all[:, 0:TOP_K].astype(jnp.int32).reshape(-1)
    n_asg = flat_e.shape[0]
    onehot = (flat_e[:, None] == jnp.arange(N_EXPERTS, dtype=jnp.int32)[None, :]).astype(F32)
    chunk = math.gcd(n_asg, 2 * LANE)
    within = jnp.einsum('ij,cjk->cik', jnp.tril(jnp.ones((chunk, chunk), F32)),
                        onehot.reshape(n_asg // chunk, chunk, N_EXPERTS))
    totals = within[:, -1, :]
    before = jnp.cumsum(totals, axis=0) - totals
    csum = (within + before[:, None, :]).reshape(n_asg, N_EXPERTS)
    rank = jnp.sum(csum * onehot, axis=1).astype(jnp.int32) - 1
    counts = (before[-1] + totals[-1]).astype(jnp.int32)
    nblk = (counts + MOE_TM - 1) // MOE_TM
    blk_end = jnp.cumsum(nblk)
    blk_start = blk_end - nblk
    first_blk = jnp.sum(onehot * blk_start.astype(F32)[None, :], axis=1).astype(jnp.int32)
    slot = first_blk * MOE_TM + rank
    n_used = blk_end[-1].reshape(1)
    pad_start = blk_start * MOE_TM + counts
    pad_count = nblk * MOE_TM - counts
    i32 = lambda a: a.astype(jnp.int32)
    return i32(blk_start), i32(nblk), i32(n_used), i32(slot), i32(pad_start), i32(pad_count)


def kernel(x_prompt, x_sample, p_prompt, p_sample, cache_k_g0, cache_v_g0, cache_k_g1, cache_v_g1,
           cache_k_g2, cache_v_g2, state_conv, norm1_g, w_in, q_gain, k_gain, conv_w, w_conv_out,
           w_attn_out, w_o, norm2_g, w_coarse, b_coarse, w_fine, b_fine, w_gate_e, w_up_e, w_down_e,
           norm3_g, w_ple_gate, w_ple):
    seq = x_prompt.shape[1]
    assert x_prompt.shape == (1, seq, D_MODEL) and norm1_g.shape[0] == 1, "one prompt, one layer"
    n_seq, t_new, _ = x_sample.shape
    assert seq % ATT_UNIT == 0 and t_new == SUBLANE
    caches_k = (cache_k_g0, cache_k_g1, cache_k_g2)
    caches_v = (cache_v_g0, cache_v_g1, cache_v_g2)
    for g in range(N_GROUPS):
        assert caches_k[g].shape == (1, n_seq, WINDOWS[g], HEADS_PER_GROUP, HEAD_DIM)
    n_s = n_seq * t_new
    n_all = seq + n_s

    xp = x_prompt[0]
    xs = x_sample.reshape(n_s, D_MODEL)
    wc = w_conv_out[0].astype(BF16)
    wa = w_attn_out[0].astype(BF16)
    wo = w_o[0].astype(BF16)
    wpg = w_ple_gate[0].astype(BF16)
    wple = w_ple[0].astype(BF16)
    wr = jnp.concatenate(
        [w_coarse[0], jnp.transpose(w_fine[0], (1, 0, 2)).reshape(D_MODEL, N_EXPERTS)], axis=1)
    wr = jnp.pad(wr, ((0, 0), (0, LANE - wr.shape[1])))
    wr_hi = wr.astype(BF16)
    wr_lo = (wr - wr_hi.astype(F32)).astype(BF16)
    br = jnp.pad(jnp.concatenate([b_coarse[0], b_fine[0].reshape(-1)]), (0, LANE - 36)).reshape(1, LANE)

    z_p, z_s = _inproj(xp, xs, norm1_g[0], w_in[0], tm=2048, tn=512)

    att_p = _prompt_attn(z_p, q_gain[0], k_gain[0])
    o_p, pks, pvs = att_p[0], att_p[1:4], att_p[4:7]
    ck = [c.reshape(n_seq, WINDOWS[g] * HEADS_PER_GROUP, HEAD_DIM) for g, c in enumerate(caches_k)]
    cv = [c.reshape(n_seq, WINDOWS[g] * HEADS_PER_GROUP, HEAD_DIM) for g, c in enumerate(caches_v)]
    att_s = _sample_attn(z_s, ck, cv, q_gain[0], k_gain[0], n_seq, t_new)
    o_s, sks, svs = att_s[0], att_s[1:4], att_s[4:7]

    mg_p, u_tail = _merge(z_p, o_p, conv_w[0], wc, wa, tm=512)
    prefix = jnp.pad(state_conv[0], ((0, 0), (0, t_new - 2), (0, 0))).reshape(n_s, D_CONV)
    mg_s, u_s = _merge(z_s, o_s, conv_w[0], wc, wa, tm=n_s, per_seq=t_new, prefix=prefix)
    x1_all, route_all = _outproj_router(xp, mg_p, xs, mg_s, wo, norm2_g[0], wr_hi, wr_lo, br)

    n_blocks = n_all * TOP_K // MOE_TM + N_EXPERTS
    blk_start, nblk, n_used, tok_slots, pad_start, pad_count = _routing_tables(route_all)
    xst = _dispatch(x1_all, tok_slots, pad_start, pad_count, n_used, n_blocks, tm=n_s)
    y_slots = _moe(xst, norm2_g[0], w_gate_e[0], w_up_e[0], w_down_e[0], blk_start, nblk, n_used)

    y_p = _combine_ple(x1_all, route_all, y_slots, tok_slots, p_prompt[0, 0], norm3_g[0], wpg, wple,
                       m=seq, tm=256, row0=0)
    y_s = _combine_ple(x1_all, route_all, y_slots, tok_slots, p_sample[0].reshape(n_s, PLE_DIM),
                       norm3_g[0], wpg, wple, m=n_s, tm=n_s, row0=seq)

    def state(a, n, g):
        return a.reshape(1, n, WINDOWS[g], HEADS_PER_GROUP, HEAD_DIM)

    pk = [state(a, 1, g) for g, a in enumerate(pks)]
    pv = [state(a, 1, g) for g, a in enumerate(pvs)]
    sk = [state(a, n_seq, g) for g, a in enumerate(sks)]
    sv = [state(a, n_seq, g) for g, a in enumerate(svs)]
    pconv = u_tail[-2:].reshape(1, 1, 2, D_CONV)
    sconv = u_s.reshape(n_seq, t_new, D_CONV)[:, t_new - 2:].reshape(1, n_seq, 2, D_CONV)
    return (y_p.reshape(1, seq, D_MODEL), y_s.reshape(n_seq, t_new, D_MODEL),
            pk[0], pv[0], pk[1], pv[1], pk[2], pv[2], pconv,
            sk[0], sv[0], sk[1], sv[1], sk[2], sv[2], sconv)
```
